```python
import jax, jax.numpy as jnp
from jax import lax
import numpy as np

D_MODEL = 2048
BATCH = 4
SEQ = 2048
DEPTH = 2

D_MIX = D_MODEL
GLA_HEADS = 6
GLA_DK = 64
GLA_DV = 128
GLA_GATE_RANK = 16
GLA_GATE_TEMP = 16.0
GLA_CHUNK = 64
GLA_WIDTH = GLA_HEADS * GLA_DV
MLA_HEADS = 6
MLA_Q_LORA = 384
MLA_KV_LORA = 256
MLA_NOPE = 128
MLA_ROPE = 64
MLA_DV = 128
MLA_WIDTH = MLA_HEADS * MLA_DV
ROPE_THETA = 10000.0
Q_BLOCK = 128
CONV_CH = D_MIX - GLA_WIDTH - MLA_WIDTH
CONV_K = 3
EPS = 1e-6

IN_SPLIT_SIZES = (
    GLA_HEADS * GLA_DK,
    GLA_HEADS * GLA_DK,
    GLA_WIDTH,
    2 * GLA_GATE_RANK,
    MLA_Q_LORA,
    MLA_KV_LORA,
    MLA_ROPE,
    CONV_CH,
    CONV_CH,
    CONV_CH,
    D_MIX,
)
IN_DIM = sum(IN_SPLIT_SIZES)

kernel_name = "bidir_hybrid_gla_mla_shortconv_adaln"


def rms_norm(x, g):
    xf = x.astype(jnp.float32)
    y = xf * lax.rsqrt(jnp.mean(xf * xf, axis=-1, keepdims=True) + EPS)
    return (y * g.astype(jnp.float32)).astype(x.dtype)


def rope(x, cos, sin):
    x1, x2 = jnp.split(x.astype(jnp.float32), 2, axis=-1)
    out = jnp.concatenate([x1 * cos - x2 * sin, x2 * cos + x1 * sin], axis=-1)
    return out.astype(x.dtype)


def gla_chunked(q, k, v, log_a, strict):
    bsz, s, h, dk = q.shape
    dv = v.shape[-1]
    n = s // GLA_CHUNK

    def to_chunks(t):
        return t.reshape(bsz, n, GLA_CHUNK, h, t.shape[-1]).transpose(0, 3, 1, 2, 4).astype(jnp.float32)

    q, k, v, g = to_chunks(q), to_chunks(k), to_chunks(v), to_chunks(log_a)
    b = jnp.cumsum(g, axis=3)
    b_last = b[:, :, :, -1:, :]
    q_dec = q * jnp.exp(b)
    k_inv = k * jnp.exp(-b)
    k_to_end = k * jnp.exp(b_last - b)
    scores = jnp.einsum('bhncd,bhnjd->bhncj', q_dec, k_inv)
    mask = jnp.tril(jnp.ones((GLA_CHUNK, GLA_CHUNK), dtype=bool), k=-1 if strict else 0)
    o_intra = jnp.einsum('bhncj,bhnje->bhnce', jnp.where(mask, scores, 0.0), v)
    chunk_kv = jnp.einsum('bhncd,bhnce->bhnde', k_to_end, v)
    chunk_decay = jnp.exp(b_last[:, :, :, 0, :])

    def step(state, inp):
        decay, kv = inp
        return decay[..., None] * state + kv, state

    init = jnp.zeros((bsz, h, dk, dv), jnp.float32)
    _, s_before = lax.scan(step, init, (jnp.moveaxis(chunk_decay, 2, 0), jnp.moveaxis(chunk_kv, 2, 0)))
    s_before = jnp.moveaxis(s_before, 0, 2)
    o = o_intra + jnp.einsum('bhncd,bhnde->bhnce', q_dec, s_before)
    return o.transpose(0, 2, 3, 1, 4).reshape(bsz, s, h, dv)


def mla_attention(q_nope, q_rope, k_nope, k_rope, v):
    bsz, s, h, _ = q_nope.shape
    nb = s // Q_BLOCK
    scale = (MLA_NOPE + MLA_ROPE) ** -0.5

    def blocks(t):
        return t.reshape(bsz, nb, Q_BLOCK, *t.shape[2:]).swapaxes(0, 1)

    def attend(qb):
        qn, qr = qb
        sc = jnp.einsum('bqhd,bkhd->bhqk', qn, k_nope) + jnp.einsum('bqhr,bkr->bhqk', qr, k_rope)
        p = jax.nn.softmax(sc.astype(jnp.float32) * scale, axis=-1).astype(v.dtype)
        return jnp.einsum('bhqk,bkhd->bqhd', p, v)

    o = lax.map(attend, (blocks(q_nope), blocks(q_rope)))
    return o.swapaxes(0, 1).reshape(bsz, s, h * v.shape[-1])


def hybrid_layer(x, c_act, cos, sin, ada_w, ada_b, norm_g, w_in,
                 gla_wg_f, gla_bg_f, gla_wg_b, gla_bg_b, gla_norm_g,
                 mla_q_norm_g, mla_kv_norm_g, mla_w_uq, mla_w_ukv, mla_out_g,
                 conv_w, conv_out_g, w_out):
    bsz, s, _ = x.shape
    shift, scale, gate = jnp.split(c_act @ ada_w + ada_b, 3, axis=-1)
    h = rms_norm(x, norm_g) * (1.0 + scale[:, None, :]) + shift[:, None, :]
    proj = h @ w_in
    split_idx = [int(i) for i in np.cumsum(IN_SPLIT_SIZES)[:-1]]
    (gq, gk, gv, g_lr, mq, mkv, mkr, cb, cc, cx, z) = jnp.split(proj, split_idx, axis=-1)

    q = gq.reshape(bsz, s, GLA_HEADS, GLA_DK) * (GLA_DK ** -0.5)
    k = gk.reshape(bsz, s, GLA_HEADS, GLA_DK)
    v = gv.reshape(bsz, s, GLA_HEADS, GLA_DV)
    lr_f, lr_b = jnp.split(g_lr, 2, axis=-1)
    la_f = jax.nn.log_sigmoid((lr_f @ gla_wg_f + gla_bg_f).astype(jnp.float32)) / GLA_GATE_TEMP
    la_b = jax.nn.log_sigmoid((lr_b @ gla_wg_b + gla_bg_b).astype(jnp.float32)) / GLA_GATE_TEMP
    la_f = la_f.reshape(bsz, s, GLA_HEADS, GLA_DK)
    la_b = la_b.reshape(bsz, s, GLA_HEADS, GLA_DK)
    o_fwd = gla_chunked(q, k, v, la_f, strict=False)
    o_bwd = jnp.flip(gla_chunked(jnp.flip(q, 1), jnp.flip(k, 1), jnp.flip(v, 1), jnp.flip(la_b, 1), strict=True), 1)
    o_gla = rms_norm((o_fwd + o_bwd).astype(x.dtype), gla_norm_g).reshape(bsz, s, GLA_WIDTH)

    cq = rms_norm(mq, mla_q_norm_g)
    qm = (cq @ mla_w_uq).reshape(bsz, s, MLA_HEADS, MLA_NOPE + MLA_ROPE)
    q_nope, q_rope = qm[..., :MLA_NOPE], rope(qm[..., MLA_NOPE:], cos[:, :, None, :], sin[:, :, None, :])
    ckv = rms_norm(mkv, mla_kv_norm_g)
    kv = (ckv @ mla_w_ukv).reshape(bsz, s, MLA_HEADS, MLA_NOPE + MLA_DV)
    k_nope, v_m = kv[..., :MLA_NOPE], kv[..., MLA_NOPE:]
    k_rope = rope(mkr, cos, sin)
    o_mla = rms_norm(mla_attention(q_nope, q_rope, k_nope, k_rope, v_m), mla_out_g)

    u = cc * cx
    up = jnp.pad(u, ((0, 0), (1, 1), (0, 0)))
    conv = up[:, :-2] * conv_w[0] + up[:, 1:-1] * conv_w[1] + up[:, 2:] * conv_w[2]
    o_conv = rms_norm(cb * conv, conv_out_g)

    y = jnp.concatenate([o_gla, o_mla, o_conv], axis=-1) * jax.nn.silu(z)
    return x + gate[:, None, :] * (y @ w_out)


def setup_inputs(seed: int = 0) -> dict:
    key = jax.random.key(seed)
    ks = jax.random.split(key, 24)
    f32 = jnp.float32

    def nrm(k, shape, std):
        return std * jax.random.normal(k, shape, f32)

    def gain(k, shape):
        return 1.0 + 0.1 * jax.random.normal(k, shape, f32)

    L = DEPTH
    offsets = jax.random.randint(ks[2], (BATCH, 1), 0, 1024, dtype=jnp.int32)
    positions = offsets + jnp.arange(SEQ, dtype=jnp.int32)[None, :]
    return {
        "x": nrm(ks[0], (BATCH, SEQ, D_MODEL), 1.0),
        "c": nrm(ks[1], (BATCH, D_MODEL), 1.0),
        "positions": positions,
        "ada_w": nrm(ks[3], (L, D_MODEL, 3 * D_MODEL), 0.5 * D_MODEL ** -0.5),
        "ada_b": nrm(ks[4], (L, 3 * D_MODEL), 0.02),
        "norm_g": gain(ks[5], (L, D_MODEL)),
        "w_in": nrm(ks[6], (L, D_MODEL, IN_DIM), D_MODEL ** -0.5),
        "gla_wg_f": nrm(ks[7], (L, GLA_GATE_RANK, GLA_HEADS * GLA_DK), GLA_GATE_RANK ** -0.5),
        "gla_bg_f": nrm(ks[8], (L, GLA_HEADS * GLA_DK), 0.1),
        "gla_wg_b": nrm(ks[9], (L, GLA_GATE_RANK, GLA_HEADS * GLA_DK), GLA_GATE_RANK ** -0.5),
        "gla_bg_b": nrm(ks[10], (L, GLA_HEADS * GLA_DK), 0.1),
        "gla_norm_g": gain(ks[11], (L, GLA_DV)),
        "mla_q_norm_g": gain(ks[12], (L, MLA_Q_LORA)),
        "mla_kv_norm_g": gain(ks[13], (L, MLA_KV_LORA)),
        "mla_w_uq": nrm(ks[14], (L, MLA_Q_LORA, MLA_HEADS * (MLA_NOPE + MLA_ROPE)), MLA_Q_LORA ** -0.5),
        "mla_w_ukv": nrm(ks[15], (L, MLA_KV_LORA, MLA_HEADS * (MLA_NOPE + MLA_DV)), MLA_KV_LORA ** -0.5),
        "mla_out_g": gain(ks[16], (L, MLA_WIDTH)),
        "conv_w": nrm(ks[17], (L, CONV_K, CONV_CH), CONV_K ** -0.5),
        "conv_out_g": gain(ks[18], (L, CONV_CH)),
        "w_out": nrm(ks[19], (L, D_MIX, D_MODEL), D_MIX ** -0.5),
        "final_g": gain(ks[20], (D_MODEL,)),
    }


def reference(x, c, positions, ada_w, ada_b, norm_g, w_in,
              gla_wg_f, gla_bg_f, gla_wg_b, gla_bg_b, gla_norm_g,
              mla_q_norm_g, mla_kv_norm_g, mla_w_uq, mla_w_ukv, mla_out_g,
              conv_w, conv_out_g, w_out, final_g):
    inv_freq = ROPE_THETA ** (-jnp.arange(0, MLA_ROPE, 2, dtype=jnp.float32) / MLA_ROPE)
    ang = positions.astype(jnp.float32)[..., None] * inv_freq
    cos, sin = jnp.cos(ang), jnp.sin(ang)
    c_act = jax.nn.silu(c)
    h = x
    for l in range(DEPTH):
        h = hybrid_layer(h, c_act, cos, sin, ada_w[l], ada_b[l], norm_g[l], w_in[l],
                         gla_wg_f[l], gla_bg_f[l], gla_wg_b[l], gla_bg_b[l], gla_norm_g[l],
                         mla_q_norm_g[l], mla_kv_norm_g[l], mla_w_uq[l], mla_w_ukv[l], mla_out_g[l],
                         conv_w[l], conv_out_g[l], w_out[l])
    return rms_norm(h, final_g)
```

```python
import functools

import jax
import jax.numpy as jnp
from jax import lax
from jax.experimental import pallas as pl
from jax.experimental.pallas import tpu as pltpu

D_MODEL = 2048
DEPTH = 2
GLA_HEADS = 6
GLA_DK = 64
GLA_DV = 128
GLA_GATE_RANK = 16
GLA_GATE_TEMP = 16.0
GLA_CHUNK = 64
GLA_WIDTH = GLA_HEADS * GLA_DV
MLA_HEADS = 6
MLA_Q_LORA = 384
MLA_KV_LORA = 256
MLA_NOPE = 128
MLA_ROPE = 64
MLA_DV = 128
MLA_WIDTH = MLA_HEADS * MLA_DV
ROPE_THETA = 10000.0
CONV_CH = D_MODEL - GLA_WIDTH - MLA_WIDTH
EPS = 1e-6

LANES = 128
HALO_ROWS = 8
MISC_W = LANES
QK_W = 2 * LANES

IN_SEGMENTS = (
    ("gq", GLA_HEADS * GLA_DK, jnp.float32),
    ("gk", GLA_HEADS * GLA_DK, jnp.float32),
    ("gv", GLA_WIDTH, jnp.bfloat16),
    ("mq", MLA_Q_LORA, jnp.float32),
    ("misc", MISC_W, jnp.float32),
    ("mkv", MLA_KV_LORA, jnp.float32),
    ("cb", CONV_CH, jnp.float32),
    ("cc", CONV_CH, jnp.float32),
    ("cx", CONV_CH, jnp.float32),
    ("z", D_MODEL, jnp.float32),
)
IN_DIM_P = sum(w for _, w, _ in IN_SEGMENTS)

ROW_TILE = 256
Q_TILE = 256
KV_ROWS = 512
VMEM_LIMIT = 56 * 1024 * 1024

_NT = (((1,), (1,)), ((), ()))
_TN = (((0,), (0,)), ((), ()))


def _rms(x, g):
    ms = jnp.mean(x * x, axis=-1, keepdims=True)
    return x * lax.rsqrt(ms + EPS) * g


def _bdot(a, b):
    return jnp.dot(a, b, preferred_element_type=jnp.float32)


def _ada_kernel(c_ref, w_ref, b_ref, o_ref):
    c = c_ref[...]
    c_act = c * jax.nn.sigmoid(c)
    o_ref[...] = jnp.dot(c_act, w_ref[...], precision=lax.Precision.HIGHEST,
                         preferred_element_type=jnp.float32) + b_ref[...]


def _ada_call(c_pad, ada_w, ada_b):
    rows = c_pad.shape[0]
    n = ada_w.shape[-1]
    tn = 1024
    return pl.pallas_call(
        _ada_kernel,
        grid=(DEPTH, n // tn),
        in_specs=[
            pl.BlockSpec((rows, D_MODEL), lambda l, j: (0, 0)),
            pl.BlockSpec((None, D_MODEL, tn), lambda l, j: (l, 0, j)),
            pl.BlockSpec((None, 1, tn), lambda l, j: (l, 0, j)),
        ],
        out_specs=pl.BlockSpec((None, rows, tn), lambda l, j: (l, 0, j)),
        out_shape=jax.ShapeDtypeStruct((DEPTH, rows, n), jnp.float32),
        compiler_params=pltpu.CompilerParams(
            dimension_semantics=("arbitrary", "arbitrary"), vmem_limit_bytes=VMEM_LIMIT),
        name="ada_mod",
    )(c_pad, ada_w, ada_b.reshape(DEPTH, 1, n))


def _rope_tab_kernel(pos_ref, invf_ref, c_ref, sa_ref, sb_ref):
    ang = pos_ref[...].astype(jnp.float32) * invf_ref[...]
    cos, sin = jnp.cos(ang), jnp.sin(ang)
    lane = lax.broadcasted_iota(jnp.int32, ang.shape, 1)
    half = MLA_ROPE // 2
    c_ref[...] = jnp.where(lane < MLA_ROPE, cos, 0.0)
    sa_ref[...] = jnp.where(lane < half, -sin, 0.0)
    sb_ref[...] = jnp.where((lane >= half) & (lane < MLA_ROPE), sin, 0.0)


def _rope_tab_call(positions):
    bsz, s = positions.shape
    half = MLA_ROPE // 2
    inv_freq = ROPE_THETA ** (-jnp.arange(0, MLA_ROPE, 2, dtype=jnp.float32) / MLA_ROPE)
    invf = jnp.concatenate([inv_freq, inv_freq, jnp.zeros((LANES - 2 * half,), jnp.float32)])
    tab = jax.ShapeDtypeStruct((bsz, s, LANES), jnp.float32)
    spec = pl.BlockSpec((None, s, LANES), lambda b: (b, 0, 0))
    return pl.pallas_call(
        _rope_tab_kernel,
        grid=(bsz,),
        in_specs=[pl.BlockSpec((None, s, 1), lambda b: (b, 0, 0)),
                  pl.BlockSpec((1, LANES), lambda b: (0, 0))],
        out_specs=[spec, spec, spec],
        out_shape=[tab, tab, tab],
        compiler_params=pltpu.CompilerParams(dimension_semantics=("arbitrary",)),
        name="rope_tables",
    )(positions.reshape(bsz, s, 1), invf.reshape(1, LANES))


def _rope(x, c, sa, sb):
    return (x * c + pltpu.roll(x, LANES - MLA_ROPE // 2, axis=1) * sa
            + pltpu.roll(x, MLA_ROPE // 2, axis=1) * sb)


def _inproj_kernel(x_ref, mod_ref, g_ref, w_ref, *out_refs):
    x = x_ref[...]
    mod = mod_ref[...]
    shift, scale = mod[0:1], mod[1:2]
    ms = jnp.mean(x * x, axis=-1, keepdims=True)
    h = (x * lax.rsqrt(ms + EPS) * g_ref[...]) * (1.0 + scale) + shift
    hb = h.astype(jnp.bfloat16)
    off = 0
    for (_, width, _), o_ref in zip(IN_SEGMENTS, out_refs):
        o_ref[...] = _bdot(hb, w_ref[:, off:off + width]).astype(o_ref.dtype)
        off += width


def _inproj_call(x2, mod_l, norm_g, w_in_p, tiles_per_seq):
    t = x2.shape[0]
    tm = ROW_TILE
    out_shape = [jax.ShapeDtypeStruct((t, w), dt) for _, w, dt in IN_SEGMENTS]
    out_specs = [pl.BlockSpec((tm, w), lambda i: (i, 0)) for _, w, _ in IN_SEGMENTS]
    return pl.pallas_call(
        _inproj_kernel,
        grid=(t // tm,),
        in_specs=[
            pl.BlockSpec((tm, D_MODEL), lambda i: (i, 0)),
            pl.BlockSpec((None, 3, D_MODEL), lambda i: (i // tiles_per_seq, 0, 0)),
            pl.BlockSpec((1, D_MODEL), lambda i: (0, 0)),
            pl.BlockSpec((D_MODEL, IN_DIM_P), lambda i: (0, 0), pipeline_mode=pl.Buffered(1)),
        ],
        out_specs=out_specs,
        out_shape=out_shape,
        compiler_params=pltpu.CompilerParams(
            dimension_semantics=("arbitrary",), vmem_limit_bytes=VMEM_LIMIT),
        name="in_proj",
    )(x2, mod_l, norm_g.reshape(1, D_MODEL), w_in_p)


def _log_sigmoid(x):
    return jnp.minimum(x, 0.0) - jnp.log(1.0 + jnp.exp(-jnp.abs(x)))


def _gla_kernel(q_ref, k_ref, v_ref, misc_ref, wgf_ref, bgf_ref, wgb_ref, bgb_ref, g_ref,
                o_ref, laf_s, lab_s):
    s = q_ref.shape[0]
    n_chunks = s // GLA_CHUNK
    cdim = GLA_CHUNK
    m = misc_ref[...].astype(jnp.bfloat16)
    inv_t = 1.0 / GLA_GATE_TEMP
    laf_s[...] = _log_sigmoid(_bdot(m, wgf_ref[...]) + bgf_ref[...]) * inv_t
    lab_s[...] = _log_sigmoid(_bdot(m, wgb_ref[...]) + bgb_ref[...]) * inv_t

    row = lax.broadcasted_iota(jnp.int32, (cdim, cdim), 0)
    col = lax.broadcasted_iota(jnp.int32, (cdim, cdim), 1)
    tri_lo = (col <= row).astype(jnp.float32)
    tri_hi = (col >= row).astype(jnp.float32)
    keep_f = col <= row
    keep_b = col > row
    lane = lax.broadcasted_iota(jnp.int32, (cdim, LANES), 1)
    head_mask = (lane < GLA_DK, lane >= GLA_DK)
    g = g_ref[...]
    q_scale = GLA_DK ** -0.5

    def chunk_terms(r, la_s, tri, edge):
        b = jnp.dot(tri, la_s[r, :], precision=lax.Precision.HIGHEST,
                    preferred_element_type=jnp.float32)
        b_last = b[edge:edge + 1, :]
        q = q_ref[r, :] * q_scale
        k = k_ref[r, :]
        q_dec = q * jnp.exp(b)
        k_inv = (k * jnp.exp(-b)).astype(jnp.bfloat16)
        k_end = (k * jnp.exp(b_last - b)).astype(jnp.bfloat16)
        return q_dec, k_inv, k_end, jnp.exp(b_last)

    def head_step(j, r, q_dec, k_inv, k_end, decay, keep, state_t):
        qj = jnp.where(head_mask[j], q_dec, 0.0).astype(jnp.bfloat16)
        sc = lax.dot_general(qj, k_inv, _NT, preferred_element_type=jnp.float32)
        sc = jnp.where(keep, sc, 0.0).astype(jnp.bfloat16)
        vj = v_ref[r, j * GLA_DV:(j + 1) * GLA_DV]
        o = _bdot(sc, vj) + lax.dot_general(qj, state_t.astype(jnp.bfloat16), _NT,
                                            preferred_element_type=jnp.float32)
        kv_t = lax.dot_general(vj, k_end, _TN, preferred_element_type=jnp.float32)
        return o, state_t * decay + kv_t

    def fwd_body(n, states):
        r = pl.ds(pl.multiple_of(n * cdim, cdim), cdim)
        q_dec, k_inv, k_end, decay = chunk_terms(r, laf_s, tri_lo, cdim - 1)
        new = []
        for j in range(2):
            o, st = head_step(j, r, q_dec, k_inv, k_end, decay, keep_f, states[j])
            o_ref[r, j * GLA_DV:(j + 1) * GLA_DV] = o
            new.append(st)
        return tuple(new)

    def bwd_body(i, states):
        n = n_chunks - 1 - i
        r = pl.ds(pl.multiple_of(n * cdim, cdim), cdim)
        q_dec, k_inv, k_end, decay = chunk_terms(r, lab_s, tri_hi, 0)
        new = []
        for j in range(2):
            o, st = head_step(j, r, q_dec, k_inv, k_end, decay, keep_b, states[j])
            tot = o_ref[r, j * GLA_DV:(j + 1) * GLA_DV] + o
            o_ref[r, j * GLA_DV:(j + 1) * GLA_DV] = _rms(tot, g)
            new.append(st)
        return tuple(new)

    zero = jnp.zeros((GLA_DV, LANES), jnp.float32)
    lax.fori_loop(0, n_chunks, fwd_body, (zero, zero))
    lax.fori_loop(0, n_chunks, bwd_body, (zero, zero))


def _gla_call(gq, gk, gv, misc, wgf_p, bgf, wgb_p, bgb, gla_norm_g):
    bsz, s, _ = gq.shape
    pairs = GLA_HEADS // 2
    qk_spec = pl.BlockSpec((None, s, LANES), lambda b, p: (b, 0, p))
    v_spec = pl.BlockSpec((None, s, 2 * GLA_DV), lambda b, p: (b, 0, p))
    w_spec = pl.BlockSpec((MISC_W, LANES), lambda b, p: (0, p))
    b_spec = pl.BlockSpec((1, LANES), lambda b, p: (0, p))
    return pl.pallas_call(
        _gla_kernel,
        grid=(bsz, pairs),
        in_specs=[qk_spec, qk_spec, v_spec,
                  pl.BlockSpec((None, s, MISC_W), lambda b, p: (b, 0, 0)),
                  w_spec, b_spec, w_spec, b_spec,
                  pl.BlockSpec((1, GLA_DV), lambda b, p: (0, 0))],
        out_specs=v_spec,
        out_shape=jax.ShapeDtypeStruct((bsz, s, GLA_WIDTH), jnp.float32),
        scratch_shapes=[pltpu.VMEM((s, LANES), jnp.float32), pltpu.VMEM((s, LANES), jnp.float32)],
        compiler_params=pltpu.CompilerParams(
            dimension_semantics=("arbitrary", "arbitrary"), vmem_limit_bytes=VMEM_LIMIT),
        name="gla",
    )(gq, gk, gv, misc, wgf_p, bgf.reshape(1, -1), wgb_p, bgb.reshape(1, -1),
      gla_norm_g.reshape(1, GLA_DV))


def _mla_kernel(mq_ref, mkv_ref, misc_ref, cq_ref, saq_ref, sbq_ref, ck_ref, sak_ref, sbk_ref,
                gq_ref, gkv_ref, wuq_ref, wuk_ref, wuv_ref, o_ref, k_s, v_s):
    s = mkv_ref.shape[0]

    @pl.when(pl.program_id(2) == 0)
    def _build_kv():
        def body(i, carry):
            r = pl.ds(pl.multiple_of(i * KV_ROWS, KV_ROWS), KV_ROWS)
            ckv = _rms(mkv_ref[r, :], gkv_ref[...]).astype(jnp.bfloat16)
            k_s[r, 0:MLA_NOPE] = _bdot(ckv, wuk_ref[...]).astype(jnp.bfloat16)
            k_s[r, MLA_NOPE:QK_W] = _rope(misc_ref[r, :], ck_ref[r, :], sak_ref[r, :],
                                          sbk_ref[r, :]).astype(jnp.bfloat16)
            v_s[r, :] = _bdot(ckv, wuv_ref[...]).astype(jnp.bfloat16)
            return carry
        lax.fori_loop(0, s // KV_ROWS, body, 0)

    cq = _rms(mq_ref[...], gq_ref[...]).astype(jnp.bfloat16)
    qf = _bdot(cq, wuq_ref[...])
    q_rope = _rope(qf[:, MLA_NOPE:QK_W], cq_ref[...], saq_ref[...], sbq_ref[...])
    q = jnp.concatenate([qf[:, 0:MLA_NOPE], q_rope], axis=1).astype(jnp.bfloat16)
    sc = lax.dot_general(q, k_s[...], _NT, preferred_element_type=jnp.float32)
    sc = sc * ((MLA_NOPE + MLA_ROPE) ** -0.5)
    p = jnp.exp(sc - jnp.max(sc, axis=-1, keepdims=True))
    denom = jnp.sum(p, axis=-1, keepdims=True)
    o = _bdot(p.astype(jnp.bfloat16), v_s[...])
    o_ref[...] = o / denom


def _mla_call(mq, mkv, misc, tabs, gq, gkv, wuq_p, wuk_p, wuv_p):
    bsz, s, _ = mq.shape
    tq = Q_TILE
    ctab, sa, sb = tabs
    q_tab = pl.BlockSpec((None, tq, LANES), lambda b, h, i: (b, i, 0))
    k_tab = pl.BlockSpec((None, s, LANES), lambda b, h, i: (b, 0, 0))
    return pl.pallas_call(
        _mla_kernel,
        grid=(bsz, MLA_HEADS, s // tq),
        in_specs=[
            pl.BlockSpec((None, tq, MLA_Q_LORA), lambda b, h, i: (b, i, 0)),
            pl.BlockSpec((None, s, MLA_KV_LORA), lambda b, h, i: (b, 0, 0)),
            pl.BlockSpec((None, s, MISC_W), lambda b, h, i: (b, 0, 0)),
            q_tab, q_tab, q_tab, k_tab, k_tab, k_tab,
            pl.BlockSpec((1, MLA_Q_LORA), lambda b, h, i: (0, 0)),
            pl.BlockSpec((1, MLA_KV_LORA), lambda b, h, i: (0, 0)),
            pl.BlockSpec((MLA_Q_LORA, QK_W), lambda b, h, i: (0, h)),
            pl.BlockSpec((MLA_KV_LORA, MLA_NOPE), lambda b, h, i: (0, h)),
            pl.BlockSpec((MLA_KV_LORA, MLA_DV), lambda b, h, i: (0, h)),
        ],
        out_specs=pl.BlockSpec((None, tq, MLA_DV), lambda b, h, i: (b, i, h)),
        out_shape=jax.ShapeDtypeStruct((bsz, s, MLA_WIDTH), jnp.float32),
        scratch_shapes=[pltpu.VMEM((s, QK_W), jnp.bfloat16), pltpu.VMEM((s, MLA_DV), jnp.bfloat16)],
        compiler_params=pltpu.CompilerParams(
            dimension_semantics=("arbitrary", "arbitrary", "arbitrary"),
            vmem_limit_bytes=VMEM_LIMIT),
        name="mla",
    )(mq, mkv, misc, ctab, sa, sb, ctab, sa, sb, gq.reshape(1, -1), gkv.reshape(1, -1),
      wuq_p, wuk_p, wuv_p)


def _outproj_kernel(tiles_per_seq, final, x_ref, mod_ref, ogla_ref, omla_ref, cb_ref, cc_ref,
                    cx_ref, ccp_ref, cxp_ref, ccn_ref, cxn_ref, z_ref, mg_ref, cw_ref, cg_ref,
                    fg_ref, w_ref, o_ref):
    j = pl.program_id(0) % tiles_per_seq
    tm = x_ref.shape[0]
    mla = _rms(omla_ref[...], mg_ref[...])

    u = cc_ref[...] * cx_ref[...]
    prev_ok = (j > 0).astype(jnp.float32)
    next_ok = (j < tiles_per_seq - 1).astype(jnp.float32)
    u_prev = ccp_ref[HALO_ROWS - 1:HALO_ROWS, :] * cxp_ref[HALO_ROWS - 1:HALO_ROWS, :] * prev_ok
    u_next = ccn_ref[0:1, :] * cxn_ref[0:1, :] * next_ok
    rows = lax.broadcasted_iota(jnp.int32, u.shape, 0)
    up = jnp.where(rows == 0, u_prev, pltpu.roll(u, 1, axis=0))
    un = jnp.where(rows == tm - 1, u_next, pltpu.roll(u, tm - 1, axis=0))
    cw = cw_ref[...]
    conv = up * cw[0:1] + u * cw[1:2] + un * cw[2:3]
    oc = _rms(cb_ref[...] * conv, cg_ref[...])

    z = z_ref[...]
    y = jnp.concatenate([ogla_ref[...], mla, oc], axis=1) * (z * jax.nn.sigmoid(z))
    acc = _bdot(y.astype(jnp.bfloat16), w_ref[...])
    gate = mod_ref[...][2:3]
    out = x_ref[...] + gate * acc
    if final:
        out = _rms(out, fg_ref[...])
    o_ref[...] = out


def _outproj_call(x2, mod_l, o_gla, o_mla, cb, cc, cx, z, mla_out_g, conv_w, conv_out_g,
                  final_g, w_out_b, tiles_per_seq, final):
    t = x2.shape[0]
    tm = ROW_TILE
    halo_per_tile = tm // HALO_ROWS
    n_halo = t // HALO_ROWS

    def row_spec(w):
        return pl.BlockSpec((tm, w), lambda i: (i, 0))

    prev_spec = pl.BlockSpec((HALO_ROWS, CONV_CH),
                             lambda i: (jnp.maximum(i * halo_per_tile - 1, 0), 0))
    next_spec = pl.BlockSpec((HALO_ROWS, CONV_CH),
                             lambda i: (jnp.minimum((i + 1) * halo_per_tile, n_halo - 1), 0))

    def full(shape):
        return pl.BlockSpec(shape, lambda i: (0,) * len(shape))

    return pl.pallas_call(
        functools.partial(_outproj_kernel, tiles_per_seq, final),
        grid=(t // tm,),
        in_specs=[
            row_spec(D_MODEL),
            pl.BlockSpec((None, 3, D_MODEL), lambda i: (i // tiles_per_seq, 0, 0)),
            row_spec(GLA_WIDTH), row_spec(MLA_WIDTH),
            row_spec(CONV_CH), row_spec(CONV_CH), row_spec(CONV_CH),
            prev_spec, prev_spec, next_spec, next_spec,
            row_spec(D_MODEL),
            full((1, MLA_WIDTH)), full((3, CONV_CH)), full((1, CONV_CH)), full((1, D_MODEL)),
            pl.BlockSpec((D_MODEL, D_MODEL), lambda i: (0, 0), pipeline_mode=pl.Buffered(1)),
        ],
        out_specs=row_spec(D_MODEL),
        out_shape=jax.ShapeDtypeStruct((t, D_MODEL), jnp.float32),
        compiler_params=pltpu.CompilerParams(
            dimension_semantics=("arbitrary",), vmem_limit_bytes=VMEM_LIMIT),
        name="out_proj",
    )(x2, mod_l, o_gla, o_mla, cb, cc, cx, cc, cx, cc, cx, z,
      mla_out_g.reshape(1, -1), conv_w, conv_out_g.reshape(1, -1), final_g.reshape(1, -1), w_out_b)


def _permute_w_in(w):
    hk = GLA_HEADS * GLA_DK
    o_glr = 2 * hk + GLA_WIDTH
    o_mq = o_glr + 2 * GLA_GATE_RANK
    o_mkv = o_mq + MLA_Q_LORA
    o_mkr = o_mkv + MLA_KV_LORA
    o_cb = o_mkr + MLA_ROPE
    pad = jnp.zeros((w.shape[0], MISC_W - MLA_ROPE - 2 * GLA_GATE_RANK), w.dtype)
    cols = [w[:, 0:o_glr], w[:, o_mq:o_mkv], w[:, o_mkr:o_cb], w[:, o_glr:o_mq], pad,
            w[:, o_mkv:o_mkr], w[:, o_cb:]]
    return jnp.concatenate(cols, axis=1).astype(jnp.bfloat16)


def _pad_gate_w(wg, row0):
    out = jnp.zeros((MISC_W, wg.shape[1]), jnp.float32)
    return out.at[row0:row0 + GLA_GATE_RANK].set(wg).astype(jnp.bfloat16)


def _permute_w_uq(w):
    w3 = w.reshape(MLA_Q_LORA, MLA_HEADS, MLA_NOPE + MLA_ROPE)
    w3 = jnp.pad(w3, ((0, 0), (0, 0), (0, QK_W - MLA_NOPE - MLA_ROPE)))
    return w3.reshape(MLA_Q_LORA, MLA_HEADS * QK_W).astype(jnp.bfloat16)


def _split_w_ukv(w):
    w3 = w.reshape(MLA_KV_LORA, MLA_HEADS, MLA_NOPE + MLA_DV)
    wk = w3[:, :, :MLA_NOPE].reshape(MLA_KV_LORA, MLA_HEADS * MLA_NOPE)
    wv = w3[:, :, MLA_NOPE:].reshape(MLA_KV_LORA, MLA_HEADS * MLA_DV)
    return wk.astype(jnp.bfloat16), wv.astype(jnp.bfloat16)


def kernel(x, c, positions, ada_w, ada_b, norm_g, w_in, gla_wg_f, gla_bg_f, gla_wg_b, gla_bg_b,
           gla_norm_g, mla_q_norm_g, mla_kv_norm_g, mla_w_uq, mla_w_ukv, mla_out_g, conv_w,
           conv_out_g, w_out, final_g):
    bsz, s, d = x.shape
    t = bsz * s
    tiles_per_seq = s // ROW_TILE

    c_pad = jnp.pad(c, ((0, HALO_ROWS - bsz), (0, 0)))
    mod = _ada_call(c_pad, ada_w, ada_b)[:, :bsz].reshape(DEPTH, bsz, 3, d)
    tabs = _rope_tab_call(positions)

    h = x.reshape(t, d)
    for l in range(DEPTH):
        segs = _inproj_call(h, mod[l], norm_g[l], _permute_w_in(w_in[l]), tiles_per_seq)
        gq, gk, gv, mq, misc, mkv, cb, cc, cx, z = segs

        def b3(a):
            return a.reshape(bsz, s, a.shape[-1])

        o_gla = _gla_call(b3(gq), b3(gk), b3(gv), b3(misc),
                          _pad_gate_w(gla_wg_f[l], MLA_ROPE), gla_bg_f[l],
                          _pad_gate_w(gla_wg_b[l], MLA_ROPE + GLA_GATE_RANK), gla_bg_b[l],
                          gla_norm_g[l])
        wuk_p, wuv_p = _split_w_ukv(mla_w_ukv[l])
        o_mla = _mla_call(b3(mq), b3(mkv), b3(misc), tabs, mla_q_norm_g[l], mla_kv_norm_g[l],
                          _permute_w_uq(mla_w_uq[l]), wuk_p, wuv_p)
        h = _outproj_call(h, mod[l], o_gla.reshape(t, -1), o_mla.reshape(t, -1), cb, cc, cx, z,
                          mla_out_g[l], conv_w[l], conv_out_g[l], final_g,
                          w_out[l].astype(jnp.bfloat16), tiles_per_seq, l == DEPTH - 1)
    return h.reshape(bsz, s, d)
```

```python
import functools
import math

import jax
import jax.numpy as jnp
from jax import lax
from jax.experimental import pallas as pl
from jax.experimental.pallas import tpu as pltpu

D_MODEL = 2048
DEPTH = 2
GLA_HEADS = 6
GLA_DK = 64
GLA_DV = 128
GLA_GATE_RANK = 16
GLA_GATE_TEMP = 16.0
GLA_CHUNK = 64
GLA_WIDTH = GLA_HEADS * GLA_DV
MLA_HEADS = 6
MLA_Q_LORA = 384
MLA_KV_LORA = 256
MLA_NOPE = 128
MLA_ROPE = 64
MLA_DV = 128
MLA_WIDTH = MLA_HEADS * MLA_DV
ROPE_THETA = 10000.0
CONV_CH = D_MODEL - GLA_WIDTH - MLA_WIDTH
EPS = 1e-6

LANES = 128
HALO_ROWS = 8
BF16_ROWS = 16
MISC_W = LANES
QK_W = 2 * LANES

IN_SEGMENTS = (
    ("gq", GLA_HEADS * GLA_DK, jnp.float32),
    ("gk", GLA_HEADS * GLA_DK, jnp.float32),
    ("gv", GLA_WIDTH, jnp.bfloat16),
    ("mq", MLA_Q_LORA, jnp.float32),
    ("misc", MISC_W, jnp.float32),
    ("mkv", MLA_KV_LORA, jnp.float32),
    ("cb", CONV_CH, jnp.float32),
    ("cc", CONV_CH, jnp.float32),
    ("cx", CONV_CH, jnp.float32),
    ("z", D_MODEL, jnp.float32),
)
IN_DIM_P = sum(w for _, w, _ in IN_SEGMENTS)

ROW_TILE = 256
Q_TILE = 512
KV_ROWS = 512
SCAN_ROWS = 256
GLA_GROUP = 8
VMEM_LIMIT = 56 * 1024 * 1024

_NT = (((1,), (1,)), ((), ()))
_TN = (((0,), (0,)), ((), ()))


def _rms(x, g):
    ms = jnp.mean(x * x, axis=-1, keepdims=True)
    return x * lax.rsqrt(ms + EPS) * g


def _bdot(a, b):
    return jnp.dot(a, b, preferred_element_type=jnp.float32)


def _ada_kernel(c_ref, w_ref, b_ref, o_ref):
    c = c_ref[...]
    c_act = c * jax.nn.sigmoid(c)
    o_ref[...] = jnp.dot(c_act, w_ref[...], precision=lax.Precision.HIGHEST,
                         preferred_element_type=jnp.float32) + b_ref[...]


def _ada_call(c_pad, ada_w, ada_b):
    rows = c_pad.shape[0]
    n = ada_w.shape[-1]
    tn = 1024
    return pl.pallas_call(
        _ada_kernel,
        grid=(DEPTH, n // tn),
        in_specs=[
            pl.BlockSpec((rows, D_MODEL), lambda l, j: (0, 0)),
            pl.BlockSpec((None, D_MODEL, tn), lambda l, j: (l, 0, j)),
            pl.BlockSpec((None, 1, tn), lambda l, j: (l, 0, j)),
        ],
        out_specs=pl.BlockSpec((None, rows, tn), lambda l, j: (l, 0, j)),
        out_shape=jax.ShapeDtypeStruct((DEPTH, rows, n), jnp.float32),
        compiler_params=pltpu.CompilerParams(
            dimension_semantics=("arbitrary", "arbitrary"), vmem_limit_bytes=VMEM_LIMIT),
        name="ada_mod",
    )(c_pad, ada_w, ada_b.reshape(DEPTH, 1, n))


def _rope_tab_kernel(pos_ref, invf_ref, c_ref, sa_ref, sb_ref):
    ang = pos_ref[...].astype(jnp.float32) * invf_ref[...]
    cos, sin = jnp.cos(ang), jnp.sin(ang)
    lane = lax.broadcasted_iota(jnp.int32, ang.shape, 1)
    half = MLA_ROPE // 2
    c_ref[...] = jnp.where(lane < MLA_ROPE, cos, 0.0)
    sa_ref[...] = jnp.where(lane < half, -sin, 0.0)
    sb_ref[...] = jnp.where((lane >= half) & (lane < MLA_ROPE), sin, 0.0)


def _rope_tab_call(positions):
    bsz, s = positions.shape
    half = MLA_ROPE // 2
    inv_freq = ROPE_THETA ** (-jnp.arange(0, MLA_ROPE, 2, dtype=jnp.float32) / MLA_ROPE)
    invf = jnp.concatenate([inv_freq, inv_freq, jnp.zeros((LANES - 2 * half,), jnp.float32)])
    tab = jax.ShapeDtypeStruct((bsz, s, LANES), jnp.float32)
    spec = pl.BlockSpec((None, s, LANES), lambda b: (b, 0, 0))
    return pl.pallas_call(
        _rope_tab_kernel,
        grid=(bsz,),
        in_specs=[pl.BlockSpec((None, s, 1), lambda b: (b, 0, 0)),
                  pl.BlockSpec((1, LANES), lambda b: (0, 0))],
        out_specs=[spec, spec, spec],
        out_shape=[tab, tab, tab],
        compiler_params=pltpu.CompilerParams(dimension_semantics=("arbitrary",)),
        name="rope_tables",
    )(positions.reshape(bsz, s, 1), invf.reshape(1, LANES))


def _rope(x, c, sa, sb):
    return (x * c + pltpu.roll(x, LANES - MLA_ROPE // 2, axis=1) * sa
            + pltpu.roll(x, MLA_ROPE // 2, axis=1) * sb)


def _inproj_kernel(x_ref, mod_ref, g_ref, w_ref, *out_refs):
    x = x_ref[...]
    mod = mod_ref[...]
    shift, scale = mod[0:1], mod[1:2]
    ms = jnp.mean(x * x, axis=-1, keepdims=True)
    h = (x * lax.rsqrt(ms + EPS) * g_ref[...]) * (1.0 + scale) + shift
    hb = h.astype(jnp.bfloat16)
    off = 0
    for (_, width, _), o_ref in zip(IN_SEGMENTS, out_refs):
        o_ref[...] = _bdot(hb, w_ref[:, off:off + width]).astype(o_ref.dtype)
        off += width


def _inproj_call(x2, mod_l, norm_g, w_in_p, tiles_per_seq):
    t = x2.shape[0]
    tm = ROW_TILE
    out_shape = [jax.ShapeDtypeStruct((t, w), dt) for _, w, dt in IN_SEGMENTS]
    out_specs = [pl.BlockSpec((tm, w), lambda i: (i, 0)) for _, w, _ in IN_SEGMENTS]
    return pl.pallas_call(
        _inproj_kernel,
        grid=(t // tm,),
        in_specs=[
            pl.BlockSpec((tm, D_MODEL), lambda i: (i, 0)),
            pl.BlockSpec((None, 3, D_MODEL), lambda i: (i // tiles_per_seq, 0, 0)),
            pl.BlockSpec((1, D_MODEL), lambda i: (0, 0)),
            pl.BlockSpec((D_MODEL, IN_DIM_P), lambda i: (0, 0), pipeline_mode=pl.Buffered(1)),
        ],
        out_specs=out_specs,
        out_shape=out_shape,
        compiler_params=pltpu.CompilerParams(
            dimension_semantics=("arbitrary",), vmem_limit_bytes=VMEM_LIMIT),
        name="in_proj",
    )(x2, mod_l, norm_g.reshape(1, D_MODEL), w_in_p)


def _log_sigmoid(x):
    return jnp.minimum(x, 0.0) - jnp.log(1.0 + jnp.exp(-jnp.abs(x)))


def _gla_kernel(q_ref, k_ref, v_ref, misc_ref, wgf_ref, bgf_ref, wgb_ref, bgb_ref, g_ref,
                o_ref, bf_s, bb_s, qcat_s, kv_s, dec_s, st_s, intra_s):
    s = q_ref.shape[0]
    cdim = GLA_CHUNK
    n_chunks = s // cdim
    pair_dk = 2 * GLA_DK
    pair_dv = 2 * GLA_DV
    inv_t = 1.0 / GLA_GATE_TEMP
    q_scale = GLA_DK ** -0.5

    pos = lax.broadcasted_iota(jnp.int32, (SCAN_ROWS, pair_dk), 0) & (cdim - 1)

    def scan_body(i, carry):
        r = pl.ds(pl.multiple_of(i * SCAN_ROWS, SCAN_ROWS), SCAN_ROWS)
        m = misc_ref[r, :].astype(jnp.bfloat16)
        bf = _log_sigmoid(_bdot(m, wgf_ref[...]) + bgf_ref[...]) * inv_t
        bb = _log_sigmoid(_bdot(m, wgb_ref[...]) + bgb_ref[...]) * inv_t
        step = 1
        while step < cdim:
            bf = bf + jnp.where(pos >= step, pltpu.roll(bf, step, axis=0), 0.0)
            bb = bb + jnp.where(pos < cdim - step, pltpu.roll(bb, SCAN_ROWS - step, axis=0), 0.0)
            step *= 2
        bf_s[r, :] = bf
        bb_s[r, :] = bb
        return carry

    lax.fori_loop(0, s // SCAN_ROWS, scan_body, 0)

    row = lax.broadcasted_iota(jnp.int32, (cdim, pair_dk), 0)
    key_pos = lax.broadcasted_iota(jnp.int32, (cdim, pair_dk), 1) & (cdim - 1)
    keep_f = key_pos <= row
    keep_b = key_pos > row
    cat_block = lax.broadcasted_iota(jnp.int32, (cdim, 2 * pair_dk), 1) // GLA_DK
    v_head = lax.broadcasted_iota(jnp.int32, (cdim, pair_dv), 1) // GLA_DV

    def intra_body(i, carry):
        chunks = [i * GLA_GROUP + c for c in range(GLA_GROUP)]
        rows = [pl.ds(pl.multiple_of(n * cdim, cdim), cdim) for n in chunks]
        scores, k_ends = [], []
        for n, r in zip(chunks, rows):
            bf = bf_s[r, :]
            bb = bb_s[r, :]
            bf_last = bf[cdim - 1:cdim, :]
            bb_last = bb[0:1, :]
            q = q_ref[r, :] * q_scale
            k = k_ref[r, :]
            q_cat = jnp.concatenate([q * jnp.exp(bf), q * jnp.exp(bb)],
                                    axis=1).astype(jnp.bfloat16)
            k_inv = jnp.concatenate([k * jnp.exp(-bf), k * jnp.exp(-bb)], axis=1)
            k_ends.append(jnp.concatenate(
                [k * jnp.exp(bf_last - bf), k * jnp.exp(bb_last - bb)], axis=1).astype(jnp.bfloat16))
            dec_s[n] = jnp.concatenate([jnp.exp(bf_last), jnp.exp(bb_last)], axis=1)
            qcat_s[r, :] = q_cat
            k_bd = jnp.concatenate([jnp.where(cat_block == c, k_inv, 0.0) for c in range(4)],
                                   axis=0).astype(jnp.bfloat16)
            scores.append(lax.dot_general(q_cat, k_bd, _NT,
                                          preferred_element_type=jnp.float32))
        for n, r, k_end in zip(chunks, rows, k_ends):
            kv_s[n] = lax.dot_general(v_ref[r, :], k_end, _TN,
                                      preferred_element_type=jnp.float32)
        for r, sc in zip(rows, scores):
            p = (jnp.where(keep_f, sc[:, 0:pair_dk], 0.0)
                 + jnp.where(keep_b, sc[:, pair_dk:], 0.0)).astype(jnp.bfloat16)
            vv = v_ref[r, :]
            v_bd = jnp.concatenate(
                [jnp.where(v_head == j, vv, jnp.zeros_like(vv)) for j in range(2)], axis=0)
            intra_s[r, :] = _bdot(p, v_bd)
        return carry

    lax.fori_loop(0, n_chunks // GLA_GROUP, intra_body, 0)

    same_head = (lax.broadcasted_iota(jnp.int32, (pair_dv, pair_dk), 0) // GLA_DV
                 == lax.broadcasted_iota(jnp.int32, (pair_dv, pair_dk), 1) // GLA_DK)

    def state_body(i, states):
        sf, sb = states
        nf = i
        nb = n_chunks - 1 - i
        st_s[nf, :, 0:pair_dk] = jnp.where(same_head, sf, 0.0).astype(jnp.bfloat16)
        st_s[nb, :, pair_dk:] = jnp.where(same_head, sb, 0.0).astype(jnp.bfloat16)
        sf = sf * dec_s[nf][:, 0:pair_dk] + kv_s[nf, :, 0:pair_dk]
        sb = sb * dec_s[nb][:, pair_dk:] + kv_s[nb, :, pair_dk:]
        return sf, sb

    zero = jnp.zeros((pair_dv, pair_dk), jnp.float32)
    lax.fori_loop(0, n_chunks, state_body, (zero, zero))

    g = g_ref[...]

    def out_body(i, carry):
        chunks = [i * GLA_GROUP + c for c in range(GLA_GROUP)]
        rows = [pl.ds(pl.multiple_of(n * cdim, cdim), cdim) for n in chunks]
        inter = [lax.dot_general(qcat_s[r, :], st_s[n], _NT, preferred_element_type=jnp.float32)
                 for n, r in zip(chunks, rows)]
        for r, o_inter in zip(rows, inter):
            o = intra_s[r, :] + o_inter
            for j in range(2):
                o_ref[r, j * GLA_DV:(j + 1) * GLA_DV] = _rms(o[:, j * GLA_DV:(j + 1) * GLA_DV], g)
        return carry

    lax.fori_loop(0, n_chunks // GLA_GROUP, out_body, 0)


def _gla_call(gq, gk, gv, misc, wgf_p, bgf, wgb_p, bgb, gla_norm_g):
    bsz, s, _ = gq.shape
    pairs = GLA_HEADS // 2
    n_chunks = s // GLA_CHUNK
    pair_dk, pair_dv = 2 * GLA_DK, 2 * GLA_DV
    qk_spec = pl.BlockSpec((None, s, pair_dk), lambda b, p: (b, 0, p))
    v_spec = pl.BlockSpec((None, s, pair_dv), lambda b, p: (b, 0, p))
    w_spec = pl.BlockSpec((MISC_W, pair_dk), lambda b, p: (0, p))
    b_spec = pl.BlockSpec((1, pair_dk), lambda b, p: (0, p))
    return pl.pallas_call(
        _gla_kernel,
        grid=(bsz, pairs),
        in_specs=[qk_spec, qk_spec, v_spec,
                  pl.BlockSpec((None, s, MISC_W), lambda b, p: (b, 0, 0)),
                  w_spec, b_spec, w_spec, b_spec,
                  pl.BlockSpec((1, GLA_DV), lambda b, p: (0, 0))],
        out_specs=v_spec,
        out_shape=jax.ShapeDtypeStruct((bsz, s, GLA_WIDTH), jnp.float32),
        scratch_shapes=[
            pltpu.VMEM((s, pair_dk), jnp.float32),
            pltpu.VMEM((s, pair_dk), jnp.float32),
            pltpu.VMEM((s, 2 * pair_dk), jnp.bfloat16),
            pltpu.VMEM((n_chunks, pair_dv, 2 * pair_dk), jnp.float32),
            pltpu.VMEM((n_chunks, 1, 2 * pair_dk), jnp.float32),
            pltpu.VMEM((n_chunks, pair_dv, 2 * pair_dk), jnp.bfloat16),
            pltpu.VMEM((s, pair_dv), jnp.float32),
        ],
        compiler_params=pltpu.CompilerParams(
            dimension_semantics=("arbitrary", "arbitrary"), vmem_limit_bytes=VMEM_LIMIT),
        name="gla",
    )(gq, gk, gv, misc, wgf_p, bgf.reshape(1, -1), wgb_p, bgb.reshape(1, -1),
      gla_norm_g.reshape(1, GLA_DV))


def _mla_kernel(mq_ref, mkv_ref, misc_ref, c_ref, sa_ref, sb_ref, gq_ref, gkv_ref,
                wuq_ref, wuk_ref, wuvt_ref, o_ref, ckv_s, q_s, k_s, vt_s):
    s = mkv_ref.shape[0]
    head = pl.program_id(1)
    blocks = [slice(i * KV_ROWS, (i + 1) * KV_ROWS) for i in range(s // KV_ROWS)]
    q_mult = (MLA_NOPE + MLA_ROPE) ** -0.5 * math.log2(math.e)

    @pl.when(head == 0)
    def _per_batch_row():
        for r in blocks:
            tabs = (c_ref[r, :], sa_ref[r, :], sb_ref[r, :])
            ckv_s[r, :] = _rms(mkv_ref[r, :], gkv_ref[...]).astype(jnp.bfloat16)
            k_s[r, MLA_NOPE:] = _rope(misc_ref[r, :], *tabs).astype(jnp.bfloat16)
            cq = _rms(mq_ref[r, :], gq_ref[...]).astype(jnp.bfloat16)
            qf = _bdot(cq, wuq_ref[...])
            for h in range(MLA_HEADS):
                qh = qf[:, h * QK_W:(h + 1) * QK_W]
                q_rope = _rope(qh[:, MLA_NOPE:], *tabs)
                q_s[h, r, :] = (jnp.concatenate([qh[:, 0:MLA_NOPE], q_rope], axis=1)
                                * q_mult).astype(jnp.bfloat16)
        vt_s[MLA_DV:, :] = jnp.ones((vt_s.shape[0] - MLA_DV, s), jnp.bfloat16)

    for r in blocks:
        ckv = ckv_s[r, :]
        k_s[r, 0:MLA_NOPE] = _bdot(ckv, wuk_ref[...]).astype(jnp.bfloat16)
        vt_s[0:MLA_DV, r] = lax.dot_general(
            wuvt_ref[...], ckv, _NT, preferred_element_type=jnp.float32).astype(jnp.bfloat16)

    def scores_t(j):
        q = q_s[head, j * Q_TILE:(j + 1) * Q_TILE, :]
        return lax.dot_general(k_s[...], q, _NT, preferred_element_type=jnp.float32)

    def finish(j, sc_t):
        p_t = jnp.exp2(sc_t - jnp.max(sc_t, axis=0, keepdims=True)).astype(jnp.bfloat16)
        o_t = _bdot(vt_s[...], p_t)
        o_ref[j * Q_TILE:(j + 1) * Q_TILE, :] = (o_t[0:MLA_DV, :] / o_t[MLA_DV:MLA_DV + 1, :]).T

    n_tiles = s // Q_TILE
    pending = scores_t(0)
    for j in range(n_tiles):
        upcoming = scores_t(j + 1) if j + 1 < n_tiles else None
        finish(j, pending)
        pending = upcoming


def _mla_call(mq, mkv, misc, tabs, gq, gkv, wuq_p, wuk_p, wuvt_p):
    bsz, s, _ = mq.shape

    def per_row(width):
        return pl.BlockSpec((None, s, width), lambda b, h: (b, 0, 0))

    return pl.pallas_call(
        _mla_kernel,
        grid=(bsz, MLA_HEADS),
        in_specs=[
            per_row(MLA_Q_LORA), per_row(MLA_KV_LORA), per_row(MISC_W),
            per_row(LANES), per_row(LANES), per_row(LANES),
            pl.BlockSpec((1, MLA_Q_LORA), lambda b, h: (0, 0)),
            pl.BlockSpec((1, MLA_KV_LORA), lambda b, h: (0, 0)),
            pl.BlockSpec((MLA_Q_LORA, MLA_HEADS * QK_W), lambda b, h: (0, 0)),
            pl.BlockSpec((MLA_KV_LORA, MLA_NOPE), lambda b, h: (0, h)),
            pl.BlockSpec((MLA_DV, MLA_KV_LORA), lambda b, h: (h, 0)),
        ],
        out_specs=pl.BlockSpec((None, s, MLA_DV), lambda b, h: (b, 0, h)),
        out_shape=jax.ShapeDtypeStruct((bsz, s, MLA_WIDTH), jnp.float32),
        scratch_shapes=[
            pltpu.VMEM((s, MLA_KV_LORA), jnp.bfloat16),
            pltpu.VMEM((MLA_HEADS, s, QK_W), jnp.bfloat16),
            pltpu.VMEM((s, QK_W), jnp.bfloat16),
            pltpu.VMEM((MLA_DV + BF16_ROWS, s), jnp.bfloat16),
        ],
        compiler_params=pltpu.CompilerParams(
            dimension_semantics=("arbitrary", "arbitrary"),
            vmem_limit_bytes=VMEM_LIMIT),
        name="mla",
    )(mq, mkv, misc, *tabs, gq.reshape(1, -1), gkv.reshape(1, -1), wuq_p, wuk_p, wuvt_p)


def _outproj_kernel(tiles_per_seq, final, x_ref, mod_ref, ogla_ref, omla_ref, cb_ref, cc_ref,
                    cx_ref, ccp_ref, cxp_ref, ccn_ref, cxn_ref, z_ref, mg_ref, cw_ref, cg_ref,
                    fg_ref, w_ref, o_ref):
    j = pl.program_id(0) % tiles_per_seq
    tm = x_ref.shape[0]
    mla = _rms(omla_ref[...], mg_ref[...])

    u = cc_ref[...] * cx_ref[...]
    prev_ok = (j > 0).astype(jnp.float32)
    next_ok = (j < tiles_per_seq - 1).astype(jnp.float32)
    u_prev = ccp_ref[HALO_ROWS - 1:HALO_ROWS, :] * cxp_ref[HALO_ROWS - 1:HALO_ROWS, :] * prev_ok
    u_next = ccn_ref[0:1, :] * cxn_ref[0:1, :] * next_ok
    rows = lax.broadcasted_iota(jnp.int32, u.shape, 0)
    up = jnp.where(rows == 0, u_prev, pltpu.roll(u, 1, axis=0))
    un = jnp.where(rows == tm - 1, u_next, pltpu.roll(u, tm - 1, axis=0))
    cw = cw_ref[...]
    conv = up * cw[0:1] + u * cw[1:2] + un * cw[2:3]
    oc = _rms(cb_ref[...] * conv, cg_ref[...])

    z = z_ref[...]
    y = jnp.concatenate([ogla_ref[...], mla, oc], axis=1) * (z * jax.nn.sigmoid(z))
    acc = _bdot(y.astype(jnp.bfloat16), w_ref[...])
    gate = mod_ref[...][2:3]
    out = x_ref[...] + gate * acc
    if final:
        out = _rms(out, fg_ref[...])
    o_ref[...] = out


def _outproj_call(x2, mod_l, o_gla, o_mla, cb, cc, cx, z, mla_out_g, conv_w, conv_out_g,
                  final_g, w_out_b, tiles_per_seq, final):
    t = x2.shape[0]
    tm = ROW_TILE
    halo_per_tile = tm // HALO_ROWS
    n_halo = t // HALO_ROWS

    def row_spec(w):
        return pl.BlockSpec((tm, w), lambda i: (i, 0))

    prev_spec = pl.BlockSpec((HALO_ROWS, CONV_CH),
                             lambda i: (jnp.maximum(i * halo_per_tile - 1, 0), 0))
    next_spec = pl.BlockSpec((HALO_ROWS, CONV_CH),
                             lambda i: (jnp.minimum((i + 1) * halo_per_tile, n_halo - 1), 0))

    def full(shape):
        return pl.BlockSpec(shape, lambda i: (0,) * len(shape))

    return pl.pallas_call(
        functools.partial(_outproj_kernel, tiles_per_seq, final),
        grid=(t // tm,),
        in_specs=[
            row_spec(D_MODEL),
            pl.BlockSpec((None, 3, D_MODEL), lambda i: (i // tiles_per_seq, 0, 0)),
            row_spec(GLA_WIDTH), row_spec(MLA_WIDTH),
            row_spec(CONV_CH), row_spec(CONV_CH), row_spec(CONV_CH),
            prev_spec, prev_spec, next_spec, next_spec,
            row_spec(D_MODEL),
            full((1, MLA_WIDTH)), full((3, CONV_CH)), full((1, CONV_CH)), full((1, D_MODEL)),
            pl.BlockSpec((D_MODEL, D_MODEL), lambda i: (0, 0), pipeline_mode=pl.Buffered(1)),
        ],
        out_specs=row_spec(D_MODEL),
        out_shape=jax.ShapeDtypeStruct((t, D_MODEL), jnp.float32),
        compiler_params=pltpu.CompilerParams(
            dimension_semantics=("arbitrary",), vmem_limit_bytes=VMEM_LIMIT),
        name="out_proj",
    )(x2, mod_l, o_gla, o_mla, cb, cc, cx, cc, cx, cc, cx, z,
      mla_out_g.reshape(1, -1), conv_w, conv_out_g.reshape(1, -1), final_g.reshape(1, -1), w_out_b)


def _permute_w_in(w):
    hk = GLA_HEADS * GLA_DK
    o_glr = 2 * hk + GLA_WIDTH
    o_mq = o_glr + 2 * GLA_GATE_RANK
    o_mkv = o_mq + MLA_Q_LORA
    o_mkr = o_mkv + MLA_KV_LORA
    o_cb = o_mkr + MLA_ROPE
    pad = jnp.zeros((w.shape[0], MISC_W - MLA_ROPE - 2 * GLA_GATE_RANK), w.dtype)
    cols = [w[:, 0:o_glr], w[:, o_mq:o_mkv], w[:, o_mkr:o_cb], w[:, o_glr:o_mq], pad,
            w[:, o_mkv:o_mkr], w[:, o_cb:]]
    return jnp.concatenate(cols, axis=1).astype(jnp.bfloat16)


def _pad_gate_w(wg, row0):
    out = jnp.zeros((MISC_W, wg.shape[1]), jnp.float32)
    return out.at[row0:row0 + GLA_GATE_RANK].set(wg).astype(jnp.bfloat16)


def _permute_w_uq(w):
    w3 = w.reshape(MLA_Q_LORA, MLA_HEADS, MLA_NOPE + MLA_ROPE)
    w3 = jnp.pad(w3, ((0, 0), (0, 0), (0, QK_W - MLA_NOPE - MLA_ROPE)))
    return w3.reshape(MLA_Q_LORA, MLA_HEADS * QK_W).astype(jnp.bfloat16)


def _split_w_ukv(w):
    w3 = w.reshape(MLA_KV_LORA, MLA_HEADS, MLA_NOPE + MLA_DV)
    wk = w3[:, :, :MLA_NOPE].reshape(MLA_KV_LORA, MLA_HEADS * MLA_NOPE)
    wv_t = w3[:, :, MLA_NOPE:].reshape(MLA_KV_LORA, MLA_HEADS * MLA_DV).T
    return wk.astype(jnp.bfloat16), wv_t.astype(jnp.bfloat16)


def kernel(x, c, positions, ada_w, ada_b, norm_g, w_in, gla_wg_f, gla_bg_f, gla_wg_b, gla_bg_b,
           gla_norm_g, mla_q_norm_g, mla_kv_norm_g, mla_w_uq, mla_w_ukv, mla_out_g, conv_w,
           conv_out_g, w_out, final_g):
    bsz, s, d = x.shape
    t = bsz * s
    tiles_per_seq = s // ROW_TILE

    c_pad = jnp.pad(c, ((0, HALO_ROWS - bsz), (0, 0)))
    mod = _ada_call(c_pad, ada_w, ada_b)[:, :bsz].reshape(DEPTH, bsz, 3, d)
    tabs = _rope_tab_call(positions)

    h = x.reshape(t, d)
    for l in range(DEPTH):
        segs = _inproj_call(h, mod[l], norm_g[l], _permute_w_in(w_in[l]), tiles_per_seq)
        gq, gk, gv, mq, misc, mkv, cb, cc, cx, z = segs

        def b3(a):
            return a.reshape(bsz, s, a.shape[-1])

        o_gla = _gla_call(b3(gq), b3(gk), b3(gv), b3(misc),
                          _pad_gate_w(gla_wg_f[l], MLA_ROPE), gla_bg_f[l],
                          _pad_gate_w(gla_wg_b[l], MLA_ROPE + GLA_GATE_RANK), gla_bg_b[l],
                          gla_norm_g[l])
        wuk_p, wuv_p = _split_w_ukv(mla_w_ukv[l])
        o_mla = _mla_call(b3(mq), b3(mkv), b3(misc), tabs, mla_q_norm_g[l], mla_kv_norm_g[l],
                          _permute_w_uq(mla_w_uq[l]), wuk_p, wuv_p)
        h = _outproj_call(h, mod[l], o_gla.reshape(t, -1), o_mla.reshape(t, -1), cb, cc, cx, z,
                          mla_out_g[l], conv_w[l], conv_out_g[l], final_g,
                          w_out[l].astype(jnp.bfloat16), tiles_per_seq, l == DEPTH - 1)
    return h.reshape(bsz, s, d)
```

```python
import functools
import math

import jax
import jax.numpy as jnp
from jax import lax
from jax.experimental import pallas as pl
from jax.experimental.pallas import tpu as pltpu

D_MODEL = 2048
DEPTH = 2
GLA_HEADS = 6
GLA_DK = 64
GLA_DV = 128
GLA_GATE_RANK = 16
GLA_GATE_TEMP = 16.0
GLA_CHUNK = 64
GLA_WIDTH = GLA_HEADS * GLA_DV
MLA_HEADS = 6
MLA_Q_LORA = 384
MLA_KV_LORA = 256
MLA_NOPE = 128
MLA_ROPE = 64
MLA_DV = 128
MLA_WIDTH = MLA_HEADS * MLA_DV
ROPE_THETA = 10000.0
CONV_CH = D_MODEL - GLA_WIDTH - MLA_WIDTH
EPS = 1e-6

LANES = 128
HALO_ROWS = 8
BF16_ROWS = 16
MISC_W = LANES
QK_W = 2 * LANES

IN_SEGMENTS = (
    ("gq", GLA_HEADS * GLA_DK, jnp.float32),
    ("gk", GLA_HEADS * GLA_DK, jnp.float32),
    ("gv", GLA_WIDTH, jnp.bfloat16),
    ("mq", MLA_Q_LORA, jnp.float32),
    ("misc", MISC_W, jnp.float32),
    ("mkv", MLA_KV_LORA, jnp.float32),
    ("cb", CONV_CH, jnp.float32),
    ("cc", CONV_CH, jnp.float32),
    ("cx", CONV_CH, jnp.float32),
    ("z", D_MODEL, jnp.float32),
)
IN_DIM_P = sum(w for _, w, _ in IN_SEGMENTS)

ROW_TILE = 256
Q_TILE = 512
KV_ROWS = 512
SCAN_ROWS = 256
GLA_GROUP = 8
VMEM_LIMIT = 56 * 1024 * 1024

_NT = (((1,), (1,)), ((), ()))
_TN = (((0,), (0,)), ((), ()))


def _rms(x, g):
    ms = jnp.mean(x * x, axis=-1, keepdims=True)
    return x * lax.rsqrt(ms + EPS) * g


def _bdot(a, b):
    return jnp.dot(a, b, preferred_element_type=jnp.float32)


def _ada_kernel(c_ref, w_ref, b_ref, o_ref):
    c = c_ref[...]
    c_act = c * jax.nn.sigmoid(c)
    o_ref[...] = jnp.dot(c_act, w_ref[...], precision=lax.Precision.HIGHEST,
                         preferred_element_type=jnp.float32) + b_ref[...]


def _ada_call(c_pad, ada_w, ada_b):
    rows = c_pad.shape[0]
    n = ada_w.shape[-1]
    tn = 1024
    return pl.pallas_call(
        _ada_kernel,
        grid=(DEPTH, n // tn),
        in_specs=[
            pl.BlockSpec((rows, D_MODEL), lambda l, j: (0, 0)),
            pl.BlockSpec((None, D_MODEL, tn), lambda l, j: (l, 0, j)),
            pl.BlockSpec((None, 1, tn), lambda l, j: (l, 0, j)),
        ],
        out_specs=pl.BlockSpec((None, rows, tn), lambda l, j: (l, 0, j)),
        out_shape=jax.ShapeDtypeStruct((DEPTH, rows, n), jnp.float32),
        compiler_params=pltpu.CompilerParams(
            dimension_semantics=("arbitrary", "arbitrary"), vmem_limit_bytes=VMEM_LIMIT),
        name="ada_mod",
    )(c_pad, ada_w, ada_b.reshape(DEPTH, 1, n))


def _rope_tab_kernel(pos_ref, invf_ref, c_ref, sa_ref, sb_ref):
    ang = pos_ref[...].astype(jnp.float32) * invf_ref[...]
    cos, sin = jnp.cos(ang), jnp.sin(ang)
    lane = lax.broadcasted_iota(jnp.int32, ang.shape, 1)
    half = MLA_ROPE // 2
    c_ref[...] = jnp.where(lane < MLA_ROPE, cos, 0.0)
    sa_ref[...] = jnp.where(lane < half, -sin, 0.0)
    sb_ref[...] = jnp.where((lane >= half) & (lane < MLA_ROPE), sin, 0.0)


def _rope_tab_call(positions):
    bsz, s = positions.shape
    half = MLA_ROPE // 2
    inv_freq = ROPE_THETA ** (-jnp.arange(0, MLA_ROPE, 2, dtype=jnp.float32) / MLA_ROPE)
    invf = jnp.concatenate([inv_freq, inv_freq, jnp.zeros((LANES - 2 * half,), jnp.float32)])
    tab = jax.ShapeDtypeStruct((bsz, s, LANES), jnp.float32)
    spec = pl.BlockSpec((None, s, LANES), lambda b: (b, 0, 0))
    return pl.pallas_call(
        _rope_tab_kernel,
        grid=(bsz,),
        in_specs=[pl.BlockSpec((None, s, 1), lambda b: (b, 0, 0)),
                  pl.BlockSpec((1, LANES), lambda b: (0, 0))],
        out_specs=[spec, spec, spec],
        out_shape=[tab, tab, tab],
        compiler_params=pltpu.CompilerParams(dimension_semantics=("arbitrary",)),
        name="rope_tables",
    )(positions.reshape(bsz, s, 1), invf.reshape(1, LANES))


def _rope(x, c, sa, sb):
    return (x * c + pltpu.roll(x, LANES - MLA_ROPE // 2, axis=1) * sa
            + pltpu.roll(x, MLA_ROPE // 2, axis=1) * sb)


def _inproj_kernel(x_ref, mod_ref, g_ref, w_ref, *out_refs):
    x = x_ref[...]
    mod = mod_ref[...]
    shift, scale = mod[0:1], mod[1:2]
    ms = jnp.mean(x * x, axis=-1, keepdims=True)
    h = (x * lax.rsqrt(ms + EPS) * g_ref[...]) * (1.0 + scale) + shift
    hb = h.astype(jnp.bfloat16)
    off = 0
    for (_, width, _), o_ref in zip(IN_SEGMENTS, out_refs):
        o_ref[...] = _bdot(hb, w_ref[:, off:off + width]).astype(o_ref.dtype)
        off += width


def _layer_spec(layer, shape):
    return pl.BlockSpec((None,) + shape, lambda *_: (layer,) + (0,) * len(shape))


def _inproj_call(layer, x2, mod, norm_g, w_in_p, tiles_per_seq):
    t = x2.shape[0]
    tm = ROW_TILE
    out_shape = [jax.ShapeDtypeStruct((t, w), dt) for _, w, dt in IN_SEGMENTS]
    out_specs = [pl.BlockSpec((tm, w), lambda i: (i, 0)) for _, w, _ in IN_SEGMENTS]
    return pl.pallas_call(
        _inproj_kernel,
        grid=(t // tm,),
        in_specs=[
            pl.BlockSpec((tm, D_MODEL), lambda i: (i, 0)),
            pl.BlockSpec((None, None, 3, D_MODEL), lambda i: (layer, i // tiles_per_seq, 0, 0)),
            _layer_spec(layer, (1, D_MODEL)),
            pl.BlockSpec((None, D_MODEL, IN_DIM_P), lambda i: (layer, 0, 0),
                         pipeline_mode=pl.Buffered(1)),
        ],
        out_specs=out_specs,
        out_shape=out_shape,
        compiler_params=pltpu.CompilerParams(
            dimension_semantics=("arbitrary",), vmem_limit_bytes=VMEM_LIMIT),
        name="in_proj",
    )(x2, mod, norm_g, w_in_p)


def _log_sigmoid(x):
    return jnp.minimum(x, 0.0) - jnp.log(1.0 + jnp.exp(-jnp.abs(x)))


def _gla_kernel(q_ref, k_ref, v_ref, misc_ref, wgf_ref, bgf_ref, wgb_ref, bgb_ref, g_ref,
                o_ref, bf_s, bb_s, qcat_s, kv_s, dec_s, st_s, intra_s):
    s = q_ref.shape[0]
    cdim = GLA_CHUNK
    n_chunks = s // cdim
    pair_dk = 2 * GLA_DK
    pair_dv = 2 * GLA_DV
    inv_t = 1.0 / GLA_GATE_TEMP
    q_scale = GLA_DK ** -0.5

    pos = lax.broadcasted_iota(jnp.int32, (SCAN_ROWS, pair_dk), 0) & (cdim - 1)

    def scan_body(i, carry):
        r = pl.ds(pl.multiple_of(i * SCAN_ROWS, SCAN_ROWS), SCAN_ROWS)
        m = misc_ref[r, :].astype(jnp.bfloat16)
        bf = _log_sigmoid(_bdot(m, wgf_ref[...]) + bgf_ref[...]) * inv_t
        bb = _log_sigmoid(_bdot(m, wgb_ref[...]) + bgb_ref[...]) * inv_t
        step = 1
        while step < cdim:
            bf = bf + jnp.where(pos >= step, pltpu.roll(bf, step, axis=0), 0.0)
            bb = bb + jnp.where(pos < cdim - step, pltpu.roll(bb, SCAN_ROWS - step, axis=0), 0.0)
            step *= 2
        bf_s[r, :] = bf
        bb_s[r, :] = bb
        return carry

    lax.fori_loop(0, s // SCAN_ROWS, scan_body, 0)

    row = lax.broadcasted_iota(jnp.int32, (cdim, pair_dk), 0)
    key_pos = lax.broadcasted_iota(jnp.int32, (cdim, pair_dk), 1) & (cdim - 1)
    keep_f = key_pos <= row
    keep_b = key_pos > row
    cat_block = lax.broadcasted_iota(jnp.int32, (cdim, 2 * pair_dk), 1) // GLA_DK
    v_head = lax.broadcasted_iota(jnp.int32, (cdim, pair_dv), 1) // GLA_DV

    def intra_body(i, carry):
        chunks = [i * GLA_GROUP + c for c in range(GLA_GROUP)]
        rows = [pl.ds(pl.multiple_of(n * cdim, cdim), cdim) for n in chunks]
        scores, k_ends = [], []
        for n, r in zip(chunks, rows):
            bf = bf_s[r, :]
            bb = bb_s[r, :]
            bf_last = bf[cdim - 1:cdim, :]
            bb_last = bb[0:1, :]
            q = q_ref[r, :] * q_scale
            k = k_ref[r, :]
            q_cat = jnp.concatenate([q * jnp.exp(bf), q * jnp.exp(bb)],
                                    axis=1).astype(jnp.bfloat16)
            k_inv = jnp.concatenate([k * jnp.exp(-bf), k * jnp.exp(-bb)], axis=1)
            k_ends.append(jnp.concatenate(
                [k * jnp.exp(bf_last - bf), k * jnp.exp(bb_last - bb)], axis=1).astype(jnp.bfloat16))
            dec_s[n] = jnp.concatenate([jnp.exp(bf_last), jnp.exp(bb_last)], axis=1)
            qcat_s[r, :] = q_cat
            k_bd = jnp.concatenate([jnp.where(cat_block == c, k_inv, 0.0) for c in range(4)],
                                   axis=0).astype(jnp.bfloat16)
            scores.append(lax.dot_general(q_cat, k_bd, _NT,
                                          preferred_element_type=jnp.float32))
        for n, r, k_end in zip(chunks, rows, k_ends):
            kv_s[n] = lax.dot_general(v_ref[r, :], k_end, _TN,
                                      preferred_element_type=jnp.float32)
        for r, sc in zip(rows, scores):
            p = (jnp.where(keep_f, sc[:, 0:pair_dk], 0.0)
                 + jnp.where(keep_b, sc[:, pair_dk:], 0.0)).astype(jnp.bfloat16)
            vv = v_ref[r, :]
            v_bd = jnp.concatenate(
                [jnp.where(v_head == j, vv, jnp.zeros_like(vv)) for j in range(2)], axis=0)
            intra_s[r, :] = _bdot(p, v_bd)
        return carry

    lax.fori_loop(0, n_chunks // GLA_GROUP, intra_body, 0)

    same_head = (lax.broadcasted_iota(jnp.int32, (pair_dv, pair_dk), 0) // GLA_DV
                 == lax.broadcasted_iota(jnp.int32, (pair_dv, pair_dk), 1) // GLA_DK)

    def state_body(i, states):
        sf, sb = states
        nf = i
        nb = n_chunks - 1 - i
        st_s[nf, :, 0:pair_dk] = jnp.where(same_head, sf, 0.0).astype(jnp.bfloat16)
        st_s[nb, :, pair_dk:] = jnp.where(same_head, sb, 0.0).astype(jnp.bfloat16)
        sf = sf * dec_s[nf][:, 0:pair_dk] + kv_s[nf, :, 0:pair_dk]
        sb = sb * dec_s[nb][:, pair_dk:] + kv_s[nb, :, pair_dk:]
        return sf, sb

    zero = jnp.zeros((pair_dv, pair_dk), jnp.float32)
    lax.fori_loop(0, n_chunks, state_body, (zero, zero))

    g = g_ref[...]

    def out_body(i, carry):
        chunks = [i * GLA_GROUP + c for c in range(GLA_GROUP)]
        rows = [pl.ds(pl.multiple_of(n * cdim, cdim), cdim) for n in chunks]
        inter = [lax.dot_general(qcat_s[r, :], st_s[n], _NT, preferred_element_type=jnp.float32)
                 for n, r in zip(chunks, rows)]
        for r, o_inter in zip(rows, inter):
            o = intra_s[r, :] + o_inter
            for j in range(2):
                o_ref[r, j * GLA_DV:(j + 1) * GLA_DV] = _rms(o[:, j * GLA_DV:(j + 1) * GLA_DV], g)
        return carry

    lax.fori_loop(0, n_chunks // GLA_GROUP, out_body, 0)


def _gla_call(layer, gq, gk, gv, misc, wg_cat, bg_cat, gla_norm_g):
    bsz, s, _ = gq.shape
    pairs = GLA_HEADS // 2
    n_chunks = s // GLA_CHUNK
    pair_dk, pair_dv = 2 * GLA_DK, 2 * GLA_DV
    qk_spec = pl.BlockSpec((None, s, pair_dk), lambda b, p: (b, 0, p))
    v_spec = pl.BlockSpec((None, s, pair_dv), lambda b, p: (b, 0, p))

    def gate_specs(direction):
        return (pl.BlockSpec((None, MISC_W, pair_dk), lambda b, p: (layer, 0, direction * pairs + p)),
                pl.BlockSpec((None, 1, pair_dk), lambda b, p: (layer, 0, direction * pairs + p)))

    return pl.pallas_call(
        _gla_kernel,
        grid=(bsz, pairs),
        in_specs=[qk_spec, qk_spec, v_spec,
                  pl.BlockSpec((None, s, MISC_W), lambda b, p: (b, 0, 0)),
                  *gate_specs(0), *gate_specs(1),
                  _layer_spec(layer, (1, GLA_DV))],
        out_specs=v_spec,
        out_shape=jax.ShapeDtypeStruct((bsz, s, GLA_WIDTH), jnp.float32),
        scratch_shapes=[
            pltpu.VMEM((s, pair_dk), jnp.float32),
            pltpu.VMEM((s, pair_dk), jnp.float32),
            pltpu.VMEM((s, 2 * pair_dk), jnp.bfloat16),
            pltpu.VMEM((n_chunks, pair_dv, 2 * pair_dk), jnp.float32),
            pltpu.VMEM((n_chunks, 1, 2 * pair_dk), jnp.float32),
            pltpu.VMEM((n_chunks, pair_dv, 2 * pair_dk), jnp.bfloat16),
            pltpu.VMEM((s, pair_dv), jnp.float32),
        ],
        compiler_params=pltpu.CompilerParams(
            dimension_semantics=("arbitrary", "arbitrary"), vmem_limit_bytes=VMEM_LIMIT),
        name="gla",
    )(gq, gk, gv, misc, wg_cat, bg_cat, wg_cat, bg_cat, gla_norm_g)


def _mla_kernel(mq_ref, mkv_ref, misc_ref, c_ref, sa_ref, sb_ref, gq_ref, gkv_ref,
                wuq_ref, wuk_ref, wuvt_ref, o_ref, ckv_s, q_s, k_s, vt_s):
    s = mkv_ref.shape[0]
    head = pl.program_id(1)
    blocks = [slice(i * KV_ROWS, (i + 1) * KV_ROWS) for i in range(s // KV_ROWS)]
    q_mult = (MLA_NOPE + MLA_ROPE) ** -0.5 * math.log2(math.e)

    @pl.when(head == 0)
    def _per_batch_row():
        for r in blocks:
            tabs = (c_ref[r, :], sa_ref[r, :], sb_ref[r, :])
            ckv_s[r, :] = _rms(mkv_ref[r, :], gkv_ref[...]).astype(jnp.bfloat16)
            k_s[r, MLA_NOPE:] = _rope(misc_ref[r, :], *tabs).astype(jnp.bfloat16)
            cq = _rms(mq_ref[r, :], gq_ref[...]).astype(jnp.bfloat16)
            qf = _bdot(cq, wuq_ref[...])
            for h in range(MLA_HEADS):
                qh = qf[:, h * QK_W:(h + 1) * QK_W]
                q_rope = _rope(qh[:, MLA_NOPE:], *tabs)
                q_s[h, r, :] = (jnp.concatenate([qh[:, 0:MLA_NOPE], q_rope], axis=1)
                                * q_mult).astype(jnp.bfloat16)
        vt_s[MLA_DV:, :] = jnp.ones((vt_s.shape[0] - MLA_DV, s), jnp.bfloat16)

    for r in blocks:
        ckv = ckv_s[r, :]
        k_s[r, 0:MLA_NOPE] = _bdot(ckv, wuk_ref[...]).astype(jnp.bfloat16)
        vt_s[0:MLA_DV, r] = lax.dot_general(
            wuvt_ref[...], ckv, _NT, preferred_element_type=jnp.float32).astype(jnp.bfloat16)

    def scores_t(j):
        q = q_s[head, j * Q_TILE:(j + 1) * Q_TILE, :]
        return lax.dot_general(k_s[...], q, _NT, preferred_element_type=jnp.float32)

    def finish(j, sc_t):
        p_t = jnp.exp2(sc_t - jnp.max(sc_t, axis=0, keepdims=True)).astype(jnp.bfloat16)
        o_t = _bdot(vt_s[...], p_t)
        o_ref[j * Q_TILE:(j + 1) * Q_TILE, :] = (o_t[0:MLA_DV, :] / o_t[MLA_DV:MLA_DV + 1, :]).T

    n_tiles = s // Q_TILE
    pending = scores_t(0)
    for j in range(n_tiles):
        upcoming = scores_t(j + 1) if j + 1 < n_tiles else None
        finish(j, pending)
        pending = upcoming


def _mla_call(layer, mq, mkv, misc, tabs, gq, gkv, wuq_p, wuk_p, wuvt_p):
    bsz, s, _ = mq.shape

    def per_row(width):
        return pl.BlockSpec((None, s, width), lambda b, h: (b, 0, 0))

    return pl.pallas_call(
        _mla_kernel,
        grid=(bsz, MLA_HEADS),
        in_specs=[
            per_row(MLA_Q_LORA), per_row(MLA_KV_LORA), per_row(MISC_W),
            per_row(LANES), per_row(LANES), per_row(LANES),
            _layer_spec(layer, (1, MLA_Q_LORA)),
            _layer_spec(layer, (1, MLA_KV_LORA)),
            _layer_spec(layer, (MLA_Q_LORA, MLA_HEADS * QK_W)),
            pl.BlockSpec((None, MLA_KV_LORA, MLA_NOPE), lambda b, h: (layer, 0, h)),
            pl.BlockSpec((None, MLA_DV, MLA_KV_LORA), lambda b, h: (layer, h, 0)),
        ],
        out_specs=pl.BlockSpec((None, s, MLA_DV), lambda b, h: (b, 0, h)),
        out_shape=jax.ShapeDtypeStruct((bsz, s, MLA_WIDTH), jnp.float32),
        scratch_shapes=[
            pltpu.VMEM((s, MLA_KV_LORA), jnp.bfloat16),
            pltpu.VMEM((MLA_HEADS, s, QK_W), jnp.bfloat16),
            pltpu.VMEM((s, QK_W), jnp.bfloat16),
            pltpu.VMEM((MLA_DV + BF16_ROWS, s), jnp.bfloat16),
        ],
        compiler_params=pltpu.CompilerParams(
            dimension_semantics=("arbitrary", "arbitrary"),
            vmem_limit_bytes=VMEM_LIMIT),
        name="mla",
    )(mq, mkv, misc, *tabs, gq, gkv, wuq_p, wuk_p, wuvt_p)


def _outproj_kernel(tiles_per_seq, final, x_ref, mod_ref, ogla_ref, omla_ref, cb_ref, cc_ref,
                    cx_ref, ccp_ref, cxp_ref, ccn_ref, cxn_ref, z_ref, mg_ref, cw_ref, cg_ref,
                    fg_ref, w_ref, o_ref):
    j = pl.program_id(0) % tiles_per_seq
    tm = x_ref.shape[0]
    mla = _rms(omla_ref[...], mg_ref[...])

    u = cc_ref[...] * cx_ref[...]
    prev_ok = (j > 0).astype(jnp.float32)
    next_ok = (j < tiles_per_seq - 1).astype(jnp.float32)
    u_prev = ccp_ref[HALO_ROWS - 1:HALO_ROWS, :] * cxp_ref[HALO_ROWS - 1:HALO_ROWS, :] * prev_ok
    u_next = ccn_ref[0:1, :] * cxn_ref[0:1, :] * next_ok
    rows = lax.broadcasted_iota(jnp.int32, u.shape, 0)
    up = jnp.where(rows == 0, u_prev, pltpu.roll(u, 1, axis=0))
    un = jnp.where(rows == tm - 1, u_next, pltpu.roll(u, tm - 1, axis=0))
    cw = cw_ref[...]
    conv = up * cw[0:1] + u * cw[1:2] + un * cw[2:3]
    oc = _rms(cb_ref[...] * conv, cg_ref[...])

    z = z_ref[...]
    y = jnp.concatenate([ogla_ref[...], mla, oc], axis=1) * (z * jax.nn.sigmoid(z))
    acc = _bdot(y.astype(jnp.bfloat16), w_ref[...])
    gate = mod_ref[...][2:3]
    out = x_ref[...] + gate * acc
    if final:
        out = _rms(out, fg_ref[...])
    o_ref[...] = out


def _outproj_call(layer, x2, mod, o_gla, o_mla, cb, cc, cx, z, mla_out_g, conv_w, conv_out_g,
                  final_g, w_out_b, tiles_per_seq, final):
    t = x2.shape[0]
    tm = ROW_TILE
    halo_per_tile = tm // HALO_ROWS
    n_halo = t // HALO_ROWS

    def row_spec(w):
        return pl.BlockSpec((tm, w), lambda i: (i, 0))

    prev_spec = pl.BlockSpec((HALO_ROWS, CONV_CH),
                             lambda i: (jnp.maximum(i * halo_per_tile - 1, 0), 0))
    next_spec = pl.BlockSpec((HALO_ROWS, CONV_CH),
                             lambda i: (jnp.minimum((i + 1) * halo_per_tile, n_halo - 1), 0))

    return pl.pallas_call(
        functools.partial(_outproj_kernel, tiles_per_seq, final),
        grid=(t // tm,),
        in_specs=[
            row_spec(D_MODEL),
            pl.BlockSpec((None, None, 3, D_MODEL), lambda i: (layer, i // tiles_per_seq, 0, 0)),
            row_spec(GLA_WIDTH), row_spec(MLA_WIDTH),
            row_spec(CONV_CH), row_spec(CONV_CH), row_spec(CONV_CH),
            prev_spec, prev_spec, next_spec, next_spec,
            row_spec(D_MODEL),
            _layer_spec(layer, (1, MLA_WIDTH)), _layer_spec(layer, (3, CONV_CH)),
            _layer_spec(layer, (1, CONV_CH)),
            pl.BlockSpec((1, D_MODEL), lambda i: (0, 0)),
            pl.BlockSpec((None, D_MODEL, D_MODEL), lambda i: (layer, 0, 0),
                         pipeline_mode=pl.Buffered(1)),
        ],
        out_specs=row_spec(D_MODEL),
        out_shape=jax.ShapeDtypeStruct((t, D_MODEL), jnp.float32),
        compiler_params=pltpu.CompilerParams(
            dimension_semantics=("arbitrary",), vmem_limit_bytes=VMEM_LIMIT),
        name="out_proj",
    )(x2, mod, o_gla, o_mla, cb, cc, cx, cc, cx, cc, cx, z,
      mla_out_g, conv_w, conv_out_g, final_g, w_out_b)


_O_GLR = 2 * GLA_HEADS * GLA_DK + GLA_WIDTH
_O_MQ = _O_GLR + 2 * GLA_GATE_RANK
_O_MKV = _O_MQ + MLA_Q_LORA
_O_MKR = _O_MKV + MLA_KV_LORA
_O_CB = _O_MKR + MLA_ROPE
IN_DIM = _O_CB + 3 * CONV_CH + D_MODEL
W_PREP_ROWS = 256


def _floor_lanes(n):
    return n // LANES * LANES


def _window(ref, lo, hi):
    base = _floor_lanes(lo)
    stop = min(-(-hi // LANES) * LANES, ref.shape[1])
    return ref[:, base:stop][:, lo - base:hi - base]


def _w_in_prep_kernel(w_ref, o_ref):
    rows = w_ref.shape[0]
    misc = jnp.concatenate(
        [_window(w_ref, _O_MKR, _O_CB), _window(w_ref, _O_GLR, _O_MQ),
         jnp.zeros((rows, MISC_W - MLA_ROPE - 2 * GLA_GATE_RANK), jnp.float32)], axis=1)
    pieces = [w_ref[:, 0:_O_GLR], _window(w_ref, _O_MQ, _O_MKV), misc,
              _window(w_ref, _O_MKV, _O_MKR), _window(w_ref, _O_CB, IN_DIM)]
    off = 0
    for piece in pieces:
        width = piece.shape[1]
        o_ref[:, off:off + width] = piece.astype(jnp.bfloat16)
        off += width


def _w_in_prep_call(w_in):
    depth, d, n = w_in.shape
    return pl.pallas_call(
        _w_in_prep_kernel,
        grid=(depth, d // W_PREP_ROWS),
        in_specs=[pl.BlockSpec((None, W_PREP_ROWS, n), lambda l, i: (l, i, 0))],
        out_specs=pl.BlockSpec((None, W_PREP_ROWS, IN_DIM_P), lambda l, i: (l, i, 0)),
        out_shape=jax.ShapeDtypeStruct((depth, d, IN_DIM_P), jnp.bfloat16),
        compiler_params=pltpu.CompilerParams(
            dimension_semantics=("arbitrary", "arbitrary"), vmem_limit_bytes=VMEM_LIMIT),
        name="w_in_prep",
    )(w_in)


def _gate_params(wg_f, bg_f, wg_b, bg_b):
    depth, rank, hk = wg_f.shape
    zeros = jnp.zeros((depth, rank, hk), jnp.float32)
    rows = jnp.concatenate([jnp.concatenate([wg_f, zeros], axis=2),
                            jnp.concatenate([zeros, wg_b], axis=2)], axis=1)
    wg_cat = jnp.pad(rows, ((0, 0), (MLA_ROPE, MISC_W - MLA_ROPE - 2 * rank), (0, 0)))
    bg_cat = jnp.concatenate([bg_f, bg_b], axis=1)[:, None, :]
    return wg_cat.astype(jnp.bfloat16), bg_cat


def _permute_w_uq(w):
    depth = w.shape[0]
    w4 = w.reshape(depth, MLA_Q_LORA, MLA_HEADS, MLA_NOPE + MLA_ROPE)
    w4 = jnp.pad(w4, ((0, 0), (0, 0), (0, 0), (0, QK_W - MLA_NOPE - MLA_ROPE)))
    return w4.reshape(depth, MLA_Q_LORA, MLA_HEADS * QK_W).astype(jnp.bfloat16)


def _split_w_ukv(w):
    depth = w.shape[0]
    w4 = w.reshape(depth, MLA_KV_LORA, MLA_HEADS, MLA_NOPE + MLA_DV)
    wk = w4[..., :MLA_NOPE].reshape(depth, MLA_KV_LORA, MLA_HEADS * MLA_NOPE)
    wv_t = w4[..., MLA_NOPE:].reshape(depth, MLA_KV_LORA, MLA_HEADS * MLA_DV).transpose(0, 2, 1)
    return wk.astype(jnp.bfloat16), wv_t.astype(jnp.bfloat16)


def kernel(x, c, positions, ada_w, ada_b, norm_g, w_in, gla_wg_f, gla_bg_f, gla_wg_b, gla_bg_b,
           gla_norm_g, mla_q_norm_g, mla_kv_norm_g, mla_w_uq, mla_w_ukv, mla_out_g, conv_w,
           conv_out_g, w_out, final_g):
    bsz, s, d = x.shape
    t = bsz * s
    tiles_per_seq = s // ROW_TILE

    c_pad = jnp.pad(c, ((0, HALO_ROWS - bsz), (0, 0)))
    mod = _ada_call(c_pad, ada_w, ada_b)[:, :bsz].reshape(DEPTH, bsz, 3, d)
    tabs = _rope_tab_call(positions)

    w_in_p = _w_in_prep_call(w_in)
    w_out_b = w_out.astype(jnp.bfloat16)
    wg_cat, bg_cat = _gate_params(gla_wg_f, gla_bg_f, gla_wg_b, gla_bg_b)
    wuq_p = _permute_w_uq(mla_w_uq)
    wuk_p, wuvt_p = _split_w_ukv(mla_w_ukv)

    def row(p):
        return p[:, None, :]

    def b3(a):
        return a.reshape(bsz, s, a.shape[-1])

    h = x.reshape(t, d)
    for l in range(DEPTH):
        segs = _inproj_call(l, h, mod, row(norm_g), w_in_p, tiles_per_seq)
        gq, gk, gv, mq, misc, mkv, cb, cc, cx, z = segs
        o_gla = _gla_call(l, b3(gq), b3(gk), b3(gv), b3(misc), wg_cat, bg_cat, row(gla_norm_g))
        o_mla = _mla_call(l, b3(mq), b3(mkv), b3(misc), tabs, row(mla_q_norm_g),
                          row(mla_kv_norm_g), wuq_p, wuk_p, wuvt_p)
        h = _outproj_call(l, h, mod, o_gla.reshape(t, -1), o_mla.reshape(t, -1), cb, cc, cx, z,
                          row(mla_out_g), conv_w, row(conv_out_g), final_g.reshape(1, d),
                          w_out_b, tiles_per_seq, l == DEPTH - 1)
    return h.reshape(bsz, s, d)
```

```python
import functools
import math

import jax
import jax.numpy as jnp
from jax import lax
from jax.experimental import pallas as pl
from jax.experimental.pallas import tpu as pltpu

D_MODEL = 2048
DEPTH = 2
GLA_HEADS = 6
GLA_DK = 64
GLA_DV = 128
GLA_GATE_RANK = 16
GLA_GATE_TEMP = 16.0
GLA_CHUNK = 64
GLA_WIDTH = GLA_HEADS * GLA_DV
MLA_HEADS = 6
MLA_Q_LORA = 384
MLA_KV_LORA = 256
MLA_NOPE = 128
MLA_ROPE = 64
MLA_DV = 128
MLA_WIDTH = MLA_HEADS * MLA_DV
ROPE_THETA = 10000.0
CONV_CH = D_MODEL - GLA_WIDTH - MLA_WIDTH
EPS = 1e-6

LANES = 128
F32_ROWS = 8
BF16_ROWS = 16
HALO_ROWS = BF16_ROWS
MISC_W = LANES
QK_W = 2 * LANES

IN_SEGMENTS = (
    ("gq", GLA_HEADS * GLA_DK, jnp.float32),
    ("gk", GLA_HEADS * GLA_DK, jnp.float32),
    ("gv", GLA_WIDTH, jnp.bfloat16),
    ("mq", MLA_Q_LORA, jnp.float32),
    ("misc", MISC_W, jnp.float32),
    ("mkv", MLA_KV_LORA, jnp.float32),
    ("cb", CONV_CH, jnp.bfloat16),
    ("cc", CONV_CH, jnp.bfloat16),
    ("cx", CONV_CH, jnp.bfloat16),
    ("z", D_MODEL, jnp.bfloat16),
)
IN_DIM_P = sum(w for _, w, _ in IN_SEGMENTS)

IN_ROW_TILE = 256
OUT_ROW_TILE = 512
Q_TILE = 512
KV_ROWS = 512
SCAN_ROWS = 256
GLA_GROUP = 8
VMEM_LIMIT = 56 * 1024 * 1024

_NT = (((1,), (1,)), ((), ()))
_TN = (((0,), (0,)), ((), ()))


def _rms(x, g):
    ms = jnp.mean(x * x, axis=-1, keepdims=True)
    return x * lax.rsqrt(ms + EPS) * g


def _bdot(a, b):
    return jnp.dot(a, b, preferred_element_type=jnp.float32)


def _ada_kernel(c_ref, w_ref, b_ref, o_ref):
    c = c_ref[...]
    c_act = c * jax.nn.sigmoid(c)
    o_ref[...] = jnp.dot(c_act, w_ref[...], precision=lax.Precision.HIGHEST,
                         preferred_element_type=jnp.float32) + b_ref[...]


def _ada_call(c_pad, ada_w, ada_b):
    rows = c_pad.shape[0]
    n = ada_w.shape[-1]
    tn = 1024
    return pl.pallas_call(
        _ada_kernel,
        grid=(DEPTH, n // tn),
        in_specs=[
            pl.BlockSpec((rows, D_MODEL), lambda l, j: (0, 0)),
            pl.BlockSpec((None, D_MODEL, tn), lambda l, j: (l, 0, j)),
            pl.BlockSpec((None, 1, tn), lambda l, j: (l, 0, j)),
        ],
        out_specs=pl.BlockSpec((None, rows, tn), lambda l, j: (l, 0, j)),
        out_shape=jax.ShapeDtypeStruct((DEPTH, rows, n), jnp.float32),
        compiler_params=pltpu.CompilerParams(
            dimension_semantics=("arbitrary", "arbitrary"), vmem_limit_bytes=VMEM_LIMIT),
        name="ada_mod",
    )(c_pad, ada_w, ada_b.reshape(DEPTH, 1, n))


def _rope_tab_kernel(pos_ref, invf_ref, c_ref, sa_ref, sb_ref):
    ang = pos_ref[...].astype(jnp.float32) * invf_ref[...]
    cos, sin = jnp.cos(ang), jnp.sin(ang)
    lane = lax.broadcasted_iota(jnp.int32, ang.shape, 1)
    half = MLA_ROPE // 2
    c_ref[...] = jnp.where(lane < MLA_ROPE, cos, 0.0)
    sa_ref[...] = jnp.where(lane < half, -sin, 0.0)
    sb_ref[...] = jnp.where((lane >= half) & (lane < MLA_ROPE), sin, 0.0)


def _rope_tab_call(positions):
    bsz, s = positions.shape
    half = MLA_ROPE // 2
    inv_freq = ROPE_THETA ** (-jnp.arange(0, MLA_ROPE, 2, dtype=jnp.float32) / MLA_ROPE)
    invf = jnp.concatenate([inv_freq, inv_freq, jnp.zeros((LANES - 2 * half,), jnp.float32)])
    tab = jax.ShapeDtypeStruct((bsz, s, LANES), jnp.float32)
    spec = pl.BlockSpec((None, s, LANES), lambda b: (b, 0, 0))
    return pl.pallas_call(
        _rope_tab_kernel,
        grid=(bsz,),
        in_specs=[pl.BlockSpec((None, s, 1), lambda b: (b, 0, 0)),
                  pl.BlockSpec((1, LANES), lambda b: (0, 0))],
        out_specs=[spec, spec, spec],
        out_shape=[tab, tab, tab],
        compiler_params=pltpu.CompilerParams(dimension_semantics=("arbitrary",)),
        name="rope_tables",
    )(positions.reshape(bsz, s, 1), invf.reshape(1, LANES))


def _rope(x, c, sa, sb):
    return (x * c + pltpu.roll(x, LANES - MLA_ROPE // 2, axis=1) * sa
            + pltpu.roll(x, MLA_ROPE // 2, axis=1) * sb)


def _inproj_kernel(x_ref, mod_ref, g_ref, w_ref, *out_refs):
    x = x_ref[...]
    mod = mod_ref[...]
    shift, scale = mod[0:1], mod[1:2]
    ms = jnp.mean(x * x, axis=-1, keepdims=True)
    h = (x * lax.rsqrt(ms + EPS) * g_ref[...]) * (1.0 + scale) + shift
    hb = h.astype(jnp.bfloat16)
    off = 0
    for (_, width, _), o_ref in zip(IN_SEGMENTS, out_refs):
        o_ref[...] = _bdot(hb, w_ref[:, off:off + width]).astype(o_ref.dtype)
        off += width


def _layer_spec(layer, shape):
    return pl.BlockSpec((None,) + shape, lambda *_: (layer,) + (0,) * len(shape))


def _inproj_call(layer, x2, mod, norm_g, w_in_p, seq_len):
    t = x2.shape[0]
    tm = IN_ROW_TILE
    tiles_per_seq = seq_len // tm
    out_shape = [jax.ShapeDtypeStruct((t, w), dt) for _, w, dt in IN_SEGMENTS]
    out_specs = [pl.BlockSpec((tm, w), lambda i: (i, 0)) for _, w, _ in IN_SEGMENTS]
    return pl.pallas_call(
        _inproj_kernel,
        grid=(t // tm,),
        in_specs=[
            pl.BlockSpec((tm, D_MODEL), lambda i: (i, 0)),
            pl.BlockSpec((None, None, 3, D_MODEL), lambda i: (layer, i // tiles_per_seq, 0, 0)),
            _layer_spec(layer, (1, D_MODEL)),
            pl.BlockSpec((None, D_MODEL, IN_DIM_P), lambda i: (layer, 0, 0),
                         pipeline_mode=pl.Buffered(1)),
        ],
        out_specs=out_specs,
        out_shape=out_shape,
        compiler_params=pltpu.CompilerParams(
            dimension_semantics=("arbitrary",), vmem_limit_bytes=VMEM_LIMIT),
        name="in_proj",
    )(x2, mod, norm_g, w_in_p)


def _log_sigmoid(x):
    return jnp.minimum(x, 0.0) - jnp.log(1.0 + jnp.exp(-jnp.abs(x)))


def _gla_kernel(q_ref, k_ref, v_ref, misc_ref, wgf_ref, bgf_ref, wgb_ref, bgb_ref, g_ref,
                o_ref, bf_s, bb_s, qcat_s, kv_s, dec_s, st_s, intra_s):
    s = q_ref.shape[0]
    cdim = GLA_CHUNK
    n_chunks = s // cdim
    pair_dk = 2 * GLA_DK
    pair_dv = 2 * GLA_DV
    inv_t = 1.0 / GLA_GATE_TEMP
    q_scale = GLA_DK ** -0.5

    pos = lax.broadcasted_iota(jnp.int32, (SCAN_ROWS, pair_dk), 0) & (cdim - 1)

    def scan_body(i, carry):
        r = pl.ds(pl.multiple_of(i * SCAN_ROWS, SCAN_ROWS), SCAN_ROWS)
        m = misc_ref[r, :].astype(jnp.bfloat16)
        bf = _log_sigmoid(_bdot(m, wgf_ref[...]) + bgf_ref[...]) * inv_t
        bb = _log_sigmoid(_bdot(m, wgb_ref[...]) + bgb_ref[...]) * inv_t
        step = 1
        while step < cdim:
            bf = bf + jnp.where(pos >= step, pltpu.roll(bf, step, axis=0), 0.0)
            bb = bb + jnp.where(pos < cdim - step, pltpu.roll(bb, SCAN_ROWS - step, axis=0), 0.0)
            step *= 2
        bf_s[r, :] = bf
        bb_s[r, :] = bb
        return carry

    lax.fori_loop(0, s // SCAN_ROWS, scan_body, 0)

    row = lax.broadcasted_iota(jnp.int32, (cdim, pair_dk), 0)
    key_pos = lax.broadcasted_iota(jnp.int32, (cdim, pair_dk), 1) & (cdim - 1)
    keep_f = key_pos <= row
    keep_b = key_pos > row
    cat_block = lax.broadcasted_iota(jnp.int32, (cdim, 2 * pair_dk), 1) // GLA_DK
    v_head = lax.broadcasted_iota(jnp.int32, (cdim, pair_dv), 1) // GLA_DV

    def intra_body(i, carry):
        chunks = [i * GLA_GROUP + c for c in range(GLA_GROUP)]
        rows = [pl.ds(pl.multiple_of(n * cdim, cdim), cdim) for n in chunks]
        scores, k_ends = [], []
        for n, r in zip(chunks, rows):
            bf = bf_s[r, :]
            bb = bb_s[r, :]
            bf_last = bf[cdim - 1:cdim, :]
            bb_last = bb[0:1, :]
            q = q_ref[r, :] * q_scale
            k = k_ref[r, :]
            q_cat = jnp.concatenate([q * jnp.exp(bf), q * jnp.exp(bb)],
                                    axis=1).astype(jnp.bfloat16)
            k_inv = jnp.concatenate([k * jnp.exp(-bf), k * jnp.exp(-bb)], axis=1)
            k_ends.append(jnp.concatenate(
                [k * jnp.exp(bf_last - bf), k * jnp.exp(bb_last - bb)], axis=1).astype(jnp.bfloat16))
            dec_s[n] = jnp.concatenate([jnp.exp(bf_last), jnp.exp(bb_last)], axis=1)
            qcat_s[r, :] = q_cat
            k_bd = jnp.concatenate([jnp.where(cat_block == c, k_inv, 0.0) for c in range(4)],
                                   axis=0).astype(jnp.bfloat16)
            scores.append(lax.dot_general(q_cat, k_bd, _NT,
                                          preferred_element_type=jnp.float32))
        for n, r, k_end in zip(chunks, rows, k_ends):
            kv_s[n] = lax.dot_general(v_ref[r, :], k_end, _TN,
                                      preferred_element_type=jnp.float32)
        for r, sc in zip(rows, scores):
            p = (jnp.where(keep_f, sc[:, 0:pair_dk], 0.0)
                 + jnp.where(keep_b, sc[:, pair_dk:], 0.0)).astype(jnp.bfloat16)
            vv = v_ref[r, :]
            v_bd = jnp.concatenate(
                [jnp.where(v_head == j, vv, jnp.zeros_like(vv)) for j in range(2)], axis=0)
            intra_s[r, :] = _bdot(p, v_bd)
        return carry

    lax.fori_loop(0, n_chunks // GLA_GROUP, intra_body, 0)

    same_head = (lax.broadcasted_iota(jnp.int32, (pair_dv, pair_dk), 0) // GLA_DV
                 == lax.broadcasted_iota(jnp.int32, (pair_dv, pair_dk), 1) // GLA_DK)

    def state_body(i, states):
        sf, sb = states
        nf = i
        nb = n_chunks - 1 - i
        st_s[nf, :, 0:pair_dk] = jnp.where(same_head, sf, 0.0).astype(jnp.bfloat16)
        st_s[nb, :, pair_dk:] = jnp.where(same_head, sb, 0.0).astype(jnp.bfloat16)
        sf = sf * dec_s[nf][:, 0:pair_dk] + kv_s[nf, :, 0:pair_dk]
        sb = sb * dec_s[nb][:, pair_dk:] + kv_s[nb, :, pair_dk:]
        return sf, sb

    zero = jnp.zeros((pair_dv, pair_dk), jnp.float32)
    lax.fori_loop(0, n_chunks, state_body, (zero, zero))

    g = g_ref[...]

    def out_body(i, carry):
        chunks = [i * GLA_GROUP + c for c in range(GLA_GROUP)]
        rows = [pl.ds(pl.multiple_of(n * cdim, cdim), cdim) for n in chunks]
        inter = [lax.dot_general(qcat_s[r, :], st_s[n], _NT, preferred_element_type=jnp.float32)
                 for n, r in zip(chunks, rows)]
        for r, o_inter in zip(rows, inter):
            o = intra_s[r, :] + o_inter
            for j in range(2):
                o_ref[r, j * GLA_DV:(j + 1) * GLA_DV] = _rms(
                    o[:, j * GLA_DV:(j + 1) * GLA_DV], g).astype(o_ref.dtype)
        return carry

    lax.fori_loop(0, n_chunks // GLA_GROUP, out_body, 0)


def _gla_call(layer, gq, gk, gv, misc, wg_cat, bg_cat, gla_norm_g):
    bsz, s, _ = gq.shape
    pairs = GLA_HEADS // 2
    n_chunks = s // GLA_CHUNK
    pair_dk, pair_dv = 2 * GLA_DK, 2 * GLA_DV
    qk_spec = pl.BlockSpec((None, s, pair_dk), lambda b, p: (b, 0, p))
    v_spec = pl.BlockSpec((None, s, pair_dv), lambda b, p: (b, 0, p))

    def gate_specs(direction):
        return (pl.BlockSpec((None, MISC_W, pair_dk), lambda b, p: (layer, 0, direction * pairs + p)),
                pl.BlockSpec((None, 1, pair_dk), lambda b, p: (layer, 0, direction * pairs + p)))

    return pl.pallas_call(
        _gla_kernel,
        grid=(bsz, pairs),
        in_specs=[qk_spec, qk_spec, v_spec,
                  pl.BlockSpec((None, s, MISC_W), lambda b, p: (b, 0, 0)),
                  *gate_specs(0), *gate_specs(1),
                  _layer_spec(layer, (1, GLA_DV))],
        out_specs=v_spec,
        out_shape=jax.ShapeDtypeStruct((bsz, s, GLA_WIDTH), jnp.bfloat16),
        scratch_shapes=[
            pltpu.VMEM((s, pair_dk), jnp.float32),
            pltpu.VMEM((s, pair_dk), jnp.float32),
            pltpu.VMEM((s, 2 * pair_dk), jnp.bfloat16),
            pltpu.VMEM((n_chunks, pair_dv, 2 * pair_dk), jnp.float32),
            pltpu.VMEM((n_chunks, 1, 2 * pair_dk), jnp.float32),
            pltpu.VMEM((n_chunks, pair_dv, 2 * pair_dk), jnp.bfloat16),
            pltpu.VMEM((s, pair_dv), jnp.float32),
        ],
        compiler_params=pltpu.CompilerParams(
            dimension_semantics=("arbitrary", "arbitrary"), vmem_limit_bytes=VMEM_LIMIT),
        name="gla",
    )(gq, gk, gv, misc, wg_cat, bg_cat, wg_cat, bg_cat, gla_norm_g)


def _mla_kernel(mq_ref, mkv_ref, misc_ref, c_ref, sa_ref, sb_ref, gq_ref, gkv_ref,
                wuq_ref, wuk_ref, wuvt_ref, o_ref, ckv_s, q_s, k_s, vt_s):
    s = mkv_ref.shape[0]
    head = pl.program_id(1)
    blocks = [slice(i * KV_ROWS, (i + 1) * KV_ROWS) for i in range(s // KV_ROWS)]
    q_mult = (MLA_NOPE + MLA_ROPE) ** -0.5 * math.log2(math.e)

    @pl.when(head == 0)
    def _per_batch_row():
        for r in blocks:
            tabs = (c_ref[r, :], sa_ref[r, :], sb_ref[r, :])
            ckv_s[r, :] = _rms(mkv_ref[r, :], gkv_ref[...]).astype(jnp.bfloat16)
            k_s[r, MLA_NOPE:] = _rope(misc_ref[r, :], *tabs).astype(jnp.bfloat16)
            cq = _rms(mq_ref[r, :], gq_ref[...]).astype(jnp.bfloat16)
            qf = _bdot(cq, wuq_ref[...])
            for h in range(MLA_HEADS):
                qh = qf[:, h * QK_W:(h + 1) * QK_W]
                q_rope = _rope(qh[:, MLA_NOPE:], *tabs)
                q_s[h, r, :] = (jnp.concatenate([qh[:, 0:MLA_NOPE], q_rope], axis=1)
                                * q_mult).astype(jnp.bfloat16)
        vt_s[MLA_DV:, :] = jnp.ones((vt_s.shape[0] - MLA_DV, s), jnp.bfloat16)

    for r in blocks:
        ckv = ckv_s[r, :]
        k_s[r, 0:MLA_NOPE] = _bdot(ckv, wuk_ref[...]).astype(jnp.bfloat16)
        vt_s[0:MLA_DV, r] = lax.dot_general(
            wuvt_ref[...], ckv, _NT, preferred_element_type=jnp.float32).astype(jnp.bfloat16)

    def scores_t(j):
        q = q_s[head, j * Q_TILE:(j + 1) * Q_TILE, :]
        return lax.dot_general(k_s[...], q, _NT, preferred_element_type=jnp.float32)

    def finish(j, sc_t):
        p_t = jnp.exp2(sc_t - jnp.max(sc_t, axis=0, keepdims=True)).astype(jnp.bfloat16)
        o_t = _bdot(vt_s[...], p_t)
        o_ref[j * Q_TILE:(j + 1) * Q_TILE, :] = (
            o_t[0:MLA_DV, :] / o_t[MLA_DV:MLA_DV + 1, :]).T.astype(o_ref.dtype)

    n_tiles = s // Q_TILE
    pending = scores_t(0)
    for j in range(n_tiles):
        upcoming = scores_t(j + 1) if j + 1 < n_tiles else None
        finish(j, pending)
        pending = upcoming


def _mla_call(layer, mq, mkv, misc, tabs, gq, gkv, wuq_p, wuk_p, wuvt_p):
    bsz, s, _ = mq.shape

    def per_row(width):
        return pl.BlockSpec((None, s, width), lambda b, h: (b, 0, 0))

    return pl.pallas_call(
        _mla_kernel,
        grid=(bsz, MLA_HEADS),
        in_specs=[
            per_row(MLA_Q_LORA), per_row(MLA_KV_LORA), per_row(MISC_W),
            per_row(LANES), per_row(LANES), per_row(LANES),
            _layer_spec(layer, (1, MLA_Q_LORA)),
            _layer_spec(layer, (1, MLA_KV_LORA)),
            _layer_spec(layer, (MLA_Q_LORA, MLA_HEADS * QK_W)),
            pl.BlockSpec((None, MLA_KV_LORA, MLA_NOPE), lambda b, h: (layer, 0, h)),
            pl.BlockSpec((None, MLA_DV, MLA_KV_LORA), lambda b, h: (layer, h, 0)),
        ],
        out_specs=pl.BlockSpec((None, s, MLA_DV), lambda b, h: (b, 0, h)),
        out_shape=jax.ShapeDtypeStruct((bsz, s, MLA_WIDTH), jnp.bfloat16),
        scratch_shapes=[
            pltpu.VMEM((s, MLA_KV_LORA), jnp.bfloat16),
            pltpu.VMEM((MLA_HEADS, s, QK_W), jnp.bfloat16),
            pltpu.VMEM((s, QK_W), jnp.bfloat16),
            pltpu.VMEM((MLA_DV + BF16_ROWS, s), jnp.bfloat16),
        ],
        compiler_params=pltpu.CompilerParams(
            dimension_semantics=("arbitrary", "arbitrary"),
            vmem_limit_bytes=VMEM_LIMIT),
        name="mla",
    )(mq, mkv, misc, *tabs, gq, gkv, wuq_p, wuk_p, wuvt_p)


def _outproj_kernel(tiles_per_seq, final, x_ref, mod_ref, ogla_ref, omla_ref, cb_ref, cc_ref,
                    cx_ref, ccp_ref, cxp_ref, ccn_ref, cxn_ref, z_ref, mg_ref, cw_ref, cg_ref,
                    fg_ref, w_ref, o_ref):
    f32 = jnp.float32
    j = pl.program_id(0) % tiles_per_seq
    tm = x_ref.shape[0]
    mla = _rms(omla_ref[...].astype(f32), mg_ref[...])

    u = cc_ref[...].astype(f32) * cx_ref[...].astype(f32)
    prev_ok = (j > 0).astype(f32)
    next_ok = (j < tiles_per_seq - 1).astype(f32)
    u_prev = (ccp_ref[...].astype(f32) * cxp_ref[...].astype(f32))[HALO_ROWS - 1:, :] * prev_ok
    u_next = (ccn_ref[...].astype(f32) * cxn_ref[...].astype(f32))[0:1, :] * next_ok
    rows = lax.broadcasted_iota(jnp.int32, u.shape, 0)
    up = jnp.where(rows == 0, u_prev, pltpu.roll(u, 1, axis=0))
    un = jnp.where(rows == tm - 1, u_next, pltpu.roll(u, tm - 1, axis=0))
    cw = cw_ref[...]
    conv = up * cw[0:1] + u * cw[1:2] + un * cw[2:3]
    oc = _rms(cb_ref[...].astype(f32) * conv, cg_ref[...])

    z = z_ref[...].astype(f32)
    y = jnp.concatenate([ogla_ref[...].astype(f32), mla, oc], axis=1) * (z * jax.nn.sigmoid(z))
    acc = _bdot(y.astype(jnp.bfloat16), w_ref[...])
    gate = mod_ref[...][2:3]
    out = x_ref[...] + gate * acc
    if final:
        out = _rms(out, fg_ref[...])
    o_ref[...] = out


def _outproj_call(layer, x2, mod, o_gla, o_mla, cb, cc, cx, z, mla_out_g, conv_w, conv_out_g,
                  final_g, w_out_b, seq_len, final):
    t = x2.shape[0]
    tm = OUT_ROW_TILE
    tiles_per_seq = seq_len // tm
    halo_per_tile = tm // HALO_ROWS
    n_halo = t // HALO_ROWS

    def row_spec(w):
        return pl.BlockSpec((tm, w), lambda i: (i, 0))

    prev_spec = pl.BlockSpec((HALO_ROWS, CONV_CH),
                             lambda i: (jnp.maximum(i * halo_per_tile - 1, 0), 0))
    next_spec = pl.BlockSpec((HALO_ROWS, CONV_CH),
                             lambda i: (jnp.minimum((i + 1) * halo_per_tile, n_halo - 1), 0))

    return pl.pallas_call(
        functools.partial(_outproj_kernel, tiles_per_seq, final),
        grid=(t // tm,),
        in_specs=[
            row_spec(D_MODEL),
            pl.BlockSpec((None, None, 3, D_MODEL), lambda i: (layer, i // tiles_per_seq, 0, 0)),
            row_spec(GLA_WIDTH), row_spec(MLA_WIDTH),
            row_spec(CONV_CH), row_spec(CONV_CH), row_spec(CONV_CH),
            prev_spec, prev_spec, next_spec, next_spec,
            row_spec(D_MODEL),
            _layer_spec(layer, (1, MLA_WIDTH)), _layer_spec(layer, (3, CONV_CH)),
            _layer_spec(layer, (1, CONV_CH)),
            pl.BlockSpec((1, D_MODEL), lambda i: (0, 0)),
            pl.BlockSpec((None, D_MODEL, D_MODEL), lambda i: (layer, 0, 0),
                         pipeline_mode=pl.Buffered(1)),
        ],
        out_specs=row_spec(D_MODEL),
        out_shape=jax.ShapeDtypeStruct((t, D_MODEL), jnp.float32),
        compiler_params=pltpu.CompilerParams(
            dimension_semantics=("arbitrary",), vmem_limit_bytes=VMEM_LIMIT),
        name="out_proj",
    )(x2, mod, o_gla, o_mla, cb, cc, cx, cc, cx, cc, cx, z,
      mla_out_g, conv_w, conv_out_g, final_g, w_out_b)


_O_GLR = 2 * GLA_HEADS * GLA_DK + GLA_WIDTH
_O_MQ = _O_GLR + 2 * GLA_GATE_RANK
_O_MKV = _O_MQ + MLA_Q_LORA
_O_MKR = _O_MKV + MLA_KV_LORA
_O_CB = _O_MKR + MLA_ROPE
IN_DIM = _O_CB + 3 * CONV_CH + D_MODEL
W_PREP_ROWS = 256


W_PREP_CHUNK = 512


def _w_in_prep_kernel(wt_ref, o_ref):
    rows = o_ref.shape[0]
    eye = (lax.broadcasted_iota(jnp.int32, (rows, rows), 0)
           == lax.broadcasted_iota(jnp.int32, (rows, rows), 1)).astype(jnp.bfloat16)

    def put(dst, block):
        o_ref[:, dst:dst + block.shape[0]] = lax.dot_general(
            eye, block.astype(jnp.bfloat16), _NT,
            preferred_element_type=jnp.float32).astype(jnp.bfloat16)

    def put_range(dst, lo, hi):
        for start in range(lo, hi, W_PREP_CHUNK):
            stop = min(start + W_PREP_CHUNK, hi)
            put(dst + start - lo, wt_ref[start:stop, :])

    pad_rows = MISC_W - MLA_ROPE - 2 * GLA_GATE_RANK
    misc = jnp.concatenate([wt_ref[_O_MKR:_O_CB, :], wt_ref[_O_GLR:_O_MQ, :],
                            jnp.zeros((pad_rows, rows), jnp.float32)], axis=0)
    off = 0
    for src in ((0, _O_GLR), (_O_MQ, _O_MKV), misc, (_O_MKV, _O_MKR), (_O_CB, IN_DIM)):
        if isinstance(src, tuple):
            put_range(off, *src)
            off += src[1] - src[0]
        else:
            put(off, src)
            off += src.shape[0]


def _w_in_prep_call(w_in):
    depth, d, n = w_in.shape
    return pl.pallas_call(
        _w_in_prep_kernel,
        grid=(depth, d // W_PREP_ROWS),
        in_specs=[pl.BlockSpec((None, n, W_PREP_ROWS), lambda l, i: (l, 0, i))],
        out_specs=pl.BlockSpec((None, W_PREP_ROWS, IN_DIM_P), lambda l, i: (l, i, 0)),
        out_shape=jax.ShapeDtypeStruct((depth, d, IN_DIM_P), jnp.bfloat16),
        compiler_params=pltpu.CompilerParams(
            dimension_semantics=("arbitrary", "arbitrary"), vmem_limit_bytes=VMEM_LIMIT),
        name="w_in_prep",
    )(jnp.swapaxes(w_in, 1, 2))


def _gate_params(wg_f, bg_f, wg_b, bg_b):
    depth, rank, hk = wg_f.shape
    zeros = jnp.zeros((depth, rank, hk), jnp.float32)
    rows = jnp.concatenate([jnp.concatenate([wg_f, zeros], axis=2),
                            jnp.concatenate([zeros, wg_b], axis=2)], axis=1)
    wg_cat = jnp.pad(rows, ((0, 0), (MLA_ROPE, MISC_W - MLA_ROPE - 2 * rank), (0, 0)))
    bg_cat = jnp.concatenate([bg_f, bg_b], axis=1)[:, None, :]
    return wg_cat.astype(jnp.bfloat16), bg_cat


def _permute_w_uq(w):
    depth = w.shape[0]
    w4 = w.reshape(depth, MLA_Q_LORA, MLA_HEADS, MLA_NOPE + MLA_ROPE)
    w4 = jnp.pad(w4, ((0, 0), (0, 0), (0, 0), (0, QK_W - MLA_NOPE - MLA_ROPE)))
    return w4.reshape(depth, MLA_Q_LORA, MLA_HEADS * QK_W).astype(jnp.bfloat16)


def _split_w_ukv(w):
    depth = w.shape[0]
    w4 = w.reshape(depth, MLA_KV_LORA, MLA_HEADS, MLA_NOPE + MLA_DV)
    wk = w4[..., :MLA_NOPE].reshape(depth, MLA_KV_LORA, MLA_HEADS * MLA_NOPE)
    wv_t = w4[..., MLA_NOPE:].reshape(depth, MLA_KV_LORA, MLA_HEADS * MLA_DV).transpose(0, 2, 1)
    return wk.astype(jnp.bfloat16), wv_t.astype(jnp.bfloat16)


def kernel(x, c, positions, ada_w, ada_b, norm_g, w_in, gla_wg_f, gla_bg_f, gla_wg_b, gla_bg_b,
           gla_norm_g, mla_q_norm_g, mla_kv_norm_g, mla_w_uq, mla_w_ukv, mla_out_g, conv_w,
           conv_out_g, w_out, final_g):
    bsz, s, d = x.shape
    t = bsz * s

    c_pad = jnp.pad(c, ((0, F32_ROWS - bsz), (0, 0)))
    mod = _ada_call(c_pad, ada_w, ada_b)[:, :bsz].reshape(DEPTH, bsz, 3, d)
    tabs = _rope_tab_call(positions)

    w_in_p = _w_in_prep_call(w_in)
    w_out_b = w_out.astype(jnp.bfloat16)
    wg_cat, bg_cat = _gate_params(gla_wg_f, gla_bg_f, gla_wg_b, gla_bg_b)
    wuq_p = _permute_w_uq(mla_w_uq)
    wuk_p, wuvt_p = _split_w_ukv(mla_w_ukv)

    def row(p):
        return p[:, None, :]

    def b3(a):
        return a.reshape(bsz, s, a.shape[-1])

    h = x.reshape(t, d)
    for l in range(DEPTH):
        segs = _inproj_call(l, h, mod, row(norm_g), w_in_p, s)
        gq, gk, gv, mq, misc, mkv, cb, cc, cx, z = segs
        o_gla = _gla_call(l, b3(gq), b3(gk), b3(gv), b3(misc), wg_cat, bg_cat, row(gla_norm_g))
        o_mla = _mla_call(l, b3(mq), b3(mkv), b3(misc), tabs, row(mla_q_norm_g),
                          row(mla_kv_norm_g), wuq_p, wuk_p, wuvt_p)
        h = _outproj_call(l, h, mod, o_gla.reshape(t, -1), o_mla.reshape(t, -1), cb, cc, cx, z,
                          row(mla_out_g), conv_w, row(conv_out_g), final_g.reshape(1, d),
                          w_out_b, s, l == DEPTH - 1)
    return h.reshape(bsz, s, d)
```

```python
import functools
import math

import jax
import jax.numpy as jnp
from jax import lax
from jax.experimental import pallas as pl
from jax.experimental.pallas import tpu as pltpu

D_MODEL = 2048
DEPTH = 2
GLA_HEADS = 6
GLA_DK = 64
GLA_DV = 128
GLA_GATE_RANK = 16
GLA_GATE_TEMP = 16.0
GLA_CHUNK = 64
GLA_WIDTH = GLA_HEADS * GLA_DV
MLA_HEADS = 6
MLA_Q_LORA = 384
MLA_KV_LORA = 256
MLA_NOPE = 128
MLA_ROPE = 64
MLA_DV = 128
MLA_WIDTH = MLA_HEADS * MLA_DV
ROPE_THETA = 10000.0
CONV_CH = D_MODEL - GLA_WIDTH - MLA_WIDTH
EPS = 1e-6

LANES = 128
F32_ROWS = 8
BF16_ROWS = 16
HALO_ROWS = BF16_ROWS
MISC_W = LANES
QK_W = 2 * LANES

IN_SEGMENTS = (
    ("gq", GLA_HEADS * GLA_DK, jnp.float32),
    ("gk", GLA_HEADS * GLA_DK, jnp.float32),
    ("gv", GLA_WIDTH, jnp.bfloat16),
    ("mq", MLA_Q_LORA, jnp.float32),
    ("misc", MISC_W, jnp.float32),
    ("mkv", MLA_KV_LORA, jnp.float32),
    ("cb", CONV_CH, jnp.bfloat16),
    ("cc", CONV_CH, jnp.bfloat16),
    ("cx", CONV_CH, jnp.bfloat16),
    ("z", D_MODEL, jnp.bfloat16),
)
IN_DIM_P = sum(w for _, w, _ in IN_SEGMENTS)

IN_ROW_TILE = 256
OUT_ROW_TILE = 512
OUT_SUB_TILE = 256
Q_TILE = 512
KV_ROWS = 512
SCAN_ROWS = 256
SCAN_GROUP = 4
GLA_GROUP = 8
VMEM_LIMIT = 56 * 1024 * 1024

_NT = (((1,), (1,)), ((), ()))
_TN = (((0,), (0,)), ((), ()))


def _rms(x, g):
    ms = jnp.mean(x * x, axis=-1, keepdims=True)
    return x * lax.rsqrt(ms + EPS) * g


def _bdot(a, b):
    return jnp.dot(a, b, preferred_element_type=jnp.float32)


def _ada_kernel(c_ref, w_ref, b_ref, o_ref):
    c = c_ref[...]
    c_act = c * jax.nn.sigmoid(c)
    o_ref[...] = jnp.dot(c_act, w_ref[...], precision=lax.Precision.HIGHEST,
                         preferred_element_type=jnp.float32) + b_ref[...]


def _ada_call(c_pad, ada_w, ada_b):
    rows = c_pad.shape[0]
    n = ada_w.shape[-1]
    tn = 1024
    return pl.pallas_call(
        _ada_kernel,
        grid=(DEPTH, n // tn),
        in_specs=[
            pl.BlockSpec((rows, D_MODEL), lambda l, j: (0, 0)),
            pl.BlockSpec((None, D_MODEL, tn), lambda l, j: (l, 0, j)),
            pl.BlockSpec((None, 1, tn), lambda l, j: (l, 0, j)),
        ],
        out_specs=pl.BlockSpec((None, rows, tn), lambda l, j: (l, 0, j)),
        out_shape=jax.ShapeDtypeStruct((DEPTH, rows, n), jnp.float32),
        compiler_params=pltpu.CompilerParams(
            dimension_semantics=("arbitrary", "arbitrary"), vmem_limit_bytes=VMEM_LIMIT),
        name="ada_mod",
    )(c_pad, ada_w, ada_b.reshape(DEPTH, 1, n))


def _rope_tab_kernel(pos_ref, invf_ref, c_ref, sa_ref, sb_ref):
    ang = pos_ref[...].astype(jnp.float32) * invf_ref[...]
    cos, sin = jnp.cos(ang), jnp.sin(ang)
    lane = lax.broadcasted_iota(jnp.int32, ang.shape, 1)
    half = MLA_ROPE // 2
    c_ref[...] = jnp.where(lane < MLA_ROPE, cos, 0.0)
    sa_ref[...] = jnp.where(lane < half, -sin, 0.0)
    sb_ref[...] = jnp.where((lane >= half) & (lane < MLA_ROPE), sin, 0.0)


def _rope_tab_call(positions):
    bsz, s = positions.shape
    half = MLA_ROPE // 2
    inv_freq = ROPE_THETA ** (-jnp.arange(0, MLA_ROPE, 2, dtype=jnp.float32) / MLA_ROPE)
    invf = jnp.concatenate([inv_freq, inv_freq, jnp.zeros((LANES - 2 * half,), jnp.float32)])
    tab = jax.ShapeDtypeStruct((bsz, s, LANES), jnp.float32)
    spec = pl.BlockSpec((None, s, LANES), lambda b: (b, 0, 0))
    return pl.pallas_call(
        _rope_tab_kernel,
        grid=(bsz,),
        in_specs=[pl.BlockSpec((None, s, 1), lambda b: (b, 0, 0)),
                  pl.BlockSpec((1, LANES), lambda b: (0, 0))],
        out_specs=[spec, spec, spec],
        out_shape=[tab, tab, tab],
        compiler_params=pltpu.CompilerParams(dimension_semantics=("arbitrary",)),
        name="rope_tables",
    )(positions.reshape(bsz, s, 1), invf.reshape(1, LANES))


def _rope(x, c, sa, sb):
    return (x * c + pltpu.roll(x, LANES - MLA_ROPE // 2, axis=1) * sa
            + pltpu.roll(x, MLA_ROPE // 2, axis=1) * sb)


def _inproj_kernel(x_ref, mod_ref, g_ref, w_ref, *out_refs):
    x = x_ref[...]
    mod = mod_ref[...]
    shift, scale = mod[0:1], mod[1:2]
    ms = jnp.mean(x * x, axis=-1, keepdims=True)
    h = (x * lax.rsqrt(ms + EPS) * g_ref[...]) * (1.0 + scale) + shift
    hb = h.astype(jnp.bfloat16)
    off = 0
    for (_, width, _), o_ref in zip(IN_SEGMENTS, out_refs):
        o_ref[...] = _bdot(hb, w_ref[:, off:off + width]).astype(o_ref.dtype)
        off += width


def _layer_spec(layer, shape):
    return pl.BlockSpec((None,) + shape, lambda *_: (layer,) + (0,) * len(shape))


def _inproj_call(layer, x2, mod, norm_g, w_in_p, seq_len):
    t = x2.shape[0]
    tm = IN_ROW_TILE
    tiles_per_seq = seq_len // tm
    out_shape = [jax.ShapeDtypeStruct((t, w), dt) for _, w, dt in IN_SEGMENTS]
    out_specs = [pl.BlockSpec((tm, w), lambda i: (i, 0)) for _, w, _ in IN_SEGMENTS]
    return pl.pallas_call(
        _inproj_kernel,
        grid=(t // tm,),
        in_specs=[
            pl.BlockSpec((tm, D_MODEL), lambda i: (i, 0)),
            pl.BlockSpec((None, None, 3, D_MODEL), lambda i: (layer, i // tiles_per_seq, 0, 0)),
            _layer_spec(layer, (1, D_MODEL)),
            pl.BlockSpec((None, D_MODEL, IN_DIM_P), lambda i: (layer, 0, 0),
                         pipeline_mode=pl.Buffered(1)),
        ],
        out_specs=out_specs,
        out_shape=out_shape,
        compiler_params=pltpu.CompilerParams(
            dimension_semantics=("arbitrary",), vmem_limit_bytes=VMEM_LIMIT),
        name="in_proj",
    )(x2, mod, norm_g, w_in_p)


def _log_sigmoid(x):
    return jnp.minimum(x, 0.0) - jnp.log(1.0 + jnp.exp(-jnp.abs(x)))


def _gla_kernel(q_ref, k_ref, v_ref, misc_ref, wgf_ref, bgf_ref, wgb_ref, bgb_ref, g_ref,
                o_ref, bf_s, bb_s, qcat_s, kv_s, dec_s, st_s, intra_s):
    s = q_ref.shape[0]
    cdim = GLA_CHUNK
    n_chunks = s // cdim
    pair_dk = 2 * GLA_DK
    pair_dv = 2 * GLA_DV
    inv_t = 1.0 / GLA_GATE_TEMP
    q_scale = GLA_DK ** -0.5

    t_row = lax.broadcasted_iota(jnp.int32, (SCAN_ROWS, SCAN_ROWS), 0)
    t_col = lax.broadcasted_iota(jnp.int32, (SCAN_ROWS, SCAN_ROWS), 1)
    same_chunk = (t_row // cdim) == (t_col // cdim)
    tri_prefix = (same_chunk & (t_col <= t_row)).astype(jnp.bfloat16)
    tri_suffix = (same_chunk & (t_col >= t_row)).astype(jnp.bfloat16)

    def split(terms):
        high = terms.astype(jnp.bfloat16)
        rest = (terms - high.astype(jnp.float32)).astype(jnp.bfloat16)
        return jnp.concatenate([high, rest], axis=1)

    def scan_body(i, carry):
        blocks = [pl.ds(pl.multiple_of((i * SCAN_GROUP + c) * SCAN_ROWS, SCAN_ROWS), SCAN_ROWS)
                  for c in range(SCAN_GROUP)]
        pre = []
        for r in blocks:
            m = misc_ref[r, :].astype(jnp.bfloat16)
            pre.append((_bdot(m, wgf_ref[...]), _bdot(m, wgb_ref[...])))
        terms = [(split(_log_sigmoid(pf + bgf_ref[...]) * inv_t),
                  split(_log_sigmoid(pb + bgb_ref[...]) * inv_t)) for pf, pb in pre]
        sums = [(_bdot(tri_prefix, tf), _bdot(tri_suffix, tb)) for tf, tb in terms]
        for r, (sf, sb) in zip(blocks, sums):
            bf_s[r, :] = sf[:, 0:pair_dk] + sf[:, pair_dk:]
            bb_s[r, :] = sb[:, 0:pair_dk] + sb[:, pair_dk:]
        return carry

    lax.fori_loop(0, s // (SCAN_ROWS * SCAN_GROUP), scan_body, 0)

    row = lax.broadcasted_iota(jnp.int32, (cdim, pair_dk), 0)
    key_pos = lax.broadcasted_iota(jnp.int32, (cdim, pair_dk), 1) & (cdim - 1)
    keep_f = key_pos <= row
    keep_b = key_pos > row
    cat_block = lax.broadcasted_iota(jnp.int32, (cdim, 2 * pair_dk), 1) // GLA_DK
    v_head = lax.broadcasted_iota(jnp.int32, (cdim, pair_dv), 1) // GLA_DV

    def intra_body(i, carry):
        chunks = [i * GLA_GROUP + c for c in range(GLA_GROUP)]
        rows = [pl.ds(pl.multiple_of(n * cdim, cdim), cdim) for n in chunks]
        scores, k_ends = [], []
        for n, r in zip(chunks, rows):
            bf = bf_s[r, :]
            bb = bb_s[r, :]
            bf_last = bf[cdim - 1:cdim, :]
            bb_last = bb[0:1, :]
            q = q_ref[r, :] * q_scale
            k = k_ref[r, :]
            q_cat = jnp.concatenate([q * jnp.exp(bf), q * jnp.exp(bb)],
                                    axis=1).astype(jnp.bfloat16)
            k_inv = jnp.concatenate([k * jnp.exp(-bf), k * jnp.exp(-bb)], axis=1)
            k_ends.append(jnp.concatenate(
                [k * jnp.exp(bf_last - bf), k * jnp.exp(bb_last - bb)], axis=1).astype(jnp.bfloat16))
            dec_s[n] = jnp.concatenate([jnp.exp(bf_last), jnp.exp(bb_last)], axis=1)
            qcat_s[r, :] = q_cat
            k_bd = jnp.concatenate([jnp.where(cat_block == c, k_inv, 0.0) for c in range(4)],
                                   axis=0).astype(jnp.bfloat16)
            scores.append(lax.dot_general(q_cat, k_bd, _NT,
                                          preferred_element_type=jnp.float32))
        for n, r, k_end in zip(chunks, rows, k_ends):
            kv_s[n] = lax.dot_general(v_ref[r, :], k_end, _TN,
                                      preferred_element_type=jnp.float32)
        for r, sc in zip(rows, scores):
            p = (jnp.where(keep_f, sc[:, 0:pair_dk], 0.0)
                 + jnp.where(keep_b, sc[:, pair_dk:], 0.0)).astype(jnp.bfloat16)
            vv = v_ref[r, :]
            v_bd = jnp.concatenate(
                [jnp.where(v_head == j, vv, jnp.zeros_like(vv)) for j in range(2)], axis=0)
            intra_s[r, :] = _bdot(p, v_bd)
        return carry

    lax.fori_loop(0, n_chunks // GLA_GROUP, intra_body, 0)

    same_head = (lax.broadcasted_iota(jnp.int32, (pair_dv, pair_dk), 0) // GLA_DV
                 == lax.broadcasted_iota(jnp.int32, (pair_dv, pair_dk), 1) // GLA_DK)

    def state_body(i, states):
        sf, sb = states
        nf = i
        nb = n_chunks - 1 - i
        st_s[nf, :, 0:pair_dk] = jnp.where(same_head, sf, 0.0).astype(jnp.bfloat16)
        st_s[nb, :, pair_dk:] = jnp.where(same_head, sb, 0.0).astype(jnp.bfloat16)
        sf = sf * dec_s[nf][:, 0:pair_dk] + kv_s[nf, :, 0:pair_dk]
        sb = sb * dec_s[nb][:, pair_dk:] + kv_s[nb, :, pair_dk:]
        return sf, sb

    zero = jnp.zeros((pair_dv, pair_dk), jnp.float32)
    lax.fori_loop(0, n_chunks, state_body, (zero, zero))

    g = g_ref[...]

    def out_body(i, carry):
        chunks = [i * GLA_GROUP + c for c in range(GLA_GROUP)]
        rows = [pl.ds(pl.multiple_of(n * cdim, cdim), cdim) for n in chunks]
        inter = [lax.dot_general(qcat_s[r, :], st_s[n], _NT, preferred_element_type=jnp.float32)
                 for n, r in zip(chunks, rows)]
        for r, o_inter in zip(rows, inter):
            o = intra_s[r, :] + o_inter
            for j in range(2):
                o_ref[r, j * GLA_DV:(j + 1) * GLA_DV] = _rms(
                    o[:, j * GLA_DV:(j + 1) * GLA_DV], g).astype(o_ref.dtype)
        return carry

    lax.fori_loop(0, n_chunks // GLA_GROUP, out_body, 0)


def _gla_call(layer, gq, gk, gv, misc, wg_cat, bg_cat, gla_norm_g):
    bsz, s, _ = gq.shape
    pairs = GLA_HEADS // 2
    n_chunks = s // GLA_CHUNK
    pair_dk, pair_dv = 2 * GLA_DK, 2 * GLA_DV
    qk_spec = pl.BlockSpec((None, s, pair_dk), lambda b, p: (b, 0, p))
    v_spec = pl.BlockSpec((None, s, pair_dv), lambda b, p: (b, 0, p))

    def gate_specs(direction):
        return (pl.BlockSpec((None, MISC_W, pair_dk), lambda b, p: (layer, 0, direction * pairs + p)),
                pl.BlockSpec((None, 1, pair_dk), lambda b, p: (layer, 0, direction * pairs + p)))

    return pl.pallas_call(
        _gla_kernel,
        grid=(bsz, pairs),
        in_specs=[qk_spec, qk_spec, v_spec,
                  pl.BlockSpec((None, s, MISC_W), lambda b, p: (b, 0, 0)),
                  *gate_specs(0), *gate_specs(1),
                  _layer_spec(layer, (1, GLA_DV))],
        out_specs=v_spec,
        out_shape=jax.ShapeDtypeStruct((bsz, s, GLA_WIDTH), jnp.bfloat16),
        scratch_shapes=[
            pltpu.VMEM((s, pair_dk), jnp.float32),
            pltpu.VMEM((s, pair_dk), jnp.float32),
            pltpu.VMEM((s, 2 * pair_dk), jnp.bfloat16),
            pltpu.VMEM((n_chunks, pair_dv, 2 * pair_dk), jnp.float32),
            pltpu.VMEM((n_chunks, 1, 2 * pair_dk), jnp.float32),
            pltpu.VMEM((n_chunks, pair_dv, 2 * pair_dk), jnp.bfloat16),
            pltpu.VMEM((s, pair_dv), jnp.float32),
        ],
        compiler_params=pltpu.CompilerParams(
            dimension_semantics=("arbitrary", "arbitrary"), vmem_limit_bytes=VMEM_LIMIT),
        name="gla",
    )(gq, gk, gv, misc, wg_cat, bg_cat, wg_cat, bg_cat, gla_norm_g)


def _mla_kernel(mq_ref, mkv_ref, misc_ref, c_ref, sa_ref, sb_ref, gq_ref, gkv_ref,
                wuq_ref, wuk_ref, wuvt_ref, o_ref, ckv_s, q_s, k_s, vt_s):
    s = mkv_ref.shape[0]
    head = pl.program_id(1)
    blocks = [slice(i * KV_ROWS, (i + 1) * KV_ROWS) for i in range(s // KV_ROWS)]
    q_mult = (MLA_NOPE + MLA_ROPE) ** -0.5 * math.log2(math.e)

    @pl.when(head == 0)
    def _per_batch_row():
        for r in blocks:
            tabs = (c_ref[r, :], sa_ref[r, :], sb_ref[r, :])
            ckv_s[r, :] = _rms(mkv_ref[r, :], gkv_ref[...]).astype(jnp.bfloat16)
            k_s[r, MLA_NOPE:] = _rope(misc_ref[r, :], *tabs).astype(jnp.bfloat16)
            cq = _rms(mq_ref[r, :], gq_ref[...]).astype(jnp.bfloat16)
            qf = _bdot(cq, wuq_ref[...])
            cos_t, sin_t = tabs[0], tabs[1] + tabs[2]
            for h in range(MLA_HEADS):
                qh = qf[:, h * QK_W:(h + 1) * QK_W]
                rot = qh[:, MLA_NOPE:]
                q_rope = rot * cos_t + pltpu.roll(rot, MLA_ROPE, axis=1) * sin_t
                q_s[h, r, :] = (jnp.concatenate([qh[:, 0:MLA_NOPE], q_rope], axis=1)
                                * q_mult).astype(jnp.bfloat16)
        vt_s[MLA_DV:, :] = jnp.ones((vt_s.shape[0] - MLA_DV, s), jnp.bfloat16)

    for r in blocks:
        ckv = ckv_s[r, :]
        k_s[r, 0:MLA_NOPE] = _bdot(ckv, wuk_ref[...]).astype(jnp.bfloat16)
        vt_s[0:MLA_DV, r] = lax.dot_general(
            wuvt_ref[...], ckv, _NT, preferred_element_type=jnp.float32).astype(jnp.bfloat16)

    def scores_t(j):
        q = q_s[head, j * Q_TILE:(j + 1) * Q_TILE, :]
        return lax.dot_general(k_s[...], q, _NT, preferred_element_type=jnp.float32)

    def finish(j, sc_t):
        p_t = jnp.exp2(sc_t - jnp.max(sc_t, axis=0, keepdims=True)).astype(jnp.bfloat16)
        o_t = _bdot(vt_s[...], p_t)
        o_ref[j * Q_TILE:(j + 1) * Q_TILE, :] = (
            o_t[0:MLA_DV, :] / o_t[MLA_DV:MLA_DV + 1, :]).T.astype(o_ref.dtype)

    n_tiles = s // Q_TILE
    pending = scores_t(0)
    for j in range(n_tiles):
        upcoming = scores_t(j + 1) if j + 1 < n_tiles else None
        finish(j, pending)
        pending = upcoming


def _mla_call(layer, mq, mkv, misc, tabs, gq, gkv, wuq_p, wuk_p, wuvt_p):
    bsz, s, _ = mq.shape

    def per_row(width):
        return pl.BlockSpec((None, s, width), lambda b, h: (b, 0, 0))

    return pl.pallas_call(
        _mla_kernel,
        grid=(bsz, MLA_HEADS),
        in_specs=[
            per_row(MLA_Q_LORA), per_row(MLA_KV_LORA), per_row(MISC_W),
            per_row(LANES), per_row(LANES), per_row(LANES),
            _layer_spec(layer, (1, MLA_Q_LORA)),
            _layer_spec(layer, (1, MLA_KV_LORA)),
            _layer_spec(layer, (MLA_Q_LORA, MLA_HEADS * QK_W)),
            pl.BlockSpec((None, MLA_KV_LORA, MLA_NOPE), lambda b, h: (layer, 0, h)),
            pl.BlockSpec((None, MLA_DV, MLA_KV_LORA), lambda b, h: (layer, h, 0)),
        ],
        out_specs=pl.BlockSpec((None, s, MLA_DV), lambda b, h: (b, 0, h)),
        out_shape=jax.ShapeDtypeStruct((bsz, s, MLA_WIDTH), jnp.bfloat16),
        scratch_shapes=[
            pltpu.VMEM((s, MLA_KV_LORA), jnp.bfloat16),
            pltpu.VMEM((MLA_HEADS, s, QK_W), jnp.bfloat16),
            pltpu.VMEM((s, QK_W), jnp.bfloat16),
            pltpu.VMEM((MLA_DV + BF16_ROWS, s), jnp.bfloat16),
        ],
        compiler_params=pltpu.CompilerParams(
            dimension_semantics=("arbitrary", "arbitrary"),
            vmem_limit_bytes=VMEM_LIMIT),
        name="mla",
    )(mq, mkv, misc, *tabs, gq, gkv, wuq_p, wuk_p, wuvt_p)


def _outproj_kernel(tiles_per_seq, final, x_ref, mod_ref, ogla_ref, omla_ref, cb_ref, cc_ref,
                    cx_ref, ccp_ref, cxp_ref, ccn_ref, cxn_ref, z_ref, mg_ref, cw_ref, cg_ref,
                    fg_ref, w_ref, o_ref):
    f32 = jnp.float32
    j = pl.program_id(0) % tiles_per_seq
    tm = x_ref.shape[0]

    u = cc_ref[...].astype(f32) * cx_ref[...].astype(f32)
    prev_ok = (j > 0).astype(f32)
    next_ok = (j < tiles_per_seq - 1).astype(f32)
    u_prev = (ccp_ref[...].astype(f32) * cxp_ref[...].astype(f32))[HALO_ROWS - 1:, :] * prev_ok
    u_next = (ccn_ref[...].astype(f32) * cxn_ref[...].astype(f32))[0:1, :] * next_ok
    rows = lax.broadcasted_iota(jnp.int32, u.shape, 0)
    up = jnp.where(rows == 0, u_prev, pltpu.roll(u, 1, axis=0))
    un = jnp.where(rows == tm - 1, u_next, pltpu.roll(u, tm - 1, axis=0))
    cw = cw_ref[...]
    conv = up * cw[0:1] + u * cw[1:2] + un * cw[2:3]
    gate = mod_ref[...][2:3]

    def gated_dot(r, lo, y_part):
        hi = lo + y_part.shape[1]
        z = z_ref[r, lo:hi].astype(f32)
        y = (y_part * (z * jax.nn.sigmoid(z))).astype(jnp.bfloat16)
        return _bdot(y, w_ref[lo:hi, :])

    def finish(r, acc):
        out = x_ref[r, :] + gate * acc
        if final:
            out = _rms(out, fg_ref[...])
        o_ref[r, :] = out

    pending = None
    for k in range(tm // OUT_SUB_TILE):
        r = slice(k * OUT_SUB_TILE, (k + 1) * OUT_SUB_TILE)
        acc = gated_dot(r, 0, ogla_ref[r, :].astype(f32))
        if pending is not None:
            finish(*pending)
        acc += gated_dot(r, GLA_WIDTH, _rms(omla_ref[r, :].astype(f32), mg_ref[...]))
        acc += gated_dot(r, GLA_WIDTH + MLA_WIDTH,
                         _rms(cb_ref[r, :].astype(f32) * conv[r, :], cg_ref[...]))
        pending = (r, acc)
    finish(*pending)


def _outproj_call(layer, x2, mod, o_gla, o_mla, cb, cc, cx, z, mla_out_g, conv_w, conv_out_g,
                  final_g, w_out_b, seq_len, final):
    t = x2.shape[0]
    tm = OUT_ROW_TILE
    tiles_per_seq = seq_len // tm
    halo_per_tile = tm // HALO_ROWS
    n_halo = t // HALO_ROWS

    def row_spec(w):
        return pl.BlockSpec((tm, w), lambda i: (i, 0))

    prev_spec = pl.BlockSpec((HALO_ROWS, CONV_CH),
                             lambda i: (jnp.maximum(i * halo_per_tile - 1, 0), 0))
    next_spec = pl.BlockSpec((HALO_ROWS, CONV_CH),
                             lambda i: (jnp.minimum((i + 1) * halo_per_tile, n_halo - 1), 0))

    return pl.pallas_call(
        functools.partial(_outproj_kernel, tiles_per_seq, final),
        grid=(t // tm,),
        in_specs=[
            row_spec(D_MODEL),
            pl.BlockSpec((None, None, 3, D_MODEL), lambda i: (layer, i // tiles_per_seq, 0, 0)),
            row_spec(GLA_WIDTH), row_spec(MLA_WIDTH),
            row_spec(CONV_CH), row_spec(CONV_CH), row_spec(CONV_CH),
            prev_spec, prev_spec, next_spec, next_spec,
            row_spec(D_MODEL),
            _layer_spec(layer, (1, MLA_WIDTH)), _layer_spec(layer, (3, CONV_CH)),
            _layer_spec(layer, (1, CONV_CH)),
            pl.BlockSpec((1, D_MODEL), lambda i: (0, 0)),
            pl.BlockSpec((None, D_MODEL, D_MODEL), lambda i: (layer, 0, 0),
                         pipeline_mode=pl.Buffered(1)),
        ],
        out_specs=row_spec(D_MODEL),
        out_shape=jax.ShapeDtypeStruct((t, D_MODEL), jnp.float32),
        compiler_params=pltpu.CompilerParams(
            dimension_semantics=("arbitrary",), vmem_limit_bytes=VMEM_LIMIT),
        name="out_proj",
    )(x2, mod, o_gla, o_mla, cb, cc, cx, cc, cx, cc, cx, z,
      mla_out_g, conv_w, conv_out_g, final_g, w_out_b)


_O_GLR = 2 * GLA_HEADS * GLA_DK + GLA_WIDTH
_O_MQ = _O_GLR + 2 * GLA_GATE_RANK
_O_MKV = _O_MQ + MLA_Q_LORA
_O_MKR = _O_MKV + MLA_KV_LORA
_O_CB = _O_MKR + MLA_ROPE
IN_DIM = _O_CB + 3 * CONV_CH + D_MODEL
W_PREP_ROWS = 256


W_PREP_CHUNK = 512


def _w_in_prep_kernel(wt_ref, o_ref):
    rows = o_ref.shape[0]
    eye = (lax.broadcasted_iota(jnp.int32, (rows, rows), 0)
           == lax.broadcasted_iota(jnp.int32, (rows, rows), 1)).astype(jnp.bfloat16)

    def put(dst, block):
        o_ref[:, dst:dst + block.shape[0]] = lax.dot_general(
            eye, block.astype(jnp.bfloat16), _NT,
            preferred_element_type=jnp.float32).astype(jnp.bfloat16)

    def put_range(dst, lo, hi):
        for start in range(lo, hi, W_PREP_CHUNK):
            stop = min(start + W_PREP_CHUNK, hi)
            put(dst + start - lo, wt_ref[start:stop, :])

    pad_rows = MISC_W - MLA_ROPE - 2 * GLA_GATE_RANK
    misc = jnp.concatenate([wt_ref[_O_MKR:_O_CB, :], wt_ref[_O_GLR:_O_MQ, :],
                            jnp.zeros((pad_rows, rows), jnp.float32)], axis=0)
    off = 0
    for src in ((0, _O_GLR), (_O_MQ, _O_MKV), misc, (_O_MKV, _O_MKR), (_O_CB, IN_DIM)):
        if isinstance(src, tuple):
            put_range(off, *src)
            off += src[1] - src[0]
        else:
            put(off, src)
            off += src.shape[0]


def _w_in_prep_call(w_in):
    depth, d, n = w_in.shape
    return pl.pallas_call(
        _w_in_prep_kernel,
        grid=(depth, d // W_PREP_ROWS),
        in_specs=[pl.BlockSpec((None, n, W_PREP_ROWS), lambda l, i: (l, 0, i))],
        out_specs=pl.BlockSpec((None, W_PREP_ROWS, IN_DIM_P), lambda l, i: (l, i, 0)),
        out_shape=jax.ShapeDtypeStruct((depth, d, IN_DIM_P), jnp.bfloat16),
        compiler_params=pltpu.CompilerParams(
            dimension_semantics=("arbitrary", "arbitrary"), vmem_limit_bytes=VMEM_LIMIT),
        name="w_in_prep",
    )(jnp.swapaxes(w_in, 1, 2))


def _gate_params(wg_f, bg_f, wg_b, bg_b):
    depth, rank, hk = wg_f.shape
    zeros = jnp.zeros((depth, rank, hk), jnp.float32)
    rows = jnp.concatenate([jnp.concatenate([wg_f, zeros], axis=2),
                            jnp.concatenate([zeros, wg_b], axis=2)], axis=1)
    wg_cat = jnp.pad(rows, ((0, 0), (MLA_ROPE, MISC_W - MLA_ROPE - 2 * rank), (0, 0)))
    bg_cat = jnp.concatenate([bg_f, bg_b], axis=1)[:, None, :]
    return wg_cat.astype(jnp.bfloat16), bg_cat


def _permute_w_uq(w):
    depth = w.shape[0]
    half = MLA_ROPE // 2
    w4 = w.reshape(depth, MLA_Q_LORA, MLA_HEADS, MLA_NOPE + MLA_ROPE)
    nope, x1, x2 = w4[..., :MLA_NOPE], w4[..., MLA_NOPE:MLA_NOPE + half], w4[..., MLA_NOPE + half:]
    cols = jnp.concatenate([nope, x1, x2, x2, x1], axis=-1)
    return cols.reshape(depth, MLA_Q_LORA, MLA_HEADS * QK_W).astype(jnp.bfloat16)


def _split_w_ukv(w):
    depth = w.shape[0]
    w4 = w.reshape(depth, MLA_KV_LORA, MLA_HEADS, MLA_NOPE + MLA_DV)
    wk = w4[..., :MLA_NOPE].reshape(depth, MLA_KV_LORA, MLA_HEADS * MLA_NOPE)
    wv_t = w4[..., MLA_NOPE:].reshape(depth, MLA_KV_LORA, MLA_HEADS * MLA_DV).transpose(0, 2, 1)
    return wk.astype(jnp.bfloat16), wv_t.astype(jnp.bfloat16)


def kernel(x, c, positions, ada_w, ada_b, norm_g, w_in, gla_wg_f, gla_bg_f, gla_wg_b, gla_bg_b,
           gla_norm_g, mla_q_norm_g, mla_kv_norm_g, mla_w_uq, mla_w_ukv, mla_out_g, conv_w,
           conv_out_g, w_out, final_g):
    bsz, s, d = x.shape
    t = bsz * s

    c_pad = jnp.pad(c, ((0, F32_ROWS - bsz), (0, 0)))
    mod = _ada_call(c_pad, ada_w, ada_b)[:, :bsz].reshape(DEPTH, bsz, 3, d)
    tabs = _rope_tab_call(positions)

    w_in_p = _w_in_prep_call(w_in)
    w_out_b = w_out.astype(jnp.bfloat16)
    wg_cat, bg_cat = _gate_params(gla_wg_f, gla_bg_f, gla_wg_b, gla_bg_b)
    wuq_p = _permute_w_uq(mla_w_uq)
    wuk_p, wuvt_p = _split_w_ukv(mla_w_ukv)

    def row(p):
        return p[:, None, :]

    def b3(a):
        return a.reshape(bsz, s, a.shape[-1])

    h = x.reshape(t, d)
    for l in range(DEPTH):
        segs = _inproj_call(l, h, mod, row(norm_g), w_in_p, s)
        gq, gk, gv, mq, misc, mkv, cb, cc, cx, z = segs
        o_gla = _gla_call(l, b3(gq), b3(gk), b3(gv), b3(misc), wg_cat, bg_cat, row(gla_norm_g))
        o_mla = _mla_call(l, b3(mq), b3(mkv), b3(misc), tabs, row(mla_q_norm_g),
                          row(mla_kv_norm_g), wuq_p, wuk_p, wuvt_p)
        h = _outproj_call(l, h, mod, o_gla.reshape(t, -1), o_mla.reshape(t, -1), cb, cc, cx, z,
                          row(mla_out_g), conv_w, row(conv_out_g), final_g.reshape(1, d),
                          w_out_b, s, l == DEPTH - 1)
    return h.reshape(bsz, s, d)
```

```python
import functools
import math

import jax
import jax.numpy as jnp
from jax import lax
from jax.experimental import pallas as pl
from jax.experimental.pallas import tpu as pltpu

D_MODEL = 2048
DEPTH = 2
GLA_HEADS = 6
GLA_DK = 64
GLA_DV = 128
GLA_GATE_RANK = 16
GLA_GATE_TEMP = 16.0
GLA_CHUNK = 64
GLA_WIDTH = GLA_HEADS * GLA_DV
MLA_HEADS = 6
MLA_Q_LORA = 384
MLA_KV_LORA = 256
MLA_NOPE = 128
MLA_ROPE = 64
MLA_DV = 128
MLA_WIDTH = MLA_HEADS * MLA_DV
ROPE_THETA = 10000.0
CONV_CH = D_MODEL - GLA_WIDTH - MLA_WIDTH
EPS = 1e-6

LANES = 128
F32_ROWS = 8
BF16_ROWS = 16
HALO_ROWS = BF16_ROWS
MISC_W = LANES
QK_W = 2 * LANES

IN_SEGMENTS = (
    ("gq", GLA_HEADS * GLA_DK, jnp.float32),
    ("gk", GLA_HEADS * GLA_DK, jnp.float32),
    ("gv", GLA_WIDTH, jnp.bfloat16),
    ("mq", MLA_Q_LORA, jnp.float32),
    ("misc", MISC_W, jnp.float32),
    ("mkv", MLA_KV_LORA, jnp.float32),
    ("cb", CONV_CH, jnp.bfloat16),
    ("cc", CONV_CH, jnp.bfloat16),
    ("cx", CONV_CH, jnp.bfloat16),
    ("z", D_MODEL, jnp.bfloat16),
)
IN_DIM_P = sum(w for _, w, _ in IN_SEGMENTS)

IN_ROW_TILE = 256
OUT_ROW_TILE = 512
OUT_SUB_TILE = 256
Q_TILE = 512
KV_ROWS = 512
SCAN_ROWS = 256
SCAN_GROUP = 4
GLA_GROUP = 8
VMEM_LIMIT = 56 * 1024 * 1024

_NT = (((1,), (1,)), ((), ()))
_TN = (((0,), (0,)), ((), ()))


def _rms(x, g):
    ms = jnp.mean(x * x, axis=-1, keepdims=True)
    return x * lax.rsqrt(ms + EPS) * g


def _bdot(a, b):
    return jnp.dot(a, b, preferred_element_type=jnp.float32)


def _ada_kernel(c_ref, w_ref, b_ref, o_ref):
    c = c_ref[...]
    c_act = c * jax.nn.sigmoid(c)
    o_ref[...] = jnp.dot(c_act, w_ref[...], precision=lax.Precision.HIGHEST,
                         preferred_element_type=jnp.float32) + b_ref[...]


def _ada_call(c_pad, ada_w, ada_b):
    rows = c_pad.shape[0]
    n = ada_w.shape[-1]
    tn = 1024
    return pl.pallas_call(
        _ada_kernel,
        grid=(DEPTH, n // tn),
        in_specs=[
            pl.BlockSpec((rows, D_MODEL), lambda l, j: (0, 0)),
            pl.BlockSpec((None, D_MODEL, tn), lambda l, j: (l, 0, j)),
            pl.BlockSpec((None, 1, tn), lambda l, j: (l, 0, j)),
        ],
        out_specs=pl.BlockSpec((None, rows, tn), lambda l, j: (l, 0, j)),
        out_shape=jax.ShapeDtypeStruct((DEPTH, rows, n), jnp.float32),
        compiler_params=pltpu.CompilerParams(
            dimension_semantics=("arbitrary", "arbitrary"), vmem_limit_bytes=VMEM_LIMIT),
        name="ada_mod",
    )(c_pad, ada_w, ada_b.reshape(DEPTH, 1, n))


def _rope_tab_kernel(pos_ref, invf_ref, c_ref, sa_ref, sb_ref):
    ang = pos_ref[...].astype(jnp.float32) * invf_ref[...]
    cos, sin = jnp.cos(ang), jnp.sin(ang)
    lane = lax.broadcasted_iota(jnp.int32, ang.shape, 1)
    half = MLA_ROPE // 2
    c_ref[...] = jnp.where(lane < MLA_ROPE, cos, 0.0)
    sa_ref[...] = jnp.where(lane < half, -sin, 0.0)
    sb_ref[...] = jnp.where((lane >= half) & (lane < MLA_ROPE), sin, 0.0)


def _rope_tab_call(positions):
    bsz, s = positions.shape
    half = MLA_ROPE // 2
    inv_freq = ROPE_THETA ** (-jnp.arange(0, MLA_ROPE, 2, dtype=jnp.float32) / MLA_ROPE)
    invf = jnp.concatenate([inv_freq, inv_freq, jnp.zeros((LANES - 2 * half,), jnp.float32)])
    tab = jax.ShapeDtypeStruct((bsz, s, LANES), jnp.float32)
    spec = pl.BlockSpec((None, s, LANES), lambda b: (b, 0, 0))
    return pl.pallas_call(
        _rope_tab_kernel,
        grid=(bsz,),
        in_specs=[pl.BlockSpec((None, s, 1), lambda b: (b, 0, 0)),
                  pl.BlockSpec((1, LANES), lambda b: (0, 0))],
        out_specs=[spec, spec, spec],
        out_shape=[tab, tab, tab],
        compiler_params=pltpu.CompilerParams(dimension_semantics=("arbitrary",)),
        name="rope_tables",
    )(positions.reshape(bsz, s, 1), invf.reshape(1, LANES))


def _rope(x, c, sa, sb):
    return (x * c + pltpu.roll(x, LANES - MLA_ROPE // 2, axis=1) * sa
            + pltpu.roll(x, MLA_ROPE // 2, axis=1) * sb)


def _inproj_kernel(x_ref, mod_ref, g_ref, w_ref, *out_refs):
    x = x_ref[...]
    mod = mod_ref[...]
    shift, scale = mod[0:1], mod[1:2]
    ms = jnp.mean(x * x, axis=-1, keepdims=True)
    h = (x * lax.rsqrt(ms + EPS) * g_ref[...]) * (1.0 + scale) + shift
    hb = h.astype(jnp.bfloat16)
    off = 0
    for (_, width, _), o_ref in zip(IN_SEGMENTS, out_refs):
        o_ref[...] = _bdot(hb, w_ref[:, off:off + width]).astype(o_ref.dtype)
        off += width


def _layer_spec(layer, shape):
    return pl.BlockSpec((None,) + shape, lambda *_: (layer,) + (0,) * len(shape))


def _inproj_call(layer, x2, mod, norm_g, w_in_p, seq_len):
    t = x2.shape[0]
    tm = IN_ROW_TILE
    tiles_per_seq = seq_len // tm
    out_shape = [jax.ShapeDtypeStruct((t, w), dt) for _, w, dt in IN_SEGMENTS]
    out_specs = [pl.BlockSpec((tm, w), lambda i: (i, 0)) for _, w, _ in IN_SEGMENTS]
    return pl.pallas_call(
        _inproj_kernel,
        grid=(t // tm,),
        in_specs=[
            pl.BlockSpec((tm, D_MODEL), lambda i: (i, 0)),
            pl.BlockSpec((None, None, 3, D_MODEL), lambda i: (layer, i // tiles_per_seq, 0, 0)),
            _layer_spec(layer, (1, D_MODEL)),
            pl.BlockSpec((None, D_MODEL, IN_DIM_P), lambda i: (layer, 0, 0),
                         pipeline_mode=pl.Buffered(1)),
        ],
        out_specs=out_specs,
        out_shape=out_shape,
        compiler_params=pltpu.CompilerParams(
            dimension_semantics=("arbitrary",), vmem_limit_bytes=VMEM_LIMIT),
        name="in_proj",
    )(x2, mod, norm_g, w_in_p)


def _log_sigmoid(x):
    return jnp.minimum(x, 0.0) - jnp.log(1.0 + jnp.exp(-jnp.abs(x)))


def _gla_kernel(q_ref, k_ref, v_ref, misc_ref, wgf_ref, bgf_ref, wgb_ref, bgb_ref, g_ref,
                o_ref, bf_s, bb_s, qcat_s, kv_s, dec_s, st_s, intra_s):
    s = q_ref.shape[0]
    cdim = GLA_CHUNK
    n_chunks = s // cdim
    pair_dk = 2 * GLA_DK
    pair_dv = 2 * GLA_DV
    inv_t = 1.0 / GLA_GATE_TEMP
    q_scale = GLA_DK ** -0.5

    t_row = lax.broadcasted_iota(jnp.int32, (SCAN_ROWS, SCAN_ROWS), 0)
    t_col = lax.broadcasted_iota(jnp.int32, (SCAN_ROWS, SCAN_ROWS), 1)
    same_chunk = (t_row // cdim) == (t_col // cdim)
    tri_prefix = (same_chunk & (t_col <= t_row)).astype(jnp.bfloat16)
    tri_suffix = (same_chunk & (t_col >= t_row)).astype(jnp.bfloat16)

    def split(terms):
        high = terms.astype(jnp.bfloat16)
        rest = (terms - high.astype(jnp.float32)).astype(jnp.bfloat16)
        return jnp.concatenate([high, rest], axis=1)

    def scan_body(i, carry):
        blocks = [pl.ds(pl.multiple_of((i * SCAN_GROUP + c) * SCAN_ROWS, SCAN_ROWS), SCAN_ROWS)
                  for c in range(SCAN_GROUP)]
        pre = []
        for r in blocks:
            m = misc_ref[r, :].astype(jnp.bfloat16)
            pre.append((_bdot(m, wgf_ref[...]), _bdot(m, wgb_ref[...])))
        terms = [(split(_log_sigmoid(pf + bgf_ref[...]) * inv_t),
                  split(_log_sigmoid(pb + bgb_ref[...]) * inv_t)) for pf, pb in pre]
        sums = [(_bdot(tri_prefix, tf), _bdot(tri_suffix, tb)) for tf, tb in terms]
        for r, (sf, sb) in zip(blocks, sums):
            bf_s[r, :] = sf[:, 0:pair_dk] + sf[:, pair_dk:]
            bb_s[r, :] = sb[:, 0:pair_dk] + sb[:, pair_dk:]
        return carry

    lax.fori_loop(0, s // (SCAN_ROWS * SCAN_GROUP), scan_body, 0)

    row = lax.broadcasted_iota(jnp.int32, (cdim, pair_dk), 0)
    key_pos = lax.broadcasted_iota(jnp.int32, (cdim, pair_dk), 1) & (cdim - 1)
    keep_f = key_pos <= row
    keep_b = key_pos > row
    cat_block = lax.broadcasted_iota(jnp.int32, (cdim, 2 * pair_dk), 1) // GLA_DK
    v_head = lax.broadcasted_iota(jnp.int32, (cdim, pair_dv), 1) // GLA_DV

    def intra_body(i, carry):
        chunks = [i * GLA_GROUP + c for c in range(GLA_GROUP)]
        rows = [pl.ds(pl.multiple_of(n * cdim, cdim), cdim) for n in chunks]
        scores, k_ends = [], []
        for n, r in zip(chunks, rows):
            bf = bf_s[r, :]
            bb = bb_s[r, :]
            bf_last = bf[cdim - 1:cdim, :]
            bb_last = bb[0:1, :]
            q = q_ref[r, :] * q_scale
            k = k_ref[r, :]
            q_cat = jnp.concatenate([q * jnp.exp(bf), q * jnp.exp(bb)],
                                    axis=1).astype(jnp.bfloat16)
            k_inv = jnp.concatenate([k * jnp.exp(-bf), k * jnp.exp(-bb)], axis=1)
            k_ends.append(jnp.concatenate(
                [k * jnp.exp(bf_last - bf), k * jnp.exp(bb_last - bb)], axis=1).astype(jnp.bfloat16))
            dec_s[n] = jnp.concatenate([jnp.exp(bf_last), jnp.exp(bb_last)], axis=1)
            qcat_s[r, :] = q_cat
            k_bd = jnp.concatenate([jnp.where(cat_block == c, k_inv, 0.0) for c in range(4)],
                                   axis=0).astype(jnp.bfloat16)
            scores.append(lax.dot_general(q_cat, k_bd, _NT,
                                          preferred_element_type=jnp.float32))
        for n, r, k_end in zip(chunks, rows, k_ends):
            kv_s[n] = lax.dot_general(v_ref[r, :], k_end, _TN,
                                      preferred_element_type=jnp.float32)
        for r, sc in zip(rows, scores):
            p = (jnp.where(keep_f, sc[:, 0:pair_dk], 0.0)
                 + jnp.where(keep_b, sc[:, pair_dk:], 0.0)).astype(jnp.bfloat16)
            vv = v_ref[r, :]
            v_bd = jnp.concatenate(
                [jnp.where(v_head == j, vv, jnp.zeros_like(vv)) for j in range(2)], axis=0)
            intra_s[r, :] = _bdot(p, v_bd)
        return carry

    lax.fori_loop(0, n_chunks // GLA_GROUP, intra_body, 0)

    same_head = (lax.broadcasted_iota(jnp.int32, (pair_dv, pair_dk), 0) // GLA_DV
                 == lax.broadcasted_iota(jnp.int32, (pair_dv, pair_dk), 1) // GLA_DK)

    def state_body(i, states):
        sf, sb = states
        nf = i
        nb = n_chunks - 1 - i
        st_s[nf, :, 0:pair_dk] = jnp.where(same_head, sf, 0.0).astype(jnp.bfloat16)
        st_s[nb, :, pair_dk:] = jnp.where(same_head, sb, 0.0).astype(jnp.bfloat16)
        sf = sf * dec_s[nf][:, 0:pair_dk] + kv_s[nf, :, 0:pair_dk]
        sb = sb * dec_s[nb][:, pair_dk:] + kv_s[nb, :, pair_dk:]
        return sf, sb

    zero = jnp.zeros((pair_dv, pair_dk), jnp.float32)
    lax.fori_loop(0, n_chunks, state_body, (zero, zero))

    g = g_ref[...]

    def out_body(i, carry):
        chunks = [i * GLA_GROUP + c for c in range(GLA_GROUP)]
        rows = [pl.ds(pl.multiple_of(n * cdim, cdim), cdim) for n in chunks]
        inter = [lax.dot_general(qcat_s[r, :], st_s[n], _NT, preferred_element_type=jnp.float32)
                 for n, r in zip(chunks, rows)]
        for r, o_inter in zip(rows, inter):
            o = intra_s[r, :] + o_inter
            for j in range(2):
                o_ref[r, j * GLA_DV:(j + 1) * GLA_DV] = _rms(
                    o[:, j * GLA_DV:(j + 1) * GLA_DV], g).astype(o_ref.dtype)
        return carry

    lax.fori_loop(0, n_chunks // GLA_GROUP, out_body, 0)


def _gla_call(layer, gq, gk, gv, misc, wg_cat, bg_cat, gla_norm_g):
    bsz, s, _ = gq.shape
    pairs = GLA_HEADS // 2
    n_chunks = s // GLA_CHUNK
    pair_dk, pair_dv = 2 * GLA_DK, 2 * GLA_DV
    qk_spec = pl.BlockSpec((None, s, pair_dk), lambda b, p: (b, 0, p))
    v_spec = pl.BlockSpec((None, s, pair_dv), lambda b, p: (b, 0, p))

    def gate_specs(direction):
        return (pl.BlockSpec((None, MISC_W, pair_dk), lambda b, p: (layer, 0, direction * pairs + p)),
                pl.BlockSpec((None, 1, pair_dk), lambda b, p: (layer, 0, direction * pairs + p)))

    return pl.pallas_call(
        _gla_kernel,
        grid=(bsz, pairs),
        in_specs=[qk_spec, qk_spec, v_spec,
                  pl.BlockSpec((None, s, MISC_W), lambda b, p: (b, 0, 0)),
                  *gate_specs(0), *gate_specs(1),
                  _layer_spec(layer, (1, GLA_DV))],
        out_specs=v_spec,
        out_shape=jax.ShapeDtypeStruct((bsz, s, GLA_WIDTH), jnp.bfloat16),
        scratch_shapes=[
            pltpu.VMEM((s, pair_dk), jnp.float32),
            pltpu.VMEM((s, pair_dk), jnp.float32),
            pltpu.VMEM((s, 2 * pair_dk), jnp.bfloat16),
            pltpu.VMEM((n_chunks, pair_dv, 2 * pair_dk), jnp.float32),
            pltpu.VMEM((n_chunks, 1, 2 * pair_dk), jnp.float32),
            pltpu.VMEM((n_chunks, pair_dv, 2 * pair_dk), jnp.bfloat16),
            pltpu.VMEM((s, pair_dv), jnp.float32),
        ],
        compiler_params=pltpu.CompilerParams(
            dimension_semantics=("arbitrary", "arbitrary"), vmem_limit_bytes=VMEM_LIMIT),
        name="gla",
    )(gq, gk, gv, misc, wg_cat, bg_cat, wg_cat, bg_cat, gla_norm_g)


def _mla_kernel(mq_ref, mkv_ref, misc_ref, c_ref, sa_ref, sb_ref, gq_ref, gkv_ref,
                wuq_ref, wuk_ref, wuvt_ref, o_ref, ckv_s, q_s, k_s, vt_s, sc_s, top_s):
    s = mkv_ref.shape[0]
    n_tiles = s // Q_TILE
    assert n_tiles % 2 == 0 and MLA_HEADS % 2 == 0
    blocks = [slice(i * KV_ROWS, (i + 1) * KV_ROWS) for i in range(s // KV_ROWS)]
    q_mult = (MLA_NOPE + MLA_ROPE) ** -0.5 * math.log2(math.e)

    for r in blocks:
        tabs = (c_ref[r, :], sa_ref[r, :], sb_ref[r, :])
        ckv_s[r, :] = _rms(mkv_ref[r, :], gkv_ref[...]).astype(jnp.bfloat16)
        k_rope = _rope(misc_ref[r, :], *tabs).astype(jnp.bfloat16)
        for slot in range(2):
            k_s[slot, r, MLA_NOPE:] = k_rope
        cq = _rms(mq_ref[r, :], gq_ref[...]).astype(jnp.bfloat16)
        qf = _bdot(cq, wuq_ref[...])
        cos_t, sin_t = tabs[0], tabs[1] + tabs[2]
        for h in range(MLA_HEADS):
            qh = qf[:, h * QK_W:(h + 1) * QK_W]
            rot = qh[:, MLA_NOPE:]
            q_rope = rot * cos_t + pltpu.roll(rot, MLA_ROPE, axis=1) * sin_t
            q_s[h, r, :] = (jnp.concatenate([qh[:, 0:MLA_NOPE], q_rope], axis=1)
                            * q_mult).astype(jnp.bfloat16)
    for slot in range(2):
        vt_s[slot, MLA_DV:, :] = jnp.ones((vt_s.shape[1] - MLA_DV, s), jnp.bfloat16)

    def build_kv(h, slot):
        for r in blocks:
            ckv = ckv_s[r, :]
            k_s[slot, r, 0:MLA_NOPE] = _bdot(ckv, wuk_ref[h]).astype(jnp.bfloat16)
            vt_s[slot, 0:MLA_DV, r] = lax.dot_general(
                wuvt_ref[h], ckv, _NT, preferred_element_type=jnp.float32).astype(jnp.bfloat16)

    def scores_t(h, slot, j):
        q = q_s[h, j * Q_TILE:(j + 1) * Q_TILE, :]
        sc_t = lax.dot_general(k_s[slot], q, _NT, preferred_element_type=jnp.float32)
        sc_s[j % 2] = sc_t
        top_s[j % 2] = jnp.broadcast_to(jnp.max(sc_t, axis=0, keepdims=True), (F32_ROWS, Q_TILE))

    def finish(h, slot, j):
        p_t = jnp.exp2(sc_s[j % 2] - top_s[j % 2][0:1, :]).astype(jnp.bfloat16)
        o_t = _bdot(vt_s[slot], p_t)
        o_ref[h, j * Q_TILE:(j + 1) * Q_TILE, :] = (
            o_t[0:MLA_DV, :] / o_t[MLA_DV:MLA_DV + 1, :]).T.astype(o_ref.dtype)

    build_kv(0, 0)
    scores_t(0, 0, 0)

    def head_pair(i, carry):
        for slot in range(2):
            h = 2 * i + slot
            h_next = jnp.minimum(h + 1, MLA_HEADS - 1)
            for j in range(n_tiles):
                if j + 1 < n_tiles:
                    scores_t(h, slot, j + 1)
                else:
                    build_kv(h_next, 1 - slot)
                    scores_t(h_next, 1 - slot, 0)
                finish(h, slot, j)
        return carry

    lax.fori_loop(0, MLA_HEADS // 2, head_pair, 0)


def _mla_call(layer, mq, mkv, misc, tabs, gq, gkv, wuq_p, wuk_p, wuvt_p):
    bsz, s, _ = mq.shape

    def per_row(width):
        return pl.BlockSpec((None, s, width), lambda b: (b, 0, 0))

    return pl.pallas_call(
        _mla_kernel,
        grid=(bsz,),
        in_specs=[
            per_row(MLA_Q_LORA), per_row(MLA_KV_LORA), per_row(MISC_W),
            per_row(LANES), per_row(LANES), per_row(LANES),
            _layer_spec(layer, (1, MLA_Q_LORA)),
            _layer_spec(layer, (1, MLA_KV_LORA)),
            _layer_spec(layer, (MLA_Q_LORA, MLA_HEADS * QK_W)),
            _layer_spec(layer, (MLA_HEADS, MLA_KV_LORA, MLA_NOPE)),
            _layer_spec(layer, (MLA_HEADS, MLA_DV, MLA_KV_LORA)),
        ],
        out_specs=pl.BlockSpec((None, MLA_HEADS, s, MLA_DV), lambda b: (b, 0, 0, 0)),
        out_shape=jax.ShapeDtypeStruct((bsz, MLA_HEADS, s, MLA_DV), jnp.bfloat16),
        scratch_shapes=[
            pltpu.VMEM((s, MLA_KV_LORA), jnp.bfloat16),
            pltpu.VMEM((MLA_HEADS, s, QK_W), jnp.bfloat16),
            pltpu.VMEM((2, s, QK_W), jnp.bfloat16),
            pltpu.VMEM((2, MLA_DV + BF16_ROWS, s), jnp.bfloat16),
            pltpu.VMEM((2, s, Q_TILE), jnp.float32),
            pltpu.VMEM((2, F32_ROWS, Q_TILE), jnp.float32),
        ],
        compiler_params=pltpu.CompilerParams(
            dimension_semantics=("arbitrary",),
            vmem_limit_bytes=VMEM_LIMIT),
        name="mla",
    )(mq, mkv, misc, *tabs, gq, gkv, wuq_p, wuk_p, wuvt_p)


def _outproj_kernel(tiles_per_seq, final, x_ref, mod_ref, ogla_ref, omla_ref, cb_ref, cc_ref,
                    cx_ref, ccp_ref, cxp_ref, ccn_ref, cxn_ref, z_ref, mg_ref, cw_ref, cg_ref,
                    fg_ref, w_ref, o_ref):
    f32 = jnp.float32
    j = pl.program_id(0) % tiles_per_seq
    tm = x_ref.shape[0]

    u = cc_ref[...].astype(f32) * cx_ref[...].astype(f32)
    prev_ok = (j > 0).astype(f32)
    next_ok = (j < tiles_per_seq - 1).astype(f32)
    u_prev = (ccp_ref[...].astype(f32) * cxp_ref[...].astype(f32))[HALO_ROWS - 1:, :] * prev_ok
    u_next = (ccn_ref[...].astype(f32) * cxn_ref[...].astype(f32))[0:1, :] * next_ok
    rows = lax.broadcasted_iota(jnp.int32, u.shape, 0)
    up = jnp.where(rows == 0, u_prev, pltpu.roll(u, 1, axis=0))
    un = jnp.where(rows == tm - 1, u_next, pltpu.roll(u, tm - 1, axis=0))
    cw = cw_ref[...]
    conv = up * cw[0:1] + u * cw[1:2] + un * cw[2:3]
    gate = mod_ref[...][2:3]

    def gated_dot(r, lo, y_part):
        hi = lo + y_part.shape[1]
        z = z_ref[r, lo:hi].astype(f32)
        y = (y_part * (z * jax.nn.sigmoid(z))).astype(jnp.bfloat16)
        return _bdot(y, w_ref[lo:hi, :])

    def finish(r, acc):
        out = x_ref[r, :] + gate * acc
        if final:
            out = _rms(out, fg_ref[...])
        o_ref[r, :] = out

    pending = None
    for k in range(tm // OUT_SUB_TILE):
        r = slice(k * OUT_SUB_TILE, (k + 1) * OUT_SUB_TILE)
        acc = gated_dot(r, 0, ogla_ref[r, :].astype(f32))
        if pending is not None:
            finish(*pending)
        omla = jnp.concatenate([omla_ref[h, r, :] for h in range(MLA_HEADS)], axis=1)
        acc += gated_dot(r, GLA_WIDTH, _rms(omla.astype(f32), mg_ref[...]))
        acc += gated_dot(r, GLA_WIDTH + MLA_WIDTH,
                         _rms(cb_ref[r, :].astype(f32) * conv[r, :], cg_ref[...]))
        pending = (r, acc)
    finish(*pending)


def _outproj_call(layer, x2, mod, o_gla, o_mla, cb, cc, cx, z, mla_out_g, conv_w, conv_out_g,
                  final_g, w_out_b, seq_len, final):
    t = x2.shape[0]
    tm = OUT_ROW_TILE
    tiles_per_seq = seq_len // tm
    halo_per_tile = tm // HALO_ROWS
    n_halo = t // HALO_ROWS

    def row_spec(w):
        return pl.BlockSpec((tm, w), lambda i: (i, 0))

    prev_spec = pl.BlockSpec((HALO_ROWS, CONV_CH),
                             lambda i: (jnp.maximum(i * halo_per_tile - 1, 0), 0))
    next_spec = pl.BlockSpec((HALO_ROWS, CONV_CH),
                             lambda i: (jnp.minimum((i + 1) * halo_per_tile, n_halo - 1), 0))

    return pl.pallas_call(
        functools.partial(_outproj_kernel, tiles_per_seq, final),
        grid=(t // tm,),
        in_specs=[
            row_spec(D_MODEL),
            pl.BlockSpec((None, None, 3, D_MODEL), lambda i: (layer, i // tiles_per_seq, 0, 0)),
            row_spec(GLA_WIDTH),
            pl.BlockSpec((None, MLA_HEADS, tm, MLA_DV),
                         lambda i: (i // tiles_per_seq, 0, i % tiles_per_seq, 0)),
            row_spec(CONV_CH), row_spec(CONV_CH), row_spec(CONV_CH),
            prev_spec, prev_spec, next_spec, next_spec,
            row_spec(D_MODEL),
            _layer_spec(layer, (1, MLA_WIDTH)), _layer_spec(layer, (3, CONV_CH)),
            _layer_spec(layer, (1, CONV_CH)),
            pl.BlockSpec((1, D_MODEL), lambda i: (0, 0)),
            pl.BlockSpec((None, D_MODEL, D_MODEL), lambda i: (layer, 0, 0),
                         pipeline_mode=pl.Buffered(1)),
        ],
        out_specs=row_spec(D_MODEL),
        out_shape=jax.ShapeDtypeStruct((t, D_MODEL), jnp.float32),
        compiler_params=pltpu.CompilerParams(
            dimension_semantics=("arbitrary",), vmem_limit_bytes=VMEM_LIMIT),
        name="out_proj",
    )(x2, mod, o_gla, o_mla, cb, cc, cx, cc, cx, cc, cx, z,
      mla_out_g, conv_w, conv_out_g, final_g, w_out_b)


_O_GLR = 2 * GLA_HEADS * GLA_DK + GLA_WIDTH
_O_MQ = _O_GLR + 2 * GLA_GATE_RANK
_O_MKV = _O_MQ + MLA_Q_LORA
_O_MKR = _O_MKV + MLA_KV_LORA
_O_CB = _O_MKR + MLA_ROPE
IN_DIM = _O_CB + 3 * CONV_CH + D_MODEL
W_PREP_ROWS = 256


W_PREP_CHUNK = 512


def _w_in_prep_kernel(wt_ref, o_ref):
    rows = o_ref.shape[0]
    eye = (lax.broadcasted_iota(jnp.int32, (rows, rows), 0)
           == lax.broadcasted_iota(jnp.int32, (rows, rows), 1)).astype(jnp.bfloat16)

    def put(dst, block):
        o_ref[:, dst:dst + block.shape[0]] = lax.dot_general(
            eye, block.astype(jnp.bfloat16), _NT,
            preferred_element_type=jnp.float32).astype(jnp.bfloat16)

    def put_range(dst, lo, hi):
        for start in range(lo, hi, W_PREP_CHUNK):
            stop = min(start + W_PREP_CHUNK, hi)
            put(dst + start - lo, wt_ref[start:stop, :])

    pad_rows = MISC_W - MLA_ROPE - 2 * GLA_GATE_RANK
    misc = jnp.concatenate([wt_ref[_O_MKR:_O_CB, :], wt_ref[_O_GLR:_O_MQ, :],
                            jnp.zeros((pad_rows, rows), jnp.float32)], axis=0)
    off = 0
    for src in ((0, _O_GLR), (_O_MQ, _O_MKV), misc, (_O_MKV, _O_MKR), (_O_CB, IN_DIM)):
        if isinstance(src, tuple):
            put_range(off, *src)
            off += src[1] - src[0]
        else:
            put(off, src)
            off += src.shape[0]


def _w_in_prep_call(w_in):
    depth, d, n = w_in.shape
    return pl.pallas_call(
        _w_in_prep_kernel,
        grid=(depth, d // W_PREP_ROWS),
        in_specs=[pl.BlockSpec((None, n, W_PREP_ROWS), lambda l, i: (l, 0, i))],
        out_specs=pl.BlockSpec((None, W_PREP_ROWS, IN_DIM_P), lambda l, i: (l, i, 0)),
        out_shape=jax.ShapeDtypeStruct((depth, d, IN_DIM_P), jnp.bfloat16),
        compiler_params=pltpu.CompilerParams(
            dimension_semantics=("arbitrary", "arbitrary"), vmem_limit_bytes=VMEM_LIMIT),
        name="w_in_prep",
    )(jnp.swapaxes(w_in, 1, 2))


def _gate_params(wg_f, bg_f, wg_b, bg_b):
    depth, rank, hk = wg_f.shape
    zeros = jnp.zeros((depth, rank, hk), jnp.float32)
    rows = jnp.concatenate([jnp.concatenate([wg_f, zeros], axis=2),
                            jnp.concatenate([zeros, wg_b], axis=2)], axis=1)
    wg_cat = jnp.pad(rows, ((0, 0), (MLA_ROPE, MISC_W - MLA_ROPE - 2 * rank), (0, 0)))
    bg_cat = jnp.concatenate([bg_f, bg_b], axis=1)[:, None, :]
    return wg_cat.astype(jnp.bfloat16), bg_cat


def _permute_w_uq(w):
    depth = w.shape[0]
    half = MLA_ROPE // 2
    w4 = w.reshape(depth, MLA_Q_LORA, MLA_HEADS, MLA_NOPE + MLA_ROPE)
    nope, x1, x2 = w4[..., :MLA_NOPE], w4[..., MLA_NOPE:MLA_NOPE + half], w4[..., MLA_NOPE + half:]
    cols = jnp.concatenate([nope, x1, x2, x2, x1], axis=-1)
    return cols.reshape(depth, MLA_Q_LORA, MLA_HEADS * QK_W).astype(jnp.bfloat16)


def _split_w_ukv(w):
    depth = w.shape[0]
    w4 = w.reshape(depth, MLA_KV_LORA, MLA_HEADS, MLA_NOPE + MLA_DV)
    wk = w4[..., :MLA_NOPE].transpose(0, 2, 1, 3)
    wv_t = w4[..., MLA_NOPE:].transpose(0, 2, 3, 1)
    return wk.astype(jnp.bfloat16), wv_t.astype(jnp.bfloat16)


def kernel(x, c, positions, ada_w, ada_b, norm_g, w_in, gla_wg_f, gla_bg_f, gla_wg_b, gla_bg_b,
           gla_norm_g, mla_q_norm_g, mla_kv_norm_g, mla_w_uq, mla_w_ukv, mla_out_g, conv_w,
           conv_out_g, w_out, final_g):
    bsz, s, d = x.shape
    t = bsz * s

    c_pad = jnp.pad(c, ((0, F32_ROWS - bsz), (0, 0)))
    mod = _ada_call(c_pad, ada_w, ada_b)[:, :bsz].reshape(DEPTH, bsz, 3, d)
    tabs = _rope_tab_call(positions)

    w_in_p = _w_in_prep_call(w_in)
    w_out_b = w_out.astype(jnp.bfloat16)
    wg_cat, bg_cat = _gate_params(gla_wg_f, gla_bg_f, gla_wg_b, gla_bg_b)
    wuq_p = _permute_w_uq(mla_w_uq)
    wuk_p, wuvt_p = _split_w_ukv(mla_w_ukv)

    def row(p):
        return p[:, None, :]

    def b3(a):
        return a.reshape(bsz, s, a.shape[-1])

    h = x.reshape(t, d)
    for l in range(DEPTH):
        segs = _inproj_call(l, h, mod, row(norm_g), w_in_p, s)
        gq, gk, gv, mq, misc, mkv, cb, cc, cx, z = segs
        o_gla = _gla_call(l, b3(gq), b3(gk), b3(gv), b3(misc), wg_cat, bg_cat, row(gla_norm_g))
        o_mla = _mla_call(l, b3(mq), b3(mkv), b3(misc), tabs, row(mla_q_norm_g),
                          row(mla_kv_norm_g), wuq_p, wuk_p, wuvt_p)
        h = _outproj_call(l, h, mod, o_gla.reshape(t, -1), o_mla, cb, cc, cx, z,
                          row(mla_out_g), conv_w, row(conv_out_g), final_g.reshape(1, d),
                          w_out_b, s, l == DEPTH - 1)
    return h.reshape(bsz, s, d)
```

```python
import functools
import math

import jax
import jax.numpy as jnp
from jax import lax
from jax.experimental import pallas as pl
from jax.experimental.pallas import tpu as pltpu

D_MODEL = 2048
DEPTH = 2
GLA_HEADS = 6
GLA_DK = 64
GLA_DV = 128
GLA_GATE_RANK = 16
GLA_GATE_TEMP = 16.0
GLA_CHUNK = 64
GLA_WIDTH = GLA_HEADS * GLA_DV
MLA_HEADS = 6
MLA_Q_LORA = 384
MLA_KV_LORA = 256
MLA_NOPE = 128
MLA_ROPE = 64
MLA_DV = 128
MLA_WIDTH = MLA_HEADS * MLA_DV
ROPE_THETA = 10000.0
CONV_CH = D_MODEL - GLA_WIDTH - MLA_WIDTH
EPS = 1e-6

LANES = 128
F32_ROWS = 8
BF16_ROWS = 16
HALO_ROWS = BF16_ROWS
MISC_W = LANES
QK_W = 2 * LANES

IN_SEGMENTS = (
    ("gq", GLA_HEADS * GLA_DK, jnp.bfloat16),
    ("gk", GLA_HEADS * GLA_DK, jnp.bfloat16),
    ("gv", GLA_WIDTH, jnp.bfloat16),
    ("mq", MLA_Q_LORA, jnp.bfloat16),
    ("misc", MISC_W, jnp.float32),
    ("mkv", MLA_KV_LORA, jnp.bfloat16),
    ("cb", CONV_CH, jnp.bfloat16),
    ("cc", CONV_CH, jnp.bfloat16),
    ("cx", CONV_CH, jnp.bfloat16),
    ("z", D_MODEL, jnp.bfloat16),
)
IN_DIM_P = sum(w for _, w, _ in IN_SEGMENTS)

IN_ROW_TILE = 512
OUT_ROW_TILE = 512
OUT_SUB_TILE = 256
Q_TILE = 512
KV_ROWS = 512
SCAN_ROWS = 256
SCAN_GROUP = 4
GLA_GROUP = 8
VMEM_LIMIT = 56 * 1024 * 1024

_NT = (((1,), (1,)), ((), ()))
_TN = (((0,), (0,)), ((), ()))


def _rms(x, g):
    ms = jnp.mean(x * x, axis=-1, keepdims=True)
    return x * lax.rsqrt(ms + EPS) * g


def _bdot(a, b):
    return jnp.dot(a, b, preferred_element_type=jnp.float32)


ADA_COLS = 1024
ADA_KROWS = 256


def _ada_kernel(ct_ref, w_ref, b_ref, o_ref, cb_s):
    d, bsz = ct_ref.shape
    tn = w_ref.shape[1]

    @pl.when((pl.program_id(0) == 0) & (pl.program_id(1) == 0))
    def _broadcast_c():
        ct = ct_ref[...]
        act = ct * jax.nn.sigmoid(ct)
        for b in range(bsz):
            cb_s[b] = jnp.broadcast_to(act[:, b:b + 1], (d, LANES))

    def body(i, accs):
        accs = list(accs)
        for g in range(ADA_KROWS // F32_ROWS):
            r = pl.ds(pl.multiple_of(i * ADA_KROWS + g * F32_ROWS, F32_ROWS), F32_ROWS)
            w = w_ref[r, :]
            for b in range(bsz):
                accs[b] = accs[b] + w * jnp.concatenate([cb_s[b, r, :]] * (tn // LANES), axis=1)
        return tuple(accs)

    zero = jnp.zeros((F32_ROWS, tn), jnp.float32)
    accs = lax.fori_loop(0, d // ADA_KROWS, body, (zero,) * bsz)
    rows = [jnp.sum(a, axis=0, keepdims=True) for a in accs]
    rows.append(jnp.zeros((o_ref.shape[0] - bsz, tn), jnp.float32))
    o_ref[...] = jnp.concatenate(rows, axis=0) + b_ref[...]


def _ada_call(c, ada_w, ada_b):
    bsz = c.shape[0]
    n = ada_w.shape[-1]
    tn = ADA_COLS
    return pl.pallas_call(
        _ada_kernel,
        grid=(DEPTH, n // tn),
        in_specs=[
            pl.BlockSpec((D_MODEL, bsz), lambda l, j: (0, 0)),
            pl.BlockSpec((None, D_MODEL, tn), lambda l, j: (l, 0, j)),
            pl.BlockSpec((None, 1, tn), lambda l, j: (l, 0, j)),
        ],
        out_specs=pl.BlockSpec((None, F32_ROWS, tn), lambda l, j: (l, 0, j)),
        out_shape=jax.ShapeDtypeStruct((DEPTH, F32_ROWS, n), jnp.float32),
        scratch_shapes=[pltpu.VMEM((bsz, D_MODEL, LANES), jnp.float32)],
        compiler_params=pltpu.CompilerParams(
            dimension_semantics=("arbitrary", "arbitrary"), vmem_limit_bytes=VMEM_LIMIT),
        name="ada_mod",
    )(c.T, ada_w, ada_b.reshape(DEPTH, 1, n))


def _rope_tab_kernel(pos_ref, invf_ref, c_ref, sa_ref, sb_ref):
    ang = pos_ref[...].astype(jnp.float32) * invf_ref[...]
    cos, sin = jnp.cos(ang), jnp.sin(ang)
    lane = lax.broadcasted_iota(jnp.int32, ang.shape, 1)
    half = MLA_ROPE // 2
    c_ref[...] = jnp.where(lane < MLA_ROPE, cos, 0.0)
    sa_ref[...] = jnp.where(lane < half, -sin, 0.0)
    sb_ref[...] = jnp.where((lane >= half) & (lane < MLA_ROPE), sin, 0.0)


def _rope_tab_call(positions):
    bsz, s = positions.shape
    half = MLA_ROPE // 2
    inv_freq = ROPE_THETA ** (-jnp.arange(0, MLA_ROPE, 2, dtype=jnp.float32) / MLA_ROPE)
    invf = jnp.concatenate([inv_freq, inv_freq, jnp.zeros((LANES - 2 * half,), jnp.float32)])
    tab = jax.ShapeDtypeStruct((bsz, s, LANES), jnp.float32)
    spec = pl.BlockSpec((None, s, LANES), lambda b: (b, 0, 0))
    return pl.pallas_call(
        _rope_tab_kernel,
        grid=(bsz,),
        in_specs=[pl.BlockSpec((None, s, 1), lambda b: (b, 0, 0)),
                  pl.BlockSpec((1, LANES), lambda b: (0, 0))],
        out_specs=[spec, spec, spec],
        out_shape=[tab, tab, tab],
        compiler_params=pltpu.CompilerParams(dimension_semantics=("arbitrary",)),
        name="rope_tables",
    )(positions.reshape(bsz, s, 1), invf.reshape(1, LANES))


def _rope(x, c, sa, sb):
    return (x * c + pltpu.roll(x, LANES - MLA_ROPE // 2, axis=1) * sa
            + pltpu.roll(x, MLA_ROPE // 2, axis=1) * sb)


def _inproj_kernel(x_ref, mod_ref, g_ref, w_ref, *out_refs):
    x = x_ref[...]
    mod = mod_ref[...]
    shift, scale = mod[0:1], mod[1:2]
    ms = jnp.mean(x * x, axis=-1, keepdims=True)
    h = (x * lax.rsqrt(ms + EPS) * g_ref[...]) * (1.0 + scale) + shift
    hb = h.astype(jnp.bfloat16)
    off = 0
    for (_, width, _), o_ref in zip(IN_SEGMENTS, out_refs):
        o_ref[...] = _bdot(hb, w_ref[:, off:off + width]).astype(o_ref.dtype)
        off += width


def _layer_spec(layer, shape):
    return pl.BlockSpec((None,) + shape, lambda *_: (layer,) + (0,) * len(shape))


def _inproj_call(layer, x2, mod, norm_g, w_in_p, seq_len):
    t = x2.shape[0]
    tm = IN_ROW_TILE
    tiles_per_seq = seq_len // tm
    out_shape = [jax.ShapeDtypeStruct((t, w), dt) for _, w, dt in IN_SEGMENTS]
    out_specs = [pl.BlockSpec((tm, w), lambda i: (i, 0)) for _, w, _ in IN_SEGMENTS]
    return pl.pallas_call(
        _inproj_kernel,
        grid=(t // tm,),
        in_specs=[
            pl.BlockSpec((tm, D_MODEL), lambda i: (i, 0)),
            pl.BlockSpec((None, None, 3, D_MODEL), lambda i: (layer, i // tiles_per_seq, 0, 0)),
            _layer_spec(layer, (1, D_MODEL)),
            pl.BlockSpec((None, D_MODEL, IN_DIM_P), lambda i: (layer, 0, 0),
                         pipeline_mode=pl.Buffered(1)),
        ],
        out_specs=out_specs,
        out_shape=out_shape,
        compiler_params=pltpu.CompilerParams(
            dimension_semantics=("arbitrary",), vmem_limit_bytes=VMEM_LIMIT),
        name="in_proj",
    )(x2, mod, norm_g, w_in_p)


def _log_sigmoid(x):
    return jnp.minimum(x, 0.0) - jnp.log(1.0 + jnp.exp(-jnp.abs(x)))


def _gla_kernel(q_ref, k_ref, v_ref, misc_ref, wgf_ref, bgf_ref, wgb_ref, bgb_ref, g_ref,
                o_ref, bf_s, bb_s, qcat_s, kv_s, dec_s, st_s, intra_s):
    s = q_ref.shape[0]
    cdim = GLA_CHUNK
    n_chunks = s // cdim
    pair_dk = 2 * GLA_DK
    pair_dv = 2 * GLA_DV
    inv_t = 1.0 / GLA_GATE_TEMP
    q_scale = GLA_DK ** -0.5

    t_row = lax.broadcasted_iota(jnp.int32, (SCAN_ROWS, SCAN_ROWS), 0)
    t_col = lax.broadcasted_iota(jnp.int32, (SCAN_ROWS, SCAN_ROWS), 1)
    same_chunk = (t_row // cdim) == (t_col // cdim)
    tri_prefix = (same_chunk & (t_col <= t_row)).astype(jnp.bfloat16)
    tri_suffix = (same_chunk & (t_col >= t_row)).astype(jnp.bfloat16)

    def split(terms):
        high = terms.astype(jnp.bfloat16)
        rest = (terms - high.astype(jnp.float32)).astype(jnp.bfloat16)
        return jnp.concatenate([high, rest], axis=1)

    def scan_body(i, carry):
        blocks = [pl.ds(pl.multiple_of((i * SCAN_GROUP + c) * SCAN_ROWS, SCAN_ROWS), SCAN_ROWS)
                  for c in range(SCAN_GROUP)]
        pre = []
        for r in blocks:
            m = misc_ref[r, :].astype(jnp.bfloat16)
            pre.append((_bdot(m, wgf_ref[...]), _bdot(m, wgb_ref[...])))
        terms = [(split(_log_sigmoid(pf + bgf_ref[...]) * inv_t),
                  split(_log_sigmoid(pb + bgb_ref[...]) * inv_t)) for pf, pb in pre]
        sums = [(_bdot(tri_prefix, tf), _bdot(tri_suffix, tb)) for tf, tb in terms]
        for r, (sf, sb) in zip(blocks, sums):
            bf_s[r, :] = sf[:, 0:pair_dk] + sf[:, pair_dk:]
            bb_s[r, :] = sb[:, 0:pair_dk] + sb[:, pair_dk:]
        return carry

    lax.fori_loop(0, s // (SCAN_ROWS * SCAN_GROUP), scan_body, 0)

    row = lax.broadcasted_iota(jnp.int32, (cdim, pair_dk), 0)
    key_pos = lax.broadcasted_iota(jnp.int32, (cdim, pair_dk), 1) & (cdim - 1)
    keep_f = key_pos <= row
    keep_b = key_pos > row
    cat_block = lax.broadcasted_iota(jnp.int32, (cdim, 2 * pair_dk), 1) // GLA_DK
    v_head = lax.broadcasted_iota(jnp.int32, (cdim, pair_dv), 1) // GLA_DV
    same_head = (lax.broadcasted_iota(jnp.int32, (pair_dv, 2 * pair_dk), 0) // GLA_DV
                 == (lax.broadcasted_iota(jnp.int32, (pair_dv, 2 * pair_dk), 1) // GLA_DK) % 2)

    def intra_body(i, carry):
        chunks = [i * GLA_GROUP + c for c in range(GLA_GROUP)]
        rows = [pl.ds(pl.multiple_of(n * cdim, cdim), cdim) for n in chunks]
        scores, k_ends = [], []
        for n, r in zip(chunks, rows):
            bf = bf_s[r, :]
            bb = bb_s[r, :]
            bf_last = bf[cdim - 1:cdim, :]
            bb_last = bb[0:1, :]
            q = q_ref[r, :].astype(jnp.float32) * q_scale
            k = k_ref[r, :].astype(jnp.float32)
            q_cat = jnp.concatenate([q * jnp.exp(bf), q * jnp.exp(bb)],
                                    axis=1).astype(jnp.bfloat16)
            k_inv = jnp.concatenate([k * jnp.exp(-bf), k * jnp.exp(-bb)], axis=1)
            k_ends.append(jnp.concatenate(
                [k * jnp.exp(bf_last - bf), k * jnp.exp(bb_last - bb)], axis=1).astype(jnp.bfloat16))
            dec_s[n] = jnp.concatenate([jnp.exp(bf_last), jnp.exp(bb_last)], axis=1)
            qcat_s[r, :] = q_cat
            k_bd = jnp.concatenate([jnp.where(cat_block == c, k_inv, 0.0) for c in range(4)],
                                   axis=0).astype(jnp.bfloat16)
            scores.append(lax.dot_general(q_cat, k_bd, _NT,
                                          preferred_element_type=jnp.float32))
        for n, r, k_end in zip(chunks, rows, k_ends):
            kv_t = lax.dot_general(v_ref[r, :], k_end, _TN,
                                   preferred_element_type=jnp.float32)
            kv_s[n] = jnp.where(same_head, kv_t, 0.0)
        for r, sc in zip(rows, scores):
            p = (jnp.where(keep_f, sc[:, 0:pair_dk], 0.0)
                 + jnp.where(keep_b, sc[:, pair_dk:], 0.0)).astype(jnp.bfloat16)
            vv = v_ref[r, :]
            v_bd = jnp.concatenate(
                [jnp.where(v_head == j, vv, jnp.zeros_like(vv)) for j in range(2)], axis=0)
            intra_s[r, :] = _bdot(p, v_bd)
        return carry

    lax.fori_loop(0, n_chunks // GLA_GROUP, intra_body, 0)

    def state_body(i, states):
        sf, sb = states
        nf = i
        nb = n_chunks - 1 - i
        st_s[nf, :, 0:pair_dk] = sf.astype(jnp.bfloat16)
        st_s[nb, :, pair_dk:] = sb.astype(jnp.bfloat16)
        sf = sf * dec_s[nf][:, 0:pair_dk] + kv_s[nf, :, 0:pair_dk]
        sb = sb * dec_s[nb][:, pair_dk:] + kv_s[nb, :, pair_dk:]
        return sf, sb

    zero = jnp.zeros((pair_dv, pair_dk), jnp.float32)
    lax.fori_loop(0, n_chunks, state_body, (zero, zero))

    g = g_ref[...]

    def out_body(i, carry):
        chunks = [i * GLA_GROUP + c for c in range(GLA_GROUP)]
        rows = [pl.ds(pl.multiple_of(n * cdim, cdim), cdim) for n in chunks]
        inter = [lax.dot_general(qcat_s[r, :], st_s[n], _NT, preferred_element_type=jnp.float32)
                 for n, r in zip(chunks, rows)]
        for r, o_inter in zip(rows, inter):
            o = intra_s[r, :] + o_inter
            for j in range(2):
                o_ref[r, j * GLA_DV:(j + 1) * GLA_DV] = _rms(
                    o[:, j * GLA_DV:(j + 1) * GLA_DV], g).astype(o_ref.dtype)
        return carry

    lax.fori_loop(0, n_chunks // GLA_GROUP, out_body, 0)


def _gla_call(layer, gq, gk, gv, misc, wg_cat, bg_cat, gla_norm_g):
    bsz, s, _ = gq.shape
    pairs = GLA_HEADS // 2
    n_chunks = s // GLA_CHUNK
    pair_dk, pair_dv = 2 * GLA_DK, 2 * GLA_DV
    qk_spec = pl.BlockSpec((None, s, pair_dk), lambda b, p: (b, 0, p))
    v_spec = pl.BlockSpec((None, s, pair_dv), lambda b, p: (b, 0, p))

    def gate_specs(direction):
        return (pl.BlockSpec((None, MISC_W, pair_dk), lambda b, p: (layer, 0, direction * pairs + p)),
                pl.BlockSpec((None, 1, pair_dk), lambda b, p: (layer, 0, direction * pairs + p)))

    return pl.pallas_call(
        _gla_kernel,
        grid=(bsz, pairs),
        in_specs=[qk_spec, qk_spec, v_spec,
                  pl.BlockSpec((None, s, MISC_W), lambda b, p: (b, 0, 0)),
                  *gate_specs(0), *gate_specs(1),
                  _layer_spec(layer, (1, GLA_DV))],
        out_specs=v_spec,
        out_shape=jax.ShapeDtypeStruct((bsz, s, GLA_WIDTH), jnp.bfloat16),
        scratch_shapes=[
            pltpu.VMEM((s, pair_dk), jnp.float32),
            pltpu.VMEM((s, pair_dk), jnp.float32),
            pltpu.VMEM((s, 2 * pair_dk), jnp.bfloat16),
            pltpu.VMEM((n_chunks, pair_dv, 2 * pair_dk), jnp.float32),
            pltpu.VMEM((n_chunks, 1, 2 * pair_dk), jnp.float32),
            pltpu.VMEM((n_chunks, pair_dv, 2 * pair_dk), jnp.bfloat16),
            pltpu.VMEM((s, pair_dv), jnp.float32),
        ],
        compiler_params=pltpu.CompilerParams(
            dimension_semantics=("arbitrary", "arbitrary"), vmem_limit_bytes=VMEM_LIMIT),
        name="gla",
    )(gq, gk, gv, misc, wg_cat, bg_cat, wg_cat, bg_cat, gla_norm_g)


def _mla_kernel(mq_ref, mkv_ref, misc_ref, c_ref, sa_ref, sb_ref, gq_ref, gkv_ref,
                wuq_ref, wuk_ref, wuvt_ref, o_ref, ckv_s, q_s, k_s, vt_s, sc_s, top_s):
    s = mkv_ref.shape[0]
    n_tiles = s // Q_TILE
    assert n_tiles % 2 == 0 and MLA_HEADS % 2 == 0
    blocks = [slice(i * KV_ROWS, (i + 1) * KV_ROWS) for i in range(s // KV_ROWS)]
    q_mult = (MLA_NOPE + MLA_ROPE) ** -0.5 * math.log2(math.e)

    for r in blocks:
        tabs = (c_ref[r, :], sa_ref[r, :], sb_ref[r, :])
        ckv_s[r, :] = _rms(mkv_ref[r, :].astype(jnp.float32), gkv_ref[...]).astype(jnp.bfloat16)
        k_rope = _rope(misc_ref[r, :], *tabs).astype(jnp.bfloat16)
        for slot in range(2):
            k_s[slot, r, MLA_NOPE:] = k_rope
        cq = _rms(mq_ref[r, :].astype(jnp.float32), gq_ref[...]).astype(jnp.bfloat16)
        qf = _bdot(cq, wuq_ref[...])
        cos_t, sin_t = tabs[0], tabs[1] + tabs[2]
        for h in range(MLA_HEADS):
            qh = qf[:, h * QK_W:(h + 1) * QK_W]
            rot = qh[:, MLA_NOPE:]
            q_rope = rot * cos_t + pltpu.roll(rot, MLA_ROPE, axis=1) * sin_t
            q_s[h, r, :] = (jnp.concatenate([qh[:, 0:MLA_NOPE], q_rope], axis=1)
                            * q_mult).astype(jnp.bfloat16)
    for slot in range(2):
        vt_s[slot, MLA_DV:, :] = jnp.ones((vt_s.shape[1] - MLA_DV, s), jnp.bfloat16)

    def build_kv(h, slot):
        for r in blocks:
            ckv = ckv_s[r, :]
            k_s[slot, r, 0:MLA_NOPE] = _bdot(ckv, wuk_ref[h]).astype(jnp.bfloat16)
            vt_s[slot, 0:MLA_DV, r] = lax.dot_general(
                wuvt_ref[h], ckv, _NT, preferred_element_type=jnp.float32).astype(jnp.bfloat16)

    def scores_t(h, slot, j):
        q = q_s[h, j * Q_TILE:(j + 1) * Q_TILE, :]
        sc_t = lax.dot_general(k_s[slot], q, _NT, preferred_element_type=jnp.float32)
        sc_s[j % 2] = sc_t
        top_s[j % 2] = jnp.broadcast_to(jnp.max(sc_t, axis=0, keepdims=True), (F32_ROWS, Q_TILE))

    def finish(h, slot, j):
        p_t = jnp.exp2(sc_s[j % 2] - top_s[j % 2][0:1, :]).astype(jnp.bfloat16)
        o_t = _bdot(vt_s[slot], p_t)
        o_ref[h, j * Q_TILE:(j + 1) * Q_TILE, :] = (
            o_t[0:MLA_DV, :] / o_t[MLA_DV:MLA_DV + 1, :]).T.astype(o_ref.dtype)

    build_kv(0, 0)
    scores_t(0, 0, 0)

    def head_pair(i, carry):
        for slot in range(2):
            h = 2 * i + slot
            h_next = jnp.minimum(h + 1, MLA_HEADS - 1)
            for j in range(n_tiles):
                if j + 1 < n_tiles:
                    scores_t(h, slot, j + 1)
                else:
                    build_kv(h_next, 1 - slot)
                    scores_t(h_next, 1 - slot, 0)
                finish(h, slot, j)
        return carry

    lax.fori_loop(0, MLA_HEADS // 2, head_pair, 0)


def _mla_call(layer, mq, mkv, misc, tabs, gq, gkv, wuq_p, wuk_p, wuvt_p):
    bsz, s, _ = mq.shape

    def per_row(width):
        return pl.BlockSpec((None, s, width), lambda b: (b, 0, 0))

    return pl.pallas_call(
        _mla_kernel,
        grid=(bsz,),
        in_specs=[
            per_row(MLA_Q_LORA), per_row(MLA_KV_LORA), per_row(MISC_W),
            per_row(LANES), per_row(LANES), per_row(LANES),
            _layer_spec(layer, (1, MLA_Q_LORA)),
            _layer_spec(layer, (1, MLA_KV_LORA)),
            _layer_spec(layer, (MLA_Q_LORA, MLA_HEADS * QK_W)),
            _layer_spec(layer, (MLA_HEADS, MLA_KV_LORA, MLA_NOPE)),
            _layer_spec(layer, (MLA_HEADS, MLA_DV, MLA_KV_LORA)),
        ],
        out_specs=pl.BlockSpec((None, MLA_HEADS, s, MLA_DV), lambda b: (b, 0, 0, 0)),
        out_shape=jax.ShapeDtypeStruct((bsz, MLA_HEADS, s, MLA_DV), jnp.bfloat16),
        scratch_shapes=[
            pltpu.VMEM((s, MLA_KV_LORA), jnp.bfloat16),
            pltpu.VMEM((MLA_HEADS, s, QK_W), jnp.bfloat16),
            pltpu.VMEM((2, s, QK_W), jnp.bfloat16),
            pltpu.VMEM((2, MLA_DV + BF16_ROWS, s), jnp.bfloat16),
            pltpu.VMEM((2, s, Q_TILE), jnp.float32),
            pltpu.VMEM((2, F32_ROWS, Q_TILE), jnp.float32),
        ],
        compiler_params=pltpu.CompilerParams(
            dimension_semantics=("arbitrary",),
            vmem_limit_bytes=VMEM_LIMIT),
        name="mla",
    )(mq, mkv, misc, *tabs, gq, gkv, wuq_p, wuk_p, wuvt_p)


def _outproj_kernel(tiles_per_seq, final, x_ref, mod_ref, ogla_ref, omla_ref, cb_ref, cc_ref,
                    cx_ref, ccp_ref, cxp_ref, ccn_ref, cxn_ref, z_ref, mg_ref, cw_ref, cg_ref,
                    fg_ref, w_ref, o_ref):
    f32 = jnp.float32
    j = pl.program_id(0) % tiles_per_seq
    tm = x_ref.shape[0]

    u = cc_ref[...].astype(f32) * cx_ref[...].astype(f32)
    prev_ok = (j > 0).astype(f32)
    next_ok = (j < tiles_per_seq - 1).astype(f32)
    u_prev = (ccp_ref[...].astype(f32) * cxp_ref[...].astype(f32))[HALO_ROWS - 1:, :] * prev_ok
    u_next = (ccn_ref[...].astype(f32) * cxn_ref[...].astype(f32))[0:1, :] * next_ok
    rows = lax.broadcasted_iota(jnp.int32, u.shape, 0)
    up = jnp.where(rows == 0, u_prev, pltpu.roll(u, 1, axis=0))
    un = jnp.where(rows == tm - 1, u_next, pltpu.roll(u, tm - 1, axis=0))
    cw = cw_ref[...]
    conv = up * cw[0:1] + u * cw[1:2] + un * cw[2:3]
    gate = mod_ref[...][2:3]

    def gated_dot(r, lo, y_part):
        hi = lo + y_part.shape[1]
        z = z_ref[r, lo:hi].astype(f32)
        y = (y_part * (z * jax.nn.sigmoid(z))).astype(jnp.bfloat16)
        return _bdot(y, w_ref[lo:hi, :])

    def finish(r, acc):
        out = x_ref[r, :] + gate * acc
        if final:
            out = _rms(out, fg_ref[...])
        o_ref[r, :] = out

    pending = None
    for k in range(tm // OUT_SUB_TILE):
        r = slice(k * OUT_SUB_TILE, (k + 1) * OUT_SUB_TILE)
        acc = gated_dot(r, 0, ogla_ref[r, :].astype(f32))
        if pending is not None:
            finish(*pending)
        omla = jnp.concatenate([omla_ref[h, r, :] for h in range(MLA_HEADS)], axis=1)
        acc += gated_dot(r, GLA_WIDTH, _rms(omla.astype(f32), mg_ref[...]))
        acc += gated_dot(r, GLA_WIDTH + MLA_WIDTH,
                         _rms(cb_ref[r, :].astype(f32) * conv[r, :], cg_ref[...]))
        pending = (r, acc)
    finish(*pending)


def _outproj_call(layer, x2, mod, o_gla, o_mla, cb, cc, cx, z, mla_out_g, conv_w, conv_out_g,
                  final_g, w_out_b, seq_len, final):
    t = x2.shape[0]
    tm = OUT_ROW_TILE
    tiles_per_seq = seq_len // tm
    halo_per_tile = tm // HALO_ROWS
    n_halo = t // HALO_ROWS

    def row_spec(w):
        return pl.BlockSpec((tm, w), lambda i: (i, 0))

    prev_spec = pl.BlockSpec((HALO_ROWS, CONV_CH),
                             lambda i: (jnp.maximum(i * halo_per_tile - 1, 0), 0))
    next_spec = pl.BlockSpec((HALO_ROWS, CONV_CH),
                             lambda i: (jnp.minimum((i + 1) * halo_per_tile, n_halo - 1), 0))

    return pl.pallas_call(
        functools.partial(_outproj_kernel, tiles_per_seq, final),
        grid=(t // tm,),
        in_specs=[
            row_spec(D_MODEL),
            pl.BlockSpec((None, None, 3, D_MODEL), lambda i: (layer, i // tiles_per_seq, 0, 0)),
            row_spec(GLA_WIDTH),
            pl.BlockSpec((None, MLA_HEADS, tm, MLA_DV),
                         lambda i: (i // tiles_per_seq, 0, i % tiles_per_seq, 0)),
            row_spec(CONV_CH), row_spec(CONV_CH), row_spec(CONV_CH),
            prev_spec, prev_spec, next_spec, next_spec,
            row_spec(D_MODEL),
            _layer_spec(layer, (1, MLA_WIDTH)), _layer_spec(layer, (3, CONV_CH)),
            _layer_spec(layer, (1, CONV_CH)),
            pl.BlockSpec((1, D_MODEL), lambda i: (0, 0)),
            pl.BlockSpec((None, D_MODEL, D_MODEL), lambda i: (layer, 0, 0),
                         pipeline_mode=pl.Buffered(1)),
        ],
        out_specs=row_spec(D_MODEL),
        out_shape=jax.ShapeDtypeStruct((t, D_MODEL), jnp.float32),
        compiler_params=pltpu.CompilerParams(
            dimension_semantics=("arbitrary",), vmem_limit_bytes=VMEM_LIMIT),
        name="out_proj",
    )(x2, mod, o_gla, o_mla, cb, cc, cx, cc, cx, cc, cx, z,
      mla_out_g, conv_w, conv_out_g, final_g, w_out_b)


_O_GLR = 2 * GLA_HEADS * GLA_DK + GLA_WIDTH
_O_MQ = _O_GLR + 2 * GLA_GATE_RANK
_O_MKV = _O_MQ + MLA_Q_LORA
_O_MKR = _O_MKV + MLA_KV_LORA
_O_CB = _O_MKR + MLA_ROPE
IN_DIM = _O_CB + 3 * CONV_CH + D_MODEL
W_PREP_ROWS = 256


W_PREP_CHUNK = 512


def _w_in_prep_kernel(wt_ref, o_ref):
    rows = o_ref.shape[0]
    eye = (lax.broadcasted_iota(jnp.int32, (rows, rows), 0)
           == lax.broadcasted_iota(jnp.int32, (rows, rows), 1)).astype(jnp.bfloat16)

    def put(dst, block):
        o_ref[:, dst:dst + block.shape[0]] = lax.dot_general(
            eye, block.astype(jnp.bfloat16), _NT,
            preferred_element_type=jnp.float32).astype(jnp.bfloat16)

    def put_range(dst, lo, hi):
        for start in range(lo, hi, W_PREP_CHUNK):
            stop = min(start + W_PREP_CHUNK, hi)
            put(dst + start - lo, wt_ref[start:stop, :])

    pad_rows = MISC_W - MLA_ROPE - 2 * GLA_GATE_RANK
    misc = jnp.concatenate([wt_ref[_O_MKR:_O_CB, :], wt_ref[_O_GLR:_O_MQ, :],
                            jnp.zeros((pad_rows, rows), jnp.float32)], axis=0)
    off = 0
    for src in ((0, _O_GLR), (_O_MQ, _O_MKV), misc, (_O_MKV, _O_MKR), (_O_CB, IN_DIM)):
        if isinstance(src, tuple):
            put_range(off, *src)
            off += src[1] - src[0]
        else:
            put(off, src)
            off += src.shape[0]


def _w_in_prep_call(w_in):
    depth, d, n = w_in.shape
    return pl.pallas_call(
        _w_in_prep_kernel,
        grid=(depth, d // W_PREP_ROWS),
        in_specs=[pl.BlockSpec((None, n, W_PREP_ROWS), lambda l, i: (l, 0, i))],
        out_specs=pl.BlockSpec((None, W_PREP_ROWS, IN_DIM_P), lambda l, i: (l, i, 0)),
        out_shape=jax.ShapeDtypeStruct((depth, d, IN_DIM_P), jnp.bfloat16),
        compiler_params=pltpu.CompilerParams(
            dimension_semantics=("arbitrary", "arbitrary"), vmem_limit_bytes=VMEM_LIMIT),
        name="w_in_prep",
    )(jnp.swapaxes(w_in, 1, 2))


def _gate_params(wg_f, bg_f, wg_b, bg_b):
    depth, rank, hk = wg_f.shape
    zeros = jnp.zeros((depth, rank, hk), jnp.float32)
    rows = jnp.concatenate([jnp.concatenate([wg_f, zeros], axis=2),
                            jnp.concatenate([zeros, wg_b], axis=2)], axis=1)
    wg_cat = jnp.pad(rows, ((0, 0), (MLA_ROPE, MISC_W - MLA_ROPE - 2 * rank), (0, 0)))
    bg_cat = jnp.concatenate([bg_f, bg_b], axis=1)[:, None, :]
    return wg_cat.astype(jnp.bfloat16), bg_cat


def _permute_w_uq(w):
    depth = w.shape[0]
    half = MLA_ROPE // 2
    w4 = w.reshape(depth, MLA_Q_LORA, MLA_HEADS, MLA_NOPE + MLA_ROPE)
    nope, x1, x2 = w4[..., :MLA_NOPE], w4[..., MLA_NOPE:MLA_NOPE + half], w4[..., MLA_NOPE + half:]
    cols = jnp.concatenate([nope, x1, x2, x2, x1], axis=-1)
    return cols.reshape(depth, MLA_Q_LORA, MLA_HEADS * QK_W).astype(jnp.bfloat16)


def _split_w_ukv(w):
    depth = w.shape[0]
    w4 = w.reshape(depth, MLA_KV_LORA, MLA_HEADS, MLA_NOPE + MLA_DV)
    wk = w4[..., :MLA_NOPE].transpose(0, 2, 1, 3)
    wv_t = w4[..., MLA_NOPE:].transpose(0, 2, 3, 1)
    return wk.astype(jnp.bfloat16), wv_t.astype(jnp.bfloat16)


def kernel(x, c, positions, ada_w, ada_b, norm_g, w_in, gla_wg_f, gla_bg_f, gla_wg_b, gla_bg_b,
           gla_norm_g, mla_q_norm_g, mla_kv_norm_g, mla_w_uq, mla_w_ukv, mla_out_g, conv_w,
           conv_out_g, w_out, final_g):
    bsz, s, d = x.shape
    t = bsz * s

    mod = _ada_call(c, ada_w, ada_b)[:, :bsz].reshape(DEPTH, bsz, 3, d)
    tabs = _rope_tab_call(positions)

    w_in_p = _w_in_prep_call(w_in)
    w_out_b = w_out.astype(jnp.bfloat16)
    wg_cat, bg_cat = _gate_params(gla_wg_f, gla_bg_f, gla_wg_b, gla_bg_b)
    wuq_p = _permute_w_uq(mla_w_uq)
    wuk_p, wuvt_p = _split_w_ukv(mla_w_ukv)

    def row(p):
        return p[:, None, :]

    def b3(a):
        return a.reshape(bsz, s, a.shape[-1])

    h = x.reshape(t, d)
    for l in range(DEPTH):
        segs = _inproj_call(l, h, mod, row(norm_g), w_in_p, s)
        gq, gk, gv, mq, misc, mkv, cb, cc, cx, z = segs
        o_gla = _gla_call(l, b3(gq), b3(gk), b3(gv), b3(misc), wg_cat, bg_cat, row(gla_norm_g))
        o_mla = _mla_call(l, b3(mq), b3(mkv), b3(misc), tabs, row(mla_q_norm_g),
                          row(mla_kv_norm_g), wuq_p, wuk_p, wuvt_p)
        h = _outproj_call(l, h, mod, o_gla.reshape(t, -1), o_mla, cb, cc, cx, z,
                          row(mla_out_g), conv_w, row(conv_out_g), final_g.reshape(1, d),
                          w_out_b, s, l == DEPTH - 1)
    return h.reshape(bsz, s, d)
```

```python
import functools
import math

import jax
import jax.numpy as jnp
from jax import lax
from jax.experimental import pallas as pl
from jax.experimental.pallas import tpu as pltpu

D_MODEL = 2048
DEPTH = 2
GLA_HEADS = 6
GLA_DK = 64
GLA_DV = 128
GLA_GATE_RANK = 16
GLA_GATE_TEMP = 16.0
GLA_CHUNK = 64
GLA_WIDTH = GLA_HEADS * GLA_DV
MLA_HEADS = 6
MLA_Q_LORA = 384
MLA_KV_LORA = 256
MLA_NOPE = 128
MLA_ROPE = 64
MLA_DV = 128
MLA_WIDTH = MLA_HEADS * MLA_DV
ROPE_THETA = 10000.0
CONV_CH = D_MODEL - GLA_WIDTH - MLA_WIDTH
EPS = 1e-6

LANES = 128
F32_ROWS = 8
BF16_ROWS = 16
HALO_ROWS = BF16_ROWS
MISC_W = LANES
QK_W = 2 * LANES

IN_SEGMENTS = (
    ("gq", GLA_HEADS * GLA_DK, jnp.bfloat16),
    ("gk", GLA_HEADS * GLA_DK, jnp.bfloat16),
    ("gv", GLA_WIDTH, jnp.bfloat16),
    ("mq", MLA_Q_LORA, jnp.bfloat16),
    ("misc", MISC_W, jnp.float32),
    ("mkv", MLA_KV_LORA, jnp.bfloat16),
    ("cb", CONV_CH, jnp.bfloat16),
    ("cc", CONV_CH, jnp.bfloat16),
    ("cx", CONV_CH, jnp.bfloat16),
    ("z", D_MODEL, jnp.bfloat16),
)
IN_DIM_P = sum(w for _, w, _ in IN_SEGMENTS)
_SEGMENT_COLS = {}
for _name, _width, _ in IN_SEGMENTS:
    _SEGMENT_COLS[_name] = (sum(w for _, w in _SEGMENT_COLS.values()), _width)
IN_OUTPUTS = tuple(seg for seg in IN_SEGMENTS if seg[0] not in ("cc", "cx", "z")) + (
    ("u", CONV_CH, jnp.bfloat16),
    ("sz", D_MODEL, jnp.bfloat16),
)

IN_ROW_TILE = 512
OUT_ROW_TILE = 512
OUT_SUB_TILE = 256
Q_TILE = 512
KV_ROWS = 512
SCAN_ROWS = 256
SCAN_GROUP = 4
GLA_GROUP = 8
VMEM_LIMIT = 56 * 1024 * 1024

_NT = (((1,), (1,)), ((), ()))
_TN = (((0,), (0,)), ((), ()))


def _rms(x, g):
    ms = jnp.mean(x * x, axis=-1, keepdims=True)
    return x * lax.rsqrt(ms + EPS) * g


def _bdot(a, b):
    return jnp.dot(a, b, preferred_element_type=jnp.float32)


ADA_COLS = 1024
ADA_KROWS = 256


def _ada_kernel(ct_ref, w_ref, b_ref, o_ref, cb_s):
    d, bsz = ct_ref.shape
    tn = w_ref.shape[1]

    @pl.when((pl.program_id(0) == 0) & (pl.program_id(1) == 0))
    def _broadcast_c():
        ct = ct_ref[...]
        act = ct * jax.nn.sigmoid(ct)
        for b in range(bsz):
            cb_s[b] = jnp.broadcast_to(act[:, b:b + 1], (d, LANES))

    def body(i, accs):
        accs = list(accs)
        for g in range(ADA_KROWS // F32_ROWS):
            r = pl.ds(pl.multiple_of(i * ADA_KROWS + g * F32_ROWS, F32_ROWS), F32_ROWS)
            w = w_ref[r, :]
            for b in range(bsz):
                accs[b] = accs[b] + w * jnp.concatenate([cb_s[b, r, :]] * (tn // LANES), axis=1)
        return tuple(accs)

    zero = jnp.zeros((F32_ROWS, tn), jnp.float32)
    accs = lax.fori_loop(0, d // ADA_KROWS, body, (zero,) * bsz)
    rows = [jnp.sum(a, axis=0, keepdims=True) for a in accs]
    rows.append(jnp.zeros((o_ref.shape[0] - bsz, tn), jnp.float32))
    o_ref[...] = jnp.concatenate(rows, axis=0) + b_ref[...]


def _ada_call(c, ada_w, ada_b):
    bsz = c.shape[0]
    n = ada_w.shape[-1]
    tn = ADA_COLS
    return pl.pallas_call(
        _ada_kernel,
        grid=(DEPTH, n // tn),
        in_specs=[
            pl.BlockSpec((D_MODEL, bsz), lambda l, j: (0, 0)),
            pl.BlockSpec((None, D_MODEL, tn), lambda l, j: (l, 0, j)),
            pl.BlockSpec((None, 1, tn), lambda l, j: (l, 0, j)),
        ],
        out_specs=pl.BlockSpec((None, F32_ROWS, tn), lambda l, j: (l, 0, j)),
        out_shape=jax.ShapeDtypeStruct((DEPTH, F32_ROWS, n), jnp.float32),
        scratch_shapes=[pltpu.VMEM((bsz, D_MODEL, LANES), jnp.float32)],
        compiler_params=pltpu.CompilerParams(
            dimension_semantics=("arbitrary", "arbitrary"), vmem_limit_bytes=VMEM_LIMIT),
        name="ada_mod",
    )(c.T, ada_w, ada_b.reshape(DEPTH, 1, n))


def _rope_tab_kernel(pos_ref, invf_ref, c_ref, sa_ref, sb_ref):
    rows = pos_ref.shape[0]
    lane = lax.broadcasted_iota(jnp.int32, (rows, LANES), 1)
    pos = jnp.where(lane < MLA_ROPE, pos_ref[:, 0:1], pos_ref[:, 1:2]).astype(jnp.float32)
    ang = pos * invf_ref[...]
    half = MLA_ROPE // 2
    in_first, in_second = lane < half, (lane >= half) & (lane < MLA_ROPE)
    cos2, sin2 = jnp.cos(ang), jnp.sin(ang)
    for r, shift in ((slice(0, rows), 0), (slice(rows, 2 * rows), MLA_ROPE)):
        cos = cos2 if shift == 0 else pltpu.roll(cos2, shift, axis=1)
        sin = sin2 if shift == 0 else pltpu.roll(sin2, shift, axis=1)
        c_ref[r, :] = jnp.where(lane < MLA_ROPE, cos, 0.0)
        sa_ref[r, :] = jnp.where(in_first, -sin, 0.0)
        sb_ref[r, :] = jnp.where(in_second, sin, 0.0)


def _rope_tab_call(positions):
    bsz, s = positions.shape
    inv_freq = ROPE_THETA ** (-jnp.arange(0, MLA_ROPE, 2, dtype=jnp.float32) / MLA_ROPE)
    invf = jnp.tile(inv_freq, LANES // inv_freq.shape[0])
    pos2 = jnp.stack([positions[:, :s // 2], positions[:, s // 2:]], axis=-1)
    tab = jax.ShapeDtypeStruct((bsz, s, LANES), jnp.float32)
    spec = pl.BlockSpec((None, s, LANES), lambda b: (b, 0, 0))
    return pl.pallas_call(
        _rope_tab_kernel,
        grid=(bsz,),
        in_specs=[pl.BlockSpec((None, s // 2, 2), lambda b: (b, 0, 0)),
                  pl.BlockSpec((1, LANES), lambda b: (0, 0))],
        out_specs=[spec, spec, spec],
        out_shape=[tab, tab, tab],
        compiler_params=pltpu.CompilerParams(dimension_semantics=("arbitrary",)),
        name="rope_tables",
    )(pos2, invf.reshape(1, LANES))


def _rope(x, c, sa, sb):
    return (x * c + pltpu.roll(x, LANES - MLA_ROPE // 2, axis=1) * sa
            + pltpu.roll(x, MLA_ROPE // 2, axis=1) * sb)


def _inproj_kernel(x_ref, mod_ref, g_ref, w_ref, *out_refs):
    x = x_ref[...]
    mod = mod_ref[...]
    shift, scale = mod[0:1], mod[1:2]
    ms = jnp.mean(x * x, axis=-1, keepdims=True)
    h = (x * lax.rsqrt(ms + EPS) * g_ref[...]) * (1.0 + scale) + shift
    hb = h.astype(jnp.bfloat16)

    def proj(name):
        lo, width = _SEGMENT_COLS[name]
        return _bdot(hb, w_ref[:, lo:lo + width])

    for (name, _, _), o_ref in reversed(list(zip(IN_OUTPUTS, out_refs))):
        if name == "u":
            val = proj("cc") * proj("cx")
        elif name == "sz":
            z = proj("z")
            val = z * jax.nn.sigmoid(z)
        else:
            val = proj(name)
        o_ref[...] = val.astype(o_ref.dtype)


def _layer_spec(layer, shape):
    return pl.BlockSpec((None,) + shape, lambda *_: (layer,) + (0,) * len(shape))


def _inproj_call(layer, x2, mod, norm_g, w_in_p, seq_len):
    t = x2.shape[0]
    tm = IN_ROW_TILE
    tiles_per_seq = seq_len // tm
    out_shape = [jax.ShapeDtypeStruct((t, w), dt) for _, w, dt in IN_OUTPUTS]
    out_specs = [pl.BlockSpec((tm, w), lambda i: (i, 0)) for _, w, _ in IN_OUTPUTS]
    return pl.pallas_call(
        _inproj_kernel,
        grid=(t // tm,),
        in_specs=[
            pl.BlockSpec((tm, D_MODEL), lambda i: (i, 0)),
            pl.BlockSpec((None, None, 3, D_MODEL), lambda i: (layer, i // tiles_per_seq, 0, 0)),
            _layer_spec(layer, (1, D_MODEL)),
            pl.BlockSpec((None, D_MODEL, IN_DIM_P), lambda i: (layer, 0, 0),
                         pipeline_mode=pl.Buffered(1)),
        ],
        out_specs=out_specs,
        out_shape=out_shape,
        compiler_params=pltpu.CompilerParams(
            dimension_semantics=("arbitrary",), vmem_limit_bytes=VMEM_LIMIT),
        name="in_proj",
    )(x2, mod, norm_g, w_in_p)


def _log_sigmoid(x):
    return jnp.minimum(x, 0.0) - jnp.log(1.0 + jnp.exp(-jnp.abs(x)))


def _gla_kernel(q_ref, k_ref, v_ref, misc_ref, wgf_ref, bgf_ref, wgb_ref, bgb_ref, g_ref,
                o_ref, bf_s, bb_s, qcat_s, kv_s, dec_s, st_s, intra_s):
    s = q_ref.shape[0]
    cdim = GLA_CHUNK
    n_chunks = s // cdim
    pair_dk = 2 * GLA_DK
    pair_dv = 2 * GLA_DV
    inv_t = 1.0 / GLA_GATE_TEMP
    q_scale = GLA_DK ** -0.5

    t_row = lax.broadcasted_iota(jnp.int32, (SCAN_ROWS, SCAN_ROWS), 0)
    t_col = lax.broadcasted_iota(jnp.int32, (SCAN_ROWS, SCAN_ROWS), 1)
    same_chunk = (t_row // cdim) == (t_col // cdim)
    tri_prefix = (same_chunk & (t_col <= t_row)).astype(jnp.bfloat16)
    tri_suffix = (same_chunk & (t_col >= t_row)).astype(jnp.bfloat16)

    def split(terms):
        high = terms.astype(jnp.bfloat16)
        rest = (terms - high.astype(jnp.float32)).astype(jnp.bfloat16)
        return jnp.concatenate([high, rest], axis=1)

    def scan_body(i, carry):
        blocks = [pl.ds(pl.multiple_of((i * SCAN_GROUP + c) * SCAN_ROWS, SCAN_ROWS), SCAN_ROWS)
                  for c in range(SCAN_GROUP)]
        pre = []
        for r in blocks:
            m = misc_ref[r, :].astype(jnp.bfloat16)
            pre.append((_bdot(m, wgf_ref[...]), _bdot(m, wgb_ref[...])))
        terms = [(split(_log_sigmoid(pf + bgf_ref[...]) * inv_t),
                  split(_log_sigmoid(pb + bgb_ref[...]) * inv_t)) for pf, pb in pre]
        sums = [(_bdot(tri_prefix, tf), _bdot(tri_suffix, tb)) for tf, tb in terms]
        for r, (sf, sb) in zip(blocks, sums):
            bf_s[r, :] = sf[:, 0:pair_dk] + sf[:, pair_dk:]
            bb_s[r, :] = sb[:, 0:pair_dk] + sb[:, pair_dk:]
        return carry

    lax.fori_loop(0, s // (SCAN_ROWS * SCAN_GROUP), scan_body, 0)

    row = lax.broadcasted_iota(jnp.int32, (cdim, pair_dk), 0)
    key_pos = lax.broadcasted_iota(jnp.int32, (cdim, pair_dk), 1) & (cdim - 1)
    keep_f = key_pos <= row
    keep_b = key_pos > row
    cat_block = lax.broadcasted_iota(jnp.int32, (cdim, 2 * pair_dk), 1) // GLA_DK
    v_head = lax.broadcasted_iota(jnp.int32, (cdim, pair_dv), 1) // GLA_DV
    same_head = (lax.broadcasted_iota(jnp.int32, (pair_dv, 2 * pair_dk), 0) // GLA_DV
                 == (lax.broadcasted_iota(jnp.int32, (pair_dv, 2 * pair_dk), 1) // GLA_DK) % 2)

    def intra_body(i, carry):
        chunks = [i * GLA_GROUP + c for c in range(GLA_GROUP)]
        rows = [pl.ds(pl.multiple_of(n * cdim, cdim), cdim) for n in chunks]
        scores, k_ends = [], []
        for n, r in zip(chunks, rows):
            bf = bf_s[r, :]
            bb = bb_s[r, :]
            bf_last = bf[cdim - 1:cdim, :]
            bb_last = bb[0:1, :]
            q = q_ref[r, :].astype(jnp.float32) * q_scale
            k = k_ref[r, :].astype(jnp.float32)
            q_cat = jnp.concatenate([q * jnp.exp(bf), q * jnp.exp(bb)],
                                    axis=1).astype(jnp.bfloat16)
            k_inv = jnp.concatenate([k * jnp.exp(-bf), k * jnp.exp(-bb)], axis=1)
            k_ends.append(jnp.concatenate(
                [k * jnp.exp(bf_last - bf), k * jnp.exp(bb_last - bb)], axis=1).astype(jnp.bfloat16))
            dec_s[n] = jnp.concatenate([jnp.exp(bf_last), jnp.exp(bb_last)], axis=1)
            qcat_s[r, :] = q_cat
            k_bd = jnp.concatenate([jnp.where(cat_block == c, k_inv, 0.0) for c in range(4)],
                                   axis=0).astype(jnp.bfloat16)
            scores.append(lax.dot_general(q_cat, k_bd, _NT,
                                          preferred_element_type=jnp.float32))
        for n, r, k_end in zip(chunks, rows, k_ends):
            kv_t = lax.dot_general(v_ref[r, :], k_end, _TN,
                                   preferred_element_type=jnp.float32)
            kv_s[n] = jnp.where(same_head, kv_t, 0.0)
        for r, sc in zip(rows, scores):
            p = (jnp.where(keep_f, sc[:, 0:pair_dk], 0.0)
                 + jnp.where(keep_b, sc[:, pair_dk:], 0.0)).astype(jnp.bfloat16)
            vv = v_ref[r, :]
            v_bd = jnp.concatenate(
                [jnp.where(v_head == j, vv, jnp.zeros_like(vv)) for j in range(2)], axis=0)
            intra_s[r, :] = _bdot(p, v_bd)
        return carry

    lax.fori_loop(0, n_chunks // GLA_GROUP, intra_body, 0)

    def state_body(i, states):
        sf, sb = states
        nf = i
        nb = n_chunks - 1 - i
        st_s[nf, :, 0:pair_dk] = sf.astype(jnp.bfloat16)
        st_s[nb, :, pair_dk:] = sb.astype(jnp.bfloat16)
        sf = sf * dec_s[nf][:, 0:pair_dk] + kv_s[nf, :, 0:pair_dk]
        sb = sb * dec_s[nb][:, pair_dk:] + kv_s[nb, :, pair_dk:]
        return sf, sb

    zero = jnp.zeros((pair_dv, pair_dk), jnp.float32)
    lax.fori_loop(0, n_chunks, state_body, (zero, zero))

    g = g_ref[...]

    def out_body(i, carry):
        chunks = [i * GLA_GROUP + c for c in range(GLA_GROUP)]
        rows = [pl.ds(pl.multiple_of(n * cdim, cdim), cdim) for n in chunks]
        inter = [lax.dot_general(qcat_s[r, :], st_s[n], _NT, preferred_element_type=jnp.float32)
                 for n, r in zip(chunks, rows)]
        for r, o_inter in zip(rows, inter):
            o = intra_s[r, :] + o_inter
            for j in range(2):
                o_ref[r, j * GLA_DV:(j + 1) * GLA_DV] = _rms(
                    o[:, j * GLA_DV:(j + 1) * GLA_DV], g).astype(o_ref.dtype)
        return carry

    lax.fori_loop(0, n_chunks // GLA_GROUP, out_body, 0)


def _gla_call(layer, gq, gk, gv, misc, wg_cat, bg_cat, gla_norm_g):
    bsz, s, _ = gq.shape
    pairs = GLA_HEADS // 2
    n_chunks = s // GLA_CHUNK
    pair_dk, pair_dv = 2 * GLA_DK, 2 * GLA_DV
    qk_spec = pl.BlockSpec((None, s, pair_dk), lambda b, p: (b, 0, p))
    v_spec = pl.BlockSpec((None, s, pair_dv), lambda b, p: (b, 0, p))

    def gate_specs(direction):
        return (pl.BlockSpec((None, MISC_W, pair_dk), lambda b, p: (layer, 0, direction * pairs + p)),
                pl.BlockSpec((None, 1, pair_dk), lambda b, p: (layer, 0, direction * pairs + p)))

    return pl.pallas_call(
        _gla_kernel,
        grid=(bsz, pairs),
        in_specs=[qk_spec, qk_spec, v_spec,
                  pl.BlockSpec((None, s, MISC_W), lambda b, p: (b, 0, 0)),
                  *gate_specs(0), *gate_specs(1),
                  _layer_spec(layer, (1, GLA_DV))],
        out_specs=v_spec,
        out_shape=jax.ShapeDtypeStruct((bsz, s, GLA_WIDTH), jnp.bfloat16),
        scratch_shapes=[
            pltpu.VMEM((s, pair_dk), jnp.float32),
            pltpu.VMEM((s, pair_dk), jnp.float32),
            pltpu.VMEM((s, 2 * pair_dk), jnp.bfloat16),
            pltpu.VMEM((n_chunks, pair_dv, 2 * pair_dk), jnp.float32),
            pltpu.VMEM((n_chunks, 1, 2 * pair_dk), jnp.float32),
            pltpu.VMEM((n_chunks, pair_dv, 2 * pair_dk), jnp.bfloat16),
            pltpu.VMEM((s, pair_dv), jnp.float32),
        ],
        compiler_params=pltpu.CompilerParams(
            dimension_semantics=("arbitrary", "arbitrary"), vmem_limit_bytes=VMEM_LIMIT),
        name="gla",
    )(gq, gk, gv, misc, wg_cat, bg_cat, wg_cat, bg_cat, gla_norm_g)


def _mla_kernel(mq_ref, mkv_ref, misc_ref, c_ref, sa_ref, sb_ref, gq_ref, gkv_ref,
                wuq_ref, wuk_ref, wuvt_ref, o_ref, ckv_s, q_s, k_s, vt_s, sc_s, top_s):
    s = mkv_ref.shape[0]
    n_tiles = s // Q_TILE
    assert n_tiles % 2 == 0 and MLA_HEADS % 2 == 0
    blocks = [slice(i * KV_ROWS, (i + 1) * KV_ROWS) for i in range(s // KV_ROWS)]
    q_mult = (MLA_NOPE + MLA_ROPE) ** -0.5 * math.log2(math.e)

    for r in blocks:
        tabs = (c_ref[r, :], sa_ref[r, :], sb_ref[r, :])
        ckv_s[r, :] = _rms(mkv_ref[r, :].astype(jnp.float32), gkv_ref[...]).astype(jnp.bfloat16)
        k_rope = _rope(misc_ref[r, :], *tabs).astype(jnp.bfloat16)
        for slot in range(2):
            k_s[slot, r, MLA_NOPE:] = k_rope
        cq = _rms(mq_ref[r, :].astype(jnp.float32), gq_ref[...]).astype(jnp.bfloat16)
        qf = _bdot(cq, wuq_ref[...])
        cos_t, sin_t = tabs[0], tabs[1] + tabs[2]
        for h in range(MLA_HEADS):
            qh = qf[:, h * QK_W:(h + 1) * QK_W]
            rot = qh[:, MLA_NOPE:]
            q_rope = rot * cos_t + pltpu.roll(rot, MLA_ROPE, axis=1) * sin_t
            q_s[h, r, :] = (jnp.concatenate([qh[:, 0:MLA_NOPE], q_rope], axis=1)
                            * q_mult).astype(jnp.bfloat16)
    for slot in range(2):
        vt_s[slot, MLA_DV:, :] = jnp.ones((vt_s.shape[1] - MLA_DV, s), jnp.bfloat16)

    def build_kv(h, slot):
        for r in blocks:
            ckv = ckv_s[r, :]
            k_s[slot, r, 0:MLA_NOPE] = _bdot(ckv, wuk_ref[h]).astype(jnp.bfloat16)
            vt_s[slot, 0:MLA_DV, r] = lax.dot_general(
                wuvt_ref[h], ckv, _NT, preferred_element_type=jnp.float32).astype(jnp.bfloat16)

    def scores_t(h, slot, j):
        q = q_s[h, j * Q_TILE:(j + 1) * Q_TILE, :]
        sc_t = lax.dot_general(k_s[slot], q, _NT, preferred_element_type=jnp.float32)
        sc_s[j % 2] = sc_t
        top_s[j % 2] = jnp.broadcast_to(jnp.max(sc_t, axis=0, keepdims=True), (F32_ROWS, Q_TILE))

    def finish(h, slot, j):
        p_t = jnp.exp2(sc_s[j % 2] - top_s[j % 2][0:1, :]).astype(jnp.bfloat16)
        o_t = _bdot(vt_s[slot], p_t)
        o_ref[h, j * Q_TILE:(j + 1) * Q_TILE, :] = (
            o_t[0:MLA_DV, :] / o_t[MLA_DV:MLA_DV + 1, :]).T.astype(o_ref.dtype)

    build_kv(0, 0)
    scores_t(0, 0, 0)

    def head_pair(i, carry):
        for slot in range(2):
            h = 2 * i + slot
            h_next = jnp.minimum(h + 1, MLA_HEADS - 1)
            for j in range(n_tiles):
                if j + 1 < n_tiles:
                    scores_t(h, slot, j + 1)
                else:
                    build_kv(h_next, 1 - slot)
                    scores_t(h_next, 1 - slot, 0)
                finish(h, slot, j)
        return carry

    lax.fori_loop(0, MLA_HEADS // 2, head_pair, 0)


def _mla_call(layer, mq, mkv, misc, tabs, gq, gkv, wuq_p, wuk_p, wuvt_p):
    bsz, s, _ = mq.shape

    def per_row(width):
        return pl.BlockSpec((None, s, width), lambda b: (b, 0, 0))

    return pl.pallas_call(
        _mla_kernel,
        grid=(bsz,),
        in_specs=[
            per_row(MLA_Q_LORA), per_row(MLA_KV_LORA), per_row(MISC_W),
            per_row(LANES), per_row(LANES), per_row(LANES),
            _layer_spec(layer, (1, MLA_Q_LORA)),
            _layer_spec(layer, (1, MLA_KV_LORA)),
            _layer_spec(layer, (MLA_Q_LORA, MLA_HEADS * QK_W)),
            _layer_spec(layer, (MLA_HEADS, MLA_KV_LORA, MLA_NOPE)),
            _layer_spec(layer, (MLA_HEADS, MLA_DV, MLA_KV_LORA)),
        ],
        out_specs=pl.BlockSpec((None, MLA_HEADS, s, MLA_DV), lambda b: (b, 0, 0, 0)),
        out_shape=jax.ShapeDtypeStruct((bsz, MLA_HEADS, s, MLA_DV), jnp.bfloat16),
        scratch_shapes=[
            pltpu.VMEM((s, MLA_KV_LORA), jnp.bfloat16),
            pltpu.VMEM((MLA_HEADS, s, QK_W), jnp.bfloat16),
            pltpu.VMEM((2, s, QK_W), jnp.bfloat16),
            pltpu.VMEM((2, MLA_DV + BF16_ROWS, s), jnp.bfloat16),
            pltpu.VMEM((2, s, Q_TILE), jnp.float32),
            pltpu.VMEM((2, F32_ROWS, Q_TILE), jnp.float32),
        ],
        compiler_params=pltpu.CompilerParams(
            dimension_semantics=("arbitrary",),
            vmem_limit_bytes=VMEM_LIMIT),
        name="mla",
    )(mq, mkv, misc, *tabs, gq, gkv, wuq_p, wuk_p, wuvt_p)


def _outproj_kernel(tiles_per_seq, final, x_ref, mod_ref, ogla_ref, omla_ref, cb_ref, u_ref,
                    up_ref, un_ref, sz_ref, mg_ref, cw_ref, cg_ref, fg_ref, w_ref, o_ref):
    f32 = jnp.float32
    j = pl.program_id(0) % tiles_per_seq
    tm = x_ref.shape[0]

    u = u_ref[...].astype(f32)
    prev_ok = (j > 0).astype(f32)
    next_ok = (j < tiles_per_seq - 1).astype(f32)
    u_prev = up_ref[...].astype(f32)[HALO_ROWS - 1:, :] * prev_ok
    u_next = un_ref[...].astype(f32)[0:1, :] * next_ok
    rows = lax.broadcasted_iota(jnp.int32, u.shape, 0)
    up = jnp.where(rows == 0, u_prev, pltpu.roll(u, 1, axis=0))
    un = jnp.where(rows == tm - 1, u_next, pltpu.roll(u, tm - 1, axis=0))
    cw = cw_ref[...]
    conv = up * cw[0:1] + u * cw[1:2] + un * cw[2:3]
    gate = mod_ref[...][2:3]

    def gated_dot(r, lo, y_part):
        hi = lo + y_part.shape[1]
        y = (y_part * sz_ref[r, lo:hi].astype(f32)).astype(jnp.bfloat16)
        return _bdot(y, w_ref[lo:hi, :])

    def finish(r, acc):
        out = x_ref[r, :] + gate * acc
        if final:
            out = _rms(out, fg_ref[...])
        o_ref[r, :] = out

    pending = None
    for k in range(tm // OUT_SUB_TILE):
        r = slice(k * OUT_SUB_TILE, (k + 1) * OUT_SUB_TILE)
        acc = gated_dot(r, 0, ogla_ref[r, :].astype(f32))
        if pending is not None:
            finish(*pending)
        omla = jnp.concatenate([omla_ref[h, r, :] for h in range(MLA_HEADS)], axis=1)
        acc += gated_dot(r, GLA_WIDTH, _rms(omla.astype(f32), mg_ref[...]))
        acc += gated_dot(r, GLA_WIDTH + MLA_WIDTH,
                         _rms(cb_ref[r, :].astype(f32) * conv[r, :], cg_ref[...]))
        pending = (r, acc)
    finish(*pending)


def _outproj_call(layer, x2, mod, o_gla, o_mla, cb, u, sz, mla_out_g, conv_w, conv_out_g,
                  final_g, w_out_b, seq_len, final):
    t = x2.shape[0]
    tm = OUT_ROW_TILE
    tiles_per_seq = seq_len // tm
    halo_per_tile = tm // HALO_ROWS
    n_halo = t // HALO_ROWS

    def row_spec(w):
        return pl.BlockSpec((tm, w), lambda i: (i, 0))

    prev_spec = pl.BlockSpec((HALO_ROWS, CONV_CH),
                             lambda i: (jnp.maximum(i * halo_per_tile - 1, 0), 0))
    next_spec = pl.BlockSpec((HALO_ROWS, CONV_CH),
                             lambda i: (jnp.minimum((i + 1) * halo_per_tile, n_halo - 1), 0))

    return pl.pallas_call(
        functools.partial(_outproj_kernel, tiles_per_seq, final),
        grid=(t // tm,),
        in_specs=[
            row_spec(D_MODEL),
            pl.BlockSpec((None, None, 3, D_MODEL), lambda i: (layer, i // tiles_per_seq, 0, 0)),
            row_spec(GLA_WIDTH),
            pl.BlockSpec((None, MLA_HEADS, tm, MLA_DV),
                         lambda i: (i // tiles_per_seq, 0, i % tiles_per_seq, 0)),
            row_spec(CONV_CH), row_spec(CONV_CH), prev_spec, next_spec,
            row_spec(D_MODEL),
            _layer_spec(layer, (1, MLA_WIDTH)), _layer_spec(layer, (3, CONV_CH)),
            _layer_spec(layer, (1, CONV_CH)),
            pl.BlockSpec((1, D_MODEL), lambda i: (0, 0)),
            pl.BlockSpec((None, D_MODEL, D_MODEL), lambda i: (layer, 0, 0),
                         pipeline_mode=pl.Buffered(1)),
        ],
        out_specs=row_spec(D_MODEL),
        out_shape=jax.ShapeDtypeStruct((t, D_MODEL), jnp.float32),
        compiler_params=pltpu.CompilerParams(
            dimension_semantics=("arbitrary",), vmem_limit_bytes=VMEM_LIMIT),
        name="out_proj",
    )(x2, mod, o_gla, o_mla, cb, u, u, u, sz,
      mla_out_g, conv_w, conv_out_g, final_g, w_out_b)


_O_GLR = 2 * GLA_HEADS * GLA_DK + GLA_WIDTH
_O_MQ = _O_GLR + 2 * GLA_GATE_RANK
_O_MKV = _O_MQ + MLA_Q_LORA
_O_MKR = _O_MKV + MLA_KV_LORA
_O_CB = _O_MKR + MLA_ROPE
IN_DIM = _O_CB + 3 * CONV_CH + D_MODEL
W_PREP_ROWS = 256


W_PREP_CHUNK = 512


def _w_in_prep_kernel(wt_ref, o_ref):
    rows = o_ref.shape[0]
    eye = (lax.broadcasted_iota(jnp.int32, (rows, rows), 0)
           == lax.broadcasted_iota(jnp.int32, (rows, rows), 1)).astype(jnp.bfloat16)

    def put(dst, block):
        o_ref[:, dst:dst + block.shape[0]] = lax.dot_general(
            eye, block.astype(jnp.bfloat16), _NT,
            preferred_element_type=jnp.float32).astype(jnp.bfloat16)

    def put_range(dst, lo, hi):
        for start in range(lo, hi, W_PREP_CHUNK):
            stop = min(start + W_PREP_CHUNK, hi)
            put(dst + start - lo, wt_ref[start:stop, :])

    pad_rows = MISC_W - MLA_ROPE - 2 * GLA_GATE_RANK
    misc = jnp.concatenate([wt_ref[_O_MKR:_O_CB, :], wt_ref[_O_GLR:_O_MQ, :],
                            jnp.zeros((pad_rows, rows), jnp.float32)], axis=0)
    off = 0
    for src in ((0, _O_GLR), (_O_MQ, _O_MKV), misc, (_O_MKV, _O_MKR), (_O_CB, IN_DIM)):
        if isinstance(src, tuple):
            put_range(off, *src)
            off += src[1] - src[0]
        else:
            put(off, src)
            off += src.shape[0]


def _w_in_prep_call(w_in):
    depth, d, n = w_in.shape
    return pl.pallas_call(
        _w_in_prep_kernel,
        grid=(depth, d // W_PREP_ROWS),
        in_specs=[pl.BlockSpec((None, n, W_PREP_ROWS), lambda l, i: (l, 0, i))],
        out_specs=pl.BlockSpec((None, W_PREP_ROWS, IN_DIM_P), lambda l, i: (l, i, 0)),
        out_shape=jax.ShapeDtypeStruct((depth, d, IN_DIM_P), jnp.bfloat16),
        compiler_params=pltpu.CompilerParams(
            dimension_semantics=("arbitrary", "arbitrary"), vmem_limit_bytes=VMEM_LIMIT),
        name="w_in_prep",
    )(jnp.swapaxes(w_in, 1, 2))


def _gate_params(wg_f, bg_f, wg_b, bg_b):
    depth, rank, hk = wg_f.shape
    zeros = jnp.zeros((depth, rank, hk), jnp.float32)
    rows = jnp.concatenate([jnp.concatenate([wg_f, zeros], axis=2),
                            jnp.concatenate([zeros, wg_b], axis=2)], axis=1)
    wg_cat = jnp.pad(rows, ((0, 0), (MLA_ROPE, MISC_W - MLA_ROPE - 2 * rank), (0, 0)))
    bg_cat = jnp.concatenate([bg_f, bg_b], axis=1)[:, None, :]
    return wg_cat.astype(jnp.bfloat16), bg_cat


def _permute_w_uq(w):
    depth = w.shape[0]
    half = MLA_ROPE // 2
    w4 = w.reshape(depth, MLA_Q_LORA, MLA_HEADS, MLA_NOPE + MLA_ROPE)
    nope, x1, x2 = w4[..., :MLA_NOPE], w4[..., MLA_NOPE:MLA_NOPE + half], w4[..., MLA_NOPE + half:]
    cols = jnp.concatenate([nope, x1, x2, x2, x1], axis=-1)
    return cols.reshape(depth, MLA_Q_LORA, MLA_HEADS * QK_W).astype(jnp.bfloat16)


def _split_w_ukv(w):
    depth = w.shape[0]
    w4 = w.reshape(depth, MLA_KV_LORA, MLA_HEADS, MLA_NOPE + MLA_DV)
    wk = w4[..., :MLA_NOPE].transpose(0, 2, 1, 3)
    wv_t = w4[..., MLA_NOPE:].transpose(0, 2, 3, 1)
    return wk.astype(jnp.bfloat16), wv_t.astype(jnp.bfloat16)


def kernel(x, c, positions, ada_w, ada_b, norm_g, w_in, gla_wg_f, gla_bg_f, gla_wg_b, gla_bg_b,
           gla_norm_g, mla_q_norm_g, mla_kv_norm_g, mla_w_uq, mla_w_ukv, mla_out_g, conv_w,
           conv_out_g, w_out, final_g):
    bsz, s, d = x.shape
    t = bsz * s

    mod = _ada_call(c, ada_w, ada_b)[:, :bsz].reshape(DEPTH, bsz, 3, d)
    tabs = _rope_tab_call(positions)

    w_in_p = _w_in_prep_call(w_in)
    w_out_b = w_out.astype(jnp.bfloat16)
    wg_cat, bg_cat = _gate_params(gla_wg_f, gla_bg_f, gla_wg_b, gla_bg_b)
    wuq_p = _permute_w_uq(mla_w_uq)
    wuk_p, wuvt_p = _split_w_ukv(mla_w_ukv)

    def row(p):
        return p[:, None, :]

    def b3(a):
        return a.reshape(bsz, s, a.shape[-1])

    h = x.reshape(t, d)
    for l in range(DEPTH):
        segs = _inproj_call(l, h, mod, row(norm_g), w_in_p, s)
        gq, gk, gv, mq, misc, mkv, cb, u, sz = segs
        o_gla = _gla_call(l, b3(gq), b3(gk), b3(gv), b3(misc), wg_cat, bg_cat, row(gla_norm_g))
        o_mla = _mla_call(l, b3(mq), b3(mkv), b3(misc), tabs, row(mla_q_norm_g),
                          row(mla_kv_norm_g), wuq_p, wuk_p, wuvt_p)
        h = _outproj_call(l, h, mod, o_gla.reshape(t, -1), o_mla, cb, u, sz,
                          row(mla_out_g), conv_w, row(conv_out_g), final_g.reshape(1, d),
                          w_out_b, s, l == DEPTH - 1)
    return h.reshape(bsz, s, d)
```

```python
import functools
import math

import jax
import jax.numpy as jnp
from jax import lax
from jax.experimental import pallas as pl
from jax.experimental.pallas import tpu as pltpu

D_MODEL = 2048
DEPTH = 2
GLA_HEADS = 6
GLA_DK = 64
GLA_DV = 128
GLA_GATE_RANK = 16
GLA_GATE_TEMP = 16.0
GLA_CHUNK = 64
GLA_WIDTH = GLA_HEADS * GLA_DV
MLA_HEADS = 6
MLA_Q_LORA = 384
MLA_KV_LORA = 256
MLA_NOPE = 128
MLA_ROPE = 64
MLA_DV = 128
MLA_WIDTH = MLA_HEADS * MLA_DV
ROPE_THETA = 10000.0
CONV_CH = D_MODEL - GLA_WIDTH - MLA_WIDTH
EPS = 1e-6

LANES = 128
F32_ROWS = 8
BF16_ROWS = 16
HALO_ROWS = BF16_ROWS
MISC_W = LANES
QK_W = 2 * LANES

IN_SEGMENTS = (
    ("gq", GLA_HEADS * GLA_DK, jnp.bfloat16),
    ("gk", GLA_HEADS * GLA_DK, jnp.bfloat16),
    ("gv", GLA_WIDTH, jnp.bfloat16),
    ("mq", MLA_Q_LORA, jnp.bfloat16),
    ("misc", MISC_W, jnp.float32),
    ("mkv", MLA_KV_LORA, jnp.bfloat16),
    ("cb", CONV_CH, jnp.bfloat16),
    ("cc", CONV_CH, jnp.bfloat16),
    ("cx", CONV_CH, jnp.bfloat16),
    ("z", D_MODEL, jnp.bfloat16),
)
IN_DIM_P = sum(w for _, w, _ in IN_SEGMENTS)
_SEGMENT_COLS = {}
for _name, _width, _ in IN_SEGMENTS:
    _SEGMENT_COLS[_name] = (sum(w for _, w in _SEGMENT_COLS.values()), _width)
IN_OUTPUTS = tuple(seg for seg in IN_SEGMENTS if seg[0] not in ("cc", "cx", "z")) + (
    ("u", CONV_CH, jnp.bfloat16),
    ("sz", D_MODEL, jnp.bfloat16),
)

IN_ROW_TILE = 512
IN_SUB_TILE = 256
OUT_ROW_TILE = 512
OUT_SUB_TILE = 256
Q_TILE = 512
KV_ROWS = 512
SCAN_ROWS = 256
SCAN_GROUP = 4
GLA_GROUP = 8
VMEM_LIMIT = 56 * 1024 * 1024

_NT = (((1,), (1,)), ((), ()))
_TN = (((0,), (0,)), ((), ()))


def _rms(x, g):
    ms = jnp.mean(x * x, axis=-1, keepdims=True)
    return x * lax.rsqrt(ms + EPS) * g


def _bdot(a, b):
    return jnp.dot(a, b, preferred_element_type=jnp.float32)


ADA_COLS = 1024
ADA_KROWS = 256


def _ada_kernel(ct_ref, w_ref, b_ref, o_ref, cb_s):
    d, bsz = ct_ref.shape
    tn = w_ref.shape[1]

    @pl.when((pl.program_id(0) == 0) & (pl.program_id(1) == 0))
    def _broadcast_c():
        ct = ct_ref[...]
        act = ct * jax.nn.sigmoid(ct)
        for b in range(bsz):
            cb_s[b] = jnp.broadcast_to(act[:, b:b + 1], (d, LANES))

    def body(i, accs):
        accs = list(accs)
        for g in range(ADA_KROWS // F32_ROWS):
            r = pl.ds(pl.multiple_of(i * ADA_KROWS + g * F32_ROWS, F32_ROWS), F32_ROWS)
            w = w_ref[r, :]
            for b in range(bsz):
                accs[b] = accs[b] + w * jnp.concatenate([cb_s[b, r, :]] * (tn // LANES), axis=1)
        return tuple(accs)

    zero = jnp.zeros((F32_ROWS, tn), jnp.float32)
    accs = lax.fori_loop(0, d // ADA_KROWS, body, (zero,) * bsz)
    rows = [jnp.sum(a, axis=0, keepdims=True) for a in accs]
    rows.append(jnp.zeros((o_ref.shape[0] - bsz, tn), jnp.float32))
    o_ref[...] = jnp.concatenate(rows, axis=0) + b_ref[...]


def _ada_call(c, ada_w, ada_b):
    bsz = c.shape[0]
    n = ada_w.shape[-1]
    tn = ADA_COLS
    return pl.pallas_call(
        _ada_kernel,
        grid=(DEPTH, n // tn),
        in_specs=[
            pl.BlockSpec((D_MODEL, bsz), lambda l, j: (0, 0)),
            pl.BlockSpec((None, D_MODEL, tn), lambda l, j: (l, 0, j)),
            pl.BlockSpec((None, 1, tn), lambda l, j: (l, 0, j)),
        ],
        out_specs=pl.BlockSpec((None, F32_ROWS, tn), lambda l, j: (l, 0, j)),
        out_shape=jax.ShapeDtypeStruct((DEPTH, F32_ROWS, n), jnp.float32),
        scratch_shapes=[pltpu.VMEM((bsz, D_MODEL, LANES), jnp.float32)],
        compiler_params=pltpu.CompilerParams(
            dimension_semantics=("arbitrary", "arbitrary"), vmem_limit_bytes=VMEM_LIMIT),
        name="ada_mod",
    )(c.T, ada_w, ada_b.reshape(DEPTH, 1, n))


def _rope_tab_kernel(pos_ref, invf_ref, c_ref, sa_ref, sb_ref):
    rows = pos_ref.shape[0]
    lane = lax.broadcasted_iota(jnp.int32, (rows, LANES), 1)
    pos = jnp.where(lane < MLA_ROPE, pos_ref[:, 0:1], pos_ref[:, 1:2]).astype(jnp.float32)
    ang = pos * invf_ref[...]
    half = MLA_ROPE // 2
    in_first, in_second = lane < half, (lane >= half) & (lane < MLA_ROPE)
    cos2, sin2 = jnp.cos(ang), jnp.sin(ang)
    for r, shift in ((slice(0, rows), 0), (slice(rows, 2 * rows), MLA_ROPE)):
        cos = cos2 if shift == 0 else pltpu.roll(cos2, shift, axis=1)
        sin = sin2 if shift == 0 else pltpu.roll(sin2, shift, axis=1)
        c_ref[r, :] = jnp.where(lane < MLA_ROPE, cos, 0.0)
        sa_ref[r, :] = jnp.where(in_first, -sin, 0.0)
        sb_ref[r, :] = jnp.where(in_second, sin, 0.0)


def _rope_tab_call(positions):
    bsz, s = positions.shape
    inv_freq = ROPE_THETA ** (-jnp.arange(0, MLA_ROPE, 2, dtype=jnp.float32) / MLA_ROPE)
    invf = jnp.tile(inv_freq, LANES // inv_freq.shape[0])
    pos2 = jnp.stack([positions[:, :s // 2], positions[:, s // 2:]], axis=-1)
    tab = jax.ShapeDtypeStruct((bsz, s, LANES), jnp.float32)
    spec = pl.BlockSpec((None, s, LANES), lambda b: (b, 0, 0))
    return pl.pallas_call(
        _rope_tab_kernel,
        grid=(bsz,),
        in_specs=[pl.BlockSpec((None, s // 2, 2), lambda b: (b, 0, 0)),
                  pl.BlockSpec((1, LANES), lambda b: (0, 0))],
        out_specs=[spec, spec, spec],
        out_shape=[tab, tab, tab],
        compiler_params=pltpu.CompilerParams(dimension_semantics=("arbitrary",)),
        name="rope_tables",
    )(pos2, invf.reshape(1, LANES))


def _rope(x, c, sa, sb):
    return (x * c + pltpu.roll(x, LANES - MLA_ROPE // 2, axis=1) * sa
            + pltpu.roll(x, MLA_ROPE // 2, axis=1) * sb)


def _inproj_kernel(x_ref, mod_ref, g_ref, w_ref, *refs):
    out_refs, h_s = refs[:-1], refs[-1]
    mod = mod_ref[...]
    shift, gain = mod[0:1], g_ref[...] * (1.0 + mod[1:2])
    n_sub = x_ref.shape[0] // IN_SUB_TILE

    def normalize(k):
        x = x_ref[k * IN_SUB_TILE:(k + 1) * IN_SUB_TILE, :]
        ms = jnp.mean(x * x, axis=-1, keepdims=True)
        h_s[k % 2] = (x * lax.rsqrt(ms + EPS) * gain + shift).astype(jnp.bfloat16)

    def project(k):
        r = slice(k * IN_SUB_TILE, (k + 1) * IN_SUB_TILE)

        def proj(name):
            lo, width = _SEGMENT_COLS[name]
            return _bdot(h_s[k % 2], w_ref[:, lo:lo + width])

        for (name, _, _), o_ref in reversed(list(zip(IN_OUTPUTS, out_refs))):
            if name == "u":
                val = proj("cc") * proj("cx")
            elif name == "sz":
                z = proj("z")
                val = z * jax.nn.sigmoid(z)
            else:
                val = proj(name)
            o_ref[r, :] = val.astype(o_ref.dtype)

    normalize(0)
    for k in range(n_sub):
        project(k)
        if k + 1 < n_sub:
            normalize(k + 1)


def _layer_spec(layer, shape):
    return pl.BlockSpec((None,) + shape, lambda *_: (layer,) + (0,) * len(shape))


def _inproj_call(layer, x2, mod, norm_g, w_in_p, seq_len):
    t = x2.shape[0]
    tm = IN_ROW_TILE
    tiles_per_seq = seq_len // tm
    out_shape = [jax.ShapeDtypeStruct((t, w), dt) for _, w, dt in IN_OUTPUTS]
    out_specs = [pl.BlockSpec((tm, w), lambda i: (i, 0)) for _, w, _ in IN_OUTPUTS]
    return pl.pallas_call(
        _inproj_kernel,
        grid=(t // tm,),
        in_specs=[
            pl.BlockSpec((tm, D_MODEL), lambda i: (i, 0)),
            pl.BlockSpec((None, None, 3, D_MODEL), lambda i: (layer, i // tiles_per_seq, 0, 0)),
            _layer_spec(layer, (1, D_MODEL)),
            pl.BlockSpec((None, D_MODEL, IN_DIM_P), lambda i: (layer, 0, 0),
                         pipeline_mode=pl.Buffered(1)),
        ],
        out_specs=out_specs,
        out_shape=out_shape,
        scratch_shapes=[pltpu.VMEM((2, IN_SUB_TILE, D_MODEL), jnp.bfloat16)],
        compiler_params=pltpu.CompilerParams(
            dimension_semantics=("arbitrary",), vmem_limit_bytes=VMEM_LIMIT),
        name="in_proj",
    )(x2, mod, norm_g, w_in_p)


def _log_sigmoid(x):
    return jnp.minimum(x, 0.0) - jnp.log(1.0 + jnp.exp(-jnp.abs(x)))


def _gla_kernel(q_ref, k_ref, v_ref, misc_ref, wgf_ref, bgf_ref, wgb_ref, bgb_ref, g_ref,
                o_ref, bf_s, bb_s, qcat_s, kv_s, dec_s, st_s, intra_s):
    s = q_ref.shape[0]
    cdim = GLA_CHUNK
    n_chunks = s // cdim
    pair_dk = 2 * GLA_DK
    pair_dv = 2 * GLA_DV
    inv_t = 1.0 / GLA_GATE_TEMP
    q_scale = GLA_DK ** -0.5

    t_row = lax.broadcasted_iota(jnp.int32, (SCAN_ROWS, SCAN_ROWS), 0)
    t_col = lax.broadcasted_iota(jnp.int32, (SCAN_ROWS, SCAN_ROWS), 1)
    same_chunk = (t_row // cdim) == (t_col // cdim)
    tri_prefix = (same_chunk & (t_col <= t_row)).astype(jnp.bfloat16)
    tri_suffix = (same_chunk & (t_col >= t_row)).astype(jnp.bfloat16)

    def split(terms):
        high = terms.astype(jnp.bfloat16)
        rest = (terms - high.astype(jnp.float32)).astype(jnp.bfloat16)
        return jnp.concatenate([high, rest], axis=1)

    def scan_body(i, carry):
        blocks = [pl.ds(pl.multiple_of((i * SCAN_GROUP + c) * SCAN_ROWS, SCAN_ROWS), SCAN_ROWS)
                  for c in range(SCAN_GROUP)]
        pre = []
        for r in blocks:
            m = misc_ref[r, :].astype(jnp.bfloat16)
            pre.append((_bdot(m, wgf_ref[...]), _bdot(m, wgb_ref[...])))
        terms = [(split(_log_sigmoid(pf + bgf_ref[...]) * inv_t),
                  split(_log_sigmoid(pb + bgb_ref[...]) * inv_t)) for pf, pb in pre]
        sums = [(_bdot(tri_prefix, tf), _bdot(tri_suffix, tb)) for tf, tb in terms]
        for r, (sf, sb) in zip(blocks, sums):
            bf_s[r, :] = sf[:, 0:pair_dk] + sf[:, pair_dk:]
            bb_s[r, :] = sb[:, 0:pair_dk] + sb[:, pair_dk:]
        return carry

    lax.fori_loop(0, s // (SCAN_ROWS * SCAN_GROUP), scan_body, 0)

    row = lax.broadcasted_iota(jnp.int32, (cdim, pair_dk), 0)
    key_pos = lax.broadcasted_iota(jnp.int32, (cdim, pair_dk), 1) & (cdim - 1)
    keep_f = key_pos <= row
    keep_b = key_pos > row
    cat_block = lax.broadcasted_iota(jnp.int32, (cdim, 2 * pair_dk), 1) // GLA_DK
    v_head = lax.broadcasted_iota(jnp.int32, (cdim, pair_dv), 1) // GLA_DV
    same_head = (lax.broadcasted_iota(jnp.int32, (pair_dv, 2 * pair_dk), 0) // GLA_DV
                 == (lax.broadcasted_iota(jnp.int32, (pair_dv, 2 * pair_dk), 1) // GLA_DK) % 2)

    def intra_body(i, carry):
        chunks = [i * GLA_GROUP + c for c in range(GLA_GROUP)]
        rows = [pl.ds(pl.multiple_of(n * cdim, cdim), cdim) for n in chunks]
        scores, k_ends = [], []
        for n, r in zip(chunks, rows):
            bf = bf_s[r, :]
            bb = bb_s[r, :]
            bf_last = bf[cdim - 1:cdim, :]
            bb_last = bb[0:1, :]
            q = q_ref[r, :].astype(jnp.float32) * q_scale
            k = k_ref[r, :].astype(jnp.float32)
            q_cat = jnp.concatenate([q * jnp.exp(bf), q * jnp.exp(bb)],
                                    axis=1).astype(jnp.bfloat16)
            k_inv = jnp.concatenate([k * jnp.exp(-bf), k * jnp.exp(-bb)], axis=1)
            k_ends.append(jnp.concatenate(
                [k * jnp.exp(bf_last - bf), k * jnp.exp(bb_last - bb)], axis=1).astype(jnp.bfloat16))
            dec_s[n] = jnp.concatenate([jnp.exp(bf_last), jnp.exp(bb_last)], axis=1)
            qcat_s[r, :] = q_cat
            k_bd = jnp.concatenate([jnp.where(cat_block == c, k_inv, 0.0) for c in range(4)],
                                   axis=0).astype(jnp.bfloat16)
            scores.append(lax.dot_general(q_cat, k_bd, _NT,
                                          preferred_element_type=jnp.float32))
        for n, r, k_end in zip(chunks, rows, k_ends):
            kv_t = lax.dot_general(v_ref[r, :], k_end, _TN,
                                   preferred_element_type=jnp.float32)
            kv_s[n] = jnp.where(same_head, kv_t, 0.0)
        for r, sc in zip(rows, scores):
            p = (jnp.where(keep_f, sc[:, 0:pair_dk], 0.0)
                 + jnp.where(keep_b, sc[:, pair_dk:], 0.0)).astype(jnp.bfloat16)
            vv = v_ref[r, :]
            v_bd = jnp.concatenate(
                [jnp.where(v_head == j, vv, jnp.zeros_like(vv)) for j in range(2)], axis=0)
            intra_s[r, :] = _bdot(p, v_bd)
        return carry

    lax.fori_loop(0, n_chunks // GLA_GROUP, intra_body, 0)

    def state_body(i, states):
        sf, sb = states
        nf = i
        nb = n_chunks - 1 - i
        st_s[nf, :, 0:pair_dk] = sf.astype(jnp.bfloat16)
        st_s[nb, :, pair_dk:] = sb.astype(jnp.bfloat16)
        sf = sf * dec_s[nf][:, 0:pair_dk] + kv_s[nf, :, 0:pair_dk]
        sb = sb * dec_s[nb][:, pair_dk:] + kv_s[nb, :, pair_dk:]
        return sf, sb

    zero = jnp.zeros((pair_dv, pair_dk), jnp.float32)
    lax.fori_loop(0, n_chunks, state_body, (zero, zero))

    g = g_ref[...]

    def out_body(i, carry):
        chunks = [i * GLA_GROUP + c for c in range(GLA_GROUP)]
        rows = [pl.ds(pl.multiple_of(n * cdim, cdim), cdim) for n in chunks]
        inter = [lax.dot_general(qcat_s[r, :], st_s[n], _NT, preferred_element_type=jnp.float32)
                 for n, r in zip(chunks, rows)]
        for r, o_inter in zip(rows, inter):
            o = intra_s[r, :] + o_inter
            for j in range(2):
                o_ref[r, j * GLA_DV:(j + 1) * GLA_DV] = _rms(
                    o[:, j * GLA_DV:(j + 1) * GLA_DV], g).astype(o_ref.dtype)
        return carry

    lax.fori_loop(0, n_chunks // GLA_GROUP, out_body, 0)


def _gla_call(layer, gq, gk, gv, misc, wg_cat, bg_cat, gla_norm_g):
    bsz, s, _ = gq.shape
    pairs = GLA_HEADS // 2
    n_chunks = s // GLA_CHUNK
    pair_dk, pair_dv = 2 * GLA_DK, 2 * GLA_DV
    qk_spec = pl.BlockSpec((None, s, pair_dk), lambda b, p: (b, 0, p))
    v_spec = pl.BlockSpec((None, s, pair_dv), lambda b, p: (b, 0, p))

    def gate_specs(direction):
        return (pl.BlockSpec((None, MISC_W, pair_dk), lambda b, p: (layer, 0, direction * pairs + p)),
                pl.BlockSpec((None, 1, pair_dk), lambda b, p: (layer, 0, direction * pairs + p)))

    return pl.pallas_call(
        _gla_kernel,
        grid=(bsz, pairs),
        in_specs=[qk_spec, qk_spec, v_spec,
                  pl.BlockSpec((None, s, MISC_W), lambda b, p: (b, 0, 0)),
                  *gate_specs(0), *gate_specs(1),
                  _layer_spec(layer, (1, GLA_DV))],
        out_specs=v_spec,
        out_shape=jax.ShapeDtypeStruct((bsz, s, GLA_WIDTH), jnp.bfloat16),
        scratch_shapes=[
            pltpu.VMEM((s, pair_dk), jnp.float32),
            pltpu.VMEM((s, pair_dk), jnp.float32),
            pltpu.VMEM((s, 2 * pair_dk), jnp.bfloat16),
            pltpu.VMEM((n_chunks, pair_dv, 2 * pair_dk), jnp.float32),
            pltpu.VMEM((n_chunks, 1, 2 * pair_dk), jnp.float32),
            pltpu.VMEM((n_chunks, pair_dv, 2 * pair_dk), jnp.bfloat16),
            pltpu.VMEM((s, pair_dv), jnp.float32),
        ],
        compiler_params=pltpu.CompilerParams(
            dimension_semantics=("arbitrary", "arbitrary"), vmem_limit_bytes=VMEM_LIMIT),
        name="gla",
    )(gq, gk, gv, misc, wg_cat, bg_cat, wg_cat, bg_cat, gla_norm_g)


def _mla_kernel(mq_ref, mkv_ref, misc_ref, c_ref, sa_ref, sb_ref, gq_ref, gkv_ref,
                wuq_ref, wuk_ref, wuvt_ref, o_ref, ckv_s, q_s, k_s, vt_s, sc_s, top_s):
    s = mkv_ref.shape[0]
    n_tiles = s // Q_TILE
    assert n_tiles % 2 == 0 and MLA_HEADS % 2 == 0
    blocks = [slice(i * KV_ROWS, (i + 1) * KV_ROWS) for i in range(s // KV_ROWS)]
    q_mult = (MLA_NOPE + MLA_ROPE) ** -0.5 * math.log2(math.e)

    for r in blocks:
        tabs = (c_ref[r, :], sa_ref[r, :], sb_ref[r, :])
        ckv_s[r, :] = _rms(mkv_ref[r, :].astype(jnp.float32), gkv_ref[...]).astype(jnp.bfloat16)
        k_rope = _rope(misc_ref[r, :], *tabs).astype(jnp.bfloat16)
        for slot in range(2):
            k_s[slot, r, MLA_NOPE:] = k_rope
        cq = _rms(mq_ref[r, :].astype(jnp.float32), gq_ref[...]).astype(jnp.bfloat16)
        qf = _bdot(cq, wuq_ref[...])
        cos_t, sin_t = tabs[0], tabs[1] + tabs[2]
        for h in range(MLA_HEADS):
            qh = qf[:, h * QK_W:(h + 1) * QK_W]
            rot = qh[:, MLA_NOPE:]
            q_rope = rot * cos_t + pltpu.roll(rot, MLA_ROPE, axis=1) * sin_t
            q_s[h, r, :] = (jnp.concatenate([qh[:, 0:MLA_NOPE], q_rope], axis=1)
                            * q_mult).astype(jnp.bfloat16)
    for slot in range(2):
        vt_s[slot, MLA_DV:, :] = jnp.ones((vt_s.shape[1] - MLA_DV, s), jnp.bfloat16)

    def build_kv(h, slot):
        for r in blocks:
            ckv = ckv_s[r, :]
            k_s[slot, r, 0:MLA_NOPE] = _bdot(ckv, wuk_ref[h]).astype(jnp.bfloat16)
            vt_s[slot, 0:MLA_DV, r] = lax.dot_general(
                wuvt_ref[h], ckv, _NT, preferred_element_type=jnp.float32).astype(jnp.bfloat16)

    def scores_t(h, slot, j):
        q = q_s[h, j * Q_TILE:(j + 1) * Q_TILE, :]
        sc_t = lax.dot_general(k_s[slot], q, _NT, preferred_element_type=jnp.float32)
        sc_s[j % 2] = sc_t
        top_s[j % 2] = jnp.broadcast_to(jnp.max(sc_t, axis=0, keepdims=True), (F32_ROWS, Q_TILE))

    def finish(h, slot, j):
        p_t = jnp.exp2(sc_s[j % 2] - top_s[j % 2][0:1, :]).astype(jnp.bfloat16)
        o_t = _bdot(vt_s[slot], p_t)
        o_ref[h, j * Q_TILE:(j + 1) * Q_TILE, :] = (
            o_t[0:MLA_DV, :] / o_t[MLA_DV:MLA_DV + 1, :]).T.astype(o_ref.dtype)

    build_kv(0, 0)
    scores_t(0, 0, 0)

    def head_pair(i, carry):
        for slot in range(2):
            h = 2 * i + slot
            h_next = jnp.minimum(h + 1, MLA_HEADS - 1)
            for j in range(n_tiles):
                if j + 1 < n_tiles:
                    scores_t(h, slot, j + 1)
                else:
                    build_kv(h_next, 1 - slot)
                    scores_t(h_next, 1 - slot, 0)
                finish(h, slot, j)
        return carry

    lax.fori_loop(0, MLA_HEADS // 2, head_pair, 0)


def _mla_call(layer, mq, mkv, misc, tabs, gq, gkv, wuq_p, wuk_p, wuvt_p):
    bsz, s, _ = mq.shape

    def per_row(width):
        return pl.BlockSpec((None, s, width), lambda b: (b, 0, 0))

    return pl.pallas_call(
        _mla_kernel,
        grid=(bsz,),
        in_specs=[
            per_row(MLA_Q_LORA), per_row(MLA_KV_LORA), per_row(MISC_W),
            per_row(LANES), per_row(LANES), per_row(LANES),
            _layer_spec(layer, (1, MLA_Q_LORA)),
            _layer_spec(layer, (1, MLA_KV_LORA)),
            _layer_spec(layer, (MLA_Q_LORA, MLA_HEADS * QK_W)),
            _layer_spec(layer, (MLA_HEADS, MLA_KV_LORA, MLA_NOPE)),
            _layer_spec(layer, (MLA_HEADS, MLA_DV, MLA_KV_LORA)),
        ],
        out_specs=pl.BlockSpec((None, MLA_HEADS, s, MLA_DV), lambda b: (b, 0, 0, 0)),
        out_shape=jax.ShapeDtypeStruct((bsz, MLA_HEADS, s, MLA_DV), jnp.bfloat16),
        scratch_shapes=[
            pltpu.VMEM((s, MLA_KV_LORA), jnp.bfloat16),
            pltpu.VMEM((MLA_HEADS, s, QK_W), jnp.bfloat16),
            pltpu.VMEM((2, s, QK_W), jnp.bfloat16),
            pltpu.VMEM((2, MLA_DV + BF16_ROWS, s), jnp.bfloat16),
            pltpu.VMEM((2, s, Q_TILE), jnp.float32),
            pltpu.VMEM((2, F32_ROWS, Q_TILE), jnp.float32),
        ],
        compiler_params=pltpu.CompilerParams(
            dimension_semantics=("arbitrary",),
            vmem_limit_bytes=VMEM_LIMIT),
        name="mla",
    )(mq, mkv, misc, *tabs, gq, gkv, wuq_p, wuk_p, wuvt_p)


def _outproj_kernel(tiles_per_seq, final, x_ref, mod_ref, ogla_ref, omla_ref, cb_ref, u_ref,
                    up_ref, un_ref, sz_ref, mg_ref, cw_ref, cg_ref, fg_ref, w_ref, o_ref):
    f32 = jnp.float32
    j = pl.program_id(0) % tiles_per_seq
    tm = x_ref.shape[0]

    u = u_ref[...].astype(f32)
    prev_ok = (j > 0).astype(f32)
    next_ok = (j < tiles_per_seq - 1).astype(f32)
    u_prev = up_ref[...].astype(f32)[HALO_ROWS - 1:, :] * prev_ok
    u_next = un_ref[...].astype(f32)[0:1, :] * next_ok
    rows = lax.broadcasted_iota(jnp.int32, u.shape, 0)
    up = jnp.where(rows == 0, u_prev, pltpu.roll(u, 1, axis=0))
    un = jnp.where(rows == tm - 1, u_next, pltpu.roll(u, tm - 1, axis=0))
    cw = cw_ref[...]
    conv = up * cw[0:1] + u * cw[1:2] + un * cw[2:3]
    gate = mod_ref[...][2:3]

    def gated_dot(r, lo, y_part):
        hi = lo + y_part.shape[1]
        y = (y_part * sz_ref[r, lo:hi].astype(f32)).astype(jnp.bfloat16)
        return _bdot(y, w_ref[lo:hi, :])

    def finish(r, acc):
        out = x_ref[r, :] + gate * acc
        if final:
            out = _rms(out, fg_ref[...])
        o_ref[r, :] = out

    pending = None
    for k in range(tm // OUT_SUB_TILE):
        r = slice(k * OUT_SUB_TILE, (k + 1) * OUT_SUB_TILE)
        acc = gated_dot(r, 0, ogla_ref[r, :].astype(f32))
        if pending is not None:
            finish(*pending)
        omla = jnp.concatenate([omla_ref[h, r, :] for h in range(MLA_HEADS)], axis=1)
        acc += gated_dot(r, GLA_WIDTH, _rms(omla.astype(f32), mg_ref[...]))
        acc += gated_dot(r, GLA_WIDTH + MLA_WIDTH,
                         _rms(cb_ref[r, :].astype(f32) * conv[r, :], cg_ref[...]))
        pending = (r, acc)
    finish(*pending)


def _outproj_call(layer, x2, mod, o_gla, o_mla, cb, u, sz, mla_out_g, conv_w, conv_out_g,
                  final_g, w_out_b, seq_len, final):
    t = x2.shape[0]
    tm = OUT_ROW_TILE
    tiles_per_seq = seq_len // tm
    halo_per_tile = tm // HALO_ROWS
    n_halo = t // HALO_ROWS

    def row_spec(w):
        return pl.BlockSpec((tm, w), lambda i: (i, 0))

    prev_spec = pl.BlockSpec((HALO_ROWS, CONV_CH),
                             lambda i: (jnp.maximum(i * halo_per_tile - 1, 0), 0))
    next_spec = pl.BlockSpec((HALO_ROWS, CONV_CH),
                             lambda i: (jnp.minimum((i + 1) * halo_per_tile, n_halo - 1), 0))

    return pl.pallas_call(
        functools.partial(_outproj_kernel, tiles_per_seq, final),
        grid=(t // tm,),
        in_specs=[
            row_spec(D_MODEL),
            pl.BlockSpec((None, None, 3, D_MODEL), lambda i: (layer, i // tiles_per_seq, 0, 0)),
            row_spec(GLA_WIDTH),
            pl.BlockSpec((None, MLA_HEADS, tm, MLA_DV),
                         lambda i: (i // tiles_per_seq, 0, i % tiles_per_seq, 0)),
            row_spec(CONV_CH), row_spec(CONV_CH), prev_spec, next_spec,
            row_spec(D_MODEL),
            _layer_spec(layer, (1, MLA_WIDTH)), _layer_spec(layer, (3, CONV_CH)),
            _layer_spec(layer, (1, CONV_CH)),
            pl.BlockSpec((1, D_MODEL), lambda i: (0, 0)),
            pl.BlockSpec((None, D_MODEL, D_MODEL), lambda i: (layer, 0, 0),
                         pipeline_mode=pl.Buffered(1)),
        ],
        out_specs=row_spec(D_MODEL),
        out_shape=jax.ShapeDtypeStruct((t, D_MODEL), jnp.float32),
        compiler_params=pltpu.CompilerParams(
            dimension_semantics=("arbitrary",), vmem_limit_bytes=VMEM_LIMIT),
        name="out_proj",
    )(x2, mod, o_gla, o_mla, cb, u, u, u, sz,
      mla_out_g, conv_w, conv_out_g, final_g, w_out_b)


_O_GLR = 2 * GLA_HEADS * GLA_DK + GLA_WIDTH
_O_MQ = _O_GLR + 2 * GLA_GATE_RANK
_O_MKV = _O_MQ + MLA_Q_LORA
_O_MKR = _O_MKV + MLA_KV_LORA
_O_CB = _O_MKR + MLA_ROPE
IN_DIM = _O_CB + 3 * CONV_CH + D_MODEL
W_PREP_ROWS = 256


W_PREP_CHUNK = 512


def _w_in_prep_kernel(wt_ref, o_ref):
    rows = o_ref.shape[0]
    eye = (lax.broadcasted_iota(jnp.int32, (rows, rows), 0)
           == lax.broadcasted_iota(jnp.int32, (rows, rows), 1)).astype(jnp.bfloat16)

    def put(dst, block):
        o_ref[:, dst:dst + block.shape[0]] = lax.dot_general(
            eye, block.astype(jnp.bfloat16), _NT,
            preferred_element_type=jnp.float32).astype(jnp.bfloat16)

    def put_range(dst, lo, hi):
        for start in range(lo, hi, W_PREP_CHUNK):
            stop = min(start + W_PREP_CHUNK, hi)
            put(dst + start - lo, wt_ref[start:stop, :])

    pad_rows = MISC_W - MLA_ROPE - 2 * GLA_GATE_RANK
    misc = jnp.concatenate([wt_ref[_O_MKR:_O_CB, :], wt_ref[_O_GLR:_O_MQ, :],
                            jnp.zeros((pad_rows, rows), jnp.float32)], axis=0)
    off = 0
    for src in ((0, _O_GLR), (_O_MQ, _O_MKV), misc, (_O_MKV, _O_MKR), (_O_CB, IN_DIM)):
        if isinstance(src, tuple):
            put_range(off, *src)
            off += src[1] - src[0]
        else:
            put(off, src)
            off += src.shape[0]


def _w_in_prep_call(w_in):
    depth, d, n = w_in.shape
    return pl.pallas_call(
        _w_in_prep_kernel,
        grid=(depth, d // W_PREP_ROWS),
        in_specs=[pl.BlockSpec((None, n, W_PREP_ROWS), lambda l, i: (l, 0, i))],
        out_specs=pl.BlockSpec((None, W_PREP_ROWS, IN_DIM_P), lambda l, i: (l, i, 0)),
        out_shape=jax.ShapeDtypeStruct((depth, d, IN_DIM_P), jnp.bfloat16),
        compiler_params=pltpu.CompilerParams(
            dimension_semantics=("arbitrary", "arbitrary"), vmem_limit_bytes=VMEM_LIMIT),
        name="w_in_prep",
    )(jnp.swapaxes(w_in, 1, 2))


def _gate_params(wg_f, bg_f, wg_b, bg_b):
    depth, rank, hk = wg_f.shape
    zeros = jnp.zeros((depth, rank, hk), jnp.float32)
    rows = jnp.concatenate([jnp.concatenate([wg_f, zeros], axis=2),
                            jnp.concatenate([zeros, wg_b], axis=2)], axis=1)
    wg_cat = jnp.pad(rows, ((0, 0), (MLA_ROPE, MISC_W - MLA_ROPE - 2 * rank), (0, 0)))
    bg_cat = jnp.concatenate([bg_f, bg_b], axis=1)[:, None, :]
    return wg_cat.astype(jnp.bfloat16), bg_cat


def _permute_w_uq(w):
    depth = w.shape[0]
    half = MLA_ROPE // 2
    w4 = w.reshape(depth, MLA_Q_LORA, MLA_HEADS, MLA_NOPE + MLA_ROPE)
    nope, x1, x2 = w4[..., :MLA_NOPE], w4[..., MLA_NOPE:MLA_NOPE + half], w4[..., MLA_NOPE + half:]
    cols = jnp.concatenate([nope, x1, x2, x2, x1], axis=-1)
    return cols.reshape(depth, MLA_Q_LORA, MLA_HEADS * QK_W).astype(jnp.bfloat16)


def _split_w_ukv(w):
    depth = w.shape[0]
    w4 = w.reshape(depth, MLA_KV_LORA, MLA_HEADS, MLA_NOPE + MLA_DV)
    wk = w4[..., :MLA_NOPE].transpose(0, 2, 1, 3)
    wv_t = w4[..., MLA_NOPE:].transpose(0, 2, 3, 1)
    return wk.astype(jnp.bfloat16), wv_t.astype(jnp.bfloat16)


def kernel(x, c, positions, ada_w, ada_b, norm_g, w_in, gla_wg_f, gla_bg_f, gla_wg_b, gla_bg_b,
           gla_norm_g, mla_q_norm_g, mla_kv_norm_g, mla_w_uq, mla_w_ukv, mla_out_g, conv_w,
           conv_out_g, w_out, final_g):
    bsz, s, d = x.shape
    t = bsz * s

    mod = _ada_call(c, ada_w, ada_b)[:, :bsz].reshape(DEPTH, bsz, 3, d)
    tabs = _rope_tab_call(positions)

    w_in_p = _w_in_prep_call(w_in)
    w_out_b = w_out.astype(jnp.bfloat16)
    wg_cat, bg_cat = _gate_params(gla_wg_f, gla_bg_f, gla_wg_b, gla_bg_b)
    wuq_p = _permute_w_uq(mla_w_uq)
    wuk_p, wuvt_p = _split_w_ukv(mla_w_ukv)

    def row(p):
        return p[:, None, :]

    def b3(a):
        return a.reshape(bsz, s, a.shape[-1])

    h = x.reshape(t, d)
    for l in range(DEPTH):
        segs = _inproj_call(l, h, mod, row(norm_g), w_in_p, s)
        gq, gk, gv, mq, misc, mkv, cb, u, sz = segs
        o_gla = _gla_call(l, b3(gq), b3(gk), b3(gv), b3(misc), wg_cat, bg_cat, row(gla_norm_g))
        o_mla = _mla_call(l, b3(mq), b3(mkv), b3(misc), tabs, row(mla_q_norm_g),
                          row(mla_kv_norm_g), wuq_p, wuk_p, wuvt_p)
        h = _outproj_call(l, h, mod, o_gla.reshape(t, -1), o_mla, cb, u, sz,
                          row(mla_out_g), conv_w, row(conv_out_g), final_g.reshape(1, d),
                          w_out_b, s, l == DEPTH - 1)
    return h.reshape(bsz, s, d)
```

```python
import functools
import math

import jax
import jax.numpy as jnp
from jax import lax
from jax.experimental import pallas as pl
from jax.experimental.pallas import tpu as pltpu

D_MODEL = 2048
DEPTH = 2
GLA_HEADS = 6
GLA_DK = 64
GLA_DV = 128
GLA_GATE_RANK = 16
GLA_GATE_TEMP = 16.0
GLA_CHUNK = 64
GLA_WIDTH = GLA_HEADS * GLA_DV
MLA_HEADS = 6
MLA_Q_LORA = 384
MLA_KV_LORA = 256
MLA_NOPE = 128
MLA_ROPE = 64
MLA_DV = 128
MLA_WIDTH = MLA_HEADS * MLA_DV
ROPE_THETA = 10000.0
CONV_CH = D_MODEL - GLA_WIDTH - MLA_WIDTH
EPS = 1e-6

LANES = 128
MXU_COLS = 256
F32_ROWS = 8
BF16_ROWS = 16
HALO_ROWS = BF16_ROWS
MISC_W = LANES
QK_W = 2 * LANES

IN_SEGMENTS = (
    ("gq", GLA_HEADS * GLA_DK, jnp.bfloat16),
    ("gk", GLA_HEADS * GLA_DK, jnp.bfloat16),
    ("gv", GLA_WIDTH, jnp.bfloat16),
    ("mq", MLA_Q_LORA, jnp.bfloat16),
    ("misc", MISC_W, jnp.float32),
    ("mkv", MLA_KV_LORA, jnp.bfloat16),
    ("cb", CONV_CH, jnp.bfloat16),
    ("cc", CONV_CH, jnp.bfloat16),
    ("cx", CONV_CH, jnp.bfloat16),
    ("z", D_MODEL, jnp.bfloat16),
)
IN_DIM_P = sum(w for _, w, _ in IN_SEGMENTS)
_SEGMENT_COLS = {}
for _name, _width, _ in IN_SEGMENTS:
    _SEGMENT_COLS[_name] = (sum(w for _, w in _SEGMENT_COLS.values()), _width)
IN_OUTPUTS = tuple(seg for seg in IN_SEGMENTS if seg[0] not in ("cc", "cx", "z")) + (
    ("u", CONV_CH, jnp.bfloat16),
    ("sz", D_MODEL, jnp.bfloat16),
)

IN_ROW_TILE = 512
IN_SUB_TILE = 256
OUT_ROW_TILE = 512
OUT_SUB_TILE = 256
Q_TILE = 512
KV_ROWS = 512
SCAN_ROWS = 256
SCAN_GROUP = 4
GLA_GROUP = 8
VMEM_LIMIT = 56 * 1024 * 1024

_NT = (((1,), (1,)), ((), ()))
_TN = (((0,), (0,)), ((), ()))


def _rms(x, g):
    ms = jnp.mean(x * x, axis=-1, keepdims=True)
    return x * lax.rsqrt(ms + EPS) * g


def _bdot(a, b):
    return jnp.dot(a, b, preferred_element_type=jnp.float32)


ADA_COLS = 1024
ADA_KROWS = 256


def _ada_kernel(ct_ref, w_ref, b_ref, o_ref, cb_s):
    d, bsz = ct_ref.shape
    tn = w_ref.shape[1]

    @pl.when((pl.program_id(0) == 0) & (pl.program_id(1) == 0))
    def _broadcast_c():
        ct = ct_ref[...]
        act = ct * jax.nn.sigmoid(ct)
        for b in range(bsz):
            cb_s[b] = jnp.broadcast_to(act[:, b:b + 1], (d, LANES))

    def body(i, accs):
        accs = list(accs)
        for g in range(ADA_KROWS // F32_ROWS):
            r = pl.ds(pl.multiple_of(i * ADA_KROWS + g * F32_ROWS, F32_ROWS), F32_ROWS)
            w = w_ref[r, :]
            for b in range(bsz):
                accs[b] = accs[b] + w * jnp.concatenate([cb_s[b, r, :]] * (tn // LANES), axis=1)
        return tuple(accs)

    zero = jnp.zeros((F32_ROWS, tn), jnp.float32)
    accs = lax.fori_loop(0, d // ADA_KROWS, body, (zero,) * bsz)
    rows = [jnp.sum(a, axis=0, keepdims=True) for a in accs]
    rows.append(jnp.zeros((o_ref.shape[0] - bsz, tn), jnp.float32))
    o_ref[...] = jnp.concatenate(rows, axis=0) + b_ref[...]


def _ada_call(c, ada_w, ada_b):
    bsz = c.shape[0]
    n = ada_w.shape[-1]
    tn = ADA_COLS
    return pl.pallas_call(
        _ada_kernel,
        grid=(DEPTH, n // tn),
        in_specs=[
            pl.BlockSpec((D_MODEL, bsz), lambda l, j: (0, 0)),
            pl.BlockSpec((None, D_MODEL, tn), lambda l, j: (l, 0, j)),
            pl.BlockSpec((None, 1, tn), lambda l, j: (l, 0, j)),
        ],
        out_specs=pl.BlockSpec((None, F32_ROWS, tn), lambda l, j: (l, 0, j)),
        out_shape=jax.ShapeDtypeStruct((DEPTH, F32_ROWS, n), jnp.float32),
        scratch_shapes=[pltpu.VMEM((bsz, D_MODEL, LANES), jnp.float32)],
        compiler_params=pltpu.CompilerParams(
            dimension_semantics=("arbitrary", "arbitrary"), vmem_limit_bytes=VMEM_LIMIT),
        name="ada_mod",
    )(c.T, ada_w, ada_b.reshape(DEPTH, 1, n))


def _rope_tab_kernel(pos_ref, invf_ref, c_ref, sa_ref, sb_ref):
    rows = pos_ref.shape[0]
    lane = lax.broadcasted_iota(jnp.int32, (rows, LANES), 1)
    pos = jnp.where(lane < MLA_ROPE, pos_ref[:, 0:1], pos_ref[:, 1:2]).astype(jnp.float32)
    ang = pos * invf_ref[...]
    half = MLA_ROPE // 2
    in_first, in_second = lane < half, (lane >= half) & (lane < MLA_ROPE)
    cos2, sin2 = jnp.cos(ang), jnp.sin(ang)
    for r, shift in ((slice(0, rows), 0), (slice(rows, 2 * rows), MLA_ROPE)):
        cos = cos2 if shift == 0 else pltpu.roll(cos2, shift, axis=1)
        sin = sin2 if shift == 0 else pltpu.roll(sin2, shift, axis=1)
        c_ref[r, :] = jnp.where(lane < MLA_ROPE, cos, 0.0)
        sa_ref[r, :] = jnp.where(in_first, -sin, 0.0)
        sb_ref[r, :] = jnp.where(in_second, sin, 0.0)


def _rope_tab_call(positions):
    bsz, s = positions.shape
    inv_freq = ROPE_THETA ** (-jnp.arange(0, MLA_ROPE, 2, dtype=jnp.float32) / MLA_ROPE)
    invf = jnp.tile(inv_freq, LANES // inv_freq.shape[0])
    pos2 = jnp.stack([positions[:, :s // 2], positions[:, s // 2:]], axis=-1)
    tab = jax.ShapeDtypeStruct((bsz, s, LANES), jnp.float32)
    spec = pl.BlockSpec((None, s, LANES), lambda b: (b, 0, 0))
    return pl.pallas_call(
        _rope_tab_kernel,
        grid=(bsz,),
        in_specs=[pl.BlockSpec((None, s // 2, 2), lambda b: (b, 0, 0)),
                  pl.BlockSpec((1, LANES), lambda b: (0, 0))],
        out_specs=[spec, spec, spec],
        out_shape=[tab, tab, tab],
        compiler_params=pltpu.CompilerParams(dimension_semantics=("arbitrary",)),
        name="rope_tables",
    )(pos2, invf.reshape(1, LANES))


def _rope(x, c, sa, sb):
    return (x * c + pltpu.roll(x, LANES - MLA_ROPE // 2, axis=1) * sa
            + pltpu.roll(x, MLA_ROPE // 2, axis=1) * sb)


def _inproj_kernel(x_ref, mod_ref, g_ref, w_ref, *refs):
    out_refs, h_s = refs[:-1], refs[-1]
    mod = mod_ref[...]
    shift, gain = mod[0:1], g_ref[...] * (1.0 + mod[1:2])
    n_sub = x_ref.shape[0] // IN_SUB_TILE

    def normalize(k):
        x = x_ref[k * IN_SUB_TILE:(k + 1) * IN_SUB_TILE, :]
        ms = jnp.mean(x * x, axis=-1, keepdims=True)
        h_s[k % 2] = (x * lax.rsqrt(ms + EPS) * gain + shift).astype(jnp.bfloat16)

    outs = {name: o_ref for (name, _, _), o_ref in zip(IN_OUTPUTS, out_refs)}

    def project(k):
        r = slice(k * IN_SUB_TILE, (k + 1) * IN_SUB_TILE)

        def proj(names):
            lo = _SEGMENT_COLS[names[0]][0]
            width = sum(_SEGMENT_COLS[n][1] for n in names)
            assert width % MXU_COLS == 0
            val = _bdot(h_s[k % 2], w_ref[:, lo:lo + width])
            return {n: val[:, _SEGMENT_COLS[n][0] - lo:_SEGMENT_COLS[n][0] - lo + _SEGMENT_COLS[n][1]]
                    for n in names}

        z = proj(("z",))["z"]
        outs["sz"][r, :] = (z * jax.nn.sigmoid(z)).astype(outs["sz"].dtype)
        conv = proj(("cb", "cc", "cx"))
        outs["u"][r, :] = (conv["cc"] * conv["cx"]).astype(outs["u"].dtype)
        outs["cb"][r, :] = conv["cb"].astype(outs["cb"].dtype)
        for names in (("gq", "gk", "gv"), ("mq", "misc", "mkv")):
            for name, val in proj(names).items():
                outs[name][r, :] = val.astype(outs[name].dtype)

    normalize(0)
    for k in range(n_sub):
        project(k)
        if k + 1 < n_sub:
            normalize(k + 1)


def _layer_spec(layer, shape):
    return pl.BlockSpec((None,) + shape, lambda *_: (layer,) + (0,) * len(shape))


def _inproj_call(layer, x2, mod, norm_g, w_in_p, seq_len):
    t = x2.shape[0]
    tm = IN_ROW_TILE
    tiles_per_seq = seq_len // tm
    out_shape = [jax.ShapeDtypeStruct((t, w), dt) for _, w, dt in IN_OUTPUTS]
    out_specs = [pl.BlockSpec((tm, w), lambda i: (i, 0)) for _, w, _ in IN_OUTPUTS]
    return pl.pallas_call(
        _inproj_kernel,
        grid=(t // tm,),
        in_specs=[
            pl.BlockSpec((tm, D_MODEL), lambda i: (i, 0)),
            pl.BlockSpec((None, None, 3, D_MODEL), lambda i: (layer, i // tiles_per_seq, 0, 0)),
            _layer_spec(layer, (1, D_MODEL)),
            pl.BlockSpec((None, D_MODEL, IN_DIM_P), lambda i: (layer, 0, 0),
                         pipeline_mode=pl.Buffered(1)),
        ],
        out_specs=out_specs,
        out_shape=out_shape,
        scratch_shapes=[pltpu.VMEM((2, IN_SUB_TILE, D_MODEL), jnp.bfloat16)],
        compiler_params=pltpu.CompilerParams(
            dimension_semantics=("arbitrary",), vmem_limit_bytes=VMEM_LIMIT),
        name="in_proj",
    )(x2, mod, norm_g, w_in_p)


def _log_sigmoid(x):
    return jnp.minimum(x, 0.0) - jnp.log(1.0 + jnp.exp(-jnp.abs(x)))


def _gla_kernel(q_ref, k_ref, v_ref, misc_ref, wgf_ref, bgf_ref, wgb_ref, bgb_ref, g_ref,
                o_ref, bf_s, bb_s, qcat_s, kv_s, dec_s, st_s, intra_s):
    s = q_ref.shape[0]
    cdim = GLA_CHUNK
    n_chunks = s // cdim
    pair_dk = 2 * GLA_DK
    pair_dv = 2 * GLA_DV
    inv_t = 1.0 / GLA_GATE_TEMP
    q_scale = GLA_DK ** -0.5

    t_row = lax.broadcasted_iota(jnp.int32, (SCAN_ROWS, SCAN_ROWS), 0)
    t_col = lax.broadcasted_iota(jnp.int32, (SCAN_ROWS, SCAN_ROWS), 1)
    same_chunk = (t_row // cdim) == (t_col // cdim)
    tri_prefix = (same_chunk & (t_col <= t_row)).astype(jnp.bfloat16)
    tri_suffix = (same_chunk & (t_col >= t_row)).astype(jnp.bfloat16)

    def split(terms):
        high = terms.astype(jnp.bfloat16)
        rest = (terms - high.astype(jnp.float32)).astype(jnp.bfloat16)
        return jnp.concatenate([high, rest], axis=1)

    def scan_body(i, carry):
        blocks = [pl.ds(pl.multiple_of((i * SCAN_GROUP + c) * SCAN_ROWS, SCAN_ROWS), SCAN_ROWS)
                  for c in range(SCAN_GROUP)]
        pre = []
        for r in blocks:
            m = misc_ref[r, :].astype(jnp.bfloat16)
            pre.append((_bdot(m, wgf_ref[...]), _bdot(m, wgb_ref[...])))
        terms = [(split(_log_sigmoid(pf + bgf_ref[...]) * inv_t),
                  split(_log_sigmoid(pb + bgb_ref[...]) * inv_t)) for pf, pb in pre]
        sums = [(_bdot(tri_prefix, tf), _bdot(tri_suffix, tb)) for tf, tb in terms]
        for r, (sf, sb) in zip(blocks, sums):
            bf_s[r, :] = sf[:, 0:pair_dk] + sf[:, pair_dk:]
            bb_s[r, :] = sb[:, 0:pair_dk] + sb[:, pair_dk:]
        return carry

    lax.fori_loop(0, s // (SCAN_ROWS * SCAN_GROUP), scan_body, 0)

    row = lax.broadcasted_iota(jnp.int32, (cdim, pair_dk), 0)
    key_pos = lax.broadcasted_iota(jnp.int32, (cdim, pair_dk), 1) & (cdim - 1)
    keep_f = key_pos <= row
    keep_b = key_pos > row
    cat_block = lax.broadcasted_iota(jnp.int32, (cdim, 2 * pair_dk), 1) // GLA_DK
    v_head = lax.broadcasted_iota(jnp.int32, (cdim, pair_dv), 1) // GLA_DV
    same_head = (lax.broadcasted_iota(jnp.int32, (pair_dv, 2 * pair_dk), 0) // GLA_DV
                 == (lax.broadcasted_iota(jnp.int32, (pair_dv, 2 * pair_dk), 1) // GLA_DK) % 2)

    def intra_body(i, carry):
        chunks = [i * GLA_GROUP + c for c in range(GLA_GROUP)]
        rows = [pl.ds(pl.multiple_of(n * cdim, cdim), cdim) for n in chunks]
        scores, k_ends = [], []
        for n, r in zip(chunks, rows):
            bf = bf_s[r, :]
            bb = bb_s[r, :]
            bf_last = bf[cdim - 1:cdim, :]
            bb_last = bb[0:1, :]
            q = q_ref[r, :].astype(jnp.float32) * q_scale
            k = k_ref[r, :].astype(jnp.float32)
            q_cat = jnp.concatenate([q * jnp.exp(bf), q * jnp.exp(bb)],
                                    axis=1).astype(jnp.bfloat16)
            k_inv = jnp.concatenate([k * jnp.exp(-bf), k * jnp.exp(-bb)], axis=1)
            k_ends.append(jnp.concatenate(
                [k * jnp.exp(bf_last - bf), k * jnp.exp(bb_last - bb)], axis=1).astype(jnp.bfloat16))
            dec_s[n] = jnp.concatenate([jnp.exp(bf_last), jnp.exp(bb_last)], axis=1)
            qcat_s[r, :] = q_cat
            k_bd = jnp.concatenate([jnp.where(cat_block == c, k_inv, 0.0) for c in range(4)],
                                   axis=0).astype(jnp.bfloat16)
            scores.append(lax.dot_general(q_cat, k_bd, _NT,
                                          preferred_element_type=jnp.float32))
        for n, r, k_end in zip(chunks, rows, k_ends):
            kv_t = lax.dot_general(v_ref[r, :], k_end, _TN,
                                   preferred_element_type=jnp.float32)
            kv_s[n] = jnp.where(same_head, kv_t, 0.0)
        for r, sc in zip(rows, scores):
            p = (jnp.where(keep_f, sc[:, 0:pair_dk], 0.0)
                 + jnp.where(keep_b, sc[:, pair_dk:], 0.0)).astype(jnp.bfloat16)
            vv = v_ref[r, :]
            v_bd = jnp.concatenate(
                [jnp.where(v_head == j, vv, jnp.zeros_like(vv)) for j in range(2)], axis=0)
            intra_s[r, :] = _bdot(p, v_bd)
        return carry

    lax.fori_loop(0, n_chunks // GLA_GROUP, intra_body, 0)

    def state_body(i, states):
        sf, sb = states
        nf = i
        nb = n_chunks - 1 - i
        st_s[nf, :, 0:pair_dk] = sf.astype(jnp.bfloat16)
        st_s[nb, :, pair_dk:] = sb.astype(jnp.bfloat16)
        sf = sf * dec_s[nf][:, 0:pair_dk] + kv_s[nf, :, 0:pair_dk]
        sb = sb * dec_s[nb][:, pair_dk:] + kv_s[nb, :, pair_dk:]
        return sf, sb

    zero = jnp.zeros((pair_dv, pair_dk), jnp.float32)
    lax.fori_loop(0, n_chunks, state_body, (zero, zero))

    g = g_ref[...]

    def out_body(i, carry):
        chunks = [i * GLA_GROUP + c for c in range(GLA_GROUP)]
        rows = [pl.ds(pl.multiple_of(n * cdim, cdim), cdim) for n in chunks]
        inter = [lax.dot_general(qcat_s[r, :], st_s[n], _NT, preferred_element_type=jnp.float32)
                 for n, r in zip(chunks, rows)]
        for r, o_inter in zip(rows, inter):
            o = intra_s[r, :] + o_inter
            for j in range(2):
                o_ref[r, j * GLA_DV:(j + 1) * GLA_DV] = _rms(
                    o[:, j * GLA_DV:(j + 1) * GLA_DV], g).astype(o_ref.dtype)
        return carry

    lax.fori_loop(0, n_chunks // GLA_GROUP, out_body, 0)


def _gla_call(layer, gq, gk, gv, misc, wg_cat, bg_cat, gla_norm_g):
    bsz, s, _ = gq.shape
    pairs = GLA_HEADS // 2
    n_chunks = s // GLA_CHUNK
    pair_dk, pair_dv = 2 * GLA_DK, 2 * GLA_DV
    qk_spec = pl.BlockSpec((None, s, pair_dk), lambda b, p: (b, 0, p))
    v_spec = pl.BlockSpec((None, s, pair_dv), lambda b, p: (b, 0, p))

    def gate_specs(direction):
        return (pl.BlockSpec((None, MISC_W, pair_dk), lambda b, p: (layer, 0, direction * pairs + p)),
                pl.BlockSpec((None, 1, pair_dk), lambda b, p: (layer, 0, direction * pairs + p)))

    return pl.pallas_call(
        _gla_kernel,
        grid=(bsz, pairs),
        in_specs=[qk_spec, qk_spec, v_spec,
                  pl.BlockSpec((None, s, MISC_W), lambda b, p: (b, 0, 0)),
                  *gate_specs(0), *gate_specs(1),
                  _layer_spec(layer, (1, GLA_DV))],
        out_specs=v_spec,
        out_shape=jax.ShapeDtypeStruct((bsz, s, GLA_WIDTH), jnp.bfloat16),
        scratch_shapes=[
            pltpu.VMEM((s, pair_dk), jnp.float32),
            pltpu.VMEM((s, pair_dk), jnp.float32),
            pltpu.VMEM((s, 2 * pair_dk), jnp.bfloat16),
            pltpu.VMEM((n_chunks, pair_dv, 2 * pair_dk), jnp.float32),
            pltpu.VMEM((n_chunks, 1, 2 * pair_dk), jnp.float32),
            pltpu.VMEM((n_chunks, pair_dv, 2 * pair_dk), jnp.bfloat16),
            pltpu.VMEM((s, pair_dv), jnp.float32),
        ],
        compiler_params=pltpu.CompilerParams(
            dimension_semantics=("arbitrary", "arbitrary"), vmem_limit_bytes=VMEM_LIMIT),
        name="gla",
    )(gq, gk, gv, misc, wg_cat, bg_cat, wg_cat, bg_cat, gla_norm_g)


def _mla_kernel(mq_ref, mkv_ref, misc_ref, c_ref, sa_ref, sb_ref, gq_ref, gkv_ref,
                wuq_ref, wuk_ref, wuvt_ref, o_ref, ckv_s, q_s, k_s, vt_s, sc_s, top_s):
    s = mkv_ref.shape[0]
    n_tiles = s // Q_TILE
    assert n_tiles % 2 == 0 and MLA_HEADS % 2 == 0
    blocks = [slice(i * KV_ROWS, (i + 1) * KV_ROWS) for i in range(s // KV_ROWS)]
    q_mult = (MLA_NOPE + MLA_ROPE) ** -0.5 * math.log2(math.e)

    for r in blocks:
        tabs = (c_ref[r, :], sa_ref[r, :], sb_ref[r, :])
        ckv_s[r, :] = _rms(mkv_ref[r, :].astype(jnp.float32), gkv_ref[...]).astype(jnp.bfloat16)
        k_rope = _rope(misc_ref[r, :], *tabs).astype(jnp.bfloat16)
        for slot in range(2):
            k_s[slot, r, MLA_NOPE:] = k_rope
        cq = _rms(mq_ref[r, :].astype(jnp.float32), gq_ref[...]).astype(jnp.bfloat16)
        qf = _bdot(cq, wuq_ref[...])
        cos_t, sin_t = tabs[0], tabs[1] + tabs[2]
        for h in range(MLA_HEADS):
            qh = qf[:, h * QK_W:(h + 1) * QK_W]
            rot = qh[:, MLA_NOPE:]
            q_rope = rot * cos_t + pltpu.roll(rot, MLA_ROPE, axis=1) * sin_t
            q_s[h, r, :] = (jnp.concatenate([qh[:, 0:MLA_NOPE], q_rope], axis=1)
                            * q_mult).astype(jnp.bfloat16)
    for slot in range(2):
        vt_s[slot, MLA_DV:, :] = jnp.ones((vt_s.shape[1] - MLA_DV, s), jnp.bfloat16)

    def build_kv(h, slot):
        for r in blocks:
            ckv = ckv_s[r, :]
            k_s[slot, r, 0:MLA_NOPE] = _bdot(ckv, wuk_ref[h]).astype(jnp.bfloat16)
            vt_s[slot, 0:MLA_DV, r] = lax.dot_general(
                wuvt_ref[h], ckv, _NT, preferred_element_type=jnp.float32).astype(jnp.bfloat16)

    def scores_t(h, slot, j):
        q = q_s[h, j * Q_TILE:(j + 1) * Q_TILE, :]
        sc_t = lax.dot_general(k_s[slot], q, _NT, preferred_element_type=jnp.float32)
        sc_s[j % 2] = sc_t
        top_s[j % 2] = jnp.broadcast_to(jnp.max(sc_t, axis=0, keepdims=True), (F32_ROWS, Q_TILE))

    def finish(h, slot, j):
        p_t = jnp.exp2(sc_s[j % 2] - top_s[j % 2][0:1, :]).astype(jnp.bfloat16)
        o_t = _bdot(vt_s[slot], p_t)
        o_ref[h, j * Q_TILE:(j + 1) * Q_TILE, :] = (
            o_t[0:MLA_DV, :] / o_t[MLA_DV:MLA_DV + 1, :]).T.astype(o_ref.dtype)

    build_kv(0, 0)
    scores_t(0, 0, 0)

    def head_pair(i, carry):
        for slot in range(2):
            h = 2 * i + slot
            h_next = jnp.minimum(h + 1, MLA_HEADS - 1)
            for j in range(n_tiles):
                if j + 1 < n_tiles:
                    scores_t(h, slot, j + 1)
                else:
                    build_kv(h_next, 1 - slot)
                    scores_t(h_next, 1 - slot, 0)
                finish(h, slot, j)
        return carry

    lax.fori_loop(0, MLA_HEADS // 2, head_pair, 0)


def _mla_call(layer, mq, mkv, misc, tabs, gq, gkv, wuq_p, wuk_p, wuvt_p):
    bsz, s, _ = mq.shape

    def per_row(width):
        return pl.BlockSpec((None, s, width), lambda b: (b, 0, 0))

    return pl.pallas_call(
        _mla_kernel,
        grid=(bsz,),
        in_specs=[
            per_row(MLA_Q_LORA), per_row(MLA_KV_LORA), per_row(MISC_W),
            per_row(LANES), per_row(LANES), per_row(LANES),
            _layer_spec(layer, (1, MLA_Q_LORA)),
            _layer_spec(layer, (1, MLA_KV_LORA)),
            _layer_spec(layer, (MLA_Q_LORA, MLA_HEADS * QK_W)),
            _layer_spec(layer, (MLA_HEADS, MLA_KV_LORA, MLA_NOPE)),
            _layer_spec(layer, (MLA_HEADS, MLA_DV, MLA_KV_LORA)),
        ],
        out_specs=pl.BlockSpec((None, MLA_HEADS, s, MLA_DV), lambda b: (b, 0, 0, 0)),
        out_shape=jax.ShapeDtypeStruct((bsz, MLA_HEADS, s, MLA_DV), jnp.bfloat16),
        scratch_shapes=[
            pltpu.VMEM((s, MLA_KV_LORA), jnp.bfloat16),
            pltpu.VMEM((MLA_HEADS, s, QK_W), jnp.bfloat16),
            pltpu.VMEM((2, s, QK_W), jnp.bfloat16),
            pltpu.VMEM((2, MLA_DV + BF16_ROWS, s), jnp.bfloat16),
            pltpu.VMEM((2, s, Q_TILE), jnp.float32),
            pltpu.VMEM((2, F32_ROWS, Q_TILE), jnp.float32),
        ],
        compiler_params=pltpu.CompilerParams(
            dimension_semantics=("arbitrary",),
            vmem_limit_bytes=VMEM_LIMIT),
        name="mla",
    )(mq, mkv, misc, *tabs, gq, gkv, wuq_p, wuk_p, wuvt_p)


def _outproj_kernel(tiles_per_seq, final, x_ref, mod_ref, ogla_ref, omla_ref, cb_ref, u_ref,
                    up_ref, un_ref, sz_ref, mg_ref, cw_ref, cg_ref, fg_ref, w_ref, o_ref):
    f32 = jnp.float32
    j = pl.program_id(0) % tiles_per_seq
    tm = x_ref.shape[0]

    u = u_ref[...].astype(f32)
    prev_ok = (j > 0).astype(f32)
    next_ok = (j < tiles_per_seq - 1).astype(f32)
    u_prev = up_ref[...].astype(f32)[HALO_ROWS - 1:, :] * prev_ok
    u_next = un_ref[...].astype(f32)[0:1, :] * next_ok
    rows = lax.broadcasted_iota(jnp.int32, u.shape, 0)
    up = jnp.where(rows == 0, u_prev, pltpu.roll(u, 1, axis=0))
    un = jnp.where(rows == tm - 1, u_next, pltpu.roll(u, tm - 1, axis=0))
    cw = cw_ref[...]
    conv = up * cw[0:1] + u * cw[1:2] + un * cw[2:3]
    gate = mod_ref[...][2:3]

    def gated_dot(r, lo, y_part):
        hi = lo + y_part.shape[1]
        y = (y_part * sz_ref[r, lo:hi].astype(f32)).astype(jnp.bfloat16)
        return _bdot(y, w_ref[lo:hi, :])

    def finish(r, acc):
        out = x_ref[r, :] + gate * acc
        if final:
            out = _rms(out, fg_ref[...])
        o_ref[r, :] = out

    pending = None
    for k in range(tm // OUT_SUB_TILE):
        r = slice(k * OUT_SUB_TILE, (k + 1) * OUT_SUB_TILE)
        acc = gated_dot(r, 0, ogla_ref[r, :].astype(f32))
        if pending is not None:
            finish(*pending)
        omla = jnp.concatenate([omla_ref[h, r, :] for h in range(MLA_HEADS)], axis=1)
        acc += gated_dot(r, GLA_WIDTH, _rms(omla.astype(f32), mg_ref[...]))
        acc += gated_dot(r, GLA_WIDTH + MLA_WIDTH,
                         _rms(cb_ref[r, :].astype(f32) * conv[r, :], cg_ref[...]))
        pending = (r, acc)
    finish(*pending)


def _outproj_call(layer, x2, mod, o_gla, o_mla, cb, u, sz, mla_out_g, conv_w, conv_out_g,
                  final_g, w_out_b, seq_len, final):
    t = x2.shape[0]
    tm = OUT_ROW_TILE
    tiles_per_seq = seq_len // tm
    halo_per_tile = tm // HALO_ROWS
    n_halo = t // HALO_ROWS

    def row_spec(w):
        return pl.BlockSpec((tm, w), lambda i: (i, 0))

    prev_spec = pl.BlockSpec((HALO_ROWS, CONV_CH),
                             lambda i: (jnp.maximum(i * halo_per_tile - 1, 0), 0))
    next_spec = pl.BlockSpec((HALO_ROWS, CONV_CH),
                             lambda i: (jnp.minimum((i + 1) * halo_per_tile, n_halo - 1), 0))

    return pl.pallas_call(
        functools.partial(_outproj_kernel, tiles_per_seq, final),
        grid=(t // tm,),
        in_specs=[
            row_spec(D_MODEL),
            pl.BlockSpec((None, None, 3, D_MODEL), lambda i: (layer, i // tiles_per_seq, 0, 0)),
            row_spec(GLA_WIDTH),
            pl.BlockSpec((None, MLA_HEADS, tm, MLA_DV),
                         lambda i: (i // tiles_per_seq, 0, i % tiles_per_seq, 0)),
            row_spec(CONV_CH), row_spec(CONV_CH), prev_spec, next_spec,
            row_spec(D_MODEL),
            _layer_spec(layer, (1, MLA_WIDTH)), _layer_spec(layer, (3, CONV_CH)),
            _layer_spec(layer, (1, CONV_CH)),
            pl.BlockSpec((1, D_MODEL), lambda i: (0, 0)),
            pl.BlockSpec((None, D_MODEL, D_MODEL), lambda i: (layer, 0, 0),
                         pipeline_mode=pl.Buffered(1)),
        ],
        out_specs=row_spec(D_MODEL),
        out_shape=jax.ShapeDtypeStruct((t, D_MODEL), jnp.float32),
        compiler_params=pltpu.CompilerParams(
            dimension_semantics=("arbitrary",), vmem_limit_bytes=VMEM_LIMIT),
        name="out_proj",
    )(x2, mod, o_gla, o_mla, cb, u, u, u, sz,
      mla_out_g, conv_w, conv_out_g, final_g, w_out_b)


_O_GLR = 2 * GLA_HEADS * GLA_DK + GLA_WIDTH
_O_MQ = _O_GLR + 2 * GLA_GATE_RANK
_O_MKV = _O_MQ + MLA_Q_LORA
_O_MKR = _O_MKV + MLA_KV_LORA
_O_CB = _O_MKR + MLA_ROPE
IN_DIM = _O_CB + 3 * CONV_CH + D_MODEL
W_PREP_ROWS = 256


W_PREP_CHUNK = 512


def _w_in_prep_kernel(wt_ref, o_ref):
    rows = o_ref.shape[0]
    eye = (lax.broadcasted_iota(jnp.int32, (rows, rows), 0)
           == lax.broadcasted_iota(jnp.int32, (rows, rows), 1)).astype(jnp.bfloat16)

    def put(dst, block):
        o_ref[:, dst:dst + block.shape[0]] = lax.dot_general(
            eye, block.astype(jnp.bfloat16), _NT,
            preferred_element_type=jnp.float32).astype(jnp.bfloat16)

    def put_range(dst, lo, hi):
        for start in range(lo, hi, W_PREP_CHUNK):
            stop = min(start + W_PREP_CHUNK, hi)
            put(dst + start - lo, wt_ref[start:stop, :])

    pad_rows = MISC_W - MLA_ROPE - 2 * GLA_GATE_RANK
    misc = jnp.concatenate([wt_ref[_O_MKR:_O_CB, :], wt_ref[_O_GLR:_O_MQ, :],
                            jnp.zeros((pad_rows, rows), jnp.float32)], axis=0)
    off = 0
    for src in ((0, _O_GLR), (_O_MQ, _O_MKV), misc, (_O_MKV, _O_MKR), (_O_CB, IN_DIM)):
        if isinstance(src, tuple):
            put_range(off, *src)
            off += src[1] - src[0]
        else:
            put(off, src)
            off += src.shape[0]


def _w_in_prep_call(w_in):
    depth, d, n = w_in.shape
    return pl.pallas_call(
        _w_in_prep_kernel,
        grid=(depth, d // W_PREP_ROWS),
        in_specs=[pl.BlockSpec((None, n, W_PREP_ROWS), lambda l, i: (l, 0, i))],
        out_specs=pl.BlockSpec((None, W_PREP_ROWS, IN_DIM_P), lambda l, i: (l, i, 0)),
        out_shape=jax.ShapeDtypeStruct((depth, d, IN_DIM_P), jnp.bfloat16),
        compiler_params=pltpu.CompilerParams(
            dimension_semantics=("arbitrary", "arbitrary"), vmem_limit_bytes=VMEM_LIMIT),
        name="w_in_prep",
    )(jnp.swapaxes(w_in, 1, 2))


def _gate_params(wg_f, bg_f, wg_b, bg_b):
    depth, rank, hk = wg_f.shape
    zeros = jnp.zeros((depth, rank, hk), jnp.float32)
    rows = jnp.concatenate([jnp.concatenate([wg_f, zeros], axis=2),
                            jnp.concatenate([zeros, wg_b], axis=2)], axis=1)
    wg_cat = jnp.pad(rows, ((0, 0), (MLA_ROPE, MISC_W - MLA_ROPE - 2 * rank), (0, 0)))
    bg_cat = jnp.concatenate([bg_f, bg_b], axis=1)[:, None, :]
    return wg_cat.astype(jnp.bfloat16), bg_cat


def _permute_w_uq(w):
    depth = w.shape[0]
    half = MLA_ROPE // 2
    w4 = w.reshape(depth, MLA_Q_LORA, MLA_HEADS, MLA_NOPE + MLA_ROPE)
    nope, x1, x2 = w4[..., :MLA_NOPE], w4[..., MLA_NOPE:MLA_NOPE + half], w4[..., MLA_NOPE + half:]
    cols = jnp.concatenate([nope, x1, x2, x2, x1], axis=-1)
    return cols.reshape(depth, MLA_Q_LORA, MLA_HEADS * QK_W).astype(jnp.bfloat16)


def _split_w_ukv(w):
    depth = w.shape[0]
    w4 = w.reshape(depth, MLA_KV_LORA, MLA_HEADS, MLA_NOPE + MLA_DV)
    wk = w4[..., :MLA_NOPE].transpose(0, 2, 1, 3)
    wv_t = w4[..., MLA_NOPE:].transpose(0, 2, 3, 1)
    return wk.astype(jnp.bfloat16), wv_t.astype(jnp.bfloat16)


def kernel(x, c, positions, ada_w, ada_b, norm_g, w_in, gla_wg_f, gla_bg_f, gla_wg_b, gla_bg_b,
           gla_norm_g, mla_q_norm_g, mla_kv_norm_g, mla_w_uq, mla_w_ukv, mla_out_g, conv_w,
           conv_out_g, w_out, final_g):
    bsz, s, d = x.shape
    t = bsz * s

    mod = _ada_call(c, ada_w, ada_b)[:, :bsz].reshape(DEPTH, bsz, 3, d)
    tabs = _rope_tab_call(positions)

    w_in_p = _w_in_prep_call(w_in)
    w_out_b = w_out.astype(jnp.bfloat16)
    wg_cat, bg_cat = _gate_params(gla_wg_f, gla_bg_f, gla_wg_b, gla_bg_b)
    wuq_p = _permute_w_uq(mla_w_uq)
    wuk_p, wuvt_p = _split_w_ukv(mla_w_ukv)

    def row(p):
        return p[:, None, :]

    def b3(a):
        return a.reshape(bsz, s, a.shape[-1])

    h = x.reshape(t, d)
    for l in range(DEPTH):
        segs = _inproj_call(l, h, mod, row(norm_g), w_in_p, s)
        gq, gk, gv, mq, misc, mkv, cb, u, sz = segs
        o_gla = _gla_call(l, b3(gq), b3(gk), b3(gv), b3(misc), wg_cat, bg_cat, row(gla_norm_g))
        o_mla = _mla_call(l, b3(mq), b3(mkv), b3(misc), tabs, row(mla_q_norm_g),
                          row(mla_kv_norm_g), wuq_p, wuk_p, wuvt_p)
        h = _outproj_call(l, h, mod, o_gla.reshape(t, -1), o_mla, cb, u, sz,
                          row(mla_out_g), conv_w, row(conv_out_g), final_g.reshape(1, d),
                          w_out_b, s, l == DEPTH - 1)
    return h.reshape(bsz, s, d)
```

```python
import functools
import math

import jax
import jax.numpy as jnp
from jax import lax
from jax.experimental import pallas as pl
from jax.experimental.pallas import tpu as pltpu

D_MODEL = 2048
DEPTH = 2
GLA_HEADS = 6
GLA_DK = 64
GLA_DV = 128
GLA_GATE_RANK = 16
GLA_GATE_TEMP = 16.0
GLA_CHUNK = 64
GLA_WIDTH = GLA_HEADS * GLA_DV
MLA_HEADS = 6
MLA_Q_LORA = 384
MLA_KV_LORA = 256
MLA_NOPE = 128
MLA_ROPE = 64
MLA_DV = 128
MLA_WIDTH = MLA_HEADS * MLA_DV
ROPE_THETA = 10000.0
CONV_CH = D_MODEL - GLA_WIDTH - MLA_WIDTH
EPS = 1e-6

LANES = 128
MXU_COLS = 256
F32_ROWS = 8
BF16_ROWS = 16
HALO_ROWS = BF16_ROWS
MISC_W = LANES
QK_W = 2 * LANES

IN_SEGMENTS = (
    ("gq", GLA_HEADS * GLA_DK, jnp.bfloat16),
    ("gk", GLA_HEADS * GLA_DK, jnp.bfloat16),
    ("gv", GLA_WIDTH, jnp.bfloat16),
    ("mq", MLA_Q_LORA, jnp.bfloat16),
    ("misc", MISC_W, jnp.float32),
    ("mkv", MLA_KV_LORA, jnp.bfloat16),
    ("cb", CONV_CH, jnp.bfloat16),
    ("cc", CONV_CH, jnp.bfloat16),
    ("cx", CONV_CH, jnp.bfloat16),
    ("z", D_MODEL, jnp.bfloat16),
)
IN_DIM_P = sum(w for _, w, _ in IN_SEGMENTS)
_SEGMENT_COLS = {}
for _name, _width, _ in IN_SEGMENTS:
    _SEGMENT_COLS[_name] = (sum(w for _, w in _SEGMENT_COLS.values()), _width)
IN_OUTPUTS = tuple(seg for seg in IN_SEGMENTS if seg[0] not in ("cc", "cx", "z")) + (
    ("u", CONV_CH, jnp.bfloat16),
    ("sz", D_MODEL, jnp.bfloat16),
)

IN_ROW_TILE = 512
IN_SUB_TILE = 256
OUT_ROW_TILE = 512
OUT_SUB_TILE = 256
Q_TILE = 512
KV_ROWS = 512
SCAN_ROWS = 256
SCAN_GROUP = 4
GLA_GROUP = 8
VMEM_LIMIT = 56 * 1024 * 1024

_NT = (((1,), (1,)), ((), ()))
_TN = (((0,), (0,)), ((), ()))


def _rms(x, g):
    ms = jnp.mean(x * x, axis=-1, keepdims=True)
    return x * lax.rsqrt(ms + EPS) * g


def _bdot(a, b):
    return jnp.dot(a, b, preferred_element_type=jnp.float32)


ADA_COLS = 1024
ADA_KROWS = 256


def _ada_kernel(ct_ref, w_ref, b_ref, o_ref, cb_s):
    d, bsz = ct_ref.shape
    tn = w_ref.shape[1]

    @pl.when((pl.program_id(0) == 0) & (pl.program_id(1) == 0))
    def _broadcast_c():
        ct = ct_ref[...]
        act = ct * jax.nn.sigmoid(ct)
        for b in range(bsz):
            cb_s[b] = jnp.broadcast_to(act[:, b:b + 1], (d, LANES))

    def body(i, accs):
        accs = list(accs)
        for g in range(ADA_KROWS // F32_ROWS):
            r = pl.ds(pl.multiple_of(i * ADA_KROWS + g * F32_ROWS, F32_ROWS), F32_ROWS)
            w = w_ref[r, :]
            for b in range(bsz):
                accs[b] = accs[b] + w * jnp.concatenate([cb_s[b, r, :]] * (tn // LANES), axis=1)
        return tuple(accs)

    zero = jnp.zeros((F32_ROWS, tn), jnp.float32)
    accs = lax.fori_loop(0, d // ADA_KROWS, body, (zero,) * bsz)
    rows = [jnp.sum(a, axis=0, keepdims=True) for a in accs]
    rows.append(jnp.zeros((o_ref.shape[0] - bsz, tn), jnp.float32))
    o_ref[...] = jnp.concatenate(rows, axis=0) + b_ref[...]


def _ada_call(c, ada_w, ada_b):
    bsz = c.shape[0]
    n = ada_w.shape[-1]
    tn = ADA_COLS
    return pl.pallas_call(
        _ada_kernel,
        grid=(DEPTH, n // tn),
        in_specs=[
            pl.BlockSpec((D_MODEL, bsz), lambda l, j: (0, 0)),
            pl.BlockSpec((None, D_MODEL, tn), lambda l, j: (l, 0, j)),
            pl.BlockSpec((None, 1, tn), lambda l, j: (l, 0, j)),
        ],
        out_specs=pl.BlockSpec((None, F32_ROWS, tn), lambda l, j: (l, 0, j)),
        out_shape=jax.ShapeDtypeStruct((DEPTH, F32_ROWS, n), jnp.float32),
        scratch_shapes=[pltpu.VMEM((bsz, D_MODEL, LANES), jnp.float32)],
        compiler_params=pltpu.CompilerParams(
            dimension_semantics=("arbitrary", "arbitrary"), vmem_limit_bytes=VMEM_LIMIT),
        name="ada_mod",
    )(c.T, ada_w, ada_b.reshape(DEPTH, 1, n))


def _rope_tables(pos_ref, invf_ref, c_ref, sa_ref, sb_ref):
    rows = pos_ref.shape[0]
    lane = lax.broadcasted_iota(jnp.int32, (rows, LANES), 1)
    pos = jnp.where(lane < MLA_ROPE, pos_ref[:, 0:1], pos_ref[:, 1:2]).astype(jnp.float32)
    ang = pos * invf_ref[...]
    half = MLA_ROPE // 2
    in_first, in_second = lane < half, (lane >= half) & (lane < MLA_ROPE)
    cos2, sin2 = jnp.cos(ang), jnp.sin(ang)
    for r, shift in ((slice(0, rows), 0), (slice(rows, 2 * rows), MLA_ROPE)):
        cos = cos2 if shift == 0 else pltpu.roll(cos2, shift, axis=1)
        sin = sin2 if shift == 0 else pltpu.roll(sin2, shift, axis=1)
        c_ref[r, :] = jnp.where(lane < MLA_ROPE, cos, 0.0)
        sa_ref[r, :] = jnp.where(in_first, -sin, 0.0)
        sb_ref[r, :] = jnp.where(in_second, sin, 0.0)


def _rope_table_inputs(positions, tile_rows):
    inv_freq = ROPE_THETA ** (-jnp.arange(0, MLA_ROPE, 2, dtype=jnp.float32) / MLA_ROPE)
    invf = jnp.tile(inv_freq, LANES // inv_freq.shape[0]).reshape(1, LANES)
    pairs = positions.reshape(-1, 2, tile_rows // 2).transpose(0, 2, 1)
    return pairs, invf


def _rope(x, c, sa, sb):
    return (x * c + pltpu.roll(x, LANES - MLA_ROPE // 2, axis=1) * sa
            + pltpu.roll(x, MLA_ROPE // 2, axis=1) * sb)


def _inproj_kernel(with_rope_tables, x_ref, mod_ref, g_ref, w_ref, wout_ref, *refs):
    n_out = len(IN_OUTPUTS)
    if with_rope_tables:
        pos_ref, invf_ref = refs[0:2]
        out_refs, wout_b_ref = refs[2:2 + n_out], refs[2 + n_out]
        _rope_tables(pos_ref, invf_ref, *refs[3 + n_out:6 + n_out])
    else:
        out_refs, wout_b_ref = refs[0:n_out], refs[n_out]
    h_s = refs[-1]
    wout_b_ref[...] = wout_ref[...].astype(jnp.bfloat16)
    mod = mod_ref[...]
    shift, gain = mod[0:1], g_ref[...] * (1.0 + mod[1:2])
    n_sub = x_ref.shape[0] // IN_SUB_TILE

    def normalize(k):
        x = x_ref[k * IN_SUB_TILE:(k + 1) * IN_SUB_TILE, :]
        ms = jnp.mean(x * x, axis=-1, keepdims=True)
        h_s[k % 2] = (x * lax.rsqrt(ms + EPS) * gain + shift).astype(jnp.bfloat16)

    outs = {name: o_ref for (name, _, _), o_ref in zip(IN_OUTPUTS, out_refs)}

    def project(k):
        r = slice(k * IN_SUB_TILE, (k + 1) * IN_SUB_TILE)

        def proj(names):
            lo = _SEGMENT_COLS[names[0]][0]
            width = sum(_SEGMENT_COLS[n][1] for n in names)
            assert width % MXU_COLS == 0
            val = _bdot(h_s[k % 2], w_ref[:, lo:lo + width])
            return {n: val[:, _SEGMENT_COLS[n][0] - lo:_SEGMENT_COLS[n][0] - lo + _SEGMENT_COLS[n][1]]
                    for n in names}

        z = proj(("z",))["z"]
        outs["sz"][r, :] = (z * jax.nn.sigmoid(z)).astype(outs["sz"].dtype)
        conv = proj(("cb", "cc", "cx"))
        outs["u"][r, :] = (conv["cc"] * conv["cx"]).astype(outs["u"].dtype)
        outs["cb"][r, :] = conv["cb"].astype(outs["cb"].dtype)
        for names in (("gq", "gk", "gv"), ("mq", "misc", "mkv")):
            for name, val in proj(names).items():
                outs[name][r, :] = val.astype(outs[name].dtype)

    normalize(0)
    for k in range(n_sub):
        project(k)
        if k + 1 < n_sub:
            normalize(k + 1)


def _layer_spec(layer, shape):
    return pl.BlockSpec((None,) + shape, lambda *_: (layer,) + (0,) * len(shape))


def _inproj_call(layer, x2, mod, norm_g, w_in_p, w_out, seq_len, positions=None):
    t = x2.shape[0]
    tm = IN_ROW_TILE
    n_steps = t // tm
    tiles_per_seq = seq_len // tm
    d_out = w_out.shape[-1]
    wout_rows = w_out.shape[1] // n_steps
    n_out = len(IN_OUTPUTS)
    in_specs = [
        pl.BlockSpec((tm, D_MODEL), lambda i: (i, 0)),
        pl.BlockSpec((None, None, 3, D_MODEL), lambda i: (layer, i // tiles_per_seq, 0, 0)),
        _layer_spec(layer, (1, D_MODEL)),
        pl.BlockSpec((None, D_MODEL, IN_DIM_P), lambda i: (layer, 0, 0),
                     pipeline_mode=pl.Buffered(1)),
        pl.BlockSpec((None, wout_rows, d_out), lambda i: (layer, i, 0)),
    ]
    operands = [x2, mod, norm_g, w_in_p, w_out]
    out_shape = [jax.ShapeDtypeStruct((t, w), dt) for _, w, dt in IN_OUTPUTS]
    out_specs = [pl.BlockSpec((tm, w), lambda i: (i, 0)) for _, w, _ in IN_OUTPUTS]
    out_shape.append(jax.ShapeDtypeStruct(w_out.shape[1:], jnp.bfloat16))
    out_specs.append(pl.BlockSpec((wout_rows, d_out), lambda i: (i, 0)))
    if positions is not None:
        pairs, invf = _rope_table_inputs(positions, tm)
        in_specs += [pl.BlockSpec((None, tm // 2, 2), lambda i: (i, 0, 0)),
                     pl.BlockSpec((1, LANES), lambda i: (0, 0))]
        operands += [pairs, invf]
        out_shape += [jax.ShapeDtypeStruct((t, LANES), jnp.float32)] * 3
        out_specs += [pl.BlockSpec((tm, LANES), lambda i: (i, 0))] * 3
    outs = pl.pallas_call(
        functools.partial(_inproj_kernel, positions is not None),
        grid=(n_steps,),
        in_specs=in_specs,
        out_specs=out_specs,
        out_shape=out_shape,
        scratch_shapes=[pltpu.VMEM((2, IN_SUB_TILE, D_MODEL), jnp.bfloat16)],
        compiler_params=pltpu.CompilerParams(
            dimension_semantics=("arbitrary",), vmem_limit_bytes=VMEM_LIMIT),
        name="in_proj",
    )(*operands)
    return outs[:n_out], outs[n_out], tuple(outs[n_out + 1:])


def _log_sigmoid(x):
    return jnp.minimum(x, 0.0) - jnp.log(1.0 + jnp.exp(-jnp.abs(x)))


def _gla_kernel(q_ref, k_ref, v_ref, misc_ref, wgf_ref, bgf_ref, wgb_ref, bgb_ref, g_ref,
                o_ref, bf_s, bb_s, qcat_s, kv_s, dec_s, st_s, intra_s):
    s = q_ref.shape[0]
    cdim = GLA_CHUNK
    n_chunks = s // cdim
    pair_dk = 2 * GLA_DK
    pair_dv = 2 * GLA_DV
    inv_t = 1.0 / GLA_GATE_TEMP
    q_scale = GLA_DK ** -0.5

    t_row = lax.broadcasted_iota(jnp.int32, (SCAN_ROWS, SCAN_ROWS), 0)
    t_col = lax.broadcasted_iota(jnp.int32, (SCAN_ROWS, SCAN_ROWS), 1)
    same_chunk = (t_row // cdim) == (t_col // cdim)
    tri_prefix = (same_chunk & (t_col <= t_row)).astype(jnp.bfloat16)
    tri_suffix = (same_chunk & (t_col >= t_row)).astype(jnp.bfloat16)

    def split(terms):
        high = terms.astype(jnp.bfloat16)
        rest = (terms - high.astype(jnp.float32)).astype(jnp.bfloat16)
        return jnp.concatenate([high, rest], axis=1)

    def scan_body(i, carry):
        blocks = [pl.ds(pl.multiple_of((i * SCAN_GROUP + c) * SCAN_ROWS, SCAN_ROWS), SCAN_ROWS)
                  for c in range(SCAN_GROUP)]
        pre = []
        for r in blocks:
            m = misc_ref[r, :].astype(jnp.bfloat16)
            pre.append((_bdot(m, wgf_ref[...]), _bdot(m, wgb_ref[...])))
        terms = [(split(_log_sigmoid(pf + bgf_ref[...]) * inv_t),
                  split(_log_sigmoid(pb + bgb_ref[...]) * inv_t)) for pf, pb in pre]
        sums = [(_bdot(tri_prefix, tf), _bdot(tri_suffix, tb)) for tf, tb in terms]
        for r, (sf, sb) in zip(blocks, sums):
            bf_s[r, :] = sf[:, 0:pair_dk] + sf[:, pair_dk:]
            bb_s[r, :] = sb[:, 0:pair_dk] + sb[:, pair_dk:]
        return carry

    lax.fori_loop(0, s // (SCAN_ROWS * SCAN_GROUP), scan_body, 0)

    row = lax.broadcasted_iota(jnp.int32, (cdim, pair_dk), 0)
    key_pos = lax.broadcasted_iota(jnp.int32, (cdim, pair_dk), 1) & (cdim - 1)
    keep_f = key_pos <= row
    keep_b = key_pos > row
    cat_block = lax.broadcasted_iota(jnp.int32, (cdim, 2 * pair_dk), 1) // GLA_DK
    v_head = lax.broadcasted_iota(jnp.int32, (cdim, pair_dv), 1) // GLA_DV
    same_head = (lax.broadcasted_iota(jnp.int32, (pair_dv, 2 * pair_dk), 0) // GLA_DV
                 == (lax.broadcasted_iota(jnp.int32, (pair_dv, 2 * pair_dk), 1) // GLA_DK) % 2)

    def intra_body(i, carry):
        chunks = [i * GLA_GROUP + c for c in range(GLA_GROUP)]
        rows = [pl.ds(pl.multiple_of(n * cdim, cdim), cdim) for n in chunks]
        scores, k_ends = [], []
        for n, r in zip(chunks, rows):
            bf = bf_s[r, :]
            bb = bb_s[r, :]
            bf_last = bf[cdim - 1:cdim, :]
            bb_last = bb[0:1, :]
            q = q_ref[r, :].astype(jnp.float32) * q_scale
            k = k_ref[r, :].astype(jnp.float32)
            q_cat = jnp.concatenate([q * jnp.exp(bf), q * jnp.exp(bb)],
                                    axis=1).astype(jnp.bfloat16)
            k_inv = jnp.concatenate([k * jnp.exp(-bf), k * jnp.exp(-bb)], axis=1)
            k_ends.append(jnp.concatenate(
                [k * jnp.exp(bf_last - bf), k * jnp.exp(bb_last - bb)], axis=1).astype(jnp.bfloat16))
            dec_s[n] = jnp.concatenate([jnp.exp(bf_last), jnp.exp(bb_last)], axis=1)
            qcat_s[r, :] = q_cat
            k_bd = jnp.concatenate([jnp.where(cat_block == c, k_inv, 0.0) for c in range(4)],
                                   axis=0).astype(jnp.bfloat16)
            scores.append(lax.dot_general(q_cat, k_bd, _NT,
                                          preferred_element_type=jnp.float32))
        for n, r, k_end in zip(chunks, rows, k_ends):
            kv_t = lax.dot_general(v_ref[r, :], k_end, _TN,
                                   preferred_element_type=jnp.float32)
            kv_s[n] = jnp.where(same_head, kv_t, 0.0)
        for r, sc in zip(rows, scores):
            p = (jnp.where(keep_f, sc[:, 0:pair_dk], 0.0)
                 + jnp.where(keep_b, sc[:, pair_dk:], 0.0)).astype(jnp.bfloat16)
            vv = v_ref[r, :]
            v_bd = jnp.concatenate(
                [jnp.where(v_head == j, vv, jnp.zeros_like(vv)) for j in range(2)], axis=0)
            intra_s[r, :] = _bdot(p, v_bd)
        return carry

    lax.fori_loop(0, n_chunks // GLA_GROUP, intra_body, 0)

    def state_body(i, states):
        sf, sb = states
        nf = i
        nb = n_chunks - 1 - i
        st_s[nf, :, 0:pair_dk] = sf.astype(jnp.bfloat16)
        st_s[nb, :, pair_dk:] = sb.astype(jnp.bfloat16)
        sf = sf * dec_s[nf][:, 0:pair_dk] + kv_s[nf, :, 0:pair_dk]
        sb = sb * dec_s[nb][:, pair_dk:] + kv_s[nb, :, pair_dk:]
        return sf, sb

    zero = jnp.zeros((pair_dv, pair_dk), jnp.float32)
    lax.fori_loop(0, n_chunks, state_body, (zero, zero))

    g = g_ref[...]

    def out_body(i, carry):
        chunks = [i * GLA_GROUP + c for c in range(GLA_GROUP)]
        rows = [pl.ds(pl.multiple_of(n * cdim, cdim), cdim) for n in chunks]
        inter = [lax.dot_general(qcat_s[r, :], st_s[n], _NT, preferred_element_type=jnp.float32)
                 for n, r in zip(chunks, rows)]
        for r, o_inter in zip(rows, inter):
            o = intra_s[r, :] + o_inter
            for j in range(2):
                o_ref[r, j * GLA_DV:(j + 1) * GLA_DV] = _rms(
                    o[:, j * GLA_DV:(j + 1) * GLA_DV], g).astype(o_ref.dtype)
        return carry

    lax.fori_loop(0, n_chunks // GLA_GROUP, out_body, 0)


def _gla_call(layer, gq, gk, gv, misc, wg_cat, bg_cat, gla_norm_g):
    bsz, s, _ = gq.shape
    pairs = GLA_HEADS // 2
    n_chunks = s // GLA_CHUNK
    pair_dk, pair_dv = 2 * GLA_DK, 2 * GLA_DV
    qk_spec = pl.BlockSpec((None, s, pair_dk), lambda b, p: (b, 0, p))
    v_spec = pl.BlockSpec((None, s, pair_dv), lambda b, p: (b, 0, p))

    def gate_specs(direction):
        return (pl.BlockSpec((None, MISC_W, pair_dk), lambda b, p: (layer, 0, direction * pairs + p)),
                pl.BlockSpec((None, 1, pair_dk), lambda b, p: (layer, 0, direction * pairs + p)))

    return pl.pallas_call(
        _gla_kernel,
        grid=(bsz, pairs),
        in_specs=[qk_spec, qk_spec, v_spec,
                  pl.BlockSpec((None, s, MISC_W), lambda b, p: (b, 0, 0)),
                  *gate_specs(0), *gate_specs(1),
                  _layer_spec(layer, (1, GLA_DV))],
        out_specs=v_spec,
        out_shape=jax.ShapeDtypeStruct((bsz, s, GLA_WIDTH), jnp.bfloat16),
        scratch_shapes=[
            pltpu.VMEM((s, pair_dk), jnp.float32),
            pltpu.VMEM((s, pair_dk), jnp.float32),
            pltpu.VMEM((s, 2 * pair_dk), jnp.bfloat16),
            pltpu.VMEM((n_chunks, pair_dv, 2 * pair_dk), jnp.float32),
            pltpu.VMEM((n_chunks, 1, 2 * pair_dk), jnp.float32),
            pltpu.VMEM((n_chunks, pair_dv, 2 * pair_dk), jnp.bfloat16),
            pltpu.VMEM((s, pair_dv), jnp.float32),
        ],
        compiler_params=pltpu.CompilerParams(
            dimension_semantics=("arbitrary", "arbitrary"), vmem_limit_bytes=VMEM_LIMIT),
        name="gla",
    )(gq, gk, gv, misc, wg_cat, bg_cat, wg_cat, bg_cat, gla_norm_g)


def _mla_kernel(mq_ref, mkv_ref, misc_ref, c_ref, sa_ref, sb_ref, gq_ref, gkv_ref,
                wuq_ref, wuk_ref, wuvt_ref, o_ref, ckv_s, q_s, k_s, vt_s, sc_s, top_s):
    s = mkv_ref.shape[0]
    n_tiles = s // Q_TILE
    assert n_tiles % 2 == 0 and MLA_HEADS % 2 == 0
    blocks = [slice(i * KV_ROWS, (i + 1) * KV_ROWS) for i in range(s // KV_ROWS)]
    q_mult = (MLA_NOPE + MLA_ROPE) ** -0.5 * math.log2(math.e)

    for r in blocks:
        tabs = (c_ref[r, :], sa_ref[r, :], sb_ref[r, :])
        ckv_s[r, :] = _rms(mkv_ref[r, :].astype(jnp.float32), gkv_ref[...]).astype(jnp.bfloat16)
        k_rope = _rope(misc_ref[r, :], *tabs).astype(jnp.bfloat16)
        for slot in range(2):
            k_s[slot, r, MLA_NOPE:] = k_rope
        cq = _rms(mq_ref[r, :].astype(jnp.float32), gq_ref[...]).astype(jnp.bfloat16)
        qf = _bdot(cq, wuq_ref[...])
        cos_t, sin_t = tabs[0], tabs[1] + tabs[2]
        for h in range(MLA_HEADS):
            qh = qf[:, h * QK_W:(h + 1) * QK_W]
            rot = qh[:, MLA_NOPE:]
            q_rope = rot * cos_t + pltpu.roll(rot, MLA_ROPE, axis=1) * sin_t
            q_s[h, r, :] = (jnp.concatenate([qh[:, 0:MLA_NOPE], q_rope], axis=1)
                            * q_mult).astype(jnp.bfloat16)
    for slot in range(2):
        vt_s[slot, MLA_DV:, :] = jnp.ones((vt_s.shape[1] - MLA_DV, s), jnp.bfloat16)

    def build_kv(h, slot):
        for r in blocks:
            ckv = ckv_s[r, :]
            k_s[slot, r, 0:MLA_NOPE] = _bdot(ckv, wuk_ref[h]).astype(jnp.bfloat16)
            vt_s[slot, 0:MLA_DV, r] = lax.dot_general(
                wuvt_ref[h], ckv, _NT, preferred_element_type=jnp.float32).astype(jnp.bfloat16)

    def scores_t(h, slot, j):
        q = q_s[h, j * Q_TILE:(j + 1) * Q_TILE, :]
        sc_t = lax.dot_general(k_s[slot], q, _NT, preferred_element_type=jnp.float32)
        sc_s[j % 2] = sc_t
        top_s[j % 2] = jnp.broadcast_to(jnp.max(sc_t, axis=0, keepdims=True), (F32_ROWS, Q_TILE))

    def finish(h, slot, j):
        p_t = jnp.exp2(sc_s[j % 2] - top_s[j % 2][0:1, :]).astype(jnp.bfloat16)
        o_t = _bdot(vt_s[slot], p_t)
        o_ref[h, j * Q_TILE:(j + 1) * Q_TILE, :] = (
            o_t[0:MLA_DV, :] / o_t[MLA_DV:MLA_DV + 1, :]).T.astype(o_ref.dtype)

    build_kv(0, 0)
    scores_t(0, 0, 0)

    def head_pair(i, carry):
        for slot in range(2):
            h = 2 * i + slot
            h_next = jnp.minimum(h + 1, MLA_HEADS - 1)
            for j in range(n_tiles):
                if j + 1 < n_tiles:
                    scores_t(h, slot, j + 1)
                else:
                    build_kv(h_next, 1 - slot)
                    scores_t(h_next, 1 - slot, 0)
                finish(h, slot, j)
        return carry

    lax.fori_loop(0, MLA_HEADS // 2, head_pair, 0)


def _mla_call(layer, mq, mkv, misc, tabs, gq, gkv, wuq_p, wuk_p, wuvt_p):
    bsz, s, _ = mq.shape

    def per_row(width):
        return pl.BlockSpec((None, s, width), lambda b: (b, 0, 0))

    return pl.pallas_call(
        _mla_kernel,
        grid=(bsz,),
        in_specs=[
            per_row(MLA_Q_LORA), per_row(MLA_KV_LORA), per_row(MISC_W),
            per_row(LANES), per_row(LANES), per_row(LANES),
            _layer_spec(layer, (1, MLA_Q_LORA)),
            _layer_spec(layer, (1, MLA_KV_LORA)),
            _layer_spec(layer, (MLA_Q_LORA, MLA_HEADS * QK_W)),
            _layer_spec(layer, (MLA_HEADS, MLA_KV_LORA, MLA_NOPE)),
            _layer_spec(layer, (MLA_HEADS, MLA_DV, MLA_KV_LORA)),
        ],
        out_specs=pl.BlockSpec((None, MLA_HEADS, s, MLA_DV), lambda b: (b, 0, 0, 0)),
        out_shape=jax.ShapeDtypeStruct((bsz, MLA_HEADS, s, MLA_DV), jnp.bfloat16),
        scratch_shapes=[
            pltpu.VMEM((s, MLA_KV_LORA), jnp.bfloat16),
            pltpu.VMEM((MLA_HEADS, s, QK_W), jnp.bfloat16),
            pltpu.VMEM((2, s, QK_W), jnp.bfloat16),
            pltpu.VMEM((2, MLA_DV + BF16_ROWS, s), jnp.bfloat16),
            pltpu.VMEM((2, s, Q_TILE), jnp.float32),
            pltpu.VMEM((2, F32_ROWS, Q_TILE), jnp.float32),
        ],
        compiler_params=pltpu.CompilerParams(
            dimension_semantics=("arbitrary",),
            vmem_limit_bytes=VMEM_LIMIT),
        name="mla",
    )(mq, mkv, misc, *tabs, gq, gkv, wuq_p, wuk_p, wuvt_p)


def _outproj_kernel(tiles_per_seq, final, x_ref, mod_ref, ogla_ref, omla_ref, cb_ref, u_ref,
                    up_ref, un_ref, sz_ref, mg_ref, cw_ref, cg_ref, fg_ref, w_ref, o_ref):
    f32 = jnp.float32
    j = pl.program_id(0) % tiles_per_seq
    tm = x_ref.shape[0]

    u = u_ref[...].astype(f32)
    prev_ok = (j > 0).astype(f32)
    next_ok = (j < tiles_per_seq - 1).astype(f32)
    u_prev = up_ref[...].astype(f32)[HALO_ROWS - 1:, :] * prev_ok
    u_next = un_ref[...].astype(f32)[0:1, :] * next_ok
    rows = lax.broadcasted_iota(jnp.int32, u.shape, 0)
    up = jnp.where(rows == 0, u_prev, pltpu.roll(u, 1, axis=0))
    un = jnp.where(rows == tm - 1, u_next, pltpu.roll(u, tm - 1, axis=0))
    cw = cw_ref[...]
    conv = up * cw[0:1] + u * cw[1:2] + un * cw[2:3]
    gate = mod_ref[...][2:3]

    def gated_dot(r, lo, y_part):
        hi = lo + y_part.shape[1]
        y = (y_part * sz_ref[r, lo:hi].astype(f32)).astype(jnp.bfloat16)
        return _bdot(y, w_ref[lo:hi, :])

    def finish(r, acc):
        out = x_ref[r, :] + gate * acc
        if final:
            out = _rms(out, fg_ref[...])
        o_ref[r, :] = out

    pending = None
    for k in range(tm // OUT_SUB_TILE):
        r = slice(k * OUT_SUB_TILE, (k + 1) * OUT_SUB_TILE)
        acc = gated_dot(r, 0, ogla_ref[r, :].astype(f32))
        if pending is not None:
            finish(*pending)
        omla = jnp.concatenate([omla_ref[h, r, :] for h in range(MLA_HEADS)], axis=1)
        acc += gated_dot(r, GLA_WIDTH, _rms(omla.astype(f32), mg_ref[...]))
        acc += gated_dot(r, GLA_WIDTH + MLA_WIDTH,
                         _rms(cb_ref[r, :].astype(f32) * conv[r, :], cg_ref[...]))
        pending = (r, acc)
    finish(*pending)


def _outproj_call(layer, x2, mod, o_gla, o_mla, cb, u, sz, mla_out_g, conv_w, conv_out_g,
                  final_g, w_out_b, seq_len, final):
    t = x2.shape[0]
    tm = OUT_ROW_TILE
    tiles_per_seq = seq_len // tm
    halo_per_tile = tm // HALO_ROWS
    n_halo = t // HALO_ROWS

    def row_spec(w):
        return pl.BlockSpec((tm, w), lambda i: (i, 0))

    prev_spec = pl.BlockSpec((HALO_ROWS, CONV_CH),
                             lambda i: (jnp.maximum(i * halo_per_tile - 1, 0), 0))
    next_spec = pl.BlockSpec((HALO_ROWS, CONV_CH),
                             lambda i: (jnp.minimum((i + 1) * halo_per_tile, n_halo - 1), 0))

    return pl.pallas_call(
        functools.partial(_outproj_kernel, tiles_per_seq, final),
        grid=(t // tm,),
        in_specs=[
            row_spec(D_MODEL),
            pl.BlockSpec((None, None, 3, D_MODEL), lambda i: (layer, i // tiles_per_seq, 0, 0)),
            row_spec(GLA_WIDTH),
            pl.BlockSpec((None, MLA_HEADS, tm, MLA_DV),
                         lambda i: (i // tiles_per_seq, 0, i % tiles_per_seq, 0)),
            row_spec(CONV_CH), row_spec(CONV_CH), prev_spec, next_spec,
            row_spec(D_MODEL),
            _layer_spec(layer, (1, MLA_WIDTH)), _layer_spec(layer, (3, CONV_CH)),
            _layer_spec(layer, (1, CONV_CH)),
            pl.BlockSpec((1, D_MODEL), lambda i: (0, 0)),
            pl.BlockSpec((D_MODEL, D_MODEL), lambda i: (0, 0), pipeline_mode=pl.Buffered(1)),
        ],
        out_specs=row_spec(D_MODEL),
        out_shape=jax.ShapeDtypeStruct((t, D_MODEL), jnp.float32),
        compiler_params=pltpu.CompilerParams(
            dimension_semantics=("arbitrary",), vmem_limit_bytes=VMEM_LIMIT),
        name="out_proj",
    )(x2, mod, o_gla, o_mla, cb, u, u, u, sz,
      mla_out_g, conv_w, conv_out_g, final_g, w_out_b)


_O_GLR = 2 * GLA_HEADS * GLA_DK + GLA_WIDTH
_O_MQ = _O_GLR + 2 * GLA_GATE_RANK
_O_MKV = _O_MQ + MLA_Q_LORA
_O_MKR = _O_MKV + MLA_KV_LORA
_O_CB = _O_MKR + MLA_ROPE
IN_DIM = _O_CB + 3 * CONV_CH + D_MODEL
W_PREP_ROWS = 256


W_PREP_CHUNK = 512


def _w_in_prep_kernel(wt_ref, o_ref):
    rows = o_ref.shape[0]
    eye = (lax.broadcasted_iota(jnp.int32, (rows, rows), 0)
           == lax.broadcasted_iota(jnp.int32, (rows, rows), 1)).astype(jnp.bfloat16)

    def put(dst, block):
        o_ref[:, dst:dst + block.shape[0]] = lax.dot_general(
            eye, block.astype(jnp.bfloat16), _NT,
            preferred_element_type=jnp.float32).astype(jnp.bfloat16)

    def put_range(dst, lo, hi):
        for start in range(lo, hi, W_PREP_CHUNK):
            stop = min(start + W_PREP_CHUNK, hi)
            put(dst + start - lo, wt_ref[start:stop, :])

    pad_rows = MISC_W - MLA_ROPE - 2 * GLA_GATE_RANK
    misc = jnp.concatenate([wt_ref[_O_MKR:_O_CB, :], wt_ref[_O_GLR:_O_MQ, :],
                            jnp.zeros((pad_rows, rows), jnp.float32)], axis=0)
    off = 0
    for src in ((0, _O_GLR), (_O_MQ, _O_MKV), misc, (_O_MKV, _O_MKR), (_O_CB, IN_DIM)):
        if isinstance(src, tuple):
            put_range(off, *src)
            off += src[1] - src[0]
        else:
            put(off, src)
            off += src.shape[0]


def _w_in_prep_call(w_in):
    depth, d, n = w_in.shape
    return pl.pallas_call(
        _w_in_prep_kernel,
        grid=(depth, d // W_PREP_ROWS),
        in_specs=[pl.BlockSpec((None, n, W_PREP_ROWS), lambda l, i: (l, 0, i))],
        out_specs=pl.BlockSpec((None, W_PREP_ROWS, IN_DIM_P), lambda l, i: (l, i, 0)),
        out_shape=jax.ShapeDtypeStruct((depth, d, IN_DIM_P), jnp.bfloat16),
        compiler_params=pltpu.CompilerParams(
            dimension_semantics=("arbitrary", "arbitrary"), vmem_limit_bytes=VMEM_LIMIT),
        name="w_in_prep",
    )(jnp.swapaxes(w_in, 1, 2))


def _gate_params(wg_f, bg_f, wg_b, bg_b):
    depth, rank, hk = wg_f.shape
    zeros = jnp.zeros((depth, rank, hk), jnp.float32)
    rows = jnp.concatenate([jnp.concatenate([wg_f, zeros], axis=2),
                            jnp.concatenate([zeros, wg_b], axis=2)], axis=1)
    wg_cat = jnp.pad(rows, ((0, 0), (MLA_ROPE, MISC_W - MLA_ROPE - 2 * rank), (0, 0)))
    bg_cat = jnp.concatenate([bg_f, bg_b], axis=1)[:, None, :]
    return wg_cat.astype(jnp.bfloat16), bg_cat


def _permute_w_uq(w):
    depth = w.shape[0]
    half = MLA_ROPE // 2
    w4 = w.reshape(depth, MLA_Q_LORA, MLA_HEADS, MLA_NOPE + MLA_ROPE)
    nope, x1, x2 = w4[..., :MLA_NOPE], w4[..., MLA_NOPE:MLA_NOPE + half], w4[..., MLA_NOPE + half:]
    cols = jnp.concatenate([nope, x1, x2, x2, x1], axis=-1)
    return cols.reshape(depth, MLA_Q_LORA, MLA_HEADS * QK_W).astype(jnp.bfloat16)


def _split_w_ukv(w):
    depth = w.shape[0]
    w4 = w.reshape(depth, MLA_KV_LORA, MLA_HEADS, MLA_NOPE + MLA_DV)
    wk = w4[..., :MLA_NOPE].transpose(0, 2, 1, 3)
    wv_t = w4[..., MLA_NOPE:].transpose(0, 2, 3, 1)
    return wk.astype(jnp.bfloat16), wv_t.astype(jnp.bfloat16)


def kernel(x, c, positions, ada_w, ada_b, norm_g, w_in, gla_wg_f, gla_bg_f, gla_wg_b, gla_bg_b,
           gla_norm_g, mla_q_norm_g, mla_kv_norm_g, mla_w_uq, mla_w_ukv, mla_out_g, conv_w,
           conv_out_g, w_out, final_g):
    bsz, s, d = x.shape
    t = bsz * s

    mod = _ada_call(c, ada_w, ada_b)[:, :bsz].reshape(DEPTH, bsz, 3, d)

    w_in_p = _w_in_prep_call(w_in)
    wg_cat, bg_cat = _gate_params(gla_wg_f, gla_bg_f, gla_wg_b, gla_bg_b)
    wuq_p = _permute_w_uq(mla_w_uq)
    wuk_p, wuvt_p = _split_w_ukv(mla_w_ukv)

    def row(p):
        return p[:, None, :]

    def b3(a):
        return a.reshape(bsz, s, a.shape[-1])

    h = x.reshape(t, d)
    for l in range(DEPTH):
        segs, w_out_b, new_tabs = _inproj_call(l, h, mod, row(norm_g), w_in_p, w_out, s,
                                               positions if l == 0 else None)
        if l == 0:
            tabs = tuple(b3(tab) for tab in new_tabs)
        gq, gk, gv, mq, misc, mkv, cb, u, sz = segs
        o_gla = _gla_call(l, b3(gq), b3(gk), b3(gv), b3(misc), wg_cat, bg_cat, row(gla_norm_g))
        o_mla = _mla_call(l, b3(mq), b3(mkv), b3(misc), tabs, row(mla_q_norm_g),
                          row(mla_kv_norm_g), wuq_p, wuk_p, wuvt_p)
        h = _outproj_call(l, h, mod, o_gla.reshape(t, -1), o_mla, cb, u, sz,
                          row(mla_out_g), conv_w, row(conv_out_g), final_g.reshape(1, d),
                          w_out_b, s, l == DEPTH - 1)
    return h.reshape(bsz, s, d)
```

```python
import functools
import math

import jax
import jax.numpy as jnp
from jax import lax
from jax.experimental import pallas as pl
from jax.experimental.pallas import tpu as pltpu

D_MODEL = 2048
DEPTH = 2
GLA_HEADS = 6
GLA_DK = 64
GLA_DV = 128
GLA_GATE_RANK = 16
GLA_GATE_TEMP = 16.0
GLA_CHUNK = 64
GLA_WIDTH = GLA_HEADS * GLA_DV
MLA_HEADS = 6
MLA_Q_LORA = 384
MLA_KV_LORA = 256
MLA_NOPE = 128
MLA_ROPE = 64
MLA_DV = 128
MLA_WIDTH = MLA_HEADS * MLA_DV
ROPE_THETA = 10000.0
CONV_CH = D_MODEL - GLA_WIDTH - MLA_WIDTH
EPS = 1e-6

LANES = 128
MXU_COLS = 256
F32_ROWS = 8
BF16_ROWS = 16
HALO_ROWS = BF16_ROWS
MISC_W = LANES
QK_W = 2 * LANES

IN_SEGMENTS = (
    ("gq", GLA_HEADS * GLA_DK, jnp.bfloat16),
    ("gk", GLA_HEADS * GLA_DK, jnp.bfloat16),
    ("gv", GLA_WIDTH, jnp.bfloat16),
    ("mq", MLA_Q_LORA, jnp.bfloat16),
    ("misc", MISC_W, jnp.float32),
    ("mkv", MLA_KV_LORA, jnp.bfloat16),
    ("cb", CONV_CH, jnp.bfloat16),
    ("cc", CONV_CH, jnp.bfloat16),
    ("cx", CONV_CH, jnp.bfloat16),
    ("z", D_MODEL, jnp.bfloat16),
)
IN_DIM_P = sum(w for _, w, _ in IN_SEGMENTS)
_SEGMENT_COLS = {}
for _name, _width, _ in IN_SEGMENTS:
    _SEGMENT_COLS[_name] = (sum(w for _, w in _SEGMENT_COLS.values()), _width)
IN_OUTPUTS = tuple(seg for seg in IN_SEGMENTS if seg[0] not in ("cc", "cx", "z")) + (
    ("u", CONV_CH, jnp.bfloat16),
    ("sz", D_MODEL, jnp.bfloat16),
)

IN_ROW_TILE = 512
IN_SUB_TILE = 256
OUT_ROW_TILE = 512
OUT_SUB_TILE = 256
Q_TILE = 512
KV_ROWS = 512
SCAN_ROWS = 256
SCAN_GROUP = 4
GLA_GROUP = 8
VMEM_LIMIT = 56 * 1024 * 1024

_NT = (((1,), (1,)), ((), ()))
_TN = (((0,), (0,)), ((), ()))


def _rms(x, g):
    ms = jnp.mean(x * x, axis=-1, keepdims=True)
    return x * lax.rsqrt(ms + EPS) * g


def _bdot(a, b):
    return jnp.dot(a, b, preferred_element_type=jnp.float32)


ADA_COLS = 1024
ADA_KROWS = 256


def _ada_kernel(ct_ref, w_ref, b_ref, o_ref, cb_s):
    d, bsz = ct_ref.shape
    tn = w_ref.shape[1]

    @pl.when((pl.program_id(0) == 0) & (pl.program_id(1) == 0))
    def _broadcast_c():
        ct = ct_ref[...]
        act = ct * jax.nn.sigmoid(ct)
        for b in range(bsz):
            cb_s[b] = jnp.broadcast_to(act[:, b:b + 1], (d, LANES))

    def body(i, accs):
        accs = list(accs)
        for g in range(ADA_KROWS // F32_ROWS):
            r = pl.ds(pl.multiple_of(i * ADA_KROWS + g * F32_ROWS, F32_ROWS), F32_ROWS)
            w = w_ref[r, :]
            for b in range(bsz):
                accs[b] = accs[b] + w * jnp.concatenate([cb_s[b, r, :]] * (tn // LANES), axis=1)
        return tuple(accs)

    zero = jnp.zeros((F32_ROWS, tn), jnp.float32)
    accs = lax.fori_loop(0, d // ADA_KROWS, body, (zero,) * bsz)
    rows = [jnp.sum(a, axis=0, keepdims=True) for a in accs]
    rows.append(jnp.zeros((o_ref.shape[0] - bsz, tn), jnp.float32))
    o_ref[...] = jnp.concatenate(rows, axis=0) + b_ref[...]


def _ada_call(c, ada_w, ada_b):
    bsz = c.shape[0]
    n = ada_w.shape[-1]
    tn = ADA_COLS
    return pl.pallas_call(
        _ada_kernel,
        grid=(DEPTH, n // tn),
        in_specs=[
            pl.BlockSpec((D_MODEL, bsz), lambda l, j: (0, 0)),
            pl.BlockSpec((None, D_MODEL, tn), lambda l, j: (l, 0, j)),
            pl.BlockSpec((None, 1, tn), lambda l, j: (l, 0, j)),
        ],
        out_specs=pl.BlockSpec((None, F32_ROWS, tn), lambda l, j: (l, 0, j)),
        out_shape=jax.ShapeDtypeStruct((DEPTH, F32_ROWS, n), jnp.float32),
        scratch_shapes=[pltpu.VMEM((bsz, D_MODEL, LANES), jnp.float32)],
        compiler_params=pltpu.CompilerParams(
            dimension_semantics=("arbitrary", "arbitrary"), vmem_limit_bytes=VMEM_LIMIT),
        name="ada_mod",
    )(c.T, ada_w, ada_b.reshape(DEPTH, 1, n))


def _rope_tables(pos_ref, invf_ref, c_ref, sa_ref, sb_ref):
    rows = pos_ref.shape[0]
    lane = lax.broadcasted_iota(jnp.int32, (rows, LANES), 1)
    pos = jnp.where(lane < MLA_ROPE, pos_ref[:, 0:1], pos_ref[:, 1:2]).astype(jnp.float32)
    ang = pos * invf_ref[...]
    half = MLA_ROPE // 2
    in_first, in_second = lane < half, (lane >= half) & (lane < MLA_ROPE)
    cos2, sin2 = jnp.cos(ang), jnp.sin(ang)
    for r, shift in ((slice(0, rows), 0), (slice(rows, 2 * rows), MLA_ROPE)):
        cos = cos2 if shift == 0 else pltpu.roll(cos2, shift, axis=1)
        sin = sin2 if shift == 0 else pltpu.roll(sin2, shift, axis=1)
        c_ref[r, :] = jnp.where(lane < MLA_ROPE, cos, 0.0)
        sa_ref[r, :] = jnp.where(in_first, -sin, 0.0)
        sb_ref[r, :] = jnp.where(in_second, sin, 0.0)


def _rope_table_inputs(positions, tile_rows):
    inv_freq = ROPE_THETA ** (-jnp.arange(0, MLA_ROPE, 2, dtype=jnp.float32) / MLA_ROPE)
    invf = jnp.tile(inv_freq, LANES // inv_freq.shape[0]).reshape(1, LANES)
    pairs = positions.reshape(-1, 2, tile_rows // 2).transpose(0, 2, 1)
    return pairs, invf


def _rope(x, c, sa, sb):
    return (x * c + pltpu.roll(x, LANES - MLA_ROPE // 2, axis=1) * sa
            + pltpu.roll(x, MLA_ROPE // 2, axis=1) * sb)


def _inproj_kernel(with_rope_tables, x_ref, mod_ref, g_ref, w_ref, wout_ref, *refs):
    n_out = len(IN_OUTPUTS)
    if with_rope_tables:
        pos_ref, invf_ref = refs[0:2]
        out_refs, wout_b_ref = refs[2:2 + n_out], refs[2 + n_out]
        _rope_tables(pos_ref, invf_ref, *refs[3 + n_out:6 + n_out])
    else:
        out_refs, wout_b_ref = refs[0:n_out], refs[n_out]
    h_s = refs[-1]
    wout_b_ref[...] = wout_ref[...].astype(jnp.bfloat16)
    mod = mod_ref[...]
    shift, gain = mod[0:1], g_ref[...] * (1.0 + mod[1:2])
    n_sub = x_ref.shape[0] // IN_SUB_TILE

    def normalize(k):
        x = x_ref[k * IN_SUB_TILE:(k + 1) * IN_SUB_TILE, :]
        ms = jnp.mean(x * x, axis=-1, keepdims=True)
        h_s[k % 2] = (x * lax.rsqrt(ms + EPS) * gain + shift).astype(jnp.bfloat16)

    outs = {name: o_ref for (name, _, _), o_ref in zip(IN_OUTPUTS, out_refs)}

    def project(k):
        r = slice(k * IN_SUB_TILE, (k + 1) * IN_SUB_TILE)

        def proj(names):
            lo = _SEGMENT_COLS[names[0]][0]
            width = sum(_SEGMENT_COLS[n][1] for n in names)
            assert width % MXU_COLS == 0
            val = _bdot(h_s[k % 2], w_ref[:, lo:lo + width])
            return {n: val[:, _SEGMENT_COLS[n][0] - lo:_SEGMENT_COLS[n][0] - lo + _SEGMENT_COLS[n][1]]
                    for n in names}

        z = proj(("z",))["z"]
        outs["sz"][r, :] = (z * jax.nn.sigmoid(z)).astype(outs["sz"].dtype)
        conv = proj(("cb", "cc", "cx"))
        outs["u"][r, :] = (conv["cc"] * conv["cx"]).astype(outs["u"].dtype)
        outs["cb"][r, :] = conv["cb"].astype(outs["cb"].dtype)
        for names in (("gq", "gk", "gv"), ("mq", "misc", "mkv")):
            for name, val in proj(names).items():
                outs[name][r, :] = val.astype(outs[name].dtype)

    normalize(0)
    for k in range(n_sub):
        project(k)
        if k + 1 < n_sub:
            normalize(k + 1)


def _layer_spec(layer, shape):
    return pl.BlockSpec((None,) + shape, lambda *_: (layer,) + (0,) * len(shape))


def _inproj_call(layer, x2, mod, norm_g, w_in_p, w_out, seq_len, positions=None):
    t = x2.shape[0]
    tm = IN_ROW_TILE
    n_steps = t // tm
    tiles_per_seq = seq_len // tm
    d_out = w_out.shape[-1]
    wout_rows = w_out.shape[1] // n_steps
    n_out = len(IN_OUTPUTS)
    in_specs = [
        pl.BlockSpec((tm, D_MODEL), lambda i: (i, 0)),
        pl.BlockSpec((None, None, 3, D_MODEL), lambda i: (layer, i // tiles_per_seq, 0, 0)),
        _layer_spec(layer, (1, D_MODEL)),
        pl.BlockSpec((D_MODEL, IN_DIM_P), lambda i: (0, 0), pipeline_mode=pl.Buffered(1)),
        pl.BlockSpec((None, wout_rows, d_out), lambda i: (layer, i, 0)),
    ]
    operands = [x2, mod, norm_g, w_in_p, w_out]
    out_shape = [jax.ShapeDtypeStruct((t, w), dt) for _, w, dt in IN_OUTPUTS]
    out_specs = [pl.BlockSpec((tm, w), lambda i: (i, 0)) for _, w, _ in IN_OUTPUTS]
    out_shape.append(jax.ShapeDtypeStruct(w_out.shape[1:], jnp.bfloat16))
    out_specs.append(pl.BlockSpec((wout_rows, d_out), lambda i: (i, 0)))
    if positions is not None:
        pairs, invf = _rope_table_inputs(positions, tm)
        in_specs += [pl.BlockSpec((None, tm // 2, 2), lambda i: (i, 0, 0)),
                     pl.BlockSpec((1, LANES), lambda i: (0, 0))]
        operands += [pairs, invf]
        out_shape += [jax.ShapeDtypeStruct((t, LANES), jnp.float32)] * 3
        out_specs += [pl.BlockSpec((tm, LANES), lambda i: (i, 0))] * 3
    outs = pl.pallas_call(
        functools.partial(_inproj_kernel, positions is not None),
        grid=(n_steps,),
        in_specs=in_specs,
        out_specs=out_specs,
        out_shape=out_shape,
        scratch_shapes=[pltpu.VMEM((2, IN_SUB_TILE, D_MODEL), jnp.bfloat16)],
        compiler_params=pltpu.CompilerParams(
            dimension_semantics=("arbitrary",), vmem_limit_bytes=VMEM_LIMIT),
        name="in_proj",
    )(*operands)
    return outs[:n_out], outs[n_out], tuple(outs[n_out + 1:])


def _log_sigmoid(x):
    return jnp.minimum(x, 0.0) - jnp.log(1.0 + jnp.exp(-jnp.abs(x)))


def _gla_kernel(n_w_slabs, q_ref, k_ref, v_ref, misc_ref, wgf_ref, bgf_ref, wgb_ref, bgb_ref,
                g_ref, *refs):
    if n_w_slabs:
        wt_ref, o_ref, w_prep_ref = refs[0:3]
        step = pl.program_id(0) * pl.num_programs(1) + pl.program_id(1)

        @pl.when(step < n_w_slabs)
        def _prep_next_layer_weight():
            _w_in_prep_kernel(wt_ref, w_prep_ref)

        refs = refs[3:]
    else:
        o_ref, refs = refs[0], refs[1:]
    bf_s, bb_s, qcat_s, kv_s, dec_s, st_s, intra_s = refs
    s = q_ref.shape[0]
    cdim = GLA_CHUNK
    n_chunks = s // cdim
    pair_dk = 2 * GLA_DK
    pair_dv = 2 * GLA_DV
    inv_t = 1.0 / GLA_GATE_TEMP
    q_scale = GLA_DK ** -0.5

    t_row = lax.broadcasted_iota(jnp.int32, (SCAN_ROWS, SCAN_ROWS), 0)
    t_col = lax.broadcasted_iota(jnp.int32, (SCAN_ROWS, SCAN_ROWS), 1)
    same_chunk = (t_row // cdim) == (t_col // cdim)
    tri_prefix = (same_chunk & (t_col <= t_row)).astype(jnp.bfloat16)
    tri_suffix = (same_chunk & (t_col >= t_row)).astype(jnp.bfloat16)

    def split(terms):
        high = terms.astype(jnp.bfloat16)
        rest = (terms - high.astype(jnp.float32)).astype(jnp.bfloat16)
        return jnp.concatenate([high, rest], axis=1)

    def scan_body(i, carry):
        blocks = [pl.ds(pl.multiple_of((i * SCAN_GROUP + c) * SCAN_ROWS, SCAN_ROWS), SCAN_ROWS)
                  for c in range(SCAN_GROUP)]
        pre = []
        for r in blocks:
            m = misc_ref[r, :].astype(jnp.bfloat16)
            pre.append((_bdot(m, wgf_ref[...]), _bdot(m, wgb_ref[...])))
        terms = [(split(_log_sigmoid(pf + bgf_ref[...]) * inv_t),
                  split(_log_sigmoid(pb + bgb_ref[...]) * inv_t)) for pf, pb in pre]
        sums = [(_bdot(tri_prefix, tf), _bdot(tri_suffix, tb)) for tf, tb in terms]
        for r, (sf, sb) in zip(blocks, sums):
            bf_s[r, :] = sf[:, 0:pair_dk] + sf[:, pair_dk:]
            bb_s[r, :] = sb[:, 0:pair_dk] + sb[:, pair_dk:]
        return carry

    lax.fori_loop(0, s // (SCAN_ROWS * SCAN_GROUP), scan_body, 0)

    row = lax.broadcasted_iota(jnp.int32, (cdim, pair_dk), 0)
    key_pos = lax.broadcasted_iota(jnp.int32, (cdim, pair_dk), 1) & (cdim - 1)
    keep_f = key_pos <= row
    keep_b = key_pos > row
    cat_block = lax.broadcasted_iota(jnp.int32, (cdim, 2 * pair_dk), 1) // GLA_DK
    v_head = lax.broadcasted_iota(jnp.int32, (cdim, pair_dv), 1) // GLA_DV
    same_head = (lax.broadcasted_iota(jnp.int32, (pair_dv, 2 * pair_dk), 0) // GLA_DV
                 == (lax.broadcasted_iota(jnp.int32, (pair_dv, 2 * pair_dk), 1) // GLA_DK) % 2)

    def intra_body(i, carry):
        chunks = [i * GLA_GROUP + c for c in range(GLA_GROUP)]
        rows = [pl.ds(pl.multiple_of(n * cdim, cdim), cdim) for n in chunks]
        scores, k_ends = [], []
        for n, r in zip(chunks, rows):
            bf = bf_s[r, :]
            bb = bb_s[r, :]
            bf_last = bf[cdim - 1:cdim, :]
            bb_last = bb[0:1, :]
            q = q_ref[r, :].astype(jnp.float32) * q_scale
            k = k_ref[r, :].astype(jnp.float32)
            q_cat = jnp.concatenate([q * jnp.exp(bf), q * jnp.exp(bb)],
                                    axis=1).astype(jnp.bfloat16)
            k_inv = jnp.concatenate([k * jnp.exp(-bf), k * jnp.exp(-bb)], axis=1)
            k_ends.append(jnp.concatenate(
                [k * jnp.exp(bf_last - bf), k * jnp.exp(bb_last - bb)], axis=1).astype(jnp.bfloat16))
            dec_s[n] = jnp.concatenate([jnp.exp(bf_last), jnp.exp(bb_last)], axis=1)
            qcat_s[r, :] = q_cat
            k_bd = jnp.concatenate([jnp.where(cat_block == c, k_inv, 0.0) for c in range(4)],
                                   axis=0).astype(jnp.bfloat16)
            scores.append(lax.dot_general(q_cat, k_bd, _NT,
                                          preferred_element_type=jnp.float32))
        for n, r, k_end in zip(chunks, rows, k_ends):
            kv_t = lax.dot_general(v_ref[r, :], k_end, _TN,
                                   preferred_element_type=jnp.float32)
            kv_s[n] = jnp.where(same_head, kv_t, 0.0)
        for r, sc in zip(rows, scores):
            p = (jnp.where(keep_f, sc[:, 0:pair_dk], 0.0)
                 + jnp.where(keep_b, sc[:, pair_dk:], 0.0)).astype(jnp.bfloat16)
            vv = v_ref[r, :]
            v_bd = jnp.concatenate(
                [jnp.where(v_head == j, vv, jnp.zeros_like(vv)) for j in range(2)], axis=0)
            intra_s[r, :] = _bdot(p, v_bd)
        return carry

    lax.fori_loop(0, n_chunks // GLA_GROUP, intra_body, 0)

    def state_body(i, states):
        sf, sb = states
        nf = i
        nb = n_chunks - 1 - i
        st_s[nf, :, 0:pair_dk] = sf.astype(jnp.bfloat16)
        st_s[nb, :, pair_dk:] = sb.astype(jnp.bfloat16)
        sf = sf * dec_s[nf][:, 0:pair_dk] + kv_s[nf, :, 0:pair_dk]
        sb = sb * dec_s[nb][:, pair_dk:] + kv_s[nb, :, pair_dk:]
        return sf, sb

    zero = jnp.zeros((pair_dv, pair_dk), jnp.float32)
    lax.fori_loop(0, n_chunks, state_body, (zero, zero))

    g = g_ref[...]

    def out_body(i, carry):
        chunks = [i * GLA_GROUP + c for c in range(GLA_GROUP)]
        rows = [pl.ds(pl.multiple_of(n * cdim, cdim), cdim) for n in chunks]
        inter = [lax.dot_general(qcat_s[r, :], st_s[n], _NT, preferred_element_type=jnp.float32)
                 for n, r in zip(chunks, rows)]
        for r, o_inter in zip(rows, inter):
            o = intra_s[r, :] + o_inter
            for j in range(2):
                o_ref[r, j * GLA_DV:(j + 1) * GLA_DV] = _rms(
                    o[:, j * GLA_DV:(j + 1) * GLA_DV], g).astype(o_ref.dtype)
        return carry

    lax.fori_loop(0, n_chunks // GLA_GROUP, out_body, 0)


def _gla_call(layer, gq, gk, gv, misc, wg_cat, bg_cat, gla_norm_g, w_in_t=None):
    bsz, s, _ = gq.shape
    pairs = GLA_HEADS // 2
    n_chunks = s // GLA_CHUNK
    pair_dk, pair_dv = 2 * GLA_DK, 2 * GLA_DV
    qk_spec = pl.BlockSpec((None, s, pair_dk), lambda b, p: (b, 0, p))
    v_spec = pl.BlockSpec((None, s, pair_dv), lambda b, p: (b, 0, p))

    def gate_specs(direction):
        return (pl.BlockSpec((None, MISC_W, pair_dk), lambda b, p: (layer, 0, direction * pairs + p)),
                pl.BlockSpec((None, 1, pair_dk), lambda b, p: (layer, 0, direction * pairs + p)))

    in_specs = [qk_spec, qk_spec, v_spec,
                pl.BlockSpec((None, s, MISC_W), lambda b, p: (b, 0, 0)),
                *gate_specs(0), *gate_specs(1),
                _layer_spec(layer, (1, GLA_DV))]
    operands = [gq, gk, gv, misc, wg_cat, bg_cat, wg_cat, bg_cat, gla_norm_g]
    out_specs = [v_spec]
    out_shape = [jax.ShapeDtypeStruct((bsz, s, GLA_WIDTH), jnp.bfloat16)]
    n_w_slabs = 0
    if w_in_t is not None:
        n_w_slabs = D_MODEL // W_PREP_ROWS
        assert n_w_slabs <= bsz * pairs

        def slab(b, p):
            return jnp.minimum(b * pairs + p, n_w_slabs - 1)

        in_specs.append(pl.BlockSpec((None, IN_DIM, W_PREP_ROWS),
                                     lambda b, p: (layer + 1, 0, slab(b, p))))
        operands.append(w_in_t)
        out_specs.append(pl.BlockSpec((W_PREP_ROWS, IN_DIM_P), lambda b, p: (slab(b, p), 0)))
        out_shape.append(jax.ShapeDtypeStruct((D_MODEL, IN_DIM_P), jnp.bfloat16))

    return pl.pallas_call(
        functools.partial(_gla_kernel, n_w_slabs),
        grid=(bsz, pairs),
        in_specs=in_specs,
        out_specs=out_specs,
        out_shape=out_shape,
        scratch_shapes=[
            pltpu.VMEM((s, pair_dk), jnp.float32),
            pltpu.VMEM((s, pair_dk), jnp.float32),
            pltpu.VMEM((s, 2 * pair_dk), jnp.bfloat16),
            pltpu.VMEM((n_chunks, pair_dv, 2 * pair_dk), jnp.float32),
            pltpu.VMEM((n_chunks, 1, 2 * pair_dk), jnp.float32),
            pltpu.VMEM((n_chunks, pair_dv, 2 * pair_dk), jnp.bfloat16),
            pltpu.VMEM((s, pair_dv), jnp.float32),
        ],
        compiler_params=pltpu.CompilerParams(
            dimension_semantics=("arbitrary", "arbitrary"), vmem_limit_bytes=VMEM_LIMIT),
        name="gla",
    )(*operands)


def _mla_kernel(mq_ref, mkv_ref, misc_ref, c_ref, sa_ref, sb_ref, gq_ref, gkv_ref,
                wuq_ref, wuk_ref, wuvt_ref, o_ref, ckv_s, q_s, k_s, vt_s, sc_s, top_s):
    s = mkv_ref.shape[0]
    n_tiles = s // Q_TILE
    assert n_tiles % 2 == 0 and MLA_HEADS % 2 == 0
    blocks = [slice(i * KV_ROWS, (i + 1) * KV_ROWS) for i in range(s // KV_ROWS)]
    q_mult = (MLA_NOPE + MLA_ROPE) ** -0.5 * math.log2(math.e)

    for r in blocks:
        tabs = (c_ref[r, :], sa_ref[r, :], sb_ref[r, :])
        ckv_s[r, :] = _rms(mkv_ref[r, :].astype(jnp.float32), gkv_ref[...]).astype(jnp.bfloat16)
        k_rope = _rope(misc_ref[r, :], *tabs).astype(jnp.bfloat16)
        for slot in range(2):
            k_s[slot, r, MLA_NOPE:] = k_rope
        cq = _rms(mq_ref[r, :].astype(jnp.float32), gq_ref[...]).astype(jnp.bfloat16)
        qf = _bdot(cq, wuq_ref[...])
        cos_t, sin_t = tabs[0], tabs[1] + tabs[2]
        for h in range(MLA_HEADS):
            qh = qf[:, h * QK_W:(h + 1) * QK_W]
            rot = qh[:, MLA_NOPE:]
            q_rope = rot * cos_t + pltpu.roll(rot, MLA_ROPE, axis=1) * sin_t
            q_s[h, r, :] = (jnp.concatenate([qh[:, 0:MLA_NOPE], q_rope], axis=1)
                            * q_mult).astype(jnp.bfloat16)
    for slot in range(2):
        vt_s[slot, MLA_DV:, :] = jnp.ones((vt_s.shape[1] - MLA_DV, s), jnp.bfloat16)

    def build_kv(h, slot):
        for r in blocks:
            ckv = ckv_s[r, :]
            k_s[slot, r, 0:MLA_NOPE] = _bdot(ckv, wuk_ref[h]).astype(jnp.bfloat16)
            vt_s[slot, 0:MLA_DV, r] = lax.dot_general(
                wuvt_ref[h], ckv, _NT, preferred_element_type=jnp.float32).astype(jnp.bfloat16)

    def scores_t(h, slot, j):
        q = q_s[h, j * Q_TILE:(j + 1) * Q_TILE, :]
        sc_t = lax.dot_general(k_s[slot], q, _NT, preferred_element_type=jnp.float32)
        sc_s[j % 2] = sc_t
        top_s[j % 2] = jnp.broadcast_to(jnp.max(sc_t, axis=0, keepdims=True), (F32_ROWS, Q_TILE))

    def finish(h, slot, j):
        p_t = jnp.exp2(sc_s[j % 2] - top_s[j % 2][0:1, :]).astype(jnp.bfloat16)
        o_t = _bdot(vt_s[slot], p_t)
        o_ref[h, j * Q_TILE:(j + 1) * Q_TILE, :] = (
            o_t[0:MLA_DV, :] / o_t[MLA_DV:MLA_DV + 1, :]).T.astype(o_ref.dtype)

    build_kv(0, 0)
    scores_t(0, 0, 0)

    def head_pair(i, carry):
        for slot in range(2):
            h = 2 * i + slot
            h_next = jnp.minimum(h + 1, MLA_HEADS - 1)
            for j in range(n_tiles):
                if j + 1 < n_tiles:
                    scores_t(h, slot, j + 1)
                else:
                    build_kv(h_next, 1 - slot)
                    scores_t(h_next, 1 - slot, 0)
                finish(h, slot, j)
        return carry

    lax.fori_loop(0, MLA_HEADS // 2, head_pair, 0)


def _mla_call(layer, mq, mkv, misc, tabs, gq, gkv, wuq_p, wuk_p, wuvt_p):
    bsz, s, _ = mq.shape

    def per_row(width):
        return pl.BlockSpec((None, s, width), lambda b: (b, 0, 0))

    return pl.pallas_call(
        _mla_kernel,
        grid=(bsz,),
        in_specs=[
            per_row(MLA_Q_LORA), per_row(MLA_KV_LORA), per_row(MISC_W),
            per_row(LANES), per_row(LANES), per_row(LANES),
            _layer_spec(layer, (1, MLA_Q_LORA)),
            _layer_spec(layer, (1, MLA_KV_LORA)),
            _layer_spec(layer, (MLA_Q_LORA, MLA_HEADS * QK_W)),
            _layer_spec(layer, (MLA_HEADS, MLA_KV_LORA, MLA_NOPE)),
            _layer_spec(layer, (MLA_HEADS, MLA_DV, MLA_KV_LORA)),
        ],
        out_specs=pl.BlockSpec((None, MLA_HEADS, s, MLA_DV), lambda b: (b, 0, 0, 0)),
        out_shape=jax.ShapeDtypeStruct((bsz, MLA_HEADS, s, MLA_DV), jnp.bfloat16),
        scratch_shapes=[
            pltpu.VMEM((s, MLA_KV_LORA), jnp.bfloat16),
            pltpu.VMEM((MLA_HEADS, s, QK_W), jnp.bfloat16),
            pltpu.VMEM((2, s, QK_W), jnp.bfloat16),
            pltpu.VMEM((2, MLA_DV + BF16_ROWS, s), jnp.bfloat16),
            pltpu.VMEM((2, s, Q_TILE), jnp.float32),
            pltpu.VMEM((2, F32_ROWS, Q_TILE), jnp.float32),
        ],
        compiler_params=pltpu.CompilerParams(
            dimension_semantics=("arbitrary",),
            vmem_limit_bytes=VMEM_LIMIT),
        name="mla",
    )(mq, mkv, misc, *tabs, gq, gkv, wuq_p, wuk_p, wuvt_p)


def _outproj_kernel(tiles_per_seq, final, x_ref, mod_ref, ogla_ref, omla_ref, cb_ref, u_ref,
                    up_ref, un_ref, sz_ref, mg_ref, cw_ref, cg_ref, fg_ref, w_ref, o_ref):
    f32 = jnp.float32
    j = pl.program_id(0) % tiles_per_seq
    tm = x_ref.shape[0]

    u = u_ref[...].astype(f32)
    prev_ok = (j > 0).astype(f32)
    next_ok = (j < tiles_per_seq - 1).astype(f32)
    u_prev = up_ref[...].astype(f32)[HALO_ROWS - 1:, :] * prev_ok
    u_next = un_ref[...].astype(f32)[0:1, :] * next_ok
    rows = lax.broadcasted_iota(jnp.int32, u.shape, 0)
    up = jnp.where(rows == 0, u_prev, pltpu.roll(u, 1, axis=0))
    un = jnp.where(rows == tm - 1, u_next, pltpu.roll(u, tm - 1, axis=0))
    cw = cw_ref[...]
    conv = up * cw[0:1] + u * cw[1:2] + un * cw[2:3]
    gate = mod_ref[...][2:3]

    def gated_dot(r, lo, y_part):
        hi = lo + y_part.shape[1]
        y = (y_part * sz_ref[r, lo:hi].astype(f32)).astype(jnp.bfloat16)
        return _bdot(y, w_ref[lo:hi, :])

    def finish(r, acc):
        out = x_ref[r, :] + gate * acc
        if final:
            out = _rms(out, fg_ref[...])
        o_ref[r, :] = out

    pending = None
    for k in range(tm // OUT_SUB_TILE):
        r = slice(k * OUT_SUB_TILE, (k + 1) * OUT_SUB_TILE)
        acc = gated_dot(r, 0, ogla_ref[r, :].astype(f32))
        if pending is not None:
            finish(*pending)
        omla = jnp.concatenate([omla_ref[h, r, :] for h in range(MLA_HEADS)], axis=1)
        acc += gated_dot(r, GLA_WIDTH, _rms(omla.astype(f32), mg_ref[...]))
        acc += gated_dot(r, GLA_WIDTH + MLA_WIDTH,
                         _rms(cb_ref[r, :].astype(f32) * conv[r, :], cg_ref[...]))
        pending = (r, acc)
    finish(*pending)


def _outproj_call(layer, x2, mod, o_gla, o_mla, cb, u, sz, mla_out_g, conv_w, conv_out_g,
                  final_g, w_out_b, seq_len, final):
    t = x2.shape[0]
    tm = OUT_ROW_TILE
    tiles_per_seq = seq_len // tm
    halo_per_tile = tm // HALO_ROWS
    n_halo = t // HALO_ROWS

    def row_spec(w):
        return pl.BlockSpec((tm, w), lambda i: (i, 0))

    prev_spec = pl.BlockSpec((HALO_ROWS, CONV_CH),
                             lambda i: (jnp.maximum(i * halo_per_tile - 1, 0), 0))
    next_spec = pl.BlockSpec((HALO_ROWS, CONV_CH),
                             lambda i: (jnp.minimum((i + 1) * halo_per_tile, n_halo - 1), 0))

    return pl.pallas_call(
        functools.partial(_outproj_kernel, tiles_per_seq, final),
        grid=(t // tm,),
        in_specs=[
            row_spec(D_MODEL),
            pl.BlockSpec((None, None, 3, D_MODEL), lambda i: (layer, i // tiles_per_seq, 0, 0)),
            row_spec(GLA_WIDTH),
            pl.BlockSpec((None, MLA_HEADS, tm, MLA_DV),
                         lambda i: (i // tiles_per_seq, 0, i % tiles_per_seq, 0)),
            row_spec(CONV_CH), row_spec(CONV_CH), prev_spec, next_spec,
            row_spec(D_MODEL),
            _layer_spec(layer, (1, MLA_WIDTH)), _layer_spec(layer, (3, CONV_CH)),
            _layer_spec(layer, (1, CONV_CH)),
            pl.BlockSpec((1, D_MODEL), lambda i: (0, 0)),
            pl.BlockSpec((D_MODEL, D_MODEL), lambda i: (0, 0), pipeline_mode=pl.Buffered(1)),
        ],
        out_specs=row_spec(D_MODEL),
        out_shape=jax.ShapeDtypeStruct((t, D_MODEL), jnp.float32),
        compiler_params=pltpu.CompilerParams(
            dimension_semantics=("arbitrary",), vmem_limit_bytes=VMEM_LIMIT),
        name="out_proj",
    )(x2, mod, o_gla, o_mla, cb, u, u, u, sz,
      mla_out_g, conv_w, conv_out_g, final_g, w_out_b)


_O_GLR = 2 * GLA_HEADS * GLA_DK + GLA_WIDTH
_O_MQ = _O_GLR + 2 * GLA_GATE_RANK
_O_MKV = _O_MQ + MLA_Q_LORA
_O_MKR = _O_MKV + MLA_KV_LORA
_O_CB = _O_MKR + MLA_ROPE
IN_DIM = _O_CB + 3 * CONV_CH + D_MODEL
W_PREP_ROWS = 256


W_PREP_CHUNK = 512


def _w_in_prep_kernel(wt_ref, o_ref):
    rows = o_ref.shape[0]
    eye = (lax.broadcasted_iota(jnp.int32, (rows, rows), 0)
           == lax.broadcasted_iota(jnp.int32, (rows, rows), 1)).astype(jnp.bfloat16)

    def put(dst, block):
        o_ref[:, dst:dst + block.shape[0]] = lax.dot_general(
            eye, block.astype(jnp.bfloat16), _NT,
            preferred_element_type=jnp.float32).astype(jnp.bfloat16)

    def put_range(dst, lo, hi):
        for start in range(lo, hi, W_PREP_CHUNK):
            stop = min(start + W_PREP_CHUNK, hi)
            put(dst + start - lo, wt_ref[start:stop, :])

    pad_rows = MISC_W - MLA_ROPE - 2 * GLA_GATE_RANK
    misc = jnp.concatenate([wt_ref[_O_MKR:_O_CB, :], wt_ref[_O_GLR:_O_MQ, :],
                            jnp.zeros((pad_rows, rows), jnp.float32)], axis=0)
    off = 0
    for src in ((0, _O_GLR), (_O_MQ, _O_MKV), misc, (_O_MKV, _O_MKR), (_O_CB, IN_DIM)):
        if isinstance(src, tuple):
            put_range(off, *src)
            off += src[1] - src[0]
        else:
            put(off, src)
            off += src.shape[0]


def _w_in_prep_call(w_in_t, layer):
    _, n, d = w_in_t.shape
    return pl.pallas_call(
        _w_in_prep_kernel,
        grid=(d // W_PREP_ROWS,),
        in_specs=[pl.BlockSpec((None, n, W_PREP_ROWS), lambda i: (layer, 0, i))],
        out_specs=pl.BlockSpec((W_PREP_ROWS, IN_DIM_P), lambda i: (i, 0)),
        out_shape=jax.ShapeDtypeStruct((d, IN_DIM_P), jnp.bfloat16),
        compiler_params=pltpu.CompilerParams(
            dimension_semantics=("arbitrary",), vmem_limit_bytes=VMEM_LIMIT),
        name="w_in_prep",
    )(w_in_t)


def _gate_params(wg_f, bg_f, wg_b, bg_b):
    depth, rank, hk = wg_f.shape
    zeros = jnp.zeros((depth, rank, hk), jnp.float32)
    rows = jnp.concatenate([jnp.concatenate([wg_f, zeros], axis=2),
                            jnp.concatenate([zeros, wg_b], axis=2)], axis=1)
    wg_cat = jnp.pad(rows, ((0, 0), (MLA_ROPE, MISC_W - MLA_ROPE - 2 * rank), (0, 0)))
    bg_cat = jnp.concatenate([bg_f, bg_b], axis=1)[:, None, :]
    return wg_cat.astype(jnp.bfloat16), bg_cat


def _permute_w_uq(w):
    depth = w.shape[0]
    half = MLA_ROPE // 2
    w4 = w.reshape(depth, MLA_Q_LORA, MLA_HEADS, MLA_NOPE + MLA_ROPE)
    nope, x1, x2 = w4[..., :MLA_NOPE], w4[..., MLA_NOPE:MLA_NOPE + half], w4[..., MLA_NOPE + half:]
    cols = jnp.concatenate([nope, x1, x2, x2, x1], axis=-1)
    return cols.reshape(depth, MLA_Q_LORA, MLA_HEADS * QK_W).astype(jnp.bfloat16)


def _split_w_ukv(w):
    depth = w.shape[0]
    w4 = w.reshape(depth, MLA_KV_LORA, MLA_HEADS, MLA_NOPE + MLA_DV)
    wk = w4[..., :MLA_NOPE].transpose(0, 2, 1, 3)
    wv_t = w4[..., MLA_NOPE:].transpose(0, 2, 3, 1)
    return wk.astype(jnp.bfloat16), wv_t.astype(jnp.bfloat16)


def kernel(x, c, positions, ada_w, ada_b, norm_g, w_in, gla_wg_f, gla_bg_f, gla_wg_b, gla_bg_b,
           gla_norm_g, mla_q_norm_g, mla_kv_norm_g, mla_w_uq, mla_w_ukv, mla_out_g, conv_w,
           conv_out_g, w_out, final_g):
    bsz, s, d = x.shape
    t = bsz * s

    mod = _ada_call(c, ada_w, ada_b)[:, :bsz].reshape(DEPTH, bsz, 3, d)

    w_in_t = jnp.swapaxes(w_in, 1, 2)
    w_in_p = _w_in_prep_call(w_in_t, 0)
    wg_cat, bg_cat = _gate_params(gla_wg_f, gla_bg_f, gla_wg_b, gla_bg_b)
    wuq_p = _permute_w_uq(mla_w_uq)
    wuk_p, wuvt_p = _split_w_ukv(mla_w_ukv)

    def row(p):
        return p[:, None, :]

    def b3(a):
        return a.reshape(bsz, s, a.shape[-1])

    h = x.reshape(t, d)
    for l in range(DEPTH):
        segs, w_out_b, new_tabs = _inproj_call(l, h, mod, row(norm_g), w_in_p, w_out, s,
                                               positions if l == 0 else None)
        if l == 0:
            tabs = tuple(b3(tab) for tab in new_tabs)
        gq, gk, gv, mq, misc, mkv, cb, u, sz = segs
        o_gla, *next_w = _gla_call(l, b3(gq), b3(gk), b3(gv), b3(misc), wg_cat, bg_cat,
                                   row(gla_norm_g), w_in_t if l + 1 < DEPTH else None)
        if next_w:
            w_in_p = next_w[0]
        o_mla = _mla_call(l, b3(mq), b3(mkv), b3(misc), tabs, row(mla_q_norm_g),
                          row(mla_kv_norm_g), wuq_p, wuk_p, wuvt_p)
        h = _outproj_call(l, h, mod, o_gla.reshape(t, -1), o_mla, cb, u, sz,
                          row(mla_out_g), conv_w, row(conv_out_g), final_g.reshape(1, d),
                          w_out_b, s, l == DEPTH - 1)
    return h.reshape(bsz, s, d)
```

```python
import functools
import math

import jax
import jax.numpy as jnp
from jax import lax
from jax.experimental import pallas as pl
from jax.experimental.pallas import tpu as pltpu

D_MODEL = 2048
DEPTH = 2
GLA_HEADS = 6
GLA_DK = 64
GLA_DV = 128
GLA_GATE_RANK = 16
GLA_GATE_TEMP = 16.0
GLA_CHUNK = 64
GLA_WIDTH = GLA_HEADS * GLA_DV
MLA_HEADS = 6
MLA_Q_LORA = 384
MLA_KV_LORA = 256
MLA_NOPE = 128
MLA_ROPE = 64
MLA_DV = 128
MLA_WIDTH = MLA_HEADS * MLA_DV
ROPE_THETA = 10000.0
CONV_CH = D_MODEL - GLA_WIDTH - MLA_WIDTH
EPS = 1e-6

LANES = 128
MXU_COLS = 256
F32_ROWS = 8
BF16_ROWS = 16
HALO_ROWS = BF16_ROWS
MISC_W = LANES
QK_W = 2 * LANES

IN_SEGMENTS = (
    ("gq", GLA_HEADS * GLA_DK, jnp.bfloat16),
    ("gk", GLA_HEADS * GLA_DK, jnp.bfloat16),
    ("gv", GLA_WIDTH, jnp.bfloat16),
    ("mq", MLA_Q_LORA, jnp.bfloat16),
    ("misc", MISC_W, jnp.float32),
    ("mkv", MLA_KV_LORA, jnp.bfloat16),
    ("cb", CONV_CH, jnp.bfloat16),
    ("cc", CONV_CH, jnp.bfloat16),
    ("cx", CONV_CH, jnp.bfloat16),
    ("z", D_MODEL, jnp.bfloat16),
)
IN_DIM_P = sum(w for _, w, _ in IN_SEGMENTS)
_SEGMENT_COLS = {}
for _name, _width, _ in IN_SEGMENTS:
    _SEGMENT_COLS[_name] = (sum(w for _, w in _SEGMENT_COLS.values()), _width)
IN_OUTPUTS = tuple(seg for seg in IN_SEGMENTS if seg[0] not in ("cc", "cx", "z")) + (
    ("u", CONV_CH, jnp.bfloat16),
    ("sz", D_MODEL, jnp.bfloat16),
)

IN_ROW_TILE = 512
IN_SUB_TILE = 256
OUT_ROW_TILE = 512
OUT_SUB_TILE = 256
Q_TILE = 512
KV_ROWS = 512
SCAN_ROWS = 256
SCAN_GROUP = 4
GLA_GROUP = 8
VMEM_LIMIT = 56 * 1024 * 1024

_NT = (((1,), (1,)), ((), ()))
_TN = (((0,), (0,)), ((), ()))


def _rms(x, g):
    ms = jnp.mean(x * x, axis=-1, keepdims=True)
    return x * lax.rsqrt(ms + EPS) * g


def _bdot(a, b):
    return jnp.dot(a, b, preferred_element_type=jnp.float32)


ADA_COLS = 1024
ADA_KROWS = 256


def _ada_kernel(ct_ref, w_ref, b_ref, o_ref, cb_s):
    d, bsz = ct_ref.shape
    tn = w_ref.shape[1]

    @pl.when((pl.program_id(0) == 0) & (pl.program_id(1) == 0))
    def _broadcast_c():
        ct = ct_ref[...]
        act = ct * jax.nn.sigmoid(ct)
        for b in range(bsz):
            cb_s[b] = jnp.broadcast_to(act[:, b:b + 1], (d, LANES))

    def body(i, accs):
        accs = list(accs)
        for g in range(ADA_KROWS // F32_ROWS):
            r = pl.ds(pl.multiple_of(i * ADA_KROWS + g * F32_ROWS, F32_ROWS), F32_ROWS)
            w = w_ref[r, :]
            for b in range(bsz):
                accs[b] = accs[b] + w * jnp.concatenate([cb_s[b, r, :]] * (tn // LANES), axis=1)
        return tuple(accs)

    zero = jnp.zeros((F32_ROWS, tn), jnp.float32)
    accs = lax.fori_loop(0, d // ADA_KROWS, body, (zero,) * bsz)
    rows = [jnp.sum(a, axis=0, keepdims=True) for a in accs]
    rows.append(jnp.zeros((o_ref.shape[0] - bsz, tn), jnp.float32))
    o_ref[...] = jnp.concatenate(rows, axis=0) + b_ref[...]


def _ada_call(c, ada_w, ada_b):
    bsz = c.shape[0]
    n = ada_w.shape[-1]
    tn = ADA_COLS
    return pl.pallas_call(
        _ada_kernel,
        grid=(DEPTH, n // tn),
        in_specs=[
            pl.BlockSpec((D_MODEL, bsz), lambda l, j: (0, 0)),
            pl.BlockSpec((None, D_MODEL, tn), lambda l, j: (l, 0, j)),
            pl.BlockSpec((None, 1, tn), lambda l, j: (l, 0, j)),
        ],
        out_specs=pl.BlockSpec((None, F32_ROWS, tn), lambda l, j: (l, 0, j)),
        out_shape=jax.ShapeDtypeStruct((DEPTH, F32_ROWS, n), jnp.float32),
        scratch_shapes=[pltpu.VMEM((bsz, D_MODEL, LANES), jnp.float32)],
        compiler_params=pltpu.CompilerParams(
            dimension_semantics=("arbitrary", "arbitrary"), vmem_limit_bytes=VMEM_LIMIT),
        name="ada_mod",
    )(c.T, ada_w, ada_b.reshape(DEPTH, 1, n))


def _rope_tables(pos_ref, invf_ref, c_ref, sa_ref, sb_ref):
    rows = pos_ref.shape[0]
    lane = lax.broadcasted_iota(jnp.int32, (rows, LANES), 1)
    pos = jnp.where(lane < MLA_ROPE, pos_ref[:, 0:1], pos_ref[:, 1:2]).astype(jnp.float32)
    ang = pos * invf_ref[...]
    half = MLA_ROPE // 2
    in_first, in_second = lane < half, (lane >= half) & (lane < MLA_ROPE)
    cos2, sin2 = jnp.cos(ang), jnp.sin(ang)
    for r, shift in ((slice(0, rows), 0), (slice(rows, 2 * rows), MLA_ROPE)):
        cos = cos2 if shift == 0 else pltpu.roll(cos2, shift, axis=1)
        sin = sin2 if shift == 0 else pltpu.roll(sin2, shift, axis=1)
        c_ref[r, :] = jnp.where(lane < MLA_ROPE, cos, 0.0)
        sa_ref[r, :] = jnp.where(in_first, -sin, 0.0)
        sb_ref[r, :] = jnp.where(in_second, sin, 0.0)


def _rope_table_inputs(positions, tile_rows):
    inv_freq = ROPE_THETA ** (-jnp.arange(0, MLA_ROPE, 2, dtype=jnp.float32) / MLA_ROPE)
    invf = jnp.tile(inv_freq, LANES // inv_freq.shape[0]).reshape(1, LANES)
    pairs = positions.reshape(-1, 2, tile_rows // 2).transpose(0, 2, 1)
    return pairs, invf


def _rope(x, c, sa, sb):
    return (x * c + pltpu.roll(x, LANES - MLA_ROPE // 2, axis=1) * sa
            + pltpu.roll(x, MLA_ROPE // 2, axis=1) * sb)


def _inproj_kernel(with_rope_tables, x_ref, mod_ref, g_ref, w_ref, wout_ref, *refs):
    n_out = len(IN_OUTPUTS)
    if with_rope_tables:
        pos_ref, invf_ref = refs[0:2]
        out_refs, wout_b_ref = refs[2:2 + n_out], refs[2 + n_out]
        _rope_tables(pos_ref, invf_ref, *refs[3 + n_out:6 + n_out])
    else:
        out_refs, wout_b_ref = refs[0:n_out], refs[n_out]
    h_s = refs[-1]
    wout_b_ref[...] = wout_ref[...].astype(jnp.bfloat16)
    mod = mod_ref[...]
    shift, gain = mod[0:1], g_ref[...] * (1.0 + mod[1:2])
    n_sub = x_ref.shape[0] // IN_SUB_TILE

    def normalize(k):
        x = x_ref[k * IN_SUB_TILE:(k + 1) * IN_SUB_TILE, :]
        ms = jnp.mean(x * x, axis=-1, keepdims=True)
        h_s[k % 2] = (x * lax.rsqrt(ms + EPS) * gain + shift).astype(jnp.bfloat16)

    outs = {name: o_ref for (name, _, _), o_ref in zip(IN_OUTPUTS, out_refs)}

    def project(k):
        r = slice(k * IN_SUB_TILE, (k + 1) * IN_SUB_TILE)

        def proj(names):
            lo = _SEGMENT_COLS[names[0]][0]
            width = sum(_SEGMENT_COLS[n][1] for n in names)
            assert width % MXU_COLS == 0
            val = _bdot(h_s[k % 2], w_ref[:, lo:lo + width])
            return {n: val[:, _SEGMENT_COLS[n][0] - lo:_SEGMENT_COLS[n][0] - lo + _SEGMENT_COLS[n][1]]
                    for n in names}

        z = proj(("z",))["z"]
        outs["sz"][r, :] = (z * jax.nn.sigmoid(z)).astype(outs["sz"].dtype)
        conv = proj(("cb", "cc", "cx"))
        outs["u"][r, :] = (conv["cc"] * conv["cx"]).astype(outs["u"].dtype)
        outs["cb"][r, :] = conv["cb"].astype(outs["cb"].dtype)
        for names in (("gq", "gk", "gv"), ("mq", "misc", "mkv")):
            for name, val in proj(names).items():
                outs[name][r, :] = val.astype(outs[name].dtype)

    normalize(0)
    for k in range(n_sub):
        project(k)
        if k + 1 < n_sub:
            normalize(k + 1)


def _layer_spec(layer, shape):
    return pl.BlockSpec((None,) + shape, lambda *_: (layer,) + (0,) * len(shape))


def _inproj_call(layer, x2, mod, norm_g, w_in_p, w_out, seq_len, positions=None):
    t = x2.shape[0]
    tm = IN_ROW_TILE
    n_steps = t // tm
    tiles_per_seq = seq_len // tm
    d_out = w_out.shape[-1]
    wout_rows = w_out.shape[1] // n_steps
    n_out = len(IN_OUTPUTS)
    in_specs = [
        pl.BlockSpec((tm, D_MODEL), lambda i: (i, 0)),
        pl.BlockSpec((None, None, 3, D_MODEL), lambda i: (layer, i // tiles_per_seq, 0, 0)),
        _layer_spec(layer, (1, D_MODEL)),
        pl.BlockSpec((D_MODEL, IN_DIM_P), lambda i: (0, 0), pipeline_mode=pl.Buffered(1)),
        pl.BlockSpec((None, wout_rows, d_out), lambda i: (layer, i, 0)),
    ]
    operands = [x2, mod, norm_g, w_in_p, w_out]
    out_shape = [jax.ShapeDtypeStruct((t, w), dt) for _, w, dt in IN_OUTPUTS]
    out_specs = [pl.BlockSpec((tm, w), lambda i: (i, 0)) for _, w, _ in IN_OUTPUTS]
    out_shape.append(jax.ShapeDtypeStruct(w_out.shape[1:], jnp.bfloat16))
    out_specs.append(pl.BlockSpec((wout_rows, d_out), lambda i: (i, 0)))
    if positions is not None:
        pairs, invf = _rope_table_inputs(positions, tm)
        in_specs += [pl.BlockSpec((None, tm // 2, 2), lambda i: (i, 0, 0)),
                     pl.BlockSpec((1, LANES), lambda i: (0, 0))]
        operands += [pairs, invf]
        out_shape += [jax.ShapeDtypeStruct((t, LANES), jnp.float32)] * 3
        out_specs += [pl.BlockSpec((tm, LANES), lambda i: (i, 0))] * 3
    outs = pl.pallas_call(
        functools.partial(_inproj_kernel, positions is not None),
        grid=(n_steps,),
        in_specs=in_specs,
        out_specs=out_specs,
        out_shape=out_shape,
        scratch_shapes=[pltpu.VMEM((2, IN_SUB_TILE, D_MODEL), jnp.bfloat16)],
        compiler_params=pltpu.CompilerParams(
            dimension_semantics=("arbitrary",), vmem_limit_bytes=VMEM_LIMIT),
        name="in_proj",
    )(*operands)
    return outs[:n_out], outs[n_out], tuple(outs[n_out + 1:])


def _log_sigmoid(x):
    return jnp.minimum(x, 0.0) - jnp.log(1.0 + jnp.exp(-jnp.abs(x)))


def _gla_kernel(n_w_slabs, q_ref, k_ref, v_ref, misc_ref, wgf_ref, bgf_ref, wgb_ref, bgb_ref,
                g_ref, *refs):
    if n_w_slabs:
        wt_ref, o_ref, w_prep_ref = refs[0:3]
        step = pl.program_id(0) * pl.num_programs(1) + pl.program_id(1)

        @pl.when(step < n_w_slabs)
        def _prep_next_layer_weight():
            _w_in_prep_kernel(wt_ref, w_prep_ref)

        refs = refs[3:]
    else:
        o_ref, refs = refs[0], refs[1:]
    bf_s, bb_s, qcat_s, kv_s, dec_s, st_s, intra_s = refs
    s = q_ref.shape[0]
    cdim = GLA_CHUNK
    n_chunks = s // cdim
    pair_dk = 2 * GLA_DK
    pair_dv = 2 * GLA_DV
    inv_t = 1.0 / GLA_GATE_TEMP
    q_scale = GLA_DK ** -0.5

    t_row = lax.broadcasted_iota(jnp.int32, (SCAN_ROWS, SCAN_ROWS), 0)
    t_col = lax.broadcasted_iota(jnp.int32, (SCAN_ROWS, SCAN_ROWS), 1)
    same_chunk = (t_row // cdim) == (t_col // cdim)
    tri_prefix = (same_chunk & (t_col <= t_row)).astype(jnp.bfloat16)
    tri_suffix = (same_chunk & (t_col >= t_row)).astype(jnp.bfloat16)

    def split(terms):
        high = terms.astype(jnp.bfloat16)
        rest = (terms - high.astype(jnp.float32)).astype(jnp.bfloat16)
        return jnp.concatenate([high, rest], axis=1)

    def scan_body(i, carry):
        blocks = [pl.ds(pl.multiple_of((i * SCAN_GROUP + c) * SCAN_ROWS, SCAN_ROWS), SCAN_ROWS)
                  for c in range(SCAN_GROUP)]
        pre = []
        for r in blocks:
            m = misc_ref[r, :].astype(jnp.bfloat16)
            pre.append((_bdot(m, wgf_ref[...]), _bdot(m, wgb_ref[...])))
        terms = [(split(_log_sigmoid(pf + bgf_ref[...]) * inv_t),
                  split(_log_sigmoid(pb + bgb_ref[...]) * inv_t)) for pf, pb in pre]
        sums = [(_bdot(tri_prefix, tf), _bdot(tri_suffix, tb)) for tf, tb in terms]
        for r, (sf, sb) in zip(blocks, sums):
            bf_s[r, :] = sf[:, 0:pair_dk] + sf[:, pair_dk:]
            bb_s[r, :] = sb[:, 0:pair_dk] + sb[:, pair_dk:]
        return carry

    lax.fori_loop(0, s // (SCAN_ROWS * SCAN_GROUP), scan_body, 0)

    row = lax.broadcasted_iota(jnp.int32, (cdim, pair_dk), 0)
    key_pos = lax.broadcasted_iota(jnp.int32, (cdim, pair_dk), 1) & (cdim - 1)
    keep_f = key_pos <= row
    keep_b = key_pos > row
    cat_block = lax.broadcasted_iota(jnp.int32, (cdim, 2 * pair_dk), 1) // GLA_DK
    v_head = lax.broadcasted_iota(jnp.int32, (cdim, pair_dv), 1) // GLA_DV
    head0_lanes = (lax.broadcasted_iota(jnp.int32, (GLA_DV, 2 * pair_dk), 1) // GLA_DK) % 2 == 0

    def intra_body(i, carry):
        chunks = [i * GLA_GROUP + c for c in range(GLA_GROUP)]
        rows = [pl.ds(pl.multiple_of(n * cdim, cdim), cdim) for n in chunks]
        scores, k_ends = [], []
        for n, r in zip(chunks, rows):
            bf = bf_s[r, :]
            bb = bb_s[r, :]
            bf_last = bf[cdim - 1:cdim, :]
            bb_last = bb[0:1, :]
            q = q_ref[r, :].astype(jnp.float32) * q_scale
            k = k_ref[r, :].astype(jnp.float32)
            q_cat = jnp.concatenate([q * jnp.exp(bf), q * jnp.exp(bb)],
                                    axis=1).astype(jnp.bfloat16)
            k_inv = jnp.concatenate([k * jnp.exp(-bf), k * jnp.exp(-bb)], axis=1)
            k_ends.append(jnp.concatenate(
                [k * jnp.exp(bf_last - bf), k * jnp.exp(bb_last - bb)], axis=1).astype(jnp.bfloat16))
            dec_s[n] = jnp.concatenate([jnp.exp(bf_last), jnp.exp(bb_last)], axis=1)
            qcat_s[r, :] = q_cat
            k_bd = jnp.concatenate([jnp.where(cat_block == c, k_inv, 0.0) for c in range(4)],
                                   axis=0).astype(jnp.bfloat16)
            scores.append(lax.dot_general(q_cat, k_bd, _NT,
                                          preferred_element_type=jnp.float32))
        for n, r, k_end in zip(chunks, rows, k_ends):
            kv_t = lax.dot_general(v_ref[r, :], k_end, _TN,
                                   preferred_element_type=jnp.float32)
            kv_s[n] = jnp.where(head0_lanes, kv_t[0:GLA_DV, :], kv_t[GLA_DV:, :])
        for r, sc in zip(rows, scores):
            p = (jnp.where(keep_f, sc[:, 0:pair_dk], 0.0)
                 + jnp.where(keep_b, sc[:, pair_dk:], 0.0)).astype(jnp.bfloat16)
            vv = v_ref[r, :]
            v_bd = jnp.concatenate(
                [jnp.where(v_head == j, vv, jnp.zeros_like(vv)) for j in range(2)], axis=0)
            intra_s[r, :] = _bdot(p, v_bd)
        return carry

    lax.fori_loop(0, n_chunks // GLA_GROUP, intra_body, 0)

    def state_body(i, states):
        sf, sb = states
        nf = i
        nb = n_chunks - 1 - i
        st_s[nf, :, 0:pair_dk] = sf.astype(jnp.bfloat16)
        st_s[nb, :, pair_dk:] = sb.astype(jnp.bfloat16)
        sf = sf * dec_s[nf][:, 0:pair_dk] + kv_s[nf, :, 0:pair_dk]
        sb = sb * dec_s[nb][:, pair_dk:] + kv_s[nb, :, pair_dk:]
        return sf, sb

    zero = jnp.zeros((GLA_DV, pair_dk), jnp.float32)
    lax.fori_loop(0, n_chunks, state_body, (zero, zero))

    g = g_ref[...]

    def out_body(i, carry):
        chunks = [i * GLA_GROUP + c for c in range(GLA_GROUP)]
        rows = [pl.ds(pl.multiple_of(n * cdim, cdim), cdim) for n in chunks]
        inter = []
        for n, r in zip(chunks, rows):
            packed = st_s[n]
            none = jnp.zeros_like(packed)
            st_bd = jnp.concatenate([jnp.where(head0_lanes, packed, none),
                                     jnp.where(head0_lanes, none, packed)], axis=0)
            inter.append(lax.dot_general(qcat_s[r, :], st_bd, _NT,
                                         preferred_element_type=jnp.float32))
        for r, o_inter in zip(rows, inter):
            o = intra_s[r, :] + o_inter
            for j in range(2):
                o_ref[r, j * GLA_DV:(j + 1) * GLA_DV] = _rms(
                    o[:, j * GLA_DV:(j + 1) * GLA_DV], g).astype(o_ref.dtype)
        return carry

    lax.fori_loop(0, n_chunks // GLA_GROUP, out_body, 0)


def _gla_call(layer, gq, gk, gv, misc, wg_cat, bg_cat, gla_norm_g, w_in_t=None):
    bsz, s, _ = gq.shape
    pairs = GLA_HEADS // 2
    n_chunks = s // GLA_CHUNK
    pair_dk, pair_dv = 2 * GLA_DK, 2 * GLA_DV
    qk_spec = pl.BlockSpec((None, s, pair_dk), lambda b, p: (b, 0, p))
    v_spec = pl.BlockSpec((None, s, pair_dv), lambda b, p: (b, 0, p))

    def gate_specs(direction):
        return (pl.BlockSpec((None, MISC_W, pair_dk), lambda b, p: (layer, 0, direction * pairs + p)),
                pl.BlockSpec((None, 1, pair_dk), lambda b, p: (layer, 0, direction * pairs + p)))

    in_specs = [qk_spec, qk_spec, v_spec,
                pl.BlockSpec((None, s, MISC_W), lambda b, p: (b, 0, 0)),
                *gate_specs(0), *gate_specs(1),
                _layer_spec(layer, (1, GLA_DV))]
    operands = [gq, gk, gv, misc, wg_cat, bg_cat, wg_cat, bg_cat, gla_norm_g]
    out_specs = [v_spec]
    out_shape = [jax.ShapeDtypeStruct((bsz, s, GLA_WIDTH), jnp.bfloat16)]
    n_w_slabs = 0
    if w_in_t is not None:
        n_w_slabs = D_MODEL // W_PREP_ROWS
        assert n_w_slabs <= bsz * pairs

        def slab(b, p):
            return jnp.minimum(b * pairs + p, n_w_slabs - 1)

        in_specs.append(pl.BlockSpec((None, IN_DIM, W_PREP_ROWS),
                                     lambda b, p: (layer + 1, 0, slab(b, p))))
        operands.append(w_in_t)
        out_specs.append(pl.BlockSpec((W_PREP_ROWS, IN_DIM_P), lambda b, p: (slab(b, p), 0)))
        out_shape.append(jax.ShapeDtypeStruct((D_MODEL, IN_DIM_P), jnp.bfloat16))

    return pl.pallas_call(
        functools.partial(_gla_kernel, n_w_slabs),
        grid=(bsz, pairs),
        in_specs=in_specs,
        out_specs=out_specs,
        out_shape=out_shape,
        scratch_shapes=[
            pltpu.VMEM((s, pair_dk), jnp.float32),
            pltpu.VMEM((s, pair_dk), jnp.float32),
            pltpu.VMEM((s, 2 * pair_dk), jnp.bfloat16),
            pltpu.VMEM((n_chunks, GLA_DV, 2 * pair_dk), jnp.float32),
            pltpu.VMEM((n_chunks, 1, 2 * pair_dk), jnp.float32),
            pltpu.VMEM((n_chunks, GLA_DV, 2 * pair_dk), jnp.bfloat16),
            pltpu.VMEM((s, pair_dv), jnp.float32),
        ],
        compiler_params=pltpu.CompilerParams(
            dimension_semantics=("arbitrary", "arbitrary"), vmem_limit_bytes=VMEM_LIMIT),
        name="gla",
    )(*operands)


def _mla_kernel(mq_ref, mkv_ref, misc_ref, c_ref, sa_ref, sb_ref, gq_ref, gkv_ref,
                wuq_ref, wuk_ref, wuvt_ref, o_ref, ckv_s, q_s, k_s, vt_s, sc_s, top_s):
    s = mkv_ref.shape[0]
    n_tiles = s // Q_TILE
    assert n_tiles % 2 == 0 and MLA_HEADS % 2 == 0
    blocks = [slice(i * KV_ROWS, (i + 1) * KV_ROWS) for i in range(s // KV_ROWS)]
    q_mult = (MLA_NOPE + MLA_ROPE) ** -0.5 * math.log2(math.e)

    for r in blocks:
        tabs = (c_ref[r, :], sa_ref[r, :], sb_ref[r, :])
        ckv_s[r, :] = _rms(mkv_ref[r, :].astype(jnp.float32), gkv_ref[...]).astype(jnp.bfloat16)
        k_rope = _rope(misc_ref[r, :], *tabs).astype(jnp.bfloat16)
        for slot in range(2):
            k_s[slot, r, MLA_NOPE:] = k_rope
        cq = _rms(mq_ref[r, :].astype(jnp.float32), gq_ref[...]).astype(jnp.bfloat16)
        qf = _bdot(cq, wuq_ref[...])
        cos_t, sin_t = tabs[0], tabs[1] + tabs[2]
        for h in range(MLA_HEADS):
            qh = qf[:, h * QK_W:(h + 1) * QK_W]
            rot = qh[:, MLA_NOPE:]
            q_rope = rot * cos_t + pltpu.roll(rot, MLA_ROPE, axis=1) * sin_t
            q_s[h, r, :] = (jnp.concatenate([qh[:, 0:MLA_NOPE], q_rope], axis=1)
                            * q_mult).astype(jnp.bfloat16)
    for slot in range(2):
        vt_s[slot, MLA_DV:, :] = jnp.ones((vt_s.shape[1] - MLA_DV, s), jnp.bfloat16)

    def build_kv(h, slot):
        for r in blocks:
            ckv = ckv_s[r, :]
            k_s[slot, r, 0:MLA_NOPE] = _bdot(ckv, wuk_ref[h]).astype(jnp.bfloat16)
            vt_s[slot, 0:MLA_DV, r] = lax.dot_general(
                wuvt_ref[h], ckv, _NT, preferred_element_type=jnp.float32).astype(jnp.bfloat16)

    def scores_t(h, slot, j):
        q = q_s[h, j * Q_TILE:(j + 1) * Q_TILE, :]
        sc_t = lax.dot_general(k_s[slot], q, _NT, preferred_element_type=jnp.float32)
        sc_s[j % 2] = sc_t
        top_s[j % 2] = jnp.broadcast_to(jnp.max(sc_t, axis=0, keepdims=True), (F32_ROWS, Q_TILE))

    def finish(h, slot, j):
        p_t = jnp.exp2(sc_s[j % 2] - top_s[j % 2][0:1, :]).astype(jnp.bfloat16)
        o_t = _bdot(vt_s[slot], p_t)
        o_ref[h, j * Q_TILE:(j + 1) * Q_TILE, :] = (
            o_t[0:MLA_DV, :] / o_t[MLA_DV:MLA_DV + 1, :]).T.astype(o_ref.dtype)

    build_kv(0, 0)
    scores_t(0, 0, 0)

    def head_pair(i, carry):
        for slot in range(2):
            h = 2 * i + slot
            h_next = jnp.minimum(h + 1, MLA_HEADS - 1)
            for j in range(n_tiles):
                if j + 1 < n_tiles:
                    scores_t(h, slot, j + 1)
                else:
                    build_kv(h_next, 1 - slot)
                    scores_t(h_next, 1 - slot, 0)
                finish(h, slot, j)
        return carry

    lax.fori_loop(0, MLA_HEADS // 2, head_pair, 0)


def _mla_call(layer, mq, mkv, misc, tabs, gq, gkv, wuq_p, wuk_p, wuvt_p):
    bsz, s, _ = mq.shape

    def per_row(width):
        return pl.BlockSpec((None, s, width), lambda b: (b, 0, 0))

    return pl.pallas_call(
        _mla_kernel,
        grid=(bsz,),
        in_specs=[
            per_row(MLA_Q_LORA), per_row(MLA_KV_LORA), per_row(MISC_W),
            per_row(LANES), per_row(LANES), per_row(LANES),
            _layer_spec(layer, (1, MLA_Q_LORA)),
            _layer_spec(layer, (1, MLA_KV_LORA)),
            _layer_spec(layer, (MLA_Q_LORA, MLA_HEADS * QK_W)),
            _layer_spec(layer, (MLA_HEADS, MLA_KV_LORA, MLA_NOPE)),
            _layer_spec(layer, (MLA_HEADS, MLA_DV, MLA_KV_LORA)),
        ],
        out_specs=pl.BlockSpec((None, MLA_HEADS, s, MLA_DV), lambda b: (b, 0, 0, 0)),
        out_shape=jax.ShapeDtypeStruct((bsz, MLA_HEADS, s, MLA_DV), jnp.bfloat16),
        scratch_shapes=[
            pltpu.VMEM((s, MLA_KV_LORA), jnp.bfloat16),
            pltpu.VMEM((MLA_HEADS, s, QK_W), jnp.bfloat16),
            pltpu.VMEM((2, s, QK_W), jnp.bfloat16),
            pltpu.VMEM((2, MLA_DV + BF16_ROWS, s), jnp.bfloat16),
            pltpu.VMEM((2, s, Q_TILE), jnp.float32),
            pltpu.VMEM((2, F32_ROWS, Q_TILE), jnp.float32),
        ],
        compiler_params=pltpu.CompilerParams(
            dimension_semantics=("arbitrary",),
            vmem_limit_bytes=VMEM_LIMIT),
        name="mla",
    )(mq, mkv, misc, *tabs, gq, gkv, wuq_p, wuk_p, wuvt_p)


def _outproj_kernel(tiles_per_seq, final, x_ref, mod_ref, ogla_ref, omla_ref, cb_ref, u_ref,
                    up_ref, un_ref, sz_ref, mg_ref, cw_ref, cg_ref, fg_ref, w_ref, o_ref):
    f32 = jnp.float32
    j = pl.program_id(0) % tiles_per_seq
    tm = x_ref.shape[0]

    u = u_ref[...].astype(f32)
    prev_ok = (j > 0).astype(f32)
    next_ok = (j < tiles_per_seq - 1).astype(f32)
    u_prev = up_ref[...].astype(f32)[HALO_ROWS - 1:, :] * prev_ok
    u_next = un_ref[...].astype(f32)[0:1, :] * next_ok
    rows = lax.broadcasted_iota(jnp.int32, u.shape, 0)
    up = jnp.where(rows == 0, u_prev, pltpu.roll(u, 1, axis=0))
    un = jnp.where(rows == tm - 1, u_next, pltpu.roll(u, tm - 1, axis=0))
    cw = cw_ref[...]
    conv = up * cw[0:1] + u * cw[1:2] + un * cw[2:3]
    gate = mod_ref[...][2:3]

    def gated_dot(r, lo, y_part):
        hi = lo + y_part.shape[1]
        y = (y_part * sz_ref[r, lo:hi].astype(f32)).astype(jnp.bfloat16)
        return _bdot(y, w_ref[lo:hi, :])

    def finish(r, acc):
        out = x_ref[r, :] + gate * acc
        if final:
            out = _rms(out, fg_ref[...])
        o_ref[r, :] = out

    pending = None
    for k in range(tm // OUT_SUB_TILE):
        r = slice(k * OUT_SUB_TILE, (k + 1) * OUT_SUB_TILE)
        acc = gated_dot(r, 0, ogla_ref[r, :].astype(f32))
        if pending is not None:
            finish(*pending)
        omla = jnp.concatenate([omla_ref[h, r, :] for h in range(MLA_HEADS)], axis=1)
        acc += gated_dot(r, GLA_WIDTH, _rms(omla.astype(f32), mg_ref[...]))
        acc += gated_dot(r, GLA_WIDTH + MLA_WIDTH,
                         _rms(cb_ref[r, :].astype(f32) * conv[r, :], cg_ref[...]))
        pending = (r, acc)
    finish(*pending)


def _outproj_call(layer, x2, mod, o_gla, o_mla, cb, u, sz, mla_out_g, conv_w, conv_out_g,
                  final_g, w_out_b, seq_len, final):
    t = x2.shape[0]
    tm = OUT_ROW_TILE
    tiles_per_seq = seq_len // tm
    halo_per_tile = tm // HALO_ROWS
    n_halo = t // HALO_ROWS

    def row_spec(w):
        return pl.BlockSpec((tm, w), lambda i: (i, 0))

    prev_spec = pl.BlockSpec((HALO_ROWS, CONV_CH),
                             lambda i: (jnp.maximum(i * halo_per_tile - 1, 0), 0))
    next_spec = pl.BlockSpec((HALO_ROWS, CONV_CH),
                             lambda i: (jnp.minimum((i + 1) * halo_per_tile, n_halo - 1), 0))

    return pl.pallas_call(
        functools.partial(_outproj_kernel, tiles_per_seq, final),
        grid=(t // tm,),
        in_specs=[
            row_spec(D_MODEL),
            pl.BlockSpec((None, None, 3, D_MODEL), lambda i: (layer, i // tiles_per_seq, 0, 0)),
            row_spec(GLA_WIDTH),
            pl.BlockSpec((None, MLA_HEADS, tm, MLA_DV),
                         lambda i: (i // tiles_per_seq, 0, i % tiles_per_seq, 0)),
            row_spec(CONV_CH), row_spec(CONV_CH), prev_spec, next_spec,
            row_spec(D_MODEL),
            _layer_spec(layer, (1, MLA_WIDTH)), _layer_spec(layer, (3, CONV_CH)),
            _layer_spec(layer, (1, CONV_CH)),
            pl.BlockSpec((1, D_MODEL), lambda i: (0, 0)),
            pl.BlockSpec((D_MODEL, D_MODEL), lambda i: (0, 0), pipeline_mode=pl.Buffered(1)),
        ],
        out_specs=row_spec(D_MODEL),
        out_shape=jax.ShapeDtypeStruct((t, D_MODEL), jnp.float32),
        compiler_params=pltpu.CompilerParams(
            dimension_semantics=("arbitrary",), vmem_limit_bytes=VMEM_LIMIT),
        name="out_proj",
    )(x2, mod, o_gla, o_mla, cb, u, u, u, sz,
      mla_out_g, conv_w, conv_out_g, final_g, w_out_b)


_O_GLR = 2 * GLA_HEADS * GLA_DK + GLA_WIDTH
_O_MQ = _O_GLR + 2 * GLA_GATE_RANK
_O_MKV = _O_MQ + MLA_Q_LORA
_O_MKR = _O_MKV + MLA_KV_LORA
_O_CB = _O_MKR + MLA_ROPE
IN_DIM = _O_CB + 3 * CONV_CH + D_MODEL
W_PREP_ROWS = 256


W_PREP_CHUNK = 512


def _w_in_prep_kernel(wt_ref, o_ref):
    rows = o_ref.shape[0]
    eye = (lax.broadcasted_iota(jnp.int32, (rows, rows), 0)
           == lax.broadcasted_iota(jnp.int32, (rows, rows), 1)).astype(jnp.bfloat16)

    def put(dst, block):
        o_ref[:, dst:dst + block.shape[0]] = lax.dot_general(
            eye, block.astype(jnp.bfloat16), _NT,
            preferred_element_type=jnp.float32).astype(jnp.bfloat16)

    def put_range(dst, lo, hi):
        for start in range(lo, hi, W_PREP_CHUNK):
            stop = min(start + W_PREP_CHUNK, hi)
            put(dst + start - lo, wt_ref[start:stop, :])

    pad_rows = MISC_W - MLA_ROPE - 2 * GLA_GATE_RANK
    misc = jnp.concatenate([wt_ref[_O_MKR:_O_CB, :], wt_ref[_O_GLR:_O_MQ, :],
                            jnp.zeros((pad_rows, rows), jnp.float32)], axis=0)
    off = 0
    for src in ((0, _O_GLR), (_O_MQ, _O_MKV), misc, (_O_MKV, _O_MKR), (_O_CB, IN_DIM)):
        if isinstance(src, tuple):
            put_range(off, *src)
            off += src[1] - src[0]
        else:
            put(off, src)
            off += src.shape[0]


def _w_in_prep_call(w_in_t, layer):
    _, n, d = w_in_t.shape
    return pl.pallas_call(
        _w_in_prep_kernel,
        grid=(d // W_PREP_ROWS,),
        in_specs=[pl.BlockSpec((None, n, W_PREP_ROWS), lambda i: (layer, 0, i))],
        out_specs=pl.BlockSpec((W_PREP_ROWS, IN_DIM_P), lambda i: (i, 0)),
        out_shape=jax.ShapeDtypeStruct((d, IN_DIM_P), jnp.bfloat16),
        compiler_params=pltpu.CompilerParams(
            dimension_semantics=("arbitrary",), vmem_limit_bytes=VMEM_LIMIT),
        name="w_in_prep",
    )(w_in_t)


def _gate_params(wg_f, bg_f, wg_b, bg_b):
    depth, rank, hk = wg_f.shape
    zeros = jnp.zeros((depth, rank, hk), jnp.float32)
    rows = jnp.concatenate([jnp.concatenate([wg_f, zeros], axis=2),
                            jnp.concatenate([zeros, wg_b], axis=2)], axis=1)
    wg_cat = jnp.pad(rows, ((0, 0), (MLA_ROPE, MISC_W - MLA_ROPE - 2 * rank), (0, 0)))
    bg_cat = jnp.concatenate([bg_f, bg_b], axis=1)[:, None, :]
    return wg_cat.astype(jnp.bfloat16), bg_cat


def _permute_w_uq(w):
    depth = w.shape[0]
    half = MLA_ROPE // 2
    w4 = w.reshape(depth, MLA_Q_LORA, MLA_HEADS, MLA_NOPE + MLA_ROPE)
    nope, x1, x2 = w4[..., :MLA_NOPE], w4[..., MLA_NOPE:MLA_NOPE + half], w4[..., MLA_NOPE + half:]
    cols = jnp.concatenate([nope, x1, x2, x2, x1], axis=-1)
    return cols.reshape(depth, MLA_Q_LORA, MLA_HEADS * QK_W).astype(jnp.bfloat16)


def _split_w_ukv(w):
    depth = w.shape[0]
    w4 = w.reshape(depth, MLA_KV_LORA, MLA_HEADS, MLA_NOPE + MLA_DV)
    wk = w4[..., :MLA_NOPE].transpose(0, 2, 1, 3)
    wv_t = w4[..., MLA_NOPE:].transpose(0, 2, 3, 1)
    return wk.astype(jnp.bfloat16), wv_t.astype(jnp.bfloat16)


def kernel(x, c, positions, ada_w, ada_b, norm_g, w_in, gla_wg_f, gla_bg_f, gla_wg_b, gla_bg_b,
           gla_norm_g, mla_q_norm_g, mla_kv_norm_g, mla_w_uq, mla_w_ukv, mla_out_g, conv_w,
           conv_out_g, w_out, final_g):
    bsz, s, d = x.shape
    t = bsz * s

    mod = _ada_call(c, ada_w, ada_b)[:, :bsz].reshape(DEPTH, bsz, 3, d)

    w_in_t = jnp.swapaxes(w_in, 1, 2)
    w_in_p = _w_in_prep_call(w_in_t, 0)
    wg_cat, bg_cat = _gate_params(gla_wg_f, gla_bg_f, gla_wg_b, gla_bg_b)
    wuq_p = _permute_w_uq(mla_w_uq)
    wuk_p, wuvt_p = _split_w_ukv(mla_w_ukv)

    def row(p):
        return p[:, None, :]

    def b3(a):
        return a.reshape(bsz, s, a.shape[-1])

    h = x.reshape(t, d)
    for l in range(DEPTH):
        segs, w_out_b, new_tabs = _inproj_call(l, h, mod, row(norm_g), w_in_p, w_out, s,
                                               positions if l == 0 else None)
        if l == 0:
            tabs = tuple(b3(tab) for tab in new_tabs)
        gq, gk, gv, mq, misc, mkv, cb, u, sz = segs
        o_gla, *next_w = _gla_call(l, b3(gq), b3(gk), b3(gv), b3(misc), wg_cat, bg_cat,
                                   row(gla_norm_g), w_in_t if l + 1 < DEPTH else None)
        if next_w:
            w_in_p = next_w[0]
        o_mla = _mla_call(l, b3(mq), b3(mkv), b3(misc), tabs, row(mla_q_norm_g),
                          row(mla_kv_norm_g), wuq_p, wuk_p, wuvt_p)
        h = _outproj_call(l, h, mod, o_gla.reshape(t, -1), o_mla, cb, u, sz,
                          row(mla_out_g), conv_w, row(conv_out_g), final_g.reshape(1, d),
                          w_out_b, s, l == DEPTH - 1)
    return h.reshape(bsz, s, d)
```

```python
import functools
import math

import jax
import jax.numpy as jnp
from jax import lax
from jax.experimental import pallas as pl
from jax.experimental.pallas import tpu as pltpu

D_MODEL = 2048
DEPTH = 2
GLA_HEADS = 6
GLA_DK = 64
GLA_DV = 128
GLA_GATE_RANK = 16
GLA_GATE_TEMP = 16.0
GLA_CHUNK = 64
GLA_WIDTH = GLA_HEADS * GLA_DV
MLA_HEADS = 6
MLA_Q_LORA = 384
MLA_KV_LORA = 256
MLA_NOPE = 128
MLA_ROPE = 64
MLA_DV = 128
MLA_WIDTH = MLA_HEADS * MLA_DV
ROPE_THETA = 10000.0
CONV_CH = D_MODEL - GLA_WIDTH - MLA_WIDTH
EPS = 1e-6

LANES = 128
MXU_COLS = 256
F32_ROWS = 8
BF16_ROWS = 16
HALO_ROWS = BF16_ROWS
MISC_W = LANES
QK_W = 2 * LANES

IN_SEGMENTS = (
    ("gq", GLA_HEADS * GLA_DK, jnp.bfloat16),
    ("gk", GLA_HEADS * GLA_DK, jnp.bfloat16),
    ("gv", GLA_WIDTH, jnp.bfloat16),
    ("mq", MLA_Q_LORA, jnp.bfloat16),
    ("misc", MISC_W, jnp.float32),
    ("mkv", MLA_KV_LORA, jnp.bfloat16),
    ("cb", CONV_CH, jnp.bfloat16),
    ("cc", CONV_CH, jnp.bfloat16),
    ("cx", CONV_CH, jnp.bfloat16),
    ("z", D_MODEL, jnp.bfloat16),
)
IN_DIM_P = sum(w for _, w, _ in IN_SEGMENTS)
_SEGMENT_COLS = {}
for _name, _width, _ in IN_SEGMENTS:
    _SEGMENT_COLS[_name] = (sum(w for _, w in _SEGMENT_COLS.values()), _width)
IN_OUTPUTS = tuple(seg for seg in IN_SEGMENTS if seg[0] not in ("cc", "cx", "z")) + (
    ("u", CONV_CH, jnp.bfloat16),
    ("sz", D_MODEL, jnp.bfloat16),
)

IN_ROW_TILE = 512
IN_SUB_TILE = 256
OUT_ROW_TILE = 512
OUT_SUB_TILE = 256
Q_TILE = 512
KV_ROWS = 512
SCAN_ROWS = 256
SCAN_GROUP = 4
GLA_GROUP = 8
VMEM_LIMIT = 56 * 1024 * 1024

_NT = (((1,), (1,)), ((), ()))
_TN = (((0,), (0,)), ((), ()))


def _rms(x, g):
    ms = jnp.mean(x * x, axis=-1, keepdims=True)
    return x * lax.rsqrt(ms + EPS) * g


def _bdot(a, b):
    return jnp.dot(a, b, preferred_element_type=jnp.float32)


ADA_COLS = 1024
ADA_KROWS = 256


def _ada_kernel(ct_ref, w_ref, b_ref, o_ref, cb_s):
    d, bsz = ct_ref.shape
    tn = w_ref.shape[1]

    @pl.when((pl.program_id(0) == 0) & (pl.program_id(1) == 0))
    def _broadcast_c():
        ct = ct_ref[...]
        act = ct * jax.nn.sigmoid(ct)
        for b in range(bsz):
            cb_s[b] = jnp.broadcast_to(act[:, b:b + 1], (d, LANES))

    def body(i, accs):
        accs = list(accs)
        for g in range(ADA_KROWS // F32_ROWS):
            r = pl.ds(pl.multiple_of(i * ADA_KROWS + g * F32_ROWS, F32_ROWS), F32_ROWS)
            w = w_ref[r, :]
            for b in range(bsz):
                accs[b] = accs[b] + w * jnp.concatenate([cb_s[b, r, :]] * (tn // LANES), axis=1)
        return tuple(accs)

    zero = jnp.zeros((F32_ROWS, tn), jnp.float32)
    accs = lax.fori_loop(0, d // ADA_KROWS, body, (zero,) * bsz)
    rows = [jnp.sum(a, axis=0, keepdims=True) for a in accs]
    rows.append(jnp.zeros((o_ref.shape[0] - bsz, tn), jnp.float32))
    o_ref[...] = jnp.concatenate(rows, axis=0) + b_ref[...]


def _ada_call(c, ada_w, ada_b):
    bsz = c.shape[0]
    n = ada_w.shape[-1]
    tn = ADA_COLS
    return pl.pallas_call(
        _ada_kernel,
        grid=(DEPTH, n // tn),
        in_specs=[
            pl.BlockSpec((D_MODEL, bsz), lambda l, j: (0, 0)),
            pl.BlockSpec((None, D_MODEL, tn), lambda l, j: (l, 0, j)),
            pl.BlockSpec((None, 1, tn), lambda l, j: (l, 0, j)),
        ],
        out_specs=pl.BlockSpec((None, F32_ROWS, tn), lambda l, j: (l, 0, j)),
        out_shape=jax.ShapeDtypeStruct((DEPTH, F32_ROWS, n), jnp.float32),
        scratch_shapes=[pltpu.VMEM((bsz, D_MODEL, LANES), jnp.float32)],
        compiler_params=pltpu.CompilerParams(
            dimension_semantics=("arbitrary", "arbitrary"), vmem_limit_bytes=VMEM_LIMIT),
        name="ada_mod",
    )(c.T, ada_w, ada_b.reshape(DEPTH, 1, n))


def _rope_tables(pos_ref, invf_ref, c_ref, sa_ref, sb_ref):
    rows = pos_ref.shape[0]
    lane = lax.broadcasted_iota(jnp.int32, (rows, LANES), 1)
    pos = jnp.where(lane < MLA_ROPE, pos_ref[:, 0:1], pos_ref[:, 1:2]).astype(jnp.float32)
    ang = pos * invf_ref[...]
    half = MLA_ROPE // 2
    in_first, in_second = lane < half, (lane >= half) & (lane < MLA_ROPE)
    cos2, sin2 = jnp.cos(ang), jnp.sin(ang)
    for r, shift in ((slice(0, rows), 0), (slice(rows, 2 * rows), MLA_ROPE)):
        cos = cos2 if shift == 0 else pltpu.roll(cos2, shift, axis=1)
        sin = sin2 if shift == 0 else pltpu.roll(sin2, shift, axis=1)
        c_ref[r, :] = jnp.where(lane < MLA_ROPE, cos, 0.0)
        sa_ref[r, :] = jnp.where(in_first, -sin, 0.0)
        sb_ref[r, :] = jnp.where(in_second, sin, 0.0)


def _rope_table_inputs(positions, tile_rows):
    inv_freq = ROPE_THETA ** (-jnp.arange(0, MLA_ROPE, 2, dtype=jnp.float32) / MLA_ROPE)
    invf = jnp.tile(inv_freq, LANES // inv_freq.shape[0]).reshape(1, LANES)
    pairs = positions.reshape(-1, 2, tile_rows // 2).transpose(0, 2, 1)
    return pairs, invf


def _rope(x, c, sa, sb):
    return (x * c + pltpu.roll(x, LANES - MLA_ROPE // 2, axis=1) * sa
            + pltpu.roll(x, MLA_ROPE // 2, axis=1) * sb)


def _inproj_kernel(layer, tiles_per_seq, with_rope_tables, x_ref, mod_ref, g_ref, w_ref,
                   wout_ref, *refs):
    n_out = len(IN_OUTPUTS)
    if with_rope_tables:
        pos_ref, invf_ref = refs[0:2]
        out_refs, wout_b_ref = refs[2:2 + n_out], refs[2 + n_out]
        _rope_tables(pos_ref, invf_ref, *refs[3 + n_out:6 + n_out])
    else:
        out_refs, wout_b_ref = refs[0:n_out], refs[n_out]
    h_s = refs[-1]
    wout_b_ref[...] = wout_ref[...].astype(jnp.bfloat16)
    mod = mod_ref[pl.ds(pl.program_id(0) // tiles_per_seq, 1), :]
    shift = mod[:, 0:D_MODEL]
    gain = g_ref[layer:layer + 1, :] * (1.0 + mod[:, D_MODEL:2 * D_MODEL])
    n_sub = x_ref.shape[0] // IN_SUB_TILE

    def normalize(k):
        x = x_ref[k * IN_SUB_TILE:(k + 1) * IN_SUB_TILE, :]
        ms = jnp.mean(x * x, axis=-1, keepdims=True)
        h_s[k % 2] = (x * lax.rsqrt(ms + EPS) * gain + shift).astype(jnp.bfloat16)

    outs = {name: o_ref for (name, _, _), o_ref in zip(IN_OUTPUTS, out_refs)}

    def project(k):
        r = slice(k * IN_SUB_TILE, (k + 1) * IN_SUB_TILE)

        def proj(names):
            lo = _SEGMENT_COLS[names[0]][0]
            width = sum(_SEGMENT_COLS[n][1] for n in names)
            assert width % MXU_COLS == 0
            val = _bdot(h_s[k % 2], w_ref[:, lo:lo + width])
            return {n: val[:, _SEGMENT_COLS[n][0] - lo:_SEGMENT_COLS[n][0] - lo + _SEGMENT_COLS[n][1]]
                    for n in names}

        z = proj(("z",))["z"]
        outs["sz"][r, :] = (z * jax.nn.sigmoid(z)).astype(outs["sz"].dtype)
        conv = proj(("cb", "cc", "cx"))
        outs["u"][r, :] = (conv["cc"] * conv["cx"]).astype(outs["u"].dtype)
        outs["cb"][r, :] = conv["cb"].astype(outs["cb"].dtype)
        for names in (("gq", "gk", "gv"), ("mq", "misc", "mkv")):
            for name, val in proj(names).items():
                outs[name][r, :] = val.astype(outs[name].dtype)

    normalize(0)
    for k in range(n_sub):
        project(k)
        if k + 1 < n_sub:
            normalize(k + 1)


def _layer_spec(layer, shape):
    return pl.BlockSpec((None,) + shape, lambda *_: (layer,) + (0,) * len(shape))


def _whole_spec(arr):
    return pl.BlockSpec(arr.shape, lambda *_: (0,) * arr.ndim)


def _inproj_call(layer, x2, mod, norm_g, w_in_p, w_out, seq_len, positions=None):
    t = x2.shape[0]
    tm = IN_ROW_TILE
    n_steps = t // tm
    tiles_per_seq = seq_len // tm
    d_out = w_out.shape[-1]
    wout_rows = w_out.shape[1] // n_steps
    n_out = len(IN_OUTPUTS)
    in_specs = [
        pl.BlockSpec((tm, D_MODEL), lambda i: (i, 0)),
        _layer_spec(layer, mod.shape[1:]),
        _whole_spec(norm_g),
        pl.BlockSpec((D_MODEL, IN_DIM_P), lambda i: (0, 0), pipeline_mode=pl.Buffered(1)),
        pl.BlockSpec((None, wout_rows, d_out), lambda i: (layer, i, 0)),
    ]
    operands = [x2, mod, norm_g, w_in_p, w_out]
    out_shape = [jax.ShapeDtypeStruct((t, w), dt) for _, w, dt in IN_OUTPUTS]
    out_specs = [pl.BlockSpec((tm, w), lambda i: (i, 0)) for _, w, _ in IN_OUTPUTS]
    out_shape.append(jax.ShapeDtypeStruct(w_out.shape[1:], jnp.bfloat16))
    out_specs.append(pl.BlockSpec((wout_rows, d_out), lambda i: (i, 0)))
    if positions is not None:
        pairs, invf = _rope_table_inputs(positions, tm)
        in_specs += [pl.BlockSpec((None, tm // 2, 2), lambda i: (i, 0, 0)),
                     pl.BlockSpec((1, LANES), lambda i: (0, 0))]
        operands += [pairs, invf]
        out_shape += [jax.ShapeDtypeStruct((t, LANES), jnp.float32)] * 3
        out_specs += [pl.BlockSpec((tm, LANES), lambda i: (i, 0))] * 3
    outs = pl.pallas_call(
        functools.partial(_inproj_kernel, layer, tiles_per_seq, positions is not None),
        grid=(n_steps,),
        in_specs=in_specs,
        out_specs=out_specs,
        out_shape=out_shape,
        scratch_shapes=[pltpu.VMEM((2, IN_SUB_TILE, D_MODEL), jnp.bfloat16)],
        compiler_params=pltpu.CompilerParams(
            dimension_semantics=("arbitrary",), vmem_limit_bytes=VMEM_LIMIT),
        name="in_proj",
    )(*operands)
    return outs[:n_out], outs[n_out], tuple(outs[n_out + 1:])


def _log_sigmoid(x):
    return jnp.minimum(x, 0.0) - jnp.log(1.0 + jnp.exp(-jnp.abs(x)))


def _gla_kernel(layer, n_w_slabs, q_ref, k_ref, v_ref, misc_ref, wgf_ref, bgf_ref, wgb_ref,
                bgb_ref, g_ref, *refs):
    if n_w_slabs:
        wt_ref, o_ref, w_prep_ref = refs[0:3]
        step = pl.program_id(0) * pl.num_programs(1) + pl.program_id(1)

        @pl.when(step < n_w_slabs)
        def _prep_next_layer_weight():
            _w_in_prep_kernel(wt_ref, w_prep_ref)

        refs = refs[3:]
    else:
        o_ref, refs = refs[0], refs[1:]
    bf_s, bb_s, qcat_s, kv_s, dec_s, st_s, intra_s = refs
    s = q_ref.shape[0]
    cdim = GLA_CHUNK
    n_chunks = s // cdim
    pair_dk = 2 * GLA_DK
    pair_dv = 2 * GLA_DV
    inv_t = 1.0 / GLA_GATE_TEMP
    q_scale = GLA_DK ** -0.5

    t_row = lax.broadcasted_iota(jnp.int32, (SCAN_ROWS, SCAN_ROWS), 0)
    t_col = lax.broadcasted_iota(jnp.int32, (SCAN_ROWS, SCAN_ROWS), 1)
    same_chunk = (t_row // cdim) == (t_col // cdim)
    tri_prefix = (same_chunk & (t_col <= t_row)).astype(jnp.bfloat16)
    tri_suffix = (same_chunk & (t_col >= t_row)).astype(jnp.bfloat16)

    def split(terms):
        high = terms.astype(jnp.bfloat16)
        rest = (terms - high.astype(jnp.float32)).astype(jnp.bfloat16)
        return jnp.concatenate([high, rest], axis=1)

    def scan_body(i, carry):
        blocks = [pl.ds(pl.multiple_of((i * SCAN_GROUP + c) * SCAN_ROWS, SCAN_ROWS), SCAN_ROWS)
                  for c in range(SCAN_GROUP)]
        pre = []
        for r in blocks:
            m = misc_ref[r, :].astype(jnp.bfloat16)
            pre.append((_bdot(m, wgf_ref[...]), _bdot(m, wgb_ref[...])))
        terms = [(split(_log_sigmoid(pf + bgf_ref[...]) * inv_t),
                  split(_log_sigmoid(pb + bgb_ref[...]) * inv_t)) for pf, pb in pre]
        sums = [(_bdot(tri_prefix, tf), _bdot(tri_suffix, tb)) for tf, tb in terms]
        for r, (sf, sb) in zip(blocks, sums):
            bf_s[r, :] = sf[:, 0:pair_dk] + sf[:, pair_dk:]
            bb_s[r, :] = sb[:, 0:pair_dk] + sb[:, pair_dk:]
        return carry

    lax.fori_loop(0, s // (SCAN_ROWS * SCAN_GROUP), scan_body, 0)

    row = lax.broadcasted_iota(jnp.int32, (cdim, pair_dk), 0)
    key_pos = lax.broadcasted_iota(jnp.int32, (cdim, pair_dk), 1) & (cdim - 1)
    keep_f = key_pos <= row
    keep_b = key_pos > row
    cat_block = lax.broadcasted_iota(jnp.int32, (cdim, 2 * pair_dk), 1) // GLA_DK
    v_head = lax.broadcasted_iota(jnp.int32, (cdim, pair_dv), 1) // GLA_DV
    head0_lanes = (lax.broadcasted_iota(jnp.int32, (GLA_DV, 2 * pair_dk), 1) // GLA_DK) % 2 == 0

    def intra_body(i, carry):
        chunks = [i * GLA_GROUP + c for c in range(GLA_GROUP)]
        rows = [pl.ds(pl.multiple_of(n * cdim, cdim), cdim) for n in chunks]
        scores, k_ends = [], []
        for n, r in zip(chunks, rows):
            bf = bf_s[r, :]
            bb = bb_s[r, :]
            bf_last = bf[cdim - 1:cdim, :]
            bb_last = bb[0:1, :]
            q = q_ref[r, :].astype(jnp.float32) * q_scale
            k = k_ref[r, :].astype(jnp.float32)
            q_cat = jnp.concatenate([q * jnp.exp(bf), q * jnp.exp(bb)],
                                    axis=1).astype(jnp.bfloat16)
            k_inv = jnp.concatenate([k * jnp.exp(-bf), k * jnp.exp(-bb)], axis=1)
            k_ends.append(jnp.concatenate(
                [k * jnp.exp(bf_last - bf), k * jnp.exp(bb_last - bb)], axis=1).astype(jnp.bfloat16))
            dec_s[n] = jnp.concatenate([jnp.exp(bf_last), jnp.exp(bb_last)], axis=1)
            qcat_s[r, :] = q_cat
            k_bd = jnp.concatenate([jnp.where(cat_block == c, k_inv, 0.0) for c in range(4)],
                                   axis=0).astype(jnp.bfloat16)
            scores.append(lax.dot_general(q_cat, k_bd, _NT,
                                          preferred_element_type=jnp.float32))
        for n, r, k_end in zip(chunks, rows, k_ends):
            kv_t = lax.dot_general(v_ref[r, :], k_end, _TN,
                                   preferred_element_type=jnp.float32)
            kv_s[n] = jnp.where(head0_lanes, kv_t[0:GLA_DV, :], kv_t[GLA_DV:, :])
        for r, sc in zip(rows, scores):
            p = (jnp.where(keep_f, sc[:, 0:pair_dk], 0.0)
                 + jnp.where(keep_b, sc[:, pair_dk:], 0.0)).astype(jnp.bfloat16)
            vv = v_ref[r, :]
            v_bd = jnp.concatenate(
                [jnp.where(v_head == j, vv, jnp.zeros_like(vv)) for j in range(2)], axis=0)
            intra_s[r, :] = _bdot(p, v_bd)
        return carry

    lax.fori_loop(0, n_chunks // GLA_GROUP, intra_body, 0)

    def state_body(i, states):
        sf, sb = states
        nf = i
        nb = n_chunks - 1 - i
        st_s[nf, :, 0:pair_dk] = sf.astype(jnp.bfloat16)
        st_s[nb, :, pair_dk:] = sb.astype(jnp.bfloat16)
        sf = sf * dec_s[nf][:, 0:pair_dk] + kv_s[nf, :, 0:pair_dk]
        sb = sb * dec_s[nb][:, pair_dk:] + kv_s[nb, :, pair_dk:]
        return sf, sb

    zero = jnp.zeros((GLA_DV, pair_dk), jnp.float32)
    lax.fori_loop(0, n_chunks, state_body, (zero, zero))

    g = g_ref[layer:layer + 1, :]

    def out_body(i, carry):
        chunks = [i * GLA_GROUP + c for c in range(GLA_GROUP)]
        rows = [pl.ds(pl.multiple_of(n * cdim, cdim), cdim) for n in chunks]
        inter = []
        for n, r in zip(chunks, rows):
            packed = st_s[n]
            none = jnp.zeros_like(packed)
            st_bd = jnp.concatenate([jnp.where(head0_lanes, packed, none),
                                     jnp.where(head0_lanes, none, packed)], axis=0)
            inter.append(lax.dot_general(qcat_s[r, :], st_bd, _NT,
                                         preferred_element_type=jnp.float32))
        for r, o_inter in zip(rows, inter):
            o = intra_s[r, :] + o_inter
            for j in range(2):
                o_ref[r, j * GLA_DV:(j + 1) * GLA_DV] = _rms(
                    o[:, j * GLA_DV:(j + 1) * GLA_DV], g).astype(o_ref.dtype)
        return carry

    lax.fori_loop(0, n_chunks // GLA_GROUP, out_body, 0)


def _gla_call(layer, gq, gk, gv, misc, wg_cat, bg_cat, gla_norm_g, w_in_t=None):
    bsz, s, _ = gq.shape
    pairs = GLA_HEADS // 2
    n_chunks = s // GLA_CHUNK
    pair_dk, pair_dv = 2 * GLA_DK, 2 * GLA_DV
    qk_spec = pl.BlockSpec((None, s, pair_dk), lambda b, p: (b, 0, p))
    v_spec = pl.BlockSpec((None, s, pair_dv), lambda b, p: (b, 0, p))

    def gate_specs(direction):
        return (pl.BlockSpec((None, MISC_W, pair_dk), lambda b, p: (layer, 0, direction * pairs + p)),
                pl.BlockSpec((None, 1, pair_dk), lambda b, p: (layer, 0, direction * pairs + p)))

    in_specs = [qk_spec, qk_spec, v_spec,
                pl.BlockSpec((None, s, MISC_W), lambda b, p: (b, 0, 0)),
                *gate_specs(0), *gate_specs(1),
                _whole_spec(gla_norm_g)]
    operands = [gq, gk, gv, misc, wg_cat, bg_cat, wg_cat, bg_cat, gla_norm_g]
    out_specs = [v_spec]
    out_shape = [jax.ShapeDtypeStruct((bsz, s, GLA_WIDTH), jnp.bfloat16)]
    n_w_slabs = 0
    if w_in_t is not None:
        n_w_slabs = D_MODEL // W_PREP_ROWS
        assert n_w_slabs <= bsz * pairs

        def slab(b, p):
            return jnp.minimum(b * pairs + p, n_w_slabs - 1)

        in_specs.append(pl.BlockSpec((None, IN_DIM, W_PREP_ROWS),
                                     lambda b, p: (layer + 1, 0, slab(b, p))))
        operands.append(w_in_t)
        out_specs.append(pl.BlockSpec((W_PREP_ROWS, IN_DIM_P), lambda b, p: (slab(b, p), 0)))
        out_shape.append(jax.ShapeDtypeStruct((D_MODEL, IN_DIM_P), jnp.bfloat16))

    return pl.pallas_call(
        functools.partial(_gla_kernel, layer, n_w_slabs),
        grid=(bsz, pairs),
        in_specs=in_specs,
        out_specs=out_specs,
        out_shape=out_shape,
        scratch_shapes=[
            pltpu.VMEM((s, pair_dk), jnp.float32),
            pltpu.VMEM((s, pair_dk), jnp.float32),
            pltpu.VMEM((s, 2 * pair_dk), jnp.bfloat16),
            pltpu.VMEM((n_chunks, GLA_DV, 2 * pair_dk), jnp.float32),
            pltpu.VMEM((n_chunks, 1, 2 * pair_dk), jnp.float32),
            pltpu.VMEM((n_chunks, GLA_DV, 2 * pair_dk), jnp.bfloat16),
            pltpu.VMEM((s, pair_dv), jnp.float32),
        ],
        compiler_params=pltpu.CompilerParams(
            dimension_semantics=("arbitrary", "arbitrary"), vmem_limit_bytes=VMEM_LIMIT),
        name="gla",
    )(*operands)


def _mla_kernel(layer, mq_ref, mkv_ref, misc_ref, c_ref, sa_ref, sb_ref, gq_ref, gkv_ref,
                wuq_ref, wuk_ref, wuvt_ref, o_ref, ckv_s, q_s, k_s, vt_s, sc_s, top_s):
    s = mkv_ref.shape[0]
    n_tiles = s // Q_TILE
    assert n_tiles % 2 == 0 and MLA_HEADS % 2 == 0
    blocks = [slice(i * KV_ROWS, (i + 1) * KV_ROWS) for i in range(s // KV_ROWS)]
    q_mult = (MLA_NOPE + MLA_ROPE) ** -0.5 * math.log2(math.e)

    for r in blocks:
        tabs = (c_ref[r, :], sa_ref[r, :], sb_ref[r, :])
        ckv_s[r, :] = _rms(mkv_ref[r, :].astype(jnp.float32),
                           gkv_ref[layer:layer + 1, :]).astype(jnp.bfloat16)
        k_rope = _rope(misc_ref[r, :], *tabs).astype(jnp.bfloat16)
        for slot in range(2):
            k_s[slot, r, MLA_NOPE:] = k_rope
        cq = _rms(mq_ref[r, :].astype(jnp.float32),
                  gq_ref[layer:layer + 1, :]).astype(jnp.bfloat16)
        qf = _bdot(cq, wuq_ref[...])
        cos_t, sin_t = tabs[0], tabs[1] + tabs[2]
        for h in range(MLA_HEADS):
            qh = qf[:, h * QK_W:(h + 1) * QK_W]
            rot = qh[:, MLA_NOPE:]
            q_rope = rot * cos_t + pltpu.roll(rot, MLA_ROPE, axis=1) * sin_t
            q_s[h, r, :] = (jnp.concatenate([qh[:, 0:MLA_NOPE], q_rope], axis=1)
                            * q_mult).astype(jnp.bfloat16)
    for slot in range(2):
        vt_s[slot, MLA_DV:, :] = jnp.ones((vt_s.shape[1] - MLA_DV, s), jnp.bfloat16)

    def build_kv(h, slot):
        for r in blocks:
            ckv = ckv_s[r, :]
            k_s[slot, r, 0:MLA_NOPE] = _bdot(ckv, wuk_ref[h]).astype(jnp.bfloat16)
            vt_s[slot, 0:MLA_DV, r] = lax.dot_general(
                wuvt_ref[h], ckv, _NT, preferred_element_type=jnp.float32).astype(jnp.bfloat16)

    def scores_t(h, slot, j):
        q = q_s[h, j * Q_TILE:(j + 1) * Q_TILE, :]
        sc_t = lax.dot_general(k_s[slot], q, _NT, preferred_element_type=jnp.float32)
        sc_s[j % 2] = sc_t
        top_s[j % 2] = jnp.broadcast_to(jnp.max(sc_t, axis=0, keepdims=True), (F32_ROWS, Q_TILE))

    def finish(h, slot, j):
        p_t = jnp.exp2(sc_s[j % 2] - top_s[j % 2][0:1, :]).astype(jnp.bfloat16)
        o_t = _bdot(vt_s[slot], p_t)
        o_ref[h, j * Q_TILE:(j + 1) * Q_TILE, :] = (
            o_t[0:MLA_DV, :] / o_t[MLA_DV:MLA_DV + 1, :]).T.astype(o_ref.dtype)

    build_kv(0, 0)
    scores_t(0, 0, 0)

    def head_pair(i, carry):
        for slot in range(2):
            h = 2 * i + slot
            h_next = jnp.minimum(h + 1, MLA_HEADS - 1)
            for j in range(n_tiles):
                if j + 1 < n_tiles:
                    scores_t(h, slot, j + 1)
                else:
                    build_kv(h_next, 1 - slot)
                    scores_t(h_next, 1 - slot, 0)
                finish(h, slot, j)
        return carry

    lax.fori_loop(0, MLA_HEADS // 2, head_pair, 0)


def _mla_call(layer, mq, mkv, misc, tabs, gq, gkv, wuq_p, wuk_p, wuvt_p):
    bsz, s, _ = mq.shape

    def per_row(width):
        return pl.BlockSpec((None, s, width), lambda b: (b, 0, 0))

    return pl.pallas_call(
        functools.partial(_mla_kernel, layer),
        grid=(bsz,),
        in_specs=[
            per_row(MLA_Q_LORA), per_row(MLA_KV_LORA), per_row(MISC_W),
            per_row(LANES), per_row(LANES), per_row(LANES),
            _whole_spec(gq), _whole_spec(gkv),
            _layer_spec(layer, (MLA_Q_LORA, MLA_HEADS * QK_W)),
            _layer_spec(layer, (MLA_HEADS, MLA_KV_LORA, MLA_NOPE)),
            _layer_spec(layer, (MLA_HEADS, MLA_DV, MLA_KV_LORA)),
        ],
        out_specs=pl.BlockSpec((None, MLA_HEADS, s, MLA_DV), lambda b: (b, 0, 0, 0)),
        out_shape=jax.ShapeDtypeStruct((bsz, MLA_HEADS, s, MLA_DV), jnp.bfloat16),
        scratch_shapes=[
            pltpu.VMEM((s, MLA_KV_LORA), jnp.bfloat16),
            pltpu.VMEM((MLA_HEADS, s, QK_W), jnp.bfloat16),
            pltpu.VMEM((2, s, QK_W), jnp.bfloat16),
            pltpu.VMEM((2, MLA_DV + BF16_ROWS, s), jnp.bfloat16),
            pltpu.VMEM((2, s, Q_TILE), jnp.float32),
            pltpu.VMEM((2, F32_ROWS, Q_TILE), jnp.float32),
        ],
        compiler_params=pltpu.CompilerParams(
            dimension_semantics=("arbitrary",),
            vmem_limit_bytes=VMEM_LIMIT),
        name="mla",
    )(mq, mkv, misc, *tabs, gq, gkv, wuq_p, wuk_p, wuvt_p)


def _outproj_kernel(layer, tiles_per_seq, final, x_ref, mod_ref, ogla_ref, omla_ref, cb_ref, u_ref,
                    up_ref, un_ref, sz_ref, mg_ref, cw_ref, cg_ref, fg_ref, w_ref, o_ref):
    f32 = jnp.float32
    j = pl.program_id(0) % tiles_per_seq
    tm = x_ref.shape[0]

    u = u_ref[...].astype(f32)
    prev_ok = (j > 0).astype(f32)
    next_ok = (j < tiles_per_seq - 1).astype(f32)
    u_prev = up_ref[...].astype(f32)[HALO_ROWS - 1:, :] * prev_ok
    u_next = un_ref[...].astype(f32)[0:1, :] * next_ok
    rows = lax.broadcasted_iota(jnp.int32, u.shape, 0)
    up = jnp.where(rows == 0, u_prev, pltpu.roll(u, 1, axis=0))
    un = jnp.where(rows == tm - 1, u_next, pltpu.roll(u, tm - 1, axis=0))
    cw = cw_ref[...]
    conv = up * cw[0:1] + u * cw[1:2] + un * cw[2:3]
    gate = mod_ref[pl.ds(pl.program_id(0) // tiles_per_seq, 1), :][:, 2 * D_MODEL:]

    def gated_dot(r, lo, y_part):
        hi = lo + y_part.shape[1]
        y = (y_part * sz_ref[r, lo:hi].astype(f32)).astype(jnp.bfloat16)
        return _bdot(y, w_ref[lo:hi, :])

    def finish(r, acc):
        out = x_ref[r, :] + gate * acc
        if final:
            out = _rms(out, fg_ref[...])
        o_ref[r, :] = out

    pending = None
    for k in range(tm // OUT_SUB_TILE):
        r = slice(k * OUT_SUB_TILE, (k + 1) * OUT_SUB_TILE)
        acc = gated_dot(r, 0, ogla_ref[r, :].astype(f32))
        if pending is not None:
            finish(*pending)
        omla = jnp.concatenate([omla_ref[h, r, :] for h in range(MLA_HEADS)], axis=1)
        acc += gated_dot(r, GLA_WIDTH, _rms(omla.astype(f32), mg_ref[layer:layer + 1, :]))
        acc += gated_dot(r, GLA_WIDTH + MLA_WIDTH,
                         _rms(cb_ref[r, :].astype(f32) * conv[r, :], cg_ref[layer:layer + 1, :]))
        pending = (r, acc)
    finish(*pending)


def _outproj_call(layer, x2, mod, o_gla, o_mla, cb, u, sz, mla_out_g, conv_w, conv_out_g,
                  final_g, w_out_b, seq_len, final):
    t = x2.shape[0]
    tm = OUT_ROW_TILE
    tiles_per_seq = seq_len // tm
    halo_per_tile = tm // HALO_ROWS
    n_halo = t // HALO_ROWS

    def row_spec(w):
        return pl.BlockSpec((tm, w), lambda i: (i, 0))

    prev_spec = pl.BlockSpec((HALO_ROWS, CONV_CH),
                             lambda i: (jnp.maximum(i * halo_per_tile - 1, 0), 0))
    next_spec = pl.BlockSpec((HALO_ROWS, CONV_CH),
                             lambda i: (jnp.minimum((i + 1) * halo_per_tile, n_halo - 1), 0))

    return pl.pallas_call(
        functools.partial(_outproj_kernel, layer, tiles_per_seq, final),
        grid=(t // tm,),
        in_specs=[
            row_spec(D_MODEL),
            _layer_spec(layer, mod.shape[1:]),
            row_spec(GLA_WIDTH),
            pl.BlockSpec((None, MLA_HEADS, tm, MLA_DV),
                         lambda i: (i // tiles_per_seq, 0, i % tiles_per_seq, 0)),
            row_spec(CONV_CH), row_spec(CONV_CH), prev_spec, next_spec,
            row_spec(D_MODEL),
            _whole_spec(mla_out_g), _layer_spec(layer, (3, CONV_CH)), _whole_spec(conv_out_g),
            pl.BlockSpec((1, D_MODEL), lambda i: (0, 0)),
            pl.BlockSpec((D_MODEL, D_MODEL), lambda i: (0, 0), pipeline_mode=pl.Buffered(1)),
        ],
        out_specs=row_spec(D_MODEL),
        out_shape=jax.ShapeDtypeStruct((t, D_MODEL), jnp.float32),
        compiler_params=pltpu.CompilerParams(
            dimension_semantics=("arbitrary",), vmem_limit_bytes=VMEM_LIMIT),
        name="out_proj",
    )(x2, mod, o_gla, o_mla, cb, u, u, u, sz,
      mla_out_g, conv_w, conv_out_g, final_g, w_out_b)


_O_GLR = 2 * GLA_HEADS * GLA_DK + GLA_WIDTH
_O_MQ = _O_GLR + 2 * GLA_GATE_RANK
_O_MKV = _O_MQ + MLA_Q_LORA
_O_MKR = _O_MKV + MLA_KV_LORA
_O_CB = _O_MKR + MLA_ROPE
IN_DIM = _O_CB + 3 * CONV_CH + D_MODEL
W_PREP_ROWS = 256


W_PREP_CHUNK = 512


def _w_in_prep_kernel(wt_ref, o_ref):
    rows = o_ref.shape[0]
    eye = (lax.broadcasted_iota(jnp.int32, (rows, rows), 0)
           == lax.broadcasted_iota(jnp.int32, (rows, rows), 1)).astype(jnp.bfloat16)

    def put(dst, block):
        o_ref[:, dst:dst + block.shape[0]] = lax.dot_general(
            eye, block.astype(jnp.bfloat16), _NT,
            preferred_element_type=jnp.float32).astype(jnp.bfloat16)

    def put_range(dst, lo, hi):
        for start in range(lo, hi, W_PREP_CHUNK):
            stop = min(start + W_PREP_CHUNK, hi)
            put(dst + start - lo, wt_ref[start:stop, :])

    pad_rows = MISC_W - MLA_ROPE - 2 * GLA_GATE_RANK
    misc = jnp.concatenate([wt_ref[_O_MKR:_O_CB, :], wt_ref[_O_GLR:_O_MQ, :],
                            jnp.zeros((pad_rows, rows), jnp.float32)], axis=0)
    off = 0
    for src in ((0, _O_GLR), (_O_MQ, _O_MKV), misc, (_O_MKV, _O_MKR), (_O_CB, IN_DIM)):
        if isinstance(src, tuple):
            put_range(off, *src)
            off += src[1] - src[0]
        else:
            put(off, src)
            off += src.shape[0]


def _w_in_prep_call(w_in_t, layer):
    _, n, d = w_in_t.shape
    return pl.pallas_call(
        _w_in_prep_kernel,
        grid=(d // W_PREP_ROWS,),
        in_specs=[pl.BlockSpec((None, n, W_PREP_ROWS), lambda i: (layer, 0, i))],
        out_specs=pl.BlockSpec((W_PREP_ROWS, IN_DIM_P), lambda i: (i, 0)),
        out_shape=jax.ShapeDtypeStruct((d, IN_DIM_P), jnp.bfloat16),
        compiler_params=pltpu.CompilerParams(
            dimension_semantics=("arbitrary",), vmem_limit_bytes=VMEM_LIMIT),
        name="w_in_prep",
    )(w_in_t)


def _gate_params(wg_f, bg_f, wg_b, bg_b):
    depth, rank, hk = wg_f.shape
    zeros = jnp.zeros((depth, rank, hk), jnp.float32)
    rows = jnp.concatenate([jnp.concatenate([wg_f, zeros], axis=2),
                            jnp.concatenate([zeros, wg_b], axis=2)], axis=1)
    wg_cat = jnp.pad(rows, ((0, 0), (MLA_ROPE, MISC_W - MLA_ROPE - 2 * rank), (0, 0)))
    bg_cat = jnp.concatenate([bg_f, bg_b], axis=1)[:, None, :]
    return wg_cat.astype(jnp.bfloat16), bg_cat


def _permute_w_uq(w):
    depth = w.shape[0]
    half = MLA_ROPE // 2
    w4 = w.reshape(depth, MLA_Q_LORA, MLA_HEADS, MLA_NOPE + MLA_ROPE)
    nope, x1, x2 = w4[..., :MLA_NOPE], w4[..., MLA_NOPE:MLA_NOPE + half], w4[..., MLA_NOPE + half:]
    cols = jnp.concatenate([nope, x1, x2, x2, x1], axis=-1)
    return cols.reshape(depth, MLA_Q_LORA, MLA_HEADS * QK_W).astype(jnp.bfloat16)


def _split_w_ukv(w):
    depth = w.shape[0]
    w4 = w.reshape(depth, MLA_KV_LORA, MLA_HEADS, MLA_NOPE + MLA_DV)
    wk = w4[..., :MLA_NOPE].transpose(0, 2, 1, 3)
    wv_t = w4[..., MLA_NOPE:].transpose(0, 2, 3, 1)
    return wk.astype(jnp.bfloat16), wv_t.astype(jnp.bfloat16)


def kernel(x, c, positions, ada_w, ada_b, norm_g, w_in, gla_wg_f, gla_bg_f, gla_wg_b, gla_bg_b,
           gla_norm_g, mla_q_norm_g, mla_kv_norm_g, mla_w_uq, mla_w_ukv, mla_out_g, conv_w,
           conv_out_g, w_out, final_g):
    bsz, s, d = x.shape
    t = bsz * s

    mod = _ada_call(c, ada_w, ada_b)

    w_in_t = jnp.swapaxes(w_in, 1, 2)
    w_in_p = _w_in_prep_call(w_in_t, 0)
    wg_cat, bg_cat = _gate_params(gla_wg_f, gla_bg_f, gla_wg_b, gla_bg_b)
    wuq_p = _permute_w_uq(mla_w_uq)
    wuk_p, wuvt_p = _split_w_ukv(mla_w_ukv)

    def b3(a):
        return a.reshape(bsz, s, a.shape[-1])

    h = x.reshape(t, d)
    for l in range(DEPTH):
        segs, w_out_b, new_tabs = _inproj_call(l, h, mod, norm_g, w_in_p, w_out, s,
                                               positions if l == 0 else None)
        if l == 0:
            tabs = tuple(b3(tab) for tab in new_tabs)
        gq, gk, gv, mq, misc, mkv, cb, u, sz = segs
        o_gla, *next_w = _gla_call(l, b3(gq), b3(gk), b3(gv), b3(misc), wg_cat, bg_cat,
                                   gla_norm_g, w_in_t if l + 1 < DEPTH else None)
        if next_w:
            w_in_p = next_w[0]
        o_mla = _mla_call(l, b3(mq), b3(mkv), b3(misc), tabs, mla_q_norm_g, mla_kv_norm_g,
                          wuq_p, wuk_p, wuvt_p)
        h = _outproj_call(l, h, mod, o_gla.reshape(t, -1), o_mla, cb, u, sz,
                          mla_out_g, conv_w, conv_out_g, final_g.reshape(1, d),
                          w_out_b, s, l == DEPTH - 1)
    return h.reshape(bsz, s, d)
```

```python
import functools
import math

import jax
import jax.numpy as jnp
from jax import lax
from jax.experimental import pallas as pl
from jax.experimental.pallas import tpu as pltpu

D_MODEL = 2048
DEPTH = 2
GLA_HEADS = 6
GLA_DK = 64
GLA_DV = 128
GLA_GATE_RANK = 16
GLA_GATE_TEMP = 16.0
GLA_CHUNK = 64
GLA_WIDTH = GLA_HEADS * GLA_DV
MLA_HEADS = 6
MLA_Q_LORA = 384
MLA_KV_LORA = 256
MLA_NOPE = 128
MLA_ROPE = 64
MLA_DV = 128
MLA_WIDTH = MLA_HEADS * MLA_DV
ROPE_THETA = 10000.0
CONV_CH = D_MODEL - GLA_WIDTH - MLA_WIDTH
EPS = 1e-6

LANES = 128
MXU_COLS = 256
F32_ROWS = 8
BF16_ROWS = 16
HALO_ROWS = BF16_ROWS
MISC_W = LANES
QK_W = 2 * LANES

IN_SEGMENTS = (
    ("gq", GLA_HEADS * GLA_DK, jnp.bfloat16),
    ("gk", GLA_HEADS * GLA_DK, jnp.bfloat16),
    ("gv", GLA_WIDTH, jnp.bfloat16),
    ("mq", MLA_Q_LORA, jnp.bfloat16),
    ("misc", MISC_W, jnp.float32),
    ("mkv", MLA_KV_LORA, jnp.bfloat16),
    ("cb", CONV_CH, jnp.bfloat16),
    ("cc", CONV_CH, jnp.bfloat16),
    ("cx", CONV_CH, jnp.bfloat16),
    ("z", D_MODEL, jnp.bfloat16),
)
IN_DIM_P = sum(w for _, w, _ in IN_SEGMENTS)
_SEGMENT_COLS = {}
for _name, _width, _ in IN_SEGMENTS:
    _SEGMENT_COLS[_name] = (sum(w for _, w in _SEGMENT_COLS.values()), _width)
IN_OUTPUTS = tuple(seg for seg in IN_SEGMENTS if seg[0] not in ("cc", "cx", "z")) + (
    ("u", CONV_CH, jnp.bfloat16),
    ("sz", D_MODEL, jnp.bfloat16),
)

IN_ROW_TILE = 512
IN_SUB_TILE = 256
OUT_ROW_TILE = 512
OUT_SUB_TILE = 256
Q_TILE = 512
KV_ROWS = 512
SCAN_ROWS = 256
SCAN_GROUP = 4
GLA_GROUP = 8
VMEM_LIMIT = 56 * 1024 * 1024

_NT = (((1,), (1,)), ((), ()))
_TN = (((0,), (0,)), ((), ()))


def _rms(x, g):
    ms = jnp.mean(x * x, axis=-1, keepdims=True)
    return x * lax.rsqrt(ms + EPS) * g


def _bdot(a, b):
    return jnp.dot(a, b, preferred_element_type=jnp.float32)


ADA_COLS = 1024


def _split_bf16(a):
    high = a.astype(jnp.bfloat16)
    return high, (a - high.astype(jnp.float32)).astype(jnp.bfloat16)


def _ada_kernel(c_ref, w_ref, b_ref, o_ref):
    c = c_ref[...]
    act = jnp.concatenate(_split_bf16(c * jax.nn.sigmoid(c)), axis=0)
    w_high, w_rest = _split_bf16(w_ref[...])
    both = _bdot(act, w_high) + _bdot(act, w_rest)
    o_ref[...] = both[0:F32_ROWS] + both[F32_ROWS:] + b_ref[...]


def _ada_call(c, ada_w, ada_b):
    bsz = c.shape[0]
    n = ada_w.shape[-1]
    tn = ADA_COLS
    return pl.pallas_call(
        _ada_kernel,
        grid=(DEPTH, n // tn),
        in_specs=[
            pl.BlockSpec((F32_ROWS, D_MODEL), lambda l, j: (0, 0)),
            pl.BlockSpec((None, D_MODEL, tn), lambda l, j: (l, 0, j)),
            pl.BlockSpec((None, 1, tn), lambda l, j: (l, 0, j)),
        ],
        out_specs=pl.BlockSpec((None, F32_ROWS, tn), lambda l, j: (l, 0, j)),
        out_shape=jax.ShapeDtypeStruct((DEPTH, F32_ROWS, n), jnp.float32),
        compiler_params=pltpu.CompilerParams(
            dimension_semantics=("arbitrary", "arbitrary"), vmem_limit_bytes=VMEM_LIMIT),
        name="ada_mod",
    )(jnp.pad(c, ((0, F32_ROWS - bsz), (0, 0))), ada_w, ada_b.reshape(DEPTH, 1, n))


def _rope_tables(pos_ref, invf_ref, c_ref, sa_ref, sb_ref):
    rows = pos_ref.shape[0]
    lane = lax.broadcasted_iota(jnp.int32, (rows, LANES), 1)
    pos = jnp.where(lane < MLA_ROPE, pos_ref[:, 0:1], pos_ref[:, 1:2]).astype(jnp.float32)
    ang = pos * invf_ref[...]
    half = MLA_ROPE // 2
    in_first, in_second = lane < half, (lane >= half) & (lane < MLA_ROPE)
    cos2, sin2 = jnp.cos(ang), jnp.sin(ang)
    for r, shift in ((slice(0, rows), 0), (slice(rows, 2 * rows), MLA_ROPE)):
        cos = cos2 if shift == 0 else pltpu.roll(cos2, shift, axis=1)
        sin = sin2 if shift == 0 else pltpu.roll(sin2, shift, axis=1)
        c_ref[r, :] = jnp.where(lane < MLA_ROPE, cos, 0.0)
        sa_ref[r, :] = jnp.where(in_first, -sin, 0.0)
        sb_ref[r, :] = jnp.where(in_second, sin, 0.0)


def _rope_table_inputs(positions, tile_rows):
    inv_freq = ROPE_THETA ** (-jnp.arange(0, MLA_ROPE, 2, dtype=jnp.float32) / MLA_ROPE)
    invf = jnp.tile(inv_freq, LANES // inv_freq.shape[0]).reshape(1, LANES)
    pairs = positions.reshape(-1, 2, tile_rows // 2).transpose(0, 2, 1)
    return pairs, invf


def _rope(x, c, sa, sb):
    return (x * c + pltpu.roll(x, LANES - MLA_ROPE // 2, axis=1) * sa
            + pltpu.roll(x, MLA_ROPE // 2, axis=1) * sb)


def _inproj_kernel(layer, tiles_per_seq, with_rope_tables, x_ref, mod_ref, g_ref, w_ref,
                   wout_ref, *refs):
    n_out = len(IN_OUTPUTS)
    if with_rope_tables:
        pos_ref, invf_ref = refs[0:2]
        out_refs, wout_b_ref = refs[2:2 + n_out], refs[2 + n_out]
        _rope_tables(pos_ref, invf_ref, *refs[3 + n_out:6 + n_out])
    else:
        out_refs, wout_b_ref = refs[0:n_out], refs[n_out]
    h_s = refs[-1]
    wout_b_ref[...] = wout_ref[...].astype(jnp.bfloat16)
    mod = mod_ref[pl.ds(pl.program_id(0) // tiles_per_seq, 1), :]
    shift = mod[:, 0:D_MODEL]
    gain = g_ref[layer:layer + 1, :] * (1.0 + mod[:, D_MODEL:2 * D_MODEL])
    n_sub = x_ref.shape[0] // IN_SUB_TILE

    def normalize(k):
        x = x_ref[k * IN_SUB_TILE:(k + 1) * IN_SUB_TILE, :]
        ms = jnp.mean(x * x, axis=-1, keepdims=True)
        h_s[k % 2] = (x * lax.rsqrt(ms + EPS) * gain + shift).astype(jnp.bfloat16)

    outs = {name: o_ref for (name, _, _), o_ref in zip(IN_OUTPUTS, out_refs)}

    def project(k):
        r = slice(k * IN_SUB_TILE, (k + 1) * IN_SUB_TILE)

        def proj(names):
            lo = _SEGMENT_COLS[names[0]][0]
            width = sum(_SEGMENT_COLS[n][1] for n in names)
            assert width % MXU_COLS == 0
            val = _bdot(h_s[k % 2], w_ref[:, lo:lo + width])
            return {n: val[:, _SEGMENT_COLS[n][0] - lo:_SEGMENT_COLS[n][0] - lo + _SEGMENT_COLS[n][1]]
                    for n in names}

        z = proj(("z",))["z"]
        outs["sz"][r, :] = (z * jax.nn.sigmoid(z)).astype(outs["sz"].dtype)
        conv = proj(("cb", "cc", "cx"))
        outs["u"][r, :] = (conv["cc"] * conv["cx"]).astype(outs["u"].dtype)
        outs["cb"][r, :] = conv["cb"].astype(outs["cb"].dtype)
        for names in (("gq", "gk", "gv"), ("mq", "misc", "mkv")):
            for name, val in proj(names).items():
                outs[name][r, :] = val.astype(outs[name].dtype)

    normalize(0)
    for k in range(n_sub):
        project(k)
        if k + 1 < n_sub:
            normalize(k + 1)


def _layer_spec(layer, shape):
    return pl.BlockSpec((None,) + shape, lambda *_: (layer,) + (0,) * len(shape))


def _whole_spec(arr):
    return pl.BlockSpec(arr.shape, lambda *_: (0,) * arr.ndim)


def _inproj_call(layer, x2, mod, norm_g, w_in_p, w_out, seq_len, positions=None):
    t = x2.shape[0]
    tm = IN_ROW_TILE
    n_steps = t // tm
    tiles_per_seq = seq_len // tm
    d_out = w_out.shape[-1]
    wout_rows = w_out.shape[1] // n_steps
    n_out = len(IN_OUTPUTS)
    in_specs = [
        pl.BlockSpec((tm, D_MODEL), lambda i: (i, 0)),
        _layer_spec(layer, mod.shape[1:]),
        _whole_spec(norm_g),
        pl.BlockSpec((D_MODEL, IN_DIM_P), lambda i: (0, 0), pipeline_mode=pl.Buffered(1)),
        pl.BlockSpec((None, wout_rows, d_out), lambda i: (layer, i, 0)),
    ]
    operands = [x2, mod, norm_g, w_in_p, w_out]
    out_shape = [jax.ShapeDtypeStruct((t, w), dt) for _, w, dt in IN_OUTPUTS]
    out_specs = [pl.BlockSpec((tm, w), lambda i: (i, 0)) for _, w, _ in IN_OUTPUTS]
    out_shape.append(jax.ShapeDtypeStruct(w_out.shape[1:], jnp.bfloat16))
    out_specs.append(pl.BlockSpec((wout_rows, d_out), lambda i: (i, 0)))
    if positions is not None:
        pairs, invf = _rope_table_inputs(positions, tm)
        in_specs += [pl.BlockSpec((None, tm // 2, 2), lambda i: (i, 0, 0)),
                     pl.BlockSpec((1, LANES), lambda i: (0, 0))]
        operands += [pairs, invf]
        out_shape += [jax.ShapeDtypeStruct((t, LANES), jnp.float32)] * 3
        out_specs += [pl.BlockSpec((tm, LANES), lambda i: (i, 0))] * 3
    outs = pl.pallas_call(
        functools.partial(_inproj_kernel, layer, tiles_per_seq, positions is not None),
        grid=(n_steps,),
        in_specs=in_specs,
        out_specs=out_specs,
        out_shape=out_shape,
        scratch_shapes=[pltpu.VMEM((2, IN_SUB_TILE, D_MODEL), jnp.bfloat16)],
        compiler_params=pltpu.CompilerParams(
            dimension_semantics=("arbitrary",), vmem_limit_bytes=VMEM_LIMIT),
        name="in_proj",
    )(*operands)
    return outs[:n_out], outs[n_out], tuple(outs[n_out + 1:])


def _log_sigmoid(x):
    return jnp.minimum(x, 0.0) - jnp.log(1.0 + jnp.exp(-jnp.abs(x)))


def _gla_kernel(layer, n_w_slabs, q_ref, k_ref, v_ref, misc_ref, wgf_ref, bgf_ref, wgb_ref,
                bgb_ref, g_ref, *refs):
    if n_w_slabs:
        wt_ref, o_ref, w_prep_ref = refs[0:3]
        step = pl.program_id(0) * pl.num_programs(1) + pl.program_id(1)

        @pl.when(step < n_w_slabs)
        def _prep_next_layer_weight():
            _w_in_prep_kernel(wt_ref, w_prep_ref)

        refs = refs[3:]
    else:
        o_ref, refs = refs[0], refs[1:]
    bf_s, bb_s, qcat_s, kv_s, dec_s, st_s, intra_s = refs
    s = q_ref.shape[0]
    cdim = GLA_CHUNK
    n_chunks = s // cdim
    pair_dk = 2 * GLA_DK
    pair_dv = 2 * GLA_DV
    inv_t = 1.0 / GLA_GATE_TEMP
    q_scale = GLA_DK ** -0.5

    t_row = lax.broadcasted_iota(jnp.int32, (SCAN_ROWS, SCAN_ROWS), 0)
    t_col = lax.broadcasted_iota(jnp.int32, (SCAN_ROWS, SCAN_ROWS), 1)
    same_chunk = (t_row // cdim) == (t_col // cdim)
    tri_prefix = (same_chunk & (t_col <= t_row)).astype(jnp.bfloat16)
    tri_suffix = (same_chunk & (t_col >= t_row)).astype(jnp.bfloat16)

    def split(terms):
        high = terms.astype(jnp.bfloat16)
        rest = (terms - high.astype(jnp.float32)).astype(jnp.bfloat16)
        return jnp.concatenate([high, rest], axis=1)

    def scan_body(i, carry):
        blocks = [pl.ds(pl.multiple_of((i * SCAN_GROUP + c) * SCAN_ROWS, SCAN_ROWS), SCAN_ROWS)
                  for c in range(SCAN_GROUP)]
        pre = []
        for r in blocks:
            m = misc_ref[r, :].astype(jnp.bfloat16)
            pre.append((_bdot(m, wgf_ref[...]), _bdot(m, wgb_ref[...])))
        terms = [(split(_log_sigmoid(pf + bgf_ref[...]) * inv_t),
                  split(_log_sigmoid(pb + bgb_ref[...]) * inv_t)) for pf, pb in pre]
        sums = [(_bdot(tri_prefix, tf), _bdot(tri_suffix, tb)) for tf, tb in terms]
        for r, (sf, sb) in zip(blocks, sums):
            bf_s[r, :] = sf[:, 0:pair_dk] + sf[:, pair_dk:]
            bb_s[r, :] = sb[:, 0:pair_dk] + sb[:, pair_dk:]
        return carry

    lax.fori_loop(0, s // (SCAN_ROWS * SCAN_GROUP), scan_body, 0)

    row = lax.broadcasted_iota(jnp.int32, (cdim, pair_dk), 0)
    key_pos = lax.broadcasted_iota(jnp.int32, (cdim, pair_dk), 1) & (cdim - 1)
    keep_f = key_pos <= row
    keep_b = key_pos > row
    cat_block = lax.broadcasted_iota(jnp.int32, (cdim, 2 * pair_dk), 1) // GLA_DK
    v_head = lax.broadcasted_iota(jnp.int32, (cdim, pair_dv), 1) // GLA_DV
    head0_lanes = (lax.broadcasted_iota(jnp.int32, (GLA_DV, 2 * pair_dk), 1) // GLA_DK) % 2 == 0

    def intra_body(i, carry):
        chunks = [i * GLA_GROUP + c for c in range(GLA_GROUP)]
        rows = [pl.ds(pl.multiple_of(n * cdim, cdim), cdim) for n in chunks]
        scores, k_ends = [], []
        for n, r in zip(chunks, rows):
            bf = bf_s[r, :]
            bb = bb_s[r, :]
            bf_last = bf[cdim - 1:cdim, :]
            bb_last = bb[0:1, :]
            q = q_ref[r, :].astype(jnp.float32) * q_scale
            k = k_ref[r, :].astype(jnp.float32)
            q_cat = jnp.concatenate([q * jnp.exp(bf), q * jnp.exp(bb)],
                                    axis=1).astype(jnp.bfloat16)
            k_inv = jnp.concatenate([k * jnp.exp(-bf), k * jnp.exp(-bb)], axis=1)
            k_ends.append(jnp.concatenate(
                [k * jnp.exp(bf_last - bf), k * jnp.exp(bb_last - bb)], axis=1).astype(jnp.bfloat16))
            dec_s[n] = jnp.concatenate([jnp.exp(bf_last), jnp.exp(bb_last)], axis=1)
            qcat_s[r, :] = q_cat
            k_bd = jnp.concatenate([jnp.where(cat_block == c, k_inv, 0.0) for c in range(4)],
                                   axis=0).astype(jnp.bfloat16)
            scores.append(lax.dot_general(q_cat, k_bd, _NT,
                                          preferred_element_type=jnp.float32))
        for n, r, k_end in zip(chunks, rows, k_ends):
            kv_t = lax.dot_general(v_ref[r, :], k_end, _TN,
                                   preferred_element_type=jnp.float32)
            kv_s[n] = jnp.where(head0_lanes, kv_t[0:GLA_DV, :], kv_t[GLA_DV:, :])
        for r, sc in zip(rows, scores):
            p = (jnp.where(keep_f, sc[:, 0:pair_dk], 0.0)
                 + jnp.where(keep_b, sc[:, pair_dk:], 0.0)).astype(jnp.bfloat16)
            vv = v_ref[r, :]
            v_bd = jnp.concatenate(
                [jnp.where(v_head == j, vv, jnp.zeros_like(vv)) for j in range(2)], axis=0)
            intra_s[r, :] = _bdot(p, v_bd)
        return carry

    lax.fori_loop(0, n_chunks // GLA_GROUP, intra_body, 0)

    def state_body(i, states):
        sf, sb = states
        nf = i
        nb = n_chunks - 1 - i
        st_s[nf, :, 0:pair_dk] = sf.astype(jnp.bfloat16)
        st_s[nb, :, pair_dk:] = sb.astype(jnp.bfloat16)
        sf = sf * dec_s[nf][:, 0:pair_dk] + kv_s[nf, :, 0:pair_dk]
        sb = sb * dec_s[nb][:, pair_dk:] + kv_s[nb, :, pair_dk:]
        return sf, sb

    zero = jnp.zeros((GLA_DV, pair_dk), jnp.float32)
    lax.fori_loop(0, n_chunks, state_body, (zero, zero))

    g = g_ref[layer:layer + 1, :]

    def out_body(i, carry):
        chunks = [i * GLA_GROUP + c for c in range(GLA_GROUP)]
        rows = [pl.ds(pl.multiple_of(n * cdim, cdim), cdim) for n in chunks]
        inter = []
        for n, r in zip(chunks, rows):
            packed = st_s[n]
            none = jnp.zeros_like(packed)
            st_bd = jnp.concatenate([jnp.where(head0_lanes, packed, none),
                                     jnp.where(head0_lanes, none, packed)], axis=0)
            inter.append(lax.dot_general(qcat_s[r, :], st_bd, _NT,
                                         preferred_element_type=jnp.float32))
        for r, o_inter in zip(rows, inter):
            o = intra_s[r, :] + o_inter
            for j in range(2):
                o_ref[r, j * GLA_DV:(j + 1) * GLA_DV] = _rms(
                    o[:, j * GLA_DV:(j + 1) * GLA_DV], g).astype(o_ref.dtype)
        return carry

    lax.fori_loop(0, n_chunks // GLA_GROUP, out_body, 0)


def _gla_call(layer, gq, gk, gv, misc, wg_cat, bg_cat, gla_norm_g, w_in_t=None):
    bsz, s, _ = gq.shape
    pairs = GLA_HEADS // 2
    n_chunks = s // GLA_CHUNK
    pair_dk, pair_dv = 2 * GLA_DK, 2 * GLA_DV
    qk_spec = pl.BlockSpec((None, s, pair_dk), lambda b, p: (b, 0, p))
    v_spec = pl.BlockSpec((None, s, pair_dv), lambda b, p: (b, 0, p))

    def gate_specs(direction):
        return (pl.BlockSpec((None, MISC_W, pair_dk), lambda b, p: (layer, 0, direction * pairs + p)),
                pl.BlockSpec((None, 1, pair_dk), lambda b, p: (layer, 0, direction * pairs + p)))

    in_specs = [qk_spec, qk_spec, v_spec,
                pl.BlockSpec((None, s, MISC_W), lambda b, p: (b, 0, 0)),
                *gate_specs(0), *gate_specs(1),
                _whole_spec(gla_norm_g)]
    operands = [gq, gk, gv, misc, wg_cat, bg_cat, wg_cat, bg_cat, gla_norm_g]
    out_specs = [v_spec]
    out_shape = [jax.ShapeDtypeStruct((bsz, s, GLA_WIDTH), jnp.bfloat16)]
    n_w_slabs = 0
    if w_in_t is not None:
        n_w_slabs = D_MODEL // W_PREP_ROWS
        assert n_w_slabs <= bsz * pairs

        def slab(b, p):
            return jnp.minimum(b * pairs + p, n_w_slabs - 1)

        in_specs.append(pl.BlockSpec((None, IN_DIM, W_PREP_ROWS),
                                     lambda b, p: (layer + 1, 0, slab(b, p))))
        operands.append(w_in_t)
        out_specs.append(pl.BlockSpec((W_PREP_ROWS, IN_DIM_P), lambda b, p: (slab(b, p), 0)))
        out_shape.append(jax.ShapeDtypeStruct((D_MODEL, IN_DIM_P), jnp.bfloat16))

    return pl.pallas_call(
        functools.partial(_gla_kernel, layer, n_w_slabs),
        grid=(bsz, pairs),
        in_specs=in_specs,
        out_specs=out_specs,
        out_shape=out_shape,
        scratch_shapes=[
            pltpu.VMEM((s, pair_dk), jnp.float32),
            pltpu.VMEM((s, pair_dk), jnp.float32),
            pltpu.VMEM((s, 2 * pair_dk), jnp.bfloat16),
            pltpu.VMEM((n_chunks, GLA_DV, 2 * pair_dk), jnp.float32),
            pltpu.VMEM((n_chunks, 1, 2 * pair_dk), jnp.float32),
            pltpu.VMEM((n_chunks, GLA_DV, 2 * pair_dk), jnp.bfloat16),
            pltpu.VMEM((s, pair_dv), jnp.float32),
        ],
        compiler_params=pltpu.CompilerParams(
            dimension_semantics=("arbitrary", "arbitrary"), vmem_limit_bytes=VMEM_LIMIT),
        name="gla",
    )(*operands)


def _mla_kernel(layer, mq_ref, mkv_ref, misc_ref, c_ref, sa_ref, sb_ref, gq_ref, gkv_ref,
                wuq_ref, wuk_ref, wuvt_ref, o_ref, ckv_s, q_s, k_s, vt_s, sc_s, top_s):
    s = mkv_ref.shape[0]
    n_tiles = s // Q_TILE
    assert n_tiles % 2 == 0 and MLA_HEADS % 2 == 0
    blocks = [slice(i * KV_ROWS, (i + 1) * KV_ROWS) for i in range(s // KV_ROWS)]
    q_mult = (MLA_NOPE + MLA_ROPE) ** -0.5 * math.log2(math.e)

    for r in blocks:
        tabs = (c_ref[r, :], sa_ref[r, :], sb_ref[r, :])
        ckv_s[r, :] = _rms(mkv_ref[r, :].astype(jnp.float32),
                           gkv_ref[layer:layer + 1, :]).astype(jnp.bfloat16)
        k_rope = _rope(misc_ref[r, :], *tabs).astype(jnp.bfloat16)
        for slot in range(2):
            k_s[slot, r, MLA_NOPE:] = k_rope
        cq = _rms(mq_ref[r, :].astype(jnp.float32),
                  gq_ref[layer:layer + 1, :]).astype(jnp.bfloat16)
        qf = _bdot(cq, wuq_ref[...])
        cos_t, sin_t = tabs[0], tabs[1] + tabs[2]
        for h in range(MLA_HEADS):
            qh = qf[:, h * QK_W:(h + 1) * QK_W]
            rot = qh[:, MLA_NOPE:]
            q_rope = rot * cos_t + pltpu.roll(rot, MLA_ROPE, axis=1) * sin_t
            q_s[h, r, :] = (jnp.concatenate([qh[:, 0:MLA_NOPE], q_rope], axis=1)
                            * q_mult).astype(jnp.bfloat16)
    for slot in range(2):
        vt_s[slot, MLA_DV:, :] = jnp.ones((vt_s.shape[1] - MLA_DV, s), jnp.bfloat16)

    def build_kv(h, slot):
        for r in blocks:
            ckv = ckv_s[r, :]
            k_s[slot, r, 0:MLA_NOPE] = _bdot(ckv, wuk_ref[h]).astype(jnp.bfloat16)
            vt_s[slot, 0:MLA_DV, r] = lax.dot_general(
                wuvt_ref[h], ckv, _NT, preferred_element_type=jnp.float32).astype(jnp.bfloat16)

    def scores_t(h, slot, j):
        q = q_s[h, j * Q_TILE:(j + 1) * Q_TILE, :]
        sc_t = lax.dot_general(k_s[slot], q, _NT, preferred_element_type=jnp.float32)
        sc_s[j % 2] = sc_t
        top_s[j % 2] = jnp.broadcast_to(jnp.max(sc_t, axis=0, keepdims=True), (F32_ROWS, Q_TILE))

    def finish(h, slot, j):
        p_t = jnp.exp2(sc_s[j % 2] - top_s[j % 2][0:1, :]).astype(jnp.bfloat16)
        o_t = _bdot(vt_s[slot], p_t)
        o_ref[h, j * Q_TILE:(j + 1) * Q_TILE, :] = (
            o_t[0:MLA_DV, :] / o_t[MLA_DV:MLA_DV + 1, :]).T.astype(o_ref.dtype)

    build_kv(0, 0)
    scores_t(0, 0, 0)

    def head_pair(i, carry):
        for slot in range(2):
            h = 2 * i + slot
            h_next = jnp.minimum(h + 1, MLA_HEADS - 1)
            for j in range(n_tiles):
                if j + 1 < n_tiles:
                    scores_t(h, slot, j + 1)
                else:
                    build_kv(h_next, 1 - slot)
                    scores_t(h_next, 1 - slot, 0)
                finish(h, slot, j)
        return carry

    lax.fori_loop(0, MLA_HEADS // 2, head_pair, 0)


def _mla_call(layer, mq, mkv, misc, tabs, gq, gkv, wuq_p, wuk_p, wuvt_p):
    bsz, s, _ = mq.shape

    def per_row(width):
        return pl.BlockSpec((None, s, width), lambda b: (b, 0, 0))

    return pl.pallas_call(
        functools.partial(_mla_kernel, layer),
        grid=(bsz,),
        in_specs=[
            per_row(MLA_Q_LORA), per_row(MLA_KV_LORA), per_row(MISC_W),
            per_row(LANES), per_row(LANES), per_row(LANES),
            _whole_spec(gq), _whole_spec(gkv),
            _layer_spec(layer, (MLA_Q_LORA, MLA_HEADS * QK_W)),
            _layer_spec(layer, (MLA_HEADS, MLA_KV_LORA, MLA_NOPE)),
            _layer_spec(layer, (MLA_HEADS, MLA_DV, MLA_KV_LORA)),
        ],
        out_specs=pl.BlockSpec((None, MLA_HEADS, s, MLA_DV), lambda b: (b, 0, 0, 0)),
        out_shape=jax.ShapeDtypeStruct((bsz, MLA_HEADS, s, MLA_DV), jnp.bfloat16),
        scratch_shapes=[
            pltpu.VMEM((s, MLA_KV_LORA), jnp.bfloat16),
            pltpu.VMEM((MLA_HEADS, s, QK_W), jnp.bfloat16),
            pltpu.VMEM((2, s, QK_W), jnp.bfloat16),
            pltpu.VMEM((2, MLA_DV + BF16_ROWS, s), jnp.bfloat16),
            pltpu.VMEM((2, s, Q_TILE), jnp.float32),
            pltpu.VMEM((2, F32_ROWS, Q_TILE), jnp.float32),
        ],
        compiler_params=pltpu.CompilerParams(
            dimension_semantics=("arbitrary",),
            vmem_limit_bytes=VMEM_LIMIT),
        name="mla",
    )(mq, mkv, misc, *tabs, gq, gkv, wuq_p, wuk_p, wuvt_p)


def _outproj_kernel(layer, tiles_per_seq, final, x_ref, mod_ref, ogla_ref, omla_ref, cb_ref, u_ref,
                    up_ref, un_ref, sz_ref, mg_ref, cw_ref, cg_ref, fg_ref, w_ref, o_ref):
    f32 = jnp.float32
    j = pl.program_id(0) % tiles_per_seq
    tm = x_ref.shape[0]

    u = u_ref[...].astype(f32)
    prev_ok = (j > 0).astype(f32)
    next_ok = (j < tiles_per_seq - 1).astype(f32)
    u_prev = up_ref[...].astype(f32)[HALO_ROWS - 1:, :] * prev_ok
    u_next = un_ref[...].astype(f32)[0:1, :] * next_ok
    rows = lax.broadcasted_iota(jnp.int32, u.shape, 0)
    up = jnp.where(rows == 0, u_prev, pltpu.roll(u, 1, axis=0))
    un = jnp.where(rows == tm - 1, u_next, pltpu.roll(u, tm - 1, axis=0))
    cw = cw_ref[...]
    conv = up * cw[0:1] + u * cw[1:2] + un * cw[2:3]
    gate = mod_ref[pl.ds(pl.program_id(0) // tiles_per_seq, 1), :][:, 2 * D_MODEL:]

    def gated_dot(r, lo, y_part):
        hi = lo + y_part.shape[1]
        y = (y_part * sz_ref[r, lo:hi].astype(f32)).astype(jnp.bfloat16)
        return _bdot(y, w_ref[lo:hi, :])

    def finish(r, acc):
        out = x_ref[r, :] + gate * acc
        if final:
            out = _rms(out, fg_ref[...])
        o_ref[r, :] = out

    pending = None
    for k in range(tm // OUT_SUB_TILE):
        r = slice(k * OUT_SUB_TILE, (k + 1) * OUT_SUB_TILE)
        acc = gated_dot(r, 0, ogla_ref[r, :].astype(f32))
        if pending is not None:
            finish(*pending)
        omla = jnp.concatenate([omla_ref[h, r, :] for h in range(MLA_HEADS)], axis=1)
        acc += gated_dot(r, GLA_WIDTH, _rms(omla.astype(f32), mg_ref[layer:layer + 1, :]))
        acc += gated_dot(r, GLA_WIDTH + MLA_WIDTH,
                         _rms(cb_ref[r, :].astype(f32) * conv[r, :], cg_ref[layer:layer + 1, :]))
        pending = (r, acc)
    finish(*pending)


def _outproj_call(layer, x2, mod, o_gla, o_mla, cb, u, sz, mla_out_g, conv_w, conv_out_g,
                  final_g, w_out_b, seq_len, final):
    t = x2.shape[0]
    tm = OUT_ROW_TILE
    tiles_per_seq = seq_len // tm
    halo_per_tile = tm // HALO_ROWS
    n_halo = t // HALO_ROWS

    def row_spec(w):
        return pl.BlockSpec((tm, w), lambda i: (i, 0))

    prev_spec = pl.BlockSpec((HALO_ROWS, CONV_CH),
                             lambda i: (jnp.maximum(i * halo_per_tile - 1, 0), 0))
    next_spec = pl.BlockSpec((HALO_ROWS, CONV_CH),
                             lambda i: (jnp.minimum((i + 1) * halo_per_tile, n_halo - 1), 0))

    return pl.pallas_call(
        functools.partial(_outproj_kernel, layer, tiles_per_seq, final),
        grid=(t // tm,),
        in_specs=[
            row_spec(D_MODEL),
            _layer_spec(layer, mod.shape[1:]),
            row_spec(GLA_WIDTH),
            pl.BlockSpec((None, MLA_HEADS, tm, MLA_DV),
                         lambda i: (i // tiles_per_seq, 0, i % tiles_per_seq, 0)),
            row_spec(CONV_CH), row_spec(CONV_CH), prev_spec, next_spec,
            row_spec(D_MODEL),
            _whole_spec(mla_out_g), _layer_spec(layer, (3, CONV_CH)), _whole_spec(conv_out_g),
            pl.BlockSpec((1, D_MODEL), lambda i: (0, 0)),
            pl.BlockSpec((D_MODEL, D_MODEL), lambda i: (0, 0), pipeline_mode=pl.Buffered(1)),
        ],
        out_specs=row_spec(D_MODEL),
        out_shape=jax.ShapeDtypeStruct((t, D_MODEL), jnp.float32),
        compiler_params=pltpu.CompilerParams(
            dimension_semantics=("arbitrary",), vmem_limit_bytes=VMEM_LIMIT),
        name="out_proj",
    )(x2, mod, o_gla, o_mla, cb, u, u, u, sz,
      mla_out_g, conv_w, conv_out_g, final_g, w_out_b)


_O_GLR = 2 * GLA_HEADS * GLA_DK + GLA_WIDTH
_O_MQ = _O_GLR + 2 * GLA_GATE_RANK
_O_MKV = _O_MQ + MLA_Q_LORA
_O_MKR = _O_MKV + MLA_KV_LORA
_O_CB = _O_MKR + MLA_ROPE
IN_DIM = _O_CB + 3 * CONV_CH + D_MODEL
W_PREP_ROWS = 256


W_PREP_CHUNK = 512


def _w_in_prep_kernel(wt_ref, o_ref):
    rows = o_ref.shape[0]
    eye = (lax.broadcasted_iota(jnp.int32, (rows, rows), 0)
           == lax.broadcasted_iota(jnp.int32, (rows, rows), 1)).astype(jnp.bfloat16)

    def put(dst, block):
        o_ref[:, dst:dst + block.shape[0]] = lax.dot_general(
            eye, block.astype(jnp.bfloat16), _NT,
            preferred_element_type=jnp.float32).astype(jnp.bfloat16)

    def put_range(dst, lo, hi):
        for start in range(lo, hi, W_PREP_CHUNK):
            stop = min(start + W_PREP_CHUNK, hi)
            put(dst + start - lo, wt_ref[start:stop, :])

    pad_rows = MISC_W - MLA_ROPE - 2 * GLA_GATE_RANK
    misc = jnp.concatenate([wt_ref[_O_MKR:_O_CB, :], wt_ref[_O_GLR:_O_MQ, :],
                            jnp.zeros((pad_rows, rows), jnp.float32)], axis=0)
    off = 0
    for src in ((0, _O_GLR), (_O_MQ, _O_MKV), misc, (_O_MKV, _O_MKR), (_O_CB, IN_DIM)):
        if isinstance(src, tuple):
            put_range(off, *src)
            off += src[1] - src[0]
        else:
            put(off, src)
            off += src.shape[0]


def _w_in_prep_call(w_in_t, layer):
    _, n, d = w_in_t.shape
    return pl.pallas_call(
        _w_in_prep_kernel,
        grid=(d // W_PREP_ROWS,),
        in_specs=[pl.BlockSpec((None, n, W_PREP_ROWS), lambda i: (layer, 0, i))],
        out_specs=pl.BlockSpec((W_PREP_ROWS, IN_DIM_P), lambda i: (i, 0)),
        out_shape=jax.ShapeDtypeStruct((d, IN_DIM_P), jnp.bfloat16),
        compiler_params=pltpu.CompilerParams(
            dimension_semantics=("arbitrary",), vmem_limit_bytes=VMEM_LIMIT),
        name="w_in_prep",
    )(w_in_t)


def _gate_params(wg_f, bg_f, wg_b, bg_b):
    depth, rank, hk = wg_f.shape
    zeros = jnp.zeros((depth, rank, hk), jnp.float32)
    rows = jnp.concatenate([jnp.concatenate([wg_f, zeros], axis=2),
                            jnp.concatenate([zeros, wg_b], axis=2)], axis=1)
    wg_cat = jnp.pad(rows, ((0, 0), (MLA_ROPE, MISC_W - MLA_ROPE - 2 * rank), (0, 0)))
    bg_cat = jnp.concatenate([bg_f, bg_b], axis=1)[:, None, :]
    return wg_cat.astype(jnp.bfloat16), bg_cat


def _permute_w_uq(w):
    depth = w.shape[0]
    half = MLA_ROPE // 2
    w4 = w.reshape(depth, MLA_Q_LORA, MLA_HEADS, MLA_NOPE + MLA_ROPE)
    nope, x1, x2 = w4[..., :MLA_NOPE], w4[..., MLA_NOPE:MLA_NOPE + half], w4[..., MLA_NOPE + half:]
    cols = jnp.concatenate([nope, x1, x2, x2, x1], axis=-1)
    return cols.reshape(depth, MLA_Q_LORA, MLA_HEADS * QK_W).astype(jnp.bfloat16)


def _split_w_ukv(w):
    depth = w.shape[0]
    w4 = w.reshape(depth, MLA_KV_LORA, MLA_HEADS, MLA_NOPE + MLA_DV)
    wk = w4[..., :MLA_NOPE].transpose(0, 2, 1, 3)
    wv_t = w4[..., MLA_NOPE:].transpose(0, 2, 3, 1)
    return wk.astype(jnp.bfloat16), wv_t.astype(jnp.bfloat16)


def kernel(x, c, positions, ada_w, ada_b, norm_g, w_in, gla_wg_f, gla_bg_f, gla_wg_b, gla_bg_b,
           gla_norm_g, mla_q_norm_g, mla_kv_norm_g, mla_w_uq, mla_w_ukv, mla_out_g, conv_w,
           conv_out_g, w_out, final_g):
    bsz, s, d = x.shape
    t = bsz * s

    mod = _ada_call(c, ada_w, ada_b)

    w_in_t = jnp.swapaxes(w_in, 1, 2)
    w_in_p = _w_in_prep_call(w_in_t, 0)
    wg_cat, bg_cat = _gate_params(gla_wg_f, gla_bg_f, gla_wg_b, gla_bg_b)
    wuq_p = _permute_w_uq(mla_w_uq)
    wuk_p, wuvt_p = _split_w_ukv(mla_w_ukv)

    def b3(a):
        return a.reshape(bsz, s, a.shape[-1])

    h = x.reshape(t, d)
    for l in range(DEPTH):
        segs, w_out_b, new_tabs = _inproj_call(l, h, mod, norm_g, w_in_p, w_out, s,
                                               positions if l == 0 else None)
        if l == 0:
            tabs = tuple(b3(tab) for tab in new_tabs)
        gq, gk, gv, mq, misc, mkv, cb, u, sz = segs
        o_gla, *next_w = _gla_call(l, b3(gq), b3(gk), b3(gv), b3(misc), wg_cat, bg_cat,
                                   gla_norm_g, w_in_t if l + 1 < DEPTH else None)
        if next_w:
            w_in_p = next_w[0]
        o_mla = _mla_call(l, b3(mq), b3(mkv), b3(misc), tabs, mla_q_norm_g, mla_kv_norm_g,
                          wuq_p, wuk_p, wuvt_p)
        h = _outproj_call(l, h, mod, o_gla.reshape(t, -1), o_mla, cb, u, sz,
                          mla_out_g, conv_w, conv_out_g, final_g.reshape(1, d),
                          w_out_b, s, l == DEPTH - 1)
    return h.reshape(bsz, s, d)
```

```python
import functools
import math

import jax
import jax.numpy as jnp
from jax import lax
from jax.experimental import pallas as pl
from jax.experimental.pallas import tpu as pltpu

D_MODEL = 2048
DEPTH = 2
GLA_HEADS = 6
GLA_DK = 64
GLA_DV = 128
GLA_GATE_RANK = 16
GLA_GATE_TEMP = 16.0
GLA_CHUNK = 64
GLA_WIDTH = GLA_HEADS * GLA_DV
MLA_HEADS = 6
MLA_Q_LORA = 384
MLA_KV_LORA = 256
MLA_NOPE = 128
MLA_ROPE = 64
MLA_DV = 128
MLA_WIDTH = MLA_HEADS * MLA_DV
ROPE_THETA = 10000.0
CONV_CH = D_MODEL - GLA_WIDTH - MLA_WIDTH
EPS = 1e-6

LANES = 128
MXU_COLS = 256
F32_ROWS = 8
BF16_ROWS = 16
HALO_ROWS = BF16_ROWS
MISC_W = LANES
QK_W = 2 * LANES

IN_SEGMENTS = (
    ("gq", GLA_HEADS * GLA_DK, jnp.bfloat16),
    ("gk", GLA_HEADS * GLA_DK, jnp.bfloat16),
    ("gv", GLA_WIDTH, jnp.bfloat16),
    ("mq", MLA_Q_LORA, jnp.bfloat16),
    ("misc", MISC_W, jnp.float32),
    ("mkv", MLA_KV_LORA, jnp.bfloat16),
    ("cb", CONV_CH, jnp.bfloat16),
    ("cc", CONV_CH, jnp.bfloat16),
    ("cx", CONV_CH, jnp.bfloat16),
    ("z", D_MODEL, jnp.bfloat16),
)
IN_DIM_P = sum(w for _, w, _ in IN_SEGMENTS)
_SEGMENT_COLS = {}
for _name, _width, _ in IN_SEGMENTS:
    _SEGMENT_COLS[_name] = (sum(w for _, w in _SEGMENT_COLS.values()), _width)
IN_OUTPUTS = tuple(seg for seg in IN_SEGMENTS if seg[0] not in ("cc", "cx", "z")) + (
    ("u", CONV_CH, jnp.bfloat16),
    ("sz", D_MODEL, jnp.bfloat16),
)

IN_ROW_TILE = 512
IN_SUB_TILE = 256
OUT_ROW_TILE = 512
OUT_SUB_TILE = 256
Q_TILE = 512
KV_ROWS = 512
SCAN_ROWS = 256
SCAN_GROUP = 4
GLA_GROUP = 8
VMEM_LIMIT = 56 * 1024 * 1024

_NT = (((1,), (1,)), ((), ()))
_TN = (((0,), (0,)), ((), ()))


def _rms(x, g):
    ms = jnp.mean(x * x, axis=-1, keepdims=True)
    return x * lax.rsqrt(ms + EPS) * g


def _bdot(a, b):
    return jnp.dot(a, b, preferred_element_type=jnp.float32)


ADA_COLS = 1024


def _split_bf16(a):
    high = a.astype(jnp.bfloat16)
    return high, (a - high.astype(jnp.float32)).astype(jnp.bfloat16)


def _ada_kernel(c_ref, w_ref, b_ref, o_ref):
    c = c_ref[...]
    act = jnp.concatenate(_split_bf16(c * jax.nn.sigmoid(c)), axis=0)
    w_high, w_rest = _split_bf16(w_ref[...])
    both = _bdot(act, w_high) + _bdot(act, w_rest)
    o_ref[...] = both[0:F32_ROWS] + both[F32_ROWS:] + b_ref[...]


def _ada_specs(layer, tn):
    in_specs = [pl.BlockSpec((F32_ROWS, D_MODEL), lambda j: (0, 0)),
                pl.BlockSpec((None, D_MODEL, tn), lambda j: (layer, 0, j)),
                pl.BlockSpec((None, 1, tn), lambda j: (layer, 0, j))]
    return in_specs, pl.BlockSpec((F32_ROWS, tn), lambda j: (0, j))


def _ada_call(layer, c_pad, ada_w, ada_b):
    n = ada_w.shape[-1]
    in_specs, out_spec = _ada_specs(layer, ADA_COLS)
    return pl.pallas_call(
        _ada_kernel,
        grid=(n // ADA_COLS,),
        in_specs=in_specs,
        out_specs=out_spec,
        out_shape=jax.ShapeDtypeStruct((F32_ROWS, n), jnp.float32),
        compiler_params=pltpu.CompilerParams(
            dimension_semantics=("arbitrary",), vmem_limit_bytes=VMEM_LIMIT),
        name="ada_mod",
    )(c_pad, ada_w, ada_b)


def _rope_tables(pos_ref, invf_ref, c_ref, sa_ref, sb_ref):
    rows = pos_ref.shape[0]
    lane = lax.broadcasted_iota(jnp.int32, (rows, LANES), 1)
    pos = jnp.where(lane < MLA_ROPE, pos_ref[:, 0:1], pos_ref[:, 1:2]).astype(jnp.float32)
    ang = pos * invf_ref[...]
    half = MLA_ROPE // 2
    in_first, in_second = lane < half, (lane >= half) & (lane < MLA_ROPE)
    cos2, sin2 = jnp.cos(ang), jnp.sin(ang)
    for r, shift in ((slice(0, rows), 0), (slice(rows, 2 * rows), MLA_ROPE)):
        cos = cos2 if shift == 0 else pltpu.roll(cos2, shift, axis=1)
        sin = sin2 if shift == 0 else pltpu.roll(sin2, shift, axis=1)
        c_ref[r, :] = jnp.where(lane < MLA_ROPE, cos, 0.0)
        sa_ref[r, :] = jnp.where(in_first, -sin, 0.0)
        sb_ref[r, :] = jnp.where(in_second, sin, 0.0)


def _rope_table_inputs(positions, tile_rows):
    inv_freq = ROPE_THETA ** (-jnp.arange(0, MLA_ROPE, 2, dtype=jnp.float32) / MLA_ROPE)
    invf = jnp.tile(inv_freq, LANES // inv_freq.shape[0]).reshape(1, LANES)
    pairs = positions.reshape(-1, 2, tile_rows // 2).transpose(0, 2, 1)
    return pairs, invf


def _rope(x, c, sa, sb):
    return (x * c + pltpu.roll(x, LANES - MLA_ROPE // 2, axis=1) * sa
            + pltpu.roll(x, MLA_ROPE // 2, axis=1) * sb)


def _inproj_kernel(layer, tiles_per_seq, with_rope_tables, x_ref, mod_ref, g_ref, w_ref,
                   wout_ref, *refs):
    n_out = len(IN_OUTPUTS)
    if with_rope_tables:
        pos_ref, invf_ref = refs[0:2]
        out_refs, wout_b_ref = refs[2:2 + n_out], refs[2 + n_out]
        _rope_tables(pos_ref, invf_ref, *refs[3 + n_out:6 + n_out])
    else:
        out_refs, wout_b_ref = refs[0:n_out], refs[n_out]
    h_s = refs[-1]
    wout_b_ref[...] = wout_ref[...].astype(jnp.bfloat16)
    mod = mod_ref[pl.ds(pl.program_id(0) // tiles_per_seq, 1), :]
    shift = mod[:, 0:D_MODEL]
    gain = g_ref[layer:layer + 1, :] * (1.0 + mod[:, D_MODEL:2 * D_MODEL])
    n_sub = x_ref.shape[0] // IN_SUB_TILE

    def normalize(k):
        x = x_ref[k * IN_SUB_TILE:(k + 1) * IN_SUB_TILE, :]
        ms = jnp.mean(x * x, axis=-1, keepdims=True)
        h_s[k % 2] = (x * lax.rsqrt(ms + EPS) * gain + shift).astype(jnp.bfloat16)

    outs = {name: o_ref for (name, _, _), o_ref in zip(IN_OUTPUTS, out_refs)}

    def project(k):
        r = slice(k * IN_SUB_TILE, (k + 1) * IN_SUB_TILE)

        def proj(names):
            lo = _SEGMENT_COLS[names[0]][0]
            width = sum(_SEGMENT_COLS[n][1] for n in names)
            assert width % MXU_COLS == 0
            val = _bdot(h_s[k % 2], w_ref[:, lo:lo + width])
            return {n: val[:, _SEGMENT_COLS[n][0] - lo:_SEGMENT_COLS[n][0] - lo + _SEGMENT_COLS[n][1]]
                    for n in names}

        z = proj(("z",))["z"]
        outs["sz"][r, :] = (z * jax.nn.sigmoid(z)).astype(outs["sz"].dtype)
        conv = proj(("cb", "cc", "cx"))
        outs["u"][r, :] = (conv["cc"] * conv["cx"]).astype(outs["u"].dtype)
        outs["cb"][r, :] = conv["cb"].astype(outs["cb"].dtype)
        for names in (("gq", "gk", "gv"), ("mq", "misc", "mkv")):
            for name, val in proj(names).items():
                outs[name][r, :] = val.astype(outs[name].dtype)

    normalize(0)
    for k in range(n_sub):
        project(k)
        if k + 1 < n_sub:
            normalize(k + 1)


def _layer_spec(layer, shape):
    return pl.BlockSpec((None,) + shape, lambda *_: (layer,) + (0,) * len(shape))


def _whole_spec(arr):
    return pl.BlockSpec(arr.shape, lambda *_: (0,) * arr.ndim)


def _inproj_call(layer, x2, mod, norm_g, w_in_p, w_out, seq_len, positions=None):
    t = x2.shape[0]
    tm = IN_ROW_TILE
    n_steps = t // tm
    tiles_per_seq = seq_len // tm
    d_out = w_out.shape[-1]
    wout_rows = w_out.shape[1] // n_steps
    n_out = len(IN_OUTPUTS)
    in_specs = [
        pl.BlockSpec((tm, D_MODEL), lambda i: (i, 0)),
        _whole_spec(mod),
        _whole_spec(norm_g),
        pl.BlockSpec((D_MODEL, IN_DIM_P), lambda i: (0, 0), pipeline_mode=pl.Buffered(1)),
        pl.BlockSpec((None, wout_rows, d_out), lambda i: (layer, i, 0)),
    ]
    operands = [x2, mod, norm_g, w_in_p, w_out]
    out_shape = [jax.ShapeDtypeStruct((t, w), dt) for _, w, dt in IN_OUTPUTS]
    out_specs = [pl.BlockSpec((tm, w), lambda i: (i, 0)) for _, w, _ in IN_OUTPUTS]
    out_shape.append(jax.ShapeDtypeStruct(w_out.shape[1:], jnp.bfloat16))
    out_specs.append(pl.BlockSpec((wout_rows, d_out), lambda i: (i, 0)))
    if positions is not None:
        pairs, invf = _rope_table_inputs(positions, tm)
        in_specs += [pl.BlockSpec((None, tm // 2, 2), lambda i: (i, 0, 0)),
                     pl.BlockSpec((1, LANES), lambda i: (0, 0))]
        operands += [pairs, invf]
        out_shape += [jax.ShapeDtypeStruct((t, LANES), jnp.float32)] * 3
        out_specs += [pl.BlockSpec((tm, LANES), lambda i: (i, 0))] * 3
    outs = pl.pallas_call(
        functools.partial(_inproj_kernel, layer, tiles_per_seq, positions is not None),
        grid=(n_steps,),
        in_specs=in_specs,
        out_specs=out_specs,
        out_shape=out_shape,
        scratch_shapes=[pltpu.VMEM((2, IN_SUB_TILE, D_MODEL), jnp.bfloat16)],
        compiler_params=pltpu.CompilerParams(
            dimension_semantics=("arbitrary",), vmem_limit_bytes=VMEM_LIMIT),
        name="in_proj",
    )(*operands)
    return outs[:n_out], outs[n_out], tuple(outs[n_out + 1:])


def _log_sigmoid(x):
    return jnp.minimum(x, 0.0) - jnp.log(1.0 + jnp.exp(-jnp.abs(x)))


def _gla_kernel(layer, n_w_slabs, q_ref, k_ref, v_ref, misc_ref, wgf_ref, bgf_ref, wgb_ref,
                bgb_ref, g_ref, *refs):
    if n_w_slabs:
        wt_ref, o_ref, w_prep_ref = refs[0:3]
        step = pl.program_id(0) * pl.num_programs(1) + pl.program_id(1)

        @pl.when(step < n_w_slabs)
        def _prep_next_layer_weight():
            _w_in_prep_kernel(wt_ref, w_prep_ref)

        refs = refs[3:]
    else:
        o_ref, refs = refs[0], refs[1:]
    bf_s, bb_s, qcat_s, kv_s, dec_s, st_s, intra_s = refs
    s = q_ref.shape[0]
    cdim = GLA_CHUNK
    n_chunks = s // cdim
    pair_dk = 2 * GLA_DK
    pair_dv = 2 * GLA_DV
    inv_t = 1.0 / GLA_GATE_TEMP
    q_scale = GLA_DK ** -0.5

    t_row = lax.broadcasted_iota(jnp.int32, (SCAN_ROWS, SCAN_ROWS), 0)
    t_col = lax.broadcasted_iota(jnp.int32, (SCAN_ROWS, SCAN_ROWS), 1)
    same_chunk = (t_row // cdim) == (t_col // cdim)
    tri_prefix = (same_chunk & (t_col <= t_row)).astype(jnp.bfloat16)
    tri_suffix = (same_chunk & (t_col >= t_row)).astype(jnp.bfloat16)

    def split(terms):
        high = terms.astype(jnp.bfloat16)
        rest = (terms - high.astype(jnp.float32)).astype(jnp.bfloat16)
        return jnp.concatenate([high, rest], axis=1)

    def scan_body(i, carry):
        blocks = [pl.ds(pl.multiple_of((i * SCAN_GROUP + c) * SCAN_ROWS, SCAN_ROWS), SCAN_ROWS)
                  for c in range(SCAN_GROUP)]
        pre = []
        for r in blocks:
            m = misc_ref[r, :].astype(jnp.bfloat16)
            pre.append((_bdot(m, wgf_ref[...]), _bdot(m, wgb_ref[...])))
        terms = [(split(_log_sigmoid(pf + bgf_ref[...]) * inv_t),
                  split(_log_sigmoid(pb + bgb_ref[...]) * inv_t)) for pf, pb in pre]
        sums = [(_bdot(tri_prefix, tf), _bdot(tri_suffix, tb)) for tf, tb in terms]
        for r, (sf, sb) in zip(blocks, sums):
            bf_s[r, :] = sf[:, 0:pair_dk] + sf[:, pair_dk:]
            bb_s[r, :] = sb[:, 0:pair_dk] + sb[:, pair_dk:]
        return carry

    lax.fori_loop(0, s // (SCAN_ROWS * SCAN_GROUP), scan_body, 0)

    row = lax.broadcasted_iota(jnp.int32, (cdim, pair_dk), 0)
    key_pos = lax.broadcasted_iota(jnp.int32, (cdim, pair_dk), 1) & (cdim - 1)
    keep_f = key_pos <= row
    keep_b = key_pos > row
    cat_block = lax.broadcasted_iota(jnp.int32, (cdim, 2 * pair_dk), 1) // GLA_DK
    v_head = lax.broadcasted_iota(jnp.int32, (cdim, pair_dv), 1) // GLA_DV
    head0_lanes = (lax.broadcasted_iota(jnp.int32, (GLA_DV, 2 * pair_dk), 1) // GLA_DK) % 2 == 0

    def intra_body(i, carry):
        chunks = [i * GLA_GROUP + c for c in range(GLA_GROUP)]
        rows = [pl.ds(pl.multiple_of(n * cdim, cdim), cdim) for n in chunks]
        scores, k_ends = [], []
        for n, r in zip(chunks, rows):
            bf = bf_s[r, :]
            bb = bb_s[r, :]
            bf_last = bf[cdim - 1:cdim, :]
            bb_last = bb[0:1, :]
            q = q_ref[r, :].astype(jnp.float32) * q_scale
            k = k_ref[r, :].astype(jnp.float32)
            q_cat = jnp.concatenate([q * jnp.exp(bf), q * jnp.exp(bb)],
                                    axis=1).astype(jnp.bfloat16)
            k_inv = jnp.concatenate([k * jnp.exp(-bf), k * jnp.exp(-bb)], axis=1)
            k_ends.append(jnp.concatenate(
                [k * jnp.exp(bf_last - bf), k * jnp.exp(bb_last - bb)], axis=1).astype(jnp.bfloat16))
            dec_s[n] = jnp.concatenate([jnp.exp(bf_last), jnp.exp(bb_last)], axis=1)
            qcat_s[r, :] = q_cat
            k_bd = jnp.concatenate([jnp.where(cat_block == c, k_inv, 0.0) for c in range(4)],
                                   axis=0).astype(jnp.bfloat16)
            scores.append(lax.dot_general(q_cat, k_bd, _NT,
                                          preferred_element_type=jnp.float32))
        for n, r, k_end in zip(chunks, rows, k_ends):
            kv_t = lax.dot_general(v_ref[r, :], k_end, _TN,
                                   preferred_element_type=jnp.float32)
            kv_s[n] = jnp.where(head0_lanes, kv_t[0:GLA_DV, :], kv_t[GLA_DV:, :])
        for r, sc in zip(rows, scores):
            p = (jnp.where(keep_f, sc[:, 0:pair_dk], 0.0)
                 + jnp.where(keep_b, sc[:, pair_dk:], 0.0)).astype(jnp.bfloat16)
            vv = v_ref[r, :]
            v_bd = jnp.concatenate(
                [jnp.where(v_head == j, vv, jnp.zeros_like(vv)) for j in range(2)], axis=0)
            intra_s[r, :] = _bdot(p, v_bd)
        return carry

    lax.fori_loop(0, n_chunks // GLA_GROUP, intra_body, 0)

    def state_body(i, states):
        sf, sb = states
        nf = i
        nb = n_chunks - 1 - i
        st_s[nf, :, 0:pair_dk] = sf.astype(jnp.bfloat16)
        st_s[nb, :, pair_dk:] = sb.astype(jnp.bfloat16)
        sf = sf * dec_s[nf][:, 0:pair_dk] + kv_s[nf, :, 0:pair_dk]
        sb = sb * dec_s[nb][:, pair_dk:] + kv_s[nb, :, pair_dk:]
        return sf, sb

    zero = jnp.zeros((GLA_DV, pair_dk), jnp.float32)
    lax.fori_loop(0, n_chunks, state_body, (zero, zero))

    g = g_ref[layer:layer + 1, :]

    def out_body(i, carry):
        chunks = [i * GLA_GROUP + c for c in range(GLA_GROUP)]
        rows = [pl.ds(pl.multiple_of(n * cdim, cdim), cdim) for n in chunks]
        inter = []
        for n, r in zip(chunks, rows):
            packed = st_s[n]
            none = jnp.zeros_like(packed)
            st_bd = jnp.concatenate([jnp.where(head0_lanes, packed, none),
                                     jnp.where(head0_lanes, none, packed)], axis=0)
            inter.append(lax.dot_general(qcat_s[r, :], st_bd, _NT,
                                         preferred_element_type=jnp.float32))
        for r, o_inter in zip(rows, inter):
            o = intra_s[r, :] + o_inter
            for j in range(2):
                o_ref[r, j * GLA_DV:(j + 1) * GLA_DV] = _rms(
                    o[:, j * GLA_DV:(j + 1) * GLA_DV], g).astype(o_ref.dtype)
        return carry

    lax.fori_loop(0, n_chunks // GLA_GROUP, out_body, 0)


def _gla_call(layer, gq, gk, gv, misc, wg_cat, bg_cat, gla_norm_g, w_in_t=None):
    bsz, s, _ = gq.shape
    pairs = GLA_HEADS // 2
    n_chunks = s // GLA_CHUNK
    pair_dk, pair_dv = 2 * GLA_DK, 2 * GLA_DV
    qk_spec = pl.BlockSpec((None, s, pair_dk), lambda b, p: (b, 0, p))
    v_spec = pl.BlockSpec((None, s, pair_dv), lambda b, p: (b, 0, p))

    def gate_specs(direction):
        return (pl.BlockSpec((None, MISC_W, pair_dk), lambda b, p: (layer, 0, direction * pairs + p)),
                pl.BlockSpec((None, 1, pair_dk), lambda b, p: (layer, 0, direction * pairs + p)))

    in_specs = [qk_spec, qk_spec, v_spec,
                pl.BlockSpec((None, s, MISC_W), lambda b, p: (b, 0, 0)),
                *gate_specs(0), *gate_specs(1),
                _whole_spec(gla_norm_g)]
    operands = [gq, gk, gv, misc, wg_cat, bg_cat, wg_cat, bg_cat, gla_norm_g]
    out_specs = [v_spec]
    out_shape = [jax.ShapeDtypeStruct((bsz, s, GLA_WIDTH), jnp.bfloat16)]
    n_w_slabs = 0
    if w_in_t is not None:
        n_w_slabs = D_MODEL // W_PREP_ROWS
        assert n_w_slabs <= bsz * pairs

        def slab(b, p):
            return jnp.minimum(b * pairs + p, n_w_slabs - 1)

        in_specs.append(pl.BlockSpec((None, IN_DIM, W_PREP_ROWS),
                                     lambda b, p: (layer + 1, 0, slab(b, p))))
        operands.append(w_in_t)
        out_specs.append(pl.BlockSpec((W_PREP_ROWS, IN_DIM_P), lambda b, p: (slab(b, p), 0)))
        out_shape.append(jax.ShapeDtypeStruct((D_MODEL, IN_DIM_P), jnp.bfloat16))

    return pl.pallas_call(
        functools.partial(_gla_kernel, layer, n_w_slabs),
        grid=(bsz, pairs),
        in_specs=in_specs,
        out_specs=out_specs,
        out_shape=out_shape,
        scratch_shapes=[
            pltpu.VMEM((s, pair_dk), jnp.float32),
            pltpu.VMEM((s, pair_dk), jnp.float32),
            pltpu.VMEM((s, 2 * pair_dk), jnp.bfloat16),
            pltpu.VMEM((n_chunks, GLA_DV, 2 * pair_dk), jnp.float32),
            pltpu.VMEM((n_chunks, 1, 2 * pair_dk), jnp.float32),
            pltpu.VMEM((n_chunks, GLA_DV, 2 * pair_dk), jnp.bfloat16),
            pltpu.VMEM((s, pair_dv), jnp.float32),
        ],
        compiler_params=pltpu.CompilerParams(
            dimension_semantics=("arbitrary", "arbitrary"), vmem_limit_bytes=VMEM_LIMIT),
        name="gla",
    )(*operands)


def _mla_kernel(layer, mq_ref, mkv_ref, misc_ref, c_ref, sa_ref, sb_ref, gq_ref, gkv_ref,
                wuq_ref, wuk_ref, wuvt_ref, o_ref, ckv_s, q_s, k_s, vt_s, sc_s, top_s):
    s = mkv_ref.shape[0]
    n_tiles = s // Q_TILE
    assert n_tiles % 2 == 0 and MLA_HEADS % 2 == 0
    blocks = [slice(i * KV_ROWS, (i + 1) * KV_ROWS) for i in range(s // KV_ROWS)]
    q_mult = (MLA_NOPE + MLA_ROPE) ** -0.5 * math.log2(math.e)

    for r in blocks:
        tabs = (c_ref[r, :], sa_ref[r, :], sb_ref[r, :])
        ckv_s[r, :] = _rms(mkv_ref[r, :].astype(jnp.float32),
                           gkv_ref[layer:layer + 1, :]).astype(jnp.bfloat16)
        k_rope = _rope(misc_ref[r, :], *tabs).astype(jnp.bfloat16)
        for slot in range(2):
            k_s[slot, r, MLA_NOPE:] = k_rope
        cq = _rms(mq_ref[r, :].astype(jnp.float32),
                  gq_ref[layer:layer + 1, :]).astype(jnp.bfloat16)
        qf = _bdot(cq, wuq_ref[...])
        cos_t, sin_t = tabs[0], tabs[1] + tabs[2]
        for h in range(MLA_HEADS):
            qh = qf[:, h * QK_W:(h + 1) * QK_W]
            rot = qh[:, MLA_NOPE:]
            q_rope = rot * cos_t + pltpu.roll(rot, MLA_ROPE, axis=1) * sin_t
            q_s[h, r, :] = (jnp.concatenate([qh[:, 0:MLA_NOPE], q_rope], axis=1)
                            * q_mult).astype(jnp.bfloat16)
    for slot in range(2):
        vt_s[slot, MLA_DV:, :] = jnp.ones((vt_s.shape[1] - MLA_DV, s), jnp.bfloat16)

    def build_kv(h, slot):
        for r in blocks:
            ckv = ckv_s[r, :]
            k_s[slot, r, 0:MLA_NOPE] = _bdot(ckv, wuk_ref[h]).astype(jnp.bfloat16)
            vt_s[slot, 0:MLA_DV, r] = lax.dot_general(
                wuvt_ref[h], ckv, _NT, preferred_element_type=jnp.float32).astype(jnp.bfloat16)

    def scores_t(h, slot, j):
        q = q_s[h, j * Q_TILE:(j + 1) * Q_TILE, :]
        sc_t = lax.dot_general(k_s[slot], q, _NT, preferred_element_type=jnp.float32)
        sc_s[j % 2] = sc_t
        top_s[j % 2] = jnp.broadcast_to(jnp.max(sc_t, axis=0, keepdims=True), (F32_ROWS, Q_TILE))

    def finish(h, slot, j):
        p_t = jnp.exp2(sc_s[j % 2] - top_s[j % 2][0:1, :]).astype(jnp.bfloat16)
        o_t = _bdot(vt_s[slot], p_t)
        o_ref[h, j * Q_TILE:(j + 1) * Q_TILE, :] = (
            o_t[0:MLA_DV, :] / o_t[MLA_DV:MLA_DV + 1, :]).T.astype(o_ref.dtype)

    build_kv(0, 0)
    scores_t(0, 0, 0)

    def head_pair(i, carry):
        for slot in range(2):
            h = 2 * i + slot
            h_next = jnp.minimum(h + 1, MLA_HEADS - 1)
            for j in range(n_tiles):
                if j + 1 < n_tiles:
                    scores_t(h, slot, j + 1)
                else:
                    build_kv(h_next, 1 - slot)
                    scores_t(h_next, 1 - slot, 0)
                finish(h, slot, j)
        return carry

    lax.fori_loop(0, MLA_HEADS // 2, head_pair, 0)


def _mla_call(layer, mq, mkv, misc, tabs, gq, gkv, wuq_p, wuk_p, wuvt_p):
    bsz, s, _ = mq.shape

    def per_row(width):
        return pl.BlockSpec((None, s, width), lambda b: (b, 0, 0))

    return pl.pallas_call(
        functools.partial(_mla_kernel, layer),
        grid=(bsz,),
        in_specs=[
            per_row(MLA_Q_LORA), per_row(MLA_KV_LORA), per_row(MISC_W),
            per_row(LANES), per_row(LANES), per_row(LANES),
            _whole_spec(gq), _whole_spec(gkv),
            _layer_spec(layer, (MLA_Q_LORA, MLA_HEADS * QK_W)),
            _layer_spec(layer, (MLA_HEADS, MLA_KV_LORA, MLA_NOPE)),
            _layer_spec(layer, (MLA_HEADS, MLA_DV, MLA_KV_LORA)),
        ],
        out_specs=pl.BlockSpec((None, MLA_HEADS, s, MLA_DV), lambda b: (b, 0, 0, 0)),
        out_shape=jax.ShapeDtypeStruct((bsz, MLA_HEADS, s, MLA_DV), jnp.bfloat16),
        scratch_shapes=[
            pltpu.VMEM((s, MLA_KV_LORA), jnp.bfloat16),
            pltpu.VMEM((MLA_HEADS, s, QK_W), jnp.bfloat16),
            pltpu.VMEM((2, s, QK_W), jnp.bfloat16),
            pltpu.VMEM((2, MLA_DV + BF16_ROWS, s), jnp.bfloat16),
            pltpu.VMEM((2, s, Q_TILE), jnp.float32),
            pltpu.VMEM((2, F32_ROWS, Q_TILE), jnp.float32),
        ],
        compiler_params=pltpu.CompilerParams(
            dimension_semantics=("arbitrary",),
            vmem_limit_bytes=VMEM_LIMIT),
        name="mla",
    )(mq, mkv, misc, *tabs, gq, gkv, wuq_p, wuk_p, wuvt_p)


def _outproj_kernel(layer, tiles_per_seq, final, with_next_mod, x_ref, mod_ref, ogla_ref, omla_ref,
                    cb_ref, u_ref, up_ref, un_ref, sz_ref, mg_ref, cw_ref, cg_ref, fg_ref, w_ref,
                    *refs):
    if with_next_mod:
        c_ref, ada_w_ref, ada_b_ref, o_ref, next_mod_ref = refs
        _ada_kernel(c_ref, ada_w_ref, ada_b_ref, next_mod_ref)
    else:
        (o_ref,) = refs
    f32 = jnp.float32
    j = pl.program_id(0) % tiles_per_seq
    tm = x_ref.shape[0]

    u = u_ref[...].astype(f32)
    prev_ok = (j > 0).astype(f32)
    next_ok = (j < tiles_per_seq - 1).astype(f32)
    u_prev = up_ref[...].astype(f32)[HALO_ROWS - 1:, :] * prev_ok
    u_next = un_ref[...].astype(f32)[0:1, :] * next_ok
    rows = lax.broadcasted_iota(jnp.int32, u.shape, 0)
    up = jnp.where(rows == 0, u_prev, pltpu.roll(u, 1, axis=0))
    un = jnp.where(rows == tm - 1, u_next, pltpu.roll(u, tm - 1, axis=0))
    cw = cw_ref[...]
    conv = up * cw[0:1] + u * cw[1:2] + un * cw[2:3]
    gate = mod_ref[pl.ds(pl.program_id(0) // tiles_per_seq, 1), :][:, 2 * D_MODEL:]

    def gated_dot(r, lo, y_part):
        hi = lo + y_part.shape[1]
        y = (y_part * sz_ref[r, lo:hi].astype(f32)).astype(jnp.bfloat16)
        return _bdot(y, w_ref[lo:hi, :])

    def finish(r, acc):
        out = x_ref[r, :] + gate * acc
        if final:
            out = _rms(out, fg_ref[...])
        o_ref[r, :] = out

    pending = None
    for k in range(tm // OUT_SUB_TILE):
        r = slice(k * OUT_SUB_TILE, (k + 1) * OUT_SUB_TILE)
        acc = gated_dot(r, 0, ogla_ref[r, :].astype(f32))
        if pending is not None:
            finish(*pending)
        omla = jnp.concatenate([omla_ref[h, r, :] for h in range(MLA_HEADS)], axis=1)
        acc += gated_dot(r, GLA_WIDTH, _rms(omla.astype(f32), mg_ref[layer:layer + 1, :]))
        acc += gated_dot(r, GLA_WIDTH + MLA_WIDTH,
                         _rms(cb_ref[r, :].astype(f32) * conv[r, :], cg_ref[layer:layer + 1, :]))
        pending = (r, acc)
    finish(*pending)


def _outproj_call(layer, x2, mod, o_gla, o_mla, cb, u, sz, mla_out_g, conv_w, conv_out_g,
                  final_g, w_out_b, seq_len, final, next_ada=None):
    t = x2.shape[0]
    tm = OUT_ROW_TILE
    n_steps = t // tm
    tiles_per_seq = seq_len // tm
    halo_per_tile = tm // HALO_ROWS
    n_halo = t // HALO_ROWS

    def row_spec(w):
        return pl.BlockSpec((tm, w), lambda i: (i, 0))

    prev_spec = pl.BlockSpec((HALO_ROWS, CONV_CH),
                             lambda i: (jnp.maximum(i * halo_per_tile - 1, 0), 0))
    next_spec = pl.BlockSpec((HALO_ROWS, CONV_CH),
                             lambda i: (jnp.minimum((i + 1) * halo_per_tile, n_halo - 1), 0))

    in_specs = [
        row_spec(D_MODEL),
        _whole_spec(mod),
        row_spec(GLA_WIDTH),
        pl.BlockSpec((None, MLA_HEADS, tm, MLA_DV),
                     lambda i: (i // tiles_per_seq, 0, i % tiles_per_seq, 0)),
        row_spec(CONV_CH), row_spec(CONV_CH), prev_spec, next_spec,
        row_spec(D_MODEL),
        _whole_spec(mla_out_g), _layer_spec(layer, (3, CONV_CH)), _whole_spec(conv_out_g),
        pl.BlockSpec((1, D_MODEL), lambda i: (0, 0)),
        pl.BlockSpec((D_MODEL, D_MODEL), lambda i: (0, 0), pipeline_mode=pl.Buffered(1)),
    ]
    operands = [x2, mod, o_gla, o_mla, cb, u, u, u, sz,
                mla_out_g, conv_w, conv_out_g, final_g, w_out_b]
    out_specs = [row_spec(D_MODEL)]
    out_shape = [jax.ShapeDtypeStruct((t, D_MODEL), jnp.float32)]
    if next_ada is not None:
        n_mod = next_ada[1].shape[-1]
        ada_in, ada_out = _ada_specs(layer + 1, n_mod // n_steps)
        in_specs += ada_in
        operands += list(next_ada)
        out_specs.append(ada_out)
        out_shape.append(jax.ShapeDtypeStruct((F32_ROWS, n_mod), jnp.float32))
    return pl.pallas_call(
        functools.partial(_outproj_kernel, layer, tiles_per_seq, final, next_ada is not None),
        grid=(n_steps,),
        in_specs=in_specs,
        out_specs=out_specs,
        out_shape=out_shape,
        compiler_params=pltpu.CompilerParams(
            dimension_semantics=("arbitrary",), vmem_limit_bytes=VMEM_LIMIT),
        name="out_proj",
    )(*operands)


_O_GLR = 2 * GLA_HEADS * GLA_DK + GLA_WIDTH
_O_MQ = _O_GLR + 2 * GLA_GATE_RANK
_O_MKV = _O_MQ + MLA_Q_LORA
_O_MKR = _O_MKV + MLA_KV_LORA
_O_CB = _O_MKR + MLA_ROPE
IN_DIM = _O_CB + 3 * CONV_CH + D_MODEL
W_PREP_ROWS = 256


W_PREP_CHUNK = 512


def _w_in_prep_kernel(wt_ref, o_ref):
    rows = o_ref.shape[0]
    eye = (lax.broadcasted_iota(jnp.int32, (rows, rows), 0)
           == lax.broadcasted_iota(jnp.int32, (rows, rows), 1)).astype(jnp.bfloat16)

    def put(dst, block):
        o_ref[:, dst:dst + block.shape[0]] = lax.dot_general(
            eye, block.astype(jnp.bfloat16), _NT,
            preferred_element_type=jnp.float32).astype(jnp.bfloat16)

    def put_range(dst, lo, hi):
        for start in range(lo, hi, W_PREP_CHUNK):
            stop = min(start + W_PREP_CHUNK, hi)
            put(dst + start - lo, wt_ref[start:stop, :])

    pad_rows = MISC_W - MLA_ROPE - 2 * GLA_GATE_RANK
    misc = jnp.concatenate([wt_ref[_O_MKR:_O_CB, :], wt_ref[_O_GLR:_O_MQ, :],
                            jnp.zeros((pad_rows, rows), jnp.float32)], axis=0)
    off = 0
    for src in ((0, _O_GLR), (_O_MQ, _O_MKV), misc, (_O_MKV, _O_MKR), (_O_CB, IN_DIM)):
        if isinstance(src, tuple):
            put_range(off, *src)
            off += src[1] - src[0]
        else:
            put(off, src)
            off += src.shape[0]


def _w_in_prep_call(w_in_t, layer):
    _, n, d = w_in_t.shape
    return pl.pallas_call(
        _w_in_prep_kernel,
        grid=(d // W_PREP_ROWS,),
        in_specs=[pl.BlockSpec((None, n, W_PREP_ROWS), lambda i: (layer, 0, i))],
        out_specs=pl.BlockSpec((W_PREP_ROWS, IN_DIM_P), lambda i: (i, 0)),
        out_shape=jax.ShapeDtypeStruct((d, IN_DIM_P), jnp.bfloat16),
        compiler_params=pltpu.CompilerParams(
            dimension_semantics=("arbitrary",), vmem_limit_bytes=VMEM_LIMIT),
        name="w_in_prep",
    )(w_in_t)


def _gate_params(wg_f, bg_f, wg_b, bg_b):
    depth, rank, hk = wg_f.shape
    zeros = jnp.zeros((depth, rank, hk), jnp.float32)
    rows = jnp.concatenate([jnp.concatenate([wg_f, zeros], axis=2),
                            jnp.concatenate([zeros, wg_b], axis=2)], axis=1)
    wg_cat = jnp.pad(rows, ((0, 0), (MLA_ROPE, MISC_W - MLA_ROPE - 2 * rank), (0, 0)))
    bg_cat = jnp.concatenate([bg_f, bg_b], axis=1)[:, None, :]
    return wg_cat.astype(jnp.bfloat16), bg_cat


def _permute_w_uq(w):
    depth = w.shape[0]
    half = MLA_ROPE // 2
    w4 = w.reshape(depth, MLA_Q_LORA, MLA_HEADS, MLA_NOPE + MLA_ROPE)
    nope, x1, x2 = w4[..., :MLA_NOPE], w4[..., MLA_NOPE:MLA_NOPE + half], w4[..., MLA_NOPE + half:]
    cols = jnp.concatenate([nope, x1, x2, x2, x1], axis=-1)
    return cols.reshape(depth, MLA_Q_LORA, MLA_HEADS * QK_W).astype(jnp.bfloat16)


def _split_w_ukv(w):
    depth = w.shape[0]
    w4 = w.reshape(depth, MLA_KV_LORA, MLA_HEADS, MLA_NOPE + MLA_DV)
    wk = w4[..., :MLA_NOPE].transpose(0, 2, 1, 3)
    wv_t = w4[..., MLA_NOPE:].transpose(0, 2, 3, 1)
    return wk.astype(jnp.bfloat16), wv_t.astype(jnp.bfloat16)


def kernel(x, c, positions, ada_w, ada_b, norm_g, w_in, gla_wg_f, gla_bg_f, gla_wg_b, gla_bg_b,
           gla_norm_g, mla_q_norm_g, mla_kv_norm_g, mla_w_uq, mla_w_ukv, mla_out_g, conv_w,
           conv_out_g, w_out, final_g):
    bsz, s, d = x.shape
    t = bsz * s

    ada = (jnp.pad(c, ((0, F32_ROWS - bsz), (0, 0))), ada_w, ada_b[:, None, :])
    mod = _ada_call(0, *ada)

    w_in_t = jnp.swapaxes(w_in, 1, 2)
    w_in_p = _w_in_prep_call(w_in_t, 0)
    wg_cat, bg_cat = _gate_params(gla_wg_f, gla_bg_f, gla_wg_b, gla_bg_b)
    wuq_p = _permute_w_uq(mla_w_uq)
    wuk_p, wuvt_p = _split_w_ukv(mla_w_ukv)

    def b3(a):
        return a.reshape(bsz, s, a.shape[-1])

    h = x.reshape(t, d)
    for l in range(DEPTH):
        segs, w_out_b, new_tabs = _inproj_call(l, h, mod, norm_g, w_in_p, w_out, s,
                                               positions if l == 0 else None)
        if l == 0:
            tabs = tuple(b3(tab) for tab in new_tabs)
        gq, gk, gv, mq, misc, mkv, cb, u, sz = segs
        o_gla, *next_w = _gla_call(l, b3(gq), b3(gk), b3(gv), b3(misc), wg_cat, bg_cat,
                                   gla_norm_g, w_in_t if l + 1 < DEPTH else None)
        if next_w:
            w_in_p = next_w[0]
        o_mla = _mla_call(l, b3(mq), b3(mkv), b3(misc), tabs, mla_q_norm_g, mla_kv_norm_g,
                          wuq_p, wuk_p, wuvt_p)
        h, *next_mod = _outproj_call(l, h, mod, o_gla.reshape(t, -1), o_mla, cb, u, sz,
                                     mla_out_g, conv_w, conv_out_g, final_g.reshape(1, d),
                                     w_out_b, s, l == DEPTH - 1,
                                     ada if l + 1 < DEPTH else None)
        if next_mod:
            mod = next_mod[0]
    return h.reshape(bsz, s, d)
```

```python
import functools
import math

import jax
import jax.numpy as jnp
from jax import lax
from jax.experimental import pallas as pl
from jax.experimental.pallas import tpu as pltpu

D_MODEL = 2048
DEPTH = 2
GLA_HEADS = 6
GLA_DK = 64
GLA_DV = 128
GLA_GATE_RANK = 16
GLA_GATE_TEMP = 16.0
GLA_CHUNK = 64
GLA_WIDTH = GLA_HEADS * GLA_DV
MLA_HEADS = 6
MLA_Q_LORA = 384
MLA_KV_LORA = 256
MLA_NOPE = 128
MLA_ROPE = 64
MLA_DV = 128
MLA_WIDTH = MLA_HEADS * MLA_DV
ROPE_THETA = 10000.0
CONV_CH = D_MODEL - GLA_WIDTH - MLA_WIDTH
EPS = 1e-6

LANES = 128
MXU_COLS = 256
F32_ROWS = 8
BF16_ROWS = 16
HALO_ROWS = BF16_ROWS
MISC_W = LANES
QK_W = 2 * LANES

IN_SEGMENTS = (
    ("gq", GLA_HEADS * GLA_DK, jnp.bfloat16),
    ("gk", GLA_HEADS * GLA_DK, jnp.bfloat16),
    ("gv", GLA_WIDTH, jnp.bfloat16),
    ("mq", MLA_Q_LORA, jnp.bfloat16),
    ("misc", MISC_W, jnp.float32),
    ("mkv", MLA_KV_LORA, jnp.bfloat16),
    ("cb", CONV_CH, jnp.bfloat16),
    ("cc", CONV_CH, jnp.bfloat16),
    ("cx", CONV_CH, jnp.bfloat16),
    ("z", D_MODEL, jnp.bfloat16),
)
IN_DIM_P = sum(w for _, w, _ in IN_SEGMENTS)
_SEGMENT_COLS = {}
for _name, _width, _ in IN_SEGMENTS:
    _SEGMENT_COLS[_name] = (sum(w for _, w in _SEGMENT_COLS.values()), _width)
IN_OUTPUTS = tuple(seg for seg in IN_SEGMENTS if seg[0] not in ("cc", "cx", "z")) + (
    ("u", CONV_CH, jnp.bfloat16),
    ("sz", D_MODEL, jnp.bfloat16),
)

IN_ROW_TILE = 512
IN_SUB_TILE = 256
OUT_ROW_TILE = 512
OUT_SUB_TILE = 256
Q_TILE = 512
KV_ROWS = 512
SCAN_ROWS = 256
SCAN_GROUP = 4
GLA_GROUP = 8
VMEM_LIMIT = 56 * 1024 * 1024

_NT = (((1,), (1,)), ((), ()))
_TN = (((0,), (0,)), ((), ()))


def _rms(x, g):
    ms = jnp.mean(x * x, axis=-1, keepdims=True)
    return x * lax.rsqrt(ms + EPS) * g


def _bdot(a, b):
    return jnp.dot(a, b, preferred_element_type=jnp.float32)


ADA_COLS = 1024


def _split_bf16(a):
    high = a.astype(jnp.bfloat16)
    return high, (a - high.astype(jnp.float32)).astype(jnp.bfloat16)


def _ada_kernel(c_ref, w_ref, b_ref, o_ref):
    c = c_ref[...]
    act = jnp.concatenate(_split_bf16(c * jax.nn.sigmoid(c)), axis=0)
    w_high, w_rest = _split_bf16(w_ref[...])
    both = _bdot(act, w_high) + _bdot(act, w_rest)
    o_ref[...] = both[0:F32_ROWS] + both[F32_ROWS:] + b_ref[...]


def _ada_specs(layer, tn):
    in_specs = [pl.BlockSpec((F32_ROWS, D_MODEL), lambda j: (0, 0)),
                pl.BlockSpec((None, D_MODEL, tn), lambda j: (layer, 0, j)),
                pl.BlockSpec((None, 1, tn), lambda j: (layer, 0, j))]
    return in_specs, pl.BlockSpec((F32_ROWS, tn), lambda j: (0, j))


def _ada_call(layer, c_pad, ada_w, ada_b):
    n = ada_w.shape[-1]
    in_specs, out_spec = _ada_specs(layer, ADA_COLS)
    return pl.pallas_call(
        _ada_kernel,
        grid=(n // ADA_COLS,),
        in_specs=in_specs,
        out_specs=out_spec,
        out_shape=jax.ShapeDtypeStruct((F32_ROWS, n), jnp.float32),
        compiler_params=pltpu.CompilerParams(
            dimension_semantics=("arbitrary",), vmem_limit_bytes=VMEM_LIMIT),
        name="ada_mod",
    )(c_pad, ada_w, ada_b)


def _rope_tables(pos_ref, invf_ref, c_ref, sa_ref, sb_ref):
    rows = pos_ref.shape[0]
    lane = lax.broadcasted_iota(jnp.int32, (rows, LANES), 1)
    pos = jnp.where(lane < MLA_ROPE, pos_ref[:, 0:1], pos_ref[:, 1:2]).astype(jnp.float32)
    ang = pos * invf_ref[...]
    half = MLA_ROPE // 2
    in_first, in_second = lane < half, (lane >= half) & (lane < MLA_ROPE)
    cos2, sin2 = jnp.cos(ang), jnp.sin(ang)
    for r, shift in ((slice(0, rows), 0), (slice(rows, 2 * rows), MLA_ROPE)):
        cos = cos2 if shift == 0 else pltpu.roll(cos2, shift, axis=1)
        sin = sin2 if shift == 0 else pltpu.roll(sin2, shift, axis=1)
        c_ref[r, :] = jnp.where(lane < MLA_ROPE, cos, 0.0)
        sa_ref[r, :] = jnp.where(in_first, -sin, 0.0)
        sb_ref[r, :] = jnp.where(in_second, sin, 0.0)


def _rope_table_inputs(positions, tile_rows):
    inv_freq = ROPE_THETA ** (-jnp.arange(0, MLA_ROPE, 2, dtype=jnp.float32) / MLA_ROPE)
    invf = jnp.tile(inv_freq, LANES // inv_freq.shape[0]).reshape(1, LANES)
    pairs = positions.reshape(-1, 2, tile_rows // 2).transpose(0, 2, 1)
    return pairs, invf


def _rope(x, c, sa, sb):
    return (x * c + pltpu.roll(x, LANES - MLA_ROPE // 2, axis=1) * sa
            + pltpu.roll(x, MLA_ROPE // 2, axis=1) * sb)


def _inproj_kernel(layer, tiles_per_seq, x_ref, mod_ref, g_ref, w_ref, wout_ref, *refs):
    out_refs, wout_b_ref, h_s = refs[:-2], refs[-2], refs[-1]
    wout_b_ref[...] = wout_ref[...].astype(jnp.bfloat16)
    mod = mod_ref[pl.ds(pl.program_id(0) // tiles_per_seq, 1), :]
    shift = mod[:, 0:D_MODEL]
    gain = g_ref[layer:layer + 1, :] * (1.0 + mod[:, D_MODEL:2 * D_MODEL])
    n_sub = x_ref.shape[0] // IN_SUB_TILE

    def normalize(k):
        x = x_ref[k * IN_SUB_TILE:(k + 1) * IN_SUB_TILE, :]
        ms = jnp.mean(x * x, axis=-1, keepdims=True)
        h_s[k % 2] = (x * lax.rsqrt(ms + EPS) * gain + shift).astype(jnp.bfloat16)

    outs = {name: o_ref for (name, _, _), o_ref in zip(IN_OUTPUTS, out_refs)}

    def project(k):
        r = slice(k * IN_SUB_TILE, (k + 1) * IN_SUB_TILE)

        def proj(names):
            lo = _SEGMENT_COLS[names[0]][0]
            width = sum(_SEGMENT_COLS[n][1] for n in names)
            assert width % MXU_COLS == 0
            val = _bdot(h_s[k % 2], w_ref[:, lo:lo + width])
            return {n: val[:, _SEGMENT_COLS[n][0] - lo:_SEGMENT_COLS[n][0] - lo + _SEGMENT_COLS[n][1]]
                    for n in names}

        z = proj(("z",))["z"]
        outs["sz"][r, :] = (z * jax.nn.sigmoid(z)).astype(outs["sz"].dtype)
        conv = proj(("cb", "cc", "cx"))
        outs["u"][r, :] = (conv["cc"] * conv["cx"]).astype(outs["u"].dtype)
        outs["cb"][r, :] = conv["cb"].astype(outs["cb"].dtype)
        for names in (("gq", "gk", "gv"), ("mq", "misc", "mkv")):
            for name, val in proj(names).items():
                outs[name][r, :] = val.astype(outs[name].dtype)

    normalize(0)
    for k in range(n_sub):
        project(k)
        if k + 1 < n_sub:
            normalize(k + 1)


def _layer_spec(layer, shape):
    return pl.BlockSpec((None,) + shape, lambda *_: (layer,) + (0,) * len(shape))


def _whole_spec(arr):
    return pl.BlockSpec(arr.shape, lambda *_: (0,) * arr.ndim)


def _inproj_call(layer, x2, mod, norm_g, w_in_p, w_out, seq_len):
    t = x2.shape[0]
    tm = IN_ROW_TILE
    n_steps = t // tm
    tiles_per_seq = seq_len // tm
    d_out = w_out.shape[-1]
    wout_rows = w_out.shape[1] // n_steps
    n_out = len(IN_OUTPUTS)
    in_specs = [
        pl.BlockSpec((tm, D_MODEL), lambda i: (i, 0)),
        _whole_spec(mod),
        _whole_spec(norm_g),
        pl.BlockSpec((D_MODEL, IN_DIM_P), lambda i: (0, 0), pipeline_mode=pl.Buffered(1)),
        pl.BlockSpec((None, wout_rows, d_out), lambda i: (layer, i, 0)),
    ]
    operands = [x2, mod, norm_g, w_in_p, w_out]
    out_shape = [jax.ShapeDtypeStruct((t, w), dt) for _, w, dt in IN_OUTPUTS]
    out_specs = [pl.BlockSpec((tm, w), lambda i: (i, 0)) for _, w, _ in IN_OUTPUTS]
    out_shape.append(jax.ShapeDtypeStruct(w_out.shape[1:], jnp.bfloat16))
    out_specs.append(pl.BlockSpec((wout_rows, d_out), lambda i: (i, 0)))
    outs = pl.pallas_call(
        functools.partial(_inproj_kernel, layer, tiles_per_seq),
        grid=(n_steps,),
        in_specs=in_specs,
        out_specs=out_specs,
        out_shape=out_shape,
        scratch_shapes=[pltpu.VMEM((2, IN_SUB_TILE, D_MODEL), jnp.bfloat16)],
        compiler_params=pltpu.CompilerParams(
            dimension_semantics=("arbitrary",), vmem_limit_bytes=VMEM_LIMIT),
        name="in_proj",
    )(*operands)
    return outs[:n_out], outs[n_out]


def _log_sigmoid(x):
    return jnp.minimum(x, 0.0) - jnp.log(1.0 + jnp.exp(-jnp.abs(x)))


def _gla_kernel(layer, n_w_slabs, q_ref, k_ref, v_ref, misc_ref, wgf_ref, bgf_ref, wgb_ref,
                bgb_ref, g_ref, *refs):
    if n_w_slabs:
        wt_ref, o_ref, w_prep_ref = refs[0:3]
        step = pl.program_id(0) * pl.num_programs(1) + pl.program_id(1)

        @pl.when(step < n_w_slabs)
        def _prep_next_layer_weight():
            _w_in_prep_kernel(wt_ref, w_prep_ref)

        refs = refs[3:]
    else:
        o_ref, refs = refs[0], refs[1:]
    bf_s, bb_s, qcat_s, kv_s, dec_s, st_s, intra_s = refs
    s = q_ref.shape[0]
    cdim = GLA_CHUNK
    n_chunks = s // cdim
    pair_dk = 2 * GLA_DK
    pair_dv = 2 * GLA_DV
    inv_t = 1.0 / GLA_GATE_TEMP
    q_scale = GLA_DK ** -0.5

    t_row = lax.broadcasted_iota(jnp.int32, (SCAN_ROWS, SCAN_ROWS), 0)
    t_col = lax.broadcasted_iota(jnp.int32, (SCAN_ROWS, SCAN_ROWS), 1)
    same_chunk = (t_row // cdim) == (t_col // cdim)
    tri_prefix = (same_chunk & (t_col <= t_row)).astype(jnp.bfloat16)
    tri_suffix = (same_chunk & (t_col >= t_row)).astype(jnp.bfloat16)

    def split(terms):
        high = terms.astype(jnp.bfloat16)
        rest = (terms - high.astype(jnp.float32)).astype(jnp.bfloat16)
        return jnp.concatenate([high, rest], axis=1)

    def scan_body(i, carry):
        blocks = [pl.ds(pl.multiple_of((i * SCAN_GROUP + c) * SCAN_ROWS, SCAN_ROWS), SCAN_ROWS)
                  for c in range(SCAN_GROUP)]
        pre = []
        for r in blocks:
            m = misc_ref[r, :].astype(jnp.bfloat16)
            pre.append((_bdot(m, wgf_ref[...]), _bdot(m, wgb_ref[...])))
        terms = [(split(_log_sigmoid(pf + bgf_ref[...]) * inv_t),
                  split(_log_sigmoid(pb + bgb_ref[...]) * inv_t)) for pf, pb in pre]
        sums = [(_bdot(tri_prefix, tf), _bdot(tri_suffix, tb)) for tf, tb in terms]
        for r, (sf, sb) in zip(blocks, sums):
            bf_s[r, :] = sf[:, 0:pair_dk] + sf[:, pair_dk:]
            bb_s[r, :] = sb[:, 0:pair_dk] + sb[:, pair_dk:]
        return carry

    lax.fori_loop(0, s // (SCAN_ROWS * SCAN_GROUP), scan_body, 0)

    row = lax.broadcasted_iota(jnp.int32, (cdim, pair_dk), 0)
    key_pos = lax.broadcasted_iota(jnp.int32, (cdim, pair_dk), 1) & (cdim - 1)
    keep_f = key_pos <= row
    keep_b = key_pos > row
    cat_block = lax.broadcasted_iota(jnp.int32, (cdim, 2 * pair_dk), 1) // GLA_DK
    v_head = lax.broadcasted_iota(jnp.int32, (cdim, pair_dv), 1) // GLA_DV
    head0_lanes = (lax.broadcasted_iota(jnp.int32, (GLA_DV, 2 * pair_dk), 1) // GLA_DK) % 2 == 0

    def intra_body(i, carry):
        chunks = [i * GLA_GROUP + c for c in range(GLA_GROUP)]
        rows = [pl.ds(pl.multiple_of(n * cdim, cdim), cdim) for n in chunks]
        scores, k_ends = [], []
        for n, r in zip(chunks, rows):
            bf = bf_s[r, :]
            bb = bb_s[r, :]
            bf_last = bf[cdim - 1:cdim, :]
            bb_last = bb[0:1, :]
            q = q_ref[r, :].astype(jnp.float32) * q_scale
            k = k_ref[r, :].astype(jnp.float32)
            q_cat = jnp.concatenate([q * jnp.exp(bf), q * jnp.exp(bb)],
                                    axis=1).astype(jnp.bfloat16)
            k_inv = jnp.concatenate([k * jnp.exp(-bf), k * jnp.exp(-bb)], axis=1)
            k_ends.append(jnp.concatenate(
                [k * jnp.exp(bf_last - bf), k * jnp.exp(bb_last - bb)], axis=1).astype(jnp.bfloat16))
            dec_s[n] = jnp.concatenate([jnp.exp(bf_last), jnp.exp(bb_last)], axis=1)
            qcat_s[r, :] = q_cat
            k_bd = jnp.concatenate([jnp.where(cat_block == c, k_inv, 0.0) for c in range(4)],
                                   axis=0).astype(jnp.bfloat16)
            scores.append(lax.dot_general(q_cat, k_bd, _NT,
                                          preferred_element_type=jnp.float32))
        for n, r, k_end in zip(chunks, rows, k_ends):
            kv_t = lax.dot_general(v_ref[r, :], k_end, _TN,
                                   preferred_element_type=jnp.float32)
            kv_s[n] = jnp.where(head0_lanes, kv_t[0:GLA_DV, :], kv_t[GLA_DV:, :])
        for r, sc in zip(rows, scores):
            p = (jnp.where(keep_f, sc[:, 0:pair_dk], 0.0)
                 + jnp.where(keep_b, sc[:, pair_dk:], 0.0)).astype(jnp.bfloat16)
            vv = v_ref[r, :]
            v_bd = jnp.concatenate(
                [jnp.where(v_head == j, vv, jnp.zeros_like(vv)) for j in range(2)], axis=0)
            intra_s[r, :] = _bdot(p, v_bd)
        return carry

    lax.fori_loop(0, n_chunks // GLA_GROUP, intra_body, 0)

    def state_body(i, states):
        sf, sb = states
        nf = i
        nb = n_chunks - 1 - i
        st_s[nf, :, 0:pair_dk] = sf.astype(jnp.bfloat16)
        st_s[nb, :, pair_dk:] = sb.astype(jnp.bfloat16)
        sf = sf * dec_s[nf][:, 0:pair_dk] + kv_s[nf, :, 0:pair_dk]
        sb = sb * dec_s[nb][:, pair_dk:] + kv_s[nb, :, pair_dk:]
        return sf, sb

    zero = jnp.zeros((GLA_DV, pair_dk), jnp.float32)
    lax.fori_loop(0, n_chunks, state_body, (zero, zero))

    g = g_ref[layer:layer + 1, :]

    def out_body(i, carry):
        chunks = [i * GLA_GROUP + c for c in range(GLA_GROUP)]
        rows = [pl.ds(pl.multiple_of(n * cdim, cdim), cdim) for n in chunks]
        inter = []
        for n, r in zip(chunks, rows):
            packed = st_s[n]
            none = jnp.zeros_like(packed)
            st_bd = jnp.concatenate([jnp.where(head0_lanes, packed, none),
                                     jnp.where(head0_lanes, none, packed)], axis=0)
            inter.append(lax.dot_general(qcat_s[r, :], st_bd, _NT,
                                         preferred_element_type=jnp.float32))
        for r, o_inter in zip(rows, inter):
            o = intra_s[r, :] + o_inter
            for j in range(2):
                o_ref[r, j * GLA_DV:(j + 1) * GLA_DV] = _rms(
                    o[:, j * GLA_DV:(j + 1) * GLA_DV], g).astype(o_ref.dtype)
        return carry

    lax.fori_loop(0, n_chunks // GLA_GROUP, out_body, 0)


def _gla_call(layer, gq, gk, gv, misc, wg_cat, bg_cat, gla_norm_g, w_in_t=None):
    bsz, s, _ = gq.shape
    pairs = GLA_HEADS // 2
    n_chunks = s // GLA_CHUNK
    pair_dk, pair_dv = 2 * GLA_DK, 2 * GLA_DV
    qk_spec = pl.BlockSpec((None, s, pair_dk), lambda b, p: (b, 0, p))
    v_spec = pl.BlockSpec((None, s, pair_dv), lambda b, p: (b, 0, p))

    def gate_specs(direction):
        return (pl.BlockSpec((None, MISC_W, pair_dk), lambda b, p: (layer, 0, direction * pairs + p)),
                pl.BlockSpec((None, 1, pair_dk), lambda b, p: (layer, 0, direction * pairs + p)))

    in_specs = [qk_spec, qk_spec, v_spec,
                pl.BlockSpec((None, s, MISC_W), lambda b, p: (b, 0, 0)),
                *gate_specs(0), *gate_specs(1),
                _whole_spec(gla_norm_g)]
    operands = [gq, gk, gv, misc, wg_cat, bg_cat, wg_cat, bg_cat, gla_norm_g]
    out_specs = [v_spec]
    out_shape = [jax.ShapeDtypeStruct((bsz, s, GLA_WIDTH), jnp.bfloat16)]
    n_w_slabs = 0
    if w_in_t is not None:
        n_w_slabs = D_MODEL // W_PREP_ROWS
        assert n_w_slabs <= bsz * pairs

        def slab(b, p):
            return jnp.minimum(b * pairs + p, n_w_slabs - 1)

        in_specs.append(pl.BlockSpec((None, IN_DIM, W_PREP_ROWS),
                                     lambda b, p: (layer + 1, 0, slab(b, p))))
        operands.append(w_in_t)
        out_specs.append(pl.BlockSpec((W_PREP_ROWS, IN_DIM_P), lambda b, p: (slab(b, p), 0)))
        out_shape.append(jax.ShapeDtypeStruct((D_MODEL, IN_DIM_P), jnp.bfloat16))

    return pl.pallas_call(
        functools.partial(_gla_kernel, layer, n_w_slabs),
        grid=(bsz, pairs),
        in_specs=in_specs,
        out_specs=out_specs,
        out_shape=out_shape,
        scratch_shapes=[
            pltpu.VMEM((s, pair_dk), jnp.float32),
            pltpu.VMEM((s, pair_dk), jnp.float32),
            pltpu.VMEM((s, 2 * pair_dk), jnp.bfloat16),
            pltpu.VMEM((n_chunks, GLA_DV, 2 * pair_dk), jnp.float32),
            pltpu.VMEM((n_chunks, 1, 2 * pair_dk), jnp.float32),
            pltpu.VMEM((n_chunks, GLA_DV, 2 * pair_dk), jnp.bfloat16),
            pltpu.VMEM((s, pair_dv), jnp.float32),
        ],
        compiler_params=pltpu.CompilerParams(
            dimension_semantics=("arbitrary", "arbitrary"), vmem_limit_bytes=VMEM_LIMIT),
        name="gla",
    )(*operands)


def _mla_kernel(layer, mq_ref, mkv_ref, misc_ref, c_ref, sa_ref, sb_ref, gq_ref, gkv_ref,
                wuq_ref, wuk_ref, wuvt_ref, o_ref, ckv_s, q_s, k_s, vt_s, sc_s, top_s):
    s = mkv_ref.shape[0]
    n_tiles = s // Q_TILE
    assert n_tiles % 2 == 0 and MLA_HEADS % 2 == 0
    blocks = [slice(i * KV_ROWS, (i + 1) * KV_ROWS) for i in range(s // KV_ROWS)]
    q_mult = (MLA_NOPE + MLA_ROPE) ** -0.5 * math.log2(math.e)

    for r in blocks:
        tabs = (c_ref[r, :], sa_ref[r, :], sb_ref[r, :])
        ckv_s[r, :] = _rms(mkv_ref[r, :].astype(jnp.float32),
                           gkv_ref[layer:layer + 1, :]).astype(jnp.bfloat16)
        k_rope = _rope(misc_ref[r, :], *tabs).astype(jnp.bfloat16)
        for slot in range(2):
            k_s[slot, r, MLA_NOPE:] = k_rope
        cq = _rms(mq_ref[r, :].astype(jnp.float32),
                  gq_ref[layer:layer + 1, :]).astype(jnp.bfloat16)
        qf = _bdot(cq, wuq_ref[...])
        cos_t, sin_t = tabs[0], tabs[1] + tabs[2]
        for h in range(MLA_HEADS):
            qh = qf[:, h * QK_W:(h + 1) * QK_W]
            rot = qh[:, MLA_NOPE:]
            q_rope = rot * cos_t + pltpu.roll(rot, MLA_ROPE, axis=1) * sin_t
            q_s[h, r, :] = (jnp.concatenate([qh[:, 0:MLA_NOPE], q_rope], axis=1)
                            * q_mult).astype(jnp.bfloat16)
    for slot in range(2):
        vt_s[slot, MLA_DV:, :] = jnp.ones((vt_s.shape[1] - MLA_DV, s), jnp.bfloat16)

    def build_kv(h, slot):
        for r in blocks:
            ckv = ckv_s[r, :]
            k_s[slot, r, 0:MLA_NOPE] = _bdot(ckv, wuk_ref[h]).astype(jnp.bfloat16)
            vt_s[slot, 0:MLA_DV, r] = lax.dot_general(
                wuvt_ref[h], ckv, _NT, preferred_element_type=jnp.float32).astype(jnp.bfloat16)

    def scores_t(h, slot, j):
        q = q_s[h, j * Q_TILE:(j + 1) * Q_TILE, :]
        sc_t = lax.dot_general(k_s[slot], q, _NT, preferred_element_type=jnp.float32)
        sc_s[j % 2] = sc_t
        top_s[j % 2] = jnp.broadcast_to(jnp.max(sc_t, axis=0, keepdims=True), (F32_ROWS, Q_TILE))

    def finish(h, slot, j):
        p_t = jnp.exp2(sc_s[j % 2] - top_s[j % 2][0:1, :]).astype(jnp.bfloat16)
        o_t = _bdot(vt_s[slot], p_t)
        o_ref[h, j * Q_TILE:(j + 1) * Q_TILE, :] = (
            o_t[0:MLA_DV, :] / o_t[MLA_DV:MLA_DV + 1, :]).T.astype(o_ref.dtype)

    build_kv(0, 0)
    scores_t(0, 0, 0)

    def head_pair(i, carry):
        for slot in range(2):
            h = 2 * i + slot
            h_next = jnp.minimum(h + 1, MLA_HEADS - 1)
            for j in range(n_tiles):
                if j + 1 < n_tiles:
                    scores_t(h, slot, j + 1)
                else:
                    build_kv(h_next, 1 - slot)
                    scores_t(h_next, 1 - slot, 0)
                finish(h, slot, j)
        return carry

    lax.fori_loop(0, MLA_HEADS // 2, head_pair, 0)


def _mla_call(layer, mq, mkv, misc, tabs, gq, gkv, wuq_p, wuk_p, wuvt_p):
    bsz, s, _ = mq.shape

    def per_row(width):
        return pl.BlockSpec((None, s, width), lambda b: (b, 0, 0))

    return pl.pallas_call(
        functools.partial(_mla_kernel, layer),
        grid=(bsz,),
        in_specs=[
            per_row(MLA_Q_LORA), per_row(MLA_KV_LORA), per_row(MISC_W),
            per_row(LANES), per_row(LANES), per_row(LANES),
            _whole_spec(gq), _whole_spec(gkv),
            _layer_spec(layer, (MLA_Q_LORA, MLA_HEADS * QK_W)),
            _layer_spec(layer, (MLA_HEADS, MLA_KV_LORA, MLA_NOPE)),
            _layer_spec(layer, (MLA_HEADS, MLA_DV, MLA_KV_LORA)),
        ],
        out_specs=pl.BlockSpec((None, MLA_HEADS, s, MLA_DV), lambda b: (b, 0, 0, 0)),
        out_shape=jax.ShapeDtypeStruct((bsz, MLA_HEADS, s, MLA_DV), jnp.bfloat16),
        scratch_shapes=[
            pltpu.VMEM((s, MLA_KV_LORA), jnp.bfloat16),
            pltpu.VMEM((MLA_HEADS, s, QK_W), jnp.bfloat16),
            pltpu.VMEM((2, s, QK_W), jnp.bfloat16),
            pltpu.VMEM((2, MLA_DV + BF16_ROWS, s), jnp.bfloat16),
            pltpu.VMEM((2, s, Q_TILE), jnp.float32),
            pltpu.VMEM((2, F32_ROWS, Q_TILE), jnp.float32),
        ],
        compiler_params=pltpu.CompilerParams(
            dimension_semantics=("arbitrary",),
            vmem_limit_bytes=VMEM_LIMIT),
        name="mla",
    )(mq, mkv, misc, *tabs, gq, gkv, wuq_p, wuk_p, wuvt_p)


def _outproj_kernel(layer, tiles_per_seq, final, with_next_mod, x_ref, mod_ref, ogla_ref, omla_ref,
                    cb_ref, u_ref, up_ref, un_ref, sz_ref, mg_ref, cw_ref, cg_ref, fg_ref, w_ref,
                    *refs):
    if with_next_mod:
        c_ref, ada_w_ref, ada_b_ref, o_ref, next_mod_ref = refs
        _ada_kernel(c_ref, ada_w_ref, ada_b_ref, next_mod_ref)
    else:
        (o_ref,) = refs
    f32 = jnp.float32
    j = pl.program_id(0) % tiles_per_seq
    tm = x_ref.shape[0]

    u = u_ref[...].astype(f32)
    prev_ok = (j > 0).astype(f32)
    next_ok = (j < tiles_per_seq - 1).astype(f32)
    u_prev = up_ref[...].astype(f32)[HALO_ROWS - 1:, :] * prev_ok
    u_next = un_ref[...].astype(f32)[0:1, :] * next_ok
    rows = lax.broadcasted_iota(jnp.int32, u.shape, 0)
    up = jnp.where(rows == 0, u_prev, pltpu.roll(u, 1, axis=0))
    un = jnp.where(rows == tm - 1, u_next, pltpu.roll(u, tm - 1, axis=0))
    cw = cw_ref[...]
    conv = up * cw[0:1] + u * cw[1:2] + un * cw[2:3]
    gate = mod_ref[pl.ds(pl.program_id(0) // tiles_per_seq, 1), :][:, 2 * D_MODEL:]

    def gated_dot(r, lo, y_part):
        hi = lo + y_part.shape[1]
        y = (y_part * sz_ref[r, lo:hi].astype(f32)).astype(jnp.bfloat16)
        return _bdot(y, w_ref[lo:hi, :])

    def finish(r, acc):
        out = x_ref[r, :] + gate * acc
        if final:
            out = _rms(out, fg_ref[...])
        o_ref[r, :] = out

    pending = None
    for k in range(tm // OUT_SUB_TILE):
        r = slice(k * OUT_SUB_TILE, (k + 1) * OUT_SUB_TILE)
        acc = gated_dot(r, 0, ogla_ref[r, :].astype(f32))
        if pending is not None:
            finish(*pending)
        omla = jnp.concatenate([omla_ref[h, r, :] for h in range(MLA_HEADS)], axis=1)
        acc += gated_dot(r, GLA_WIDTH, _rms(omla.astype(f32), mg_ref[layer:layer + 1, :]))
        acc += gated_dot(r, GLA_WIDTH + MLA_WIDTH,
                         _rms(cb_ref[r, :].astype(f32) * conv[r, :], cg_ref[layer:layer + 1, :]))
        pending = (r, acc)
    finish(*pending)


def _outproj_call(layer, x2, mod, o_gla, o_mla, cb, u, sz, mla_out_g, conv_w, conv_out_g,
                  final_g, w_out_b, seq_len, final, next_ada=None):
    t = x2.shape[0]
    tm = OUT_ROW_TILE
    n_steps = t // tm
    tiles_per_seq = seq_len // tm
    halo_per_tile = tm // HALO_ROWS
    n_halo = t // HALO_ROWS

    def row_spec(w):
        return pl.BlockSpec((tm, w), lambda i: (i, 0))

    prev_spec = pl.BlockSpec((HALO_ROWS, CONV_CH),
                             lambda i: (jnp.maximum(i * halo_per_tile - 1, 0), 0))
    next_spec = pl.BlockSpec((HALO_ROWS, CONV_CH),
                             lambda i: (jnp.minimum((i + 1) * halo_per_tile, n_halo - 1), 0))

    in_specs = [
        row_spec(D_MODEL),
        _whole_spec(mod),
        row_spec(GLA_WIDTH),
        pl.BlockSpec((None, MLA_HEADS, tm, MLA_DV),
                     lambda i: (i // tiles_per_seq, 0, i % tiles_per_seq, 0)),
        row_spec(CONV_CH), row_spec(CONV_CH), prev_spec, next_spec,
        row_spec(D_MODEL),
        _whole_spec(mla_out_g), _layer_spec(layer, (3, CONV_CH)), _whole_spec(conv_out_g),
        pl.BlockSpec((1, D_MODEL), lambda i: (0, 0)),
        pl.BlockSpec((D_MODEL, D_MODEL), lambda i: (0, 0), pipeline_mode=pl.Buffered(1)),
    ]
    operands = [x2, mod, o_gla, o_mla, cb, u, u, u, sz,
                mla_out_g, conv_w, conv_out_g, final_g, w_out_b]
    out_specs = [row_spec(D_MODEL)]
    out_shape = [jax.ShapeDtypeStruct((t, D_MODEL), jnp.float32)]
    if next_ada is not None:
        n_mod = next_ada[1].shape[-1]
        ada_in, ada_out = _ada_specs(layer + 1, n_mod // n_steps)
        in_specs += ada_in
        operands += list(next_ada)
        out_specs.append(ada_out)
        out_shape.append(jax.ShapeDtypeStruct((F32_ROWS, n_mod), jnp.float32))
    return pl.pallas_call(
        functools.partial(_outproj_kernel, layer, tiles_per_seq, final, next_ada is not None),
        grid=(n_steps,),
        in_specs=in_specs,
        out_specs=out_specs,
        out_shape=out_shape,
        compiler_params=pltpu.CompilerParams(
            dimension_semantics=("arbitrary",), vmem_limit_bytes=VMEM_LIMIT),
        name="out_proj",
    )(*operands)


_O_GLR = 2 * GLA_HEADS * GLA_DK + GLA_WIDTH
_O_MQ = _O_GLR + 2 * GLA_GATE_RANK
_O_MKV = _O_MQ + MLA_Q_LORA
_O_MKR = _O_MKV + MLA_KV_LORA
_O_CB = _O_MKR + MLA_ROPE
IN_DIM = _O_CB + 3 * CONV_CH + D_MODEL
W_PREP_ROWS = 256


W_PREP_CHUNK = 512


def _w_in_prep_kernel(wt_ref, o_ref):
    rows = o_ref.shape[0]
    eye = (lax.broadcasted_iota(jnp.int32, (rows, rows), 0)
           == lax.broadcasted_iota(jnp.int32, (rows, rows), 1)).astype(jnp.bfloat16)

    def put(dst, block):
        o_ref[:, dst:dst + block.shape[0]] = lax.dot_general(
            eye, block.astype(jnp.bfloat16), _NT,
            preferred_element_type=jnp.float32).astype(jnp.bfloat16)

    def put_range(dst, lo, hi):
        for start in range(lo, hi, W_PREP_CHUNK):
            stop = min(start + W_PREP_CHUNK, hi)
            put(dst + start - lo, wt_ref[start:stop, :])

    pad_rows = MISC_W - MLA_ROPE - 2 * GLA_GATE_RANK
    misc = jnp.concatenate([wt_ref[_O_MKR:_O_CB, :], wt_ref[_O_GLR:_O_MQ, :],
                            jnp.zeros((pad_rows, rows), jnp.float32)], axis=0)
    off = 0
    for src in ((0, _O_GLR), (_O_MQ, _O_MKV), misc, (_O_MKV, _O_MKR), (_O_CB, IN_DIM)):
        if isinstance(src, tuple):
            put_range(off, *src)
            off += src[1] - src[0]
        else:
            put(off, src)
            off += src.shape[0]


def _first_prep_kernel(wt_ref, pos_ref, invf_ref, o_ref, c_ref, sa_ref, sb_ref):
    _w_in_prep_kernel(wt_ref, o_ref)
    _rope_tables(pos_ref, invf_ref, c_ref, sa_ref, sb_ref)


def _first_prep_call(w_in_t, positions):
    _, n, d = w_in_t.shape
    n_steps = d // W_PREP_ROWS
    t = positions.size
    tile = t // n_steps
    pairs, invf = _rope_table_inputs(positions, tile)
    tab_spec = pl.BlockSpec((tile, LANES), lambda i: (i, 0))
    tab_shape = jax.ShapeDtypeStruct((t, LANES), jnp.float32)
    w_in_p, *tabs = pl.pallas_call(
        _first_prep_kernel,
        grid=(n_steps,),
        in_specs=[pl.BlockSpec((None, n, W_PREP_ROWS), lambda i: (0, 0, i)),
                  pl.BlockSpec((None, tile // 2, 2), lambda i: (i, 0, 0)),
                  pl.BlockSpec((1, LANES), lambda i: (0, 0))],
        out_specs=[pl.BlockSpec((W_PREP_ROWS, IN_DIM_P), lambda i: (i, 0))] + [tab_spec] * 3,
        out_shape=[jax.ShapeDtypeStruct((d, IN_DIM_P), jnp.bfloat16)] + [tab_shape] * 3,
        compiler_params=pltpu.CompilerParams(
            dimension_semantics=("arbitrary",), vmem_limit_bytes=VMEM_LIMIT),
        name="w_in_prep",
    )(w_in_t, pairs, invf)
    return w_in_p, tabs


def _gate_params(wg_f, bg_f, wg_b, bg_b):
    depth, rank, hk = wg_f.shape
    zeros = jnp.zeros((depth, rank, hk), jnp.float32)
    rows = jnp.concatenate([jnp.concatenate([wg_f, zeros], axis=2),
                            jnp.concatenate([zeros, wg_b], axis=2)], axis=1)
    wg_cat = jnp.pad(rows, ((0, 0), (MLA_ROPE, MISC_W - MLA_ROPE - 2 * rank), (0, 0)))
    bg_cat = jnp.concatenate([bg_f, bg_b], axis=1)[:, None, :]
    return wg_cat.astype(jnp.bfloat16), bg_cat


def _permute_w_uq(w):
    depth = w.shape[0]
    half = MLA_ROPE // 2
    w4 = w.reshape(depth, MLA_Q_LORA, MLA_HEADS, MLA_NOPE + MLA_ROPE)
    nope, x1, x2 = w4[..., :MLA_NOPE], w4[..., MLA_NOPE:MLA_NOPE + half], w4[..., MLA_NOPE + half:]
    cols = jnp.concatenate([nope, x1, x2, x2, x1], axis=-1)
    return cols.reshape(depth, MLA_Q_LORA, MLA_HEADS * QK_W).astype(jnp.bfloat16)


def _split_w_ukv(w):
    depth = w.shape[0]
    w4 = w.reshape(depth, MLA_KV_LORA, MLA_HEADS, MLA_NOPE + MLA_DV)
    wk = w4[..., :MLA_NOPE].transpose(0, 2, 1, 3)
    wv_t = w4[..., MLA_NOPE:].transpose(0, 2, 3, 1)
    return wk.astype(jnp.bfloat16), wv_t.astype(jnp.bfloat16)


def kernel(x, c, positions, ada_w, ada_b, norm_g, w_in, gla_wg_f, gla_bg_f, gla_wg_b, gla_bg_b,
           gla_norm_g, mla_q_norm_g, mla_kv_norm_g, mla_w_uq, mla_w_ukv, mla_out_g, conv_w,
           conv_out_g, w_out, final_g):
    bsz, s, d = x.shape
    t = bsz * s

    ada = (jnp.pad(c, ((0, F32_ROWS - bsz), (0, 0))), ada_w, ada_b[:, None, :])
    mod = _ada_call(0, *ada)

    def b3(a):
        return a.reshape(bsz, s, a.shape[-1])

    w_in_t = jnp.swapaxes(w_in, 1, 2)
    w_in_p, tabs = _first_prep_call(w_in_t, positions)
    tabs = tuple(b3(tab) for tab in tabs)
    wg_cat, bg_cat = _gate_params(gla_wg_f, gla_bg_f, gla_wg_b, gla_bg_b)
    wuq_p = _permute_w_uq(mla_w_uq)
    wuk_p, wuvt_p = _split_w_ukv(mla_w_ukv)

    h = x.reshape(t, d)
    for l in range(DEPTH):
        segs, w_out_b = _inproj_call(l, h, mod, norm_g, w_in_p, w_out, s)
        gq, gk, gv, mq, misc, mkv, cb, u, sz = segs
        o_gla, *next_w = _gla_call(l, b3(gq), b3(gk), b3(gv), b3(misc), wg_cat, bg_cat,
                                   gla_norm_g, w_in_t if l + 1 < DEPTH else None)
        if next_w:
            w_in_p = next_w[0]
        o_mla = _mla_call(l, b3(mq), b3(mkv), b3(misc), tabs, mla_q_norm_g, mla_kv_norm_g,
                          wuq_p, wuk_p, wuvt_p)
        h, *next_mod = _outproj_call(l, h, mod, o_gla.reshape(t, -1), o_mla, cb, u, sz,
                                     mla_out_g, conv_w, conv_out_g, final_g.reshape(1, d),
                                     w_out_b, s, l == DEPTH - 1,
                                     ada if l + 1 < DEPTH else None)
        if next_mod:
            mod = next_mod[0]
    return h.reshape(bsz, s, d)
```

```python
import functools
import math

import jax
import jax.numpy as jnp
from jax import lax
from jax.experimental import pallas as pl
from jax.experimental.pallas import tpu as pltpu

D_MODEL = 2048
DEPTH = 2
GLA_HEADS = 6
GLA_DK = 64
GLA_DV = 128
GLA_GATE_RANK = 16
GLA_GATE_TEMP = 16.0
GLA_CHUNK = 64
GLA_WIDTH = GLA_HEADS * GLA_DV
MLA_HEADS = 6
MLA_Q_LORA = 384
MLA_KV_LORA = 256
MLA_NOPE = 128
MLA_ROPE = 64
MLA_DV = 128
MLA_WIDTH = MLA_HEADS * MLA_DV
ROPE_THETA = 10000.0
CONV_CH = D_MODEL - GLA_WIDTH - MLA_WIDTH
EPS = 1e-6

LANES = 128
MXU_COLS = 256
F32_ROWS = 8
BF16_ROWS = 16
HALO_ROWS = BF16_ROWS
MISC_W = LANES
QK_W = 2 * LANES

IN_SEGMENTS = (
    ("gq", GLA_HEADS * GLA_DK, jnp.bfloat16),
    ("gk", GLA_HEADS * GLA_DK, jnp.bfloat16),
    ("gv", GLA_WIDTH, jnp.bfloat16),
    ("mq", MLA_Q_LORA, jnp.bfloat16),
    ("misc", MISC_W, jnp.float32),
    ("mkv", MLA_KV_LORA, jnp.bfloat16),
    ("cb", CONV_CH, jnp.bfloat16),
    ("cc", CONV_CH, jnp.bfloat16),
    ("cx", CONV_CH, jnp.bfloat16),
    ("z", D_MODEL, jnp.bfloat16),
)
IN_DIM_P = sum(w for _, w, _ in IN_SEGMENTS)
_SEGMENT_COLS = {}
for _name, _width, _ in IN_SEGMENTS:
    _SEGMENT_COLS[_name] = (sum(w for _, w in _SEGMENT_COLS.values()), _width)
IN_OUTPUTS = tuple(seg for seg in IN_SEGMENTS if seg[0] not in ("cc", "cx", "z")) + (
    ("u", CONV_CH, jnp.bfloat16),
    ("sz", D_MODEL, jnp.bfloat16),
)

IN_ROW_TILE = 512
IN_SUB_TILE = 256
OUT_ROW_TILE = 512
OUT_SUB_TILE = 256
Q_TILE = 512
KV_ROWS = 512
SCAN_ROWS = 256
SCAN_GROUP = 4
GLA_GROUP = 8
VMEM_LIMIT = 56 * 1024 * 1024

_NT = (((1,), (1,)), ((), ()))
_TN = (((0,), (0,)), ((), ()))


def _rms(x, g):
    ms = jnp.mean(x * x, axis=-1, keepdims=True)
    return x * lax.rsqrt(ms + EPS) * g


def _bdot(a, b):
    return jnp.dot(a, b, preferred_element_type=jnp.float32)


ADA_COLS = 1024


def _split_bf16(a):
    high = a.astype(jnp.bfloat16)
    return high, (a - high.astype(jnp.float32)).astype(jnp.bfloat16)


def _ada_kernel(c_ref, w_ref, b_ref, o_ref):
    c = c_ref[...]
    act = jnp.concatenate(_split_bf16(c * jax.nn.sigmoid(c)), axis=0)
    w_high, w_rest = _split_bf16(w_ref[...])
    both = _bdot(act, w_high) + _bdot(act, w_rest)
    o_ref[...] = both[0:F32_ROWS] + both[F32_ROWS:] + b_ref[...]


def _ada_specs(layer, tn):
    in_specs = [pl.BlockSpec((F32_ROWS, D_MODEL), lambda j: (0, 0)),
                pl.BlockSpec((None, D_MODEL, tn), lambda j: (layer, 0, j)),
                pl.BlockSpec((None, 1, tn), lambda j: (layer, 0, j))]
    return in_specs, pl.BlockSpec((F32_ROWS, tn), lambda j: (0, j))


def _ada_call(layer, c_pad, ada_w, ada_b):
    n = ada_w.shape[-1]
    in_specs, out_spec = _ada_specs(layer, ADA_COLS)
    return pl.pallas_call(
        _ada_kernel,
        grid=(n // ADA_COLS,),
        in_specs=in_specs,
        out_specs=out_spec,
        out_shape=jax.ShapeDtypeStruct((F32_ROWS, n), jnp.float32),
        compiler_params=pltpu.CompilerParams(
            dimension_semantics=("arbitrary",), vmem_limit_bytes=VMEM_LIMIT),
        name="ada_mod",
    )(c_pad, ada_w, ada_b)


def _rope_tables(pos_ref, invf_ref, c_ref, sa_ref, sb_ref):
    rows = pos_ref.shape[0]
    lane = lax.broadcasted_iota(jnp.int32, (rows, LANES), 1)
    pos = jnp.where(lane < MLA_ROPE, pos_ref[:, 0:1], pos_ref[:, 1:2]).astype(jnp.float32)
    ang = pos * invf_ref[...]
    half = MLA_ROPE // 2
    in_first, in_second = lane < half, (lane >= half) & (lane < MLA_ROPE)
    cos2, sin2 = jnp.cos(ang), jnp.sin(ang)
    for r, shift in ((slice(0, rows), 0), (slice(rows, 2 * rows), MLA_ROPE)):
        cos = cos2 if shift == 0 else pltpu.roll(cos2, shift, axis=1)
        sin = sin2 if shift == 0 else pltpu.roll(sin2, shift, axis=1)
        c_ref[r, :] = jnp.where(lane < MLA_ROPE, cos, 0.0)
        sa_ref[r, :] = jnp.where(in_first, -sin, 0.0)
        sb_ref[r, :] = jnp.where(in_second, sin, 0.0)


def _rope_table_inputs(positions, tile_rows):
    inv_freq = ROPE_THETA ** (-jnp.arange(0, MLA_ROPE, 2, dtype=jnp.float32) / MLA_ROPE)
    invf = jnp.tile(inv_freq, LANES // inv_freq.shape[0]).reshape(1, LANES)
    pairs = positions.reshape(-1, 2, tile_rows // 2).transpose(0, 2, 1)
    return pairs, invf


def _rope(x, c, sa, sb):
    return (x * c + pltpu.roll(x, LANES - MLA_ROPE // 2, axis=1) * sa
            + pltpu.roll(x, MLA_ROPE // 2, axis=1) * sb)


def _inproj_kernel(layer, tiles_per_seq, x_ref, mod_ref, g_ref, w_ref, wout_ref, *refs):
    out_refs, wout_b_ref, h_s = refs[:-2], refs[-2], refs[-1]
    wout_b_ref[...] = wout_ref[...].astype(jnp.bfloat16)
    mod = mod_ref[pl.ds(pl.program_id(0) // tiles_per_seq, 1), :]
    shift = mod[:, 0:D_MODEL]
    gain = g_ref[layer:layer + 1, :] * (1.0 + mod[:, D_MODEL:2 * D_MODEL])
    n_sub = x_ref.shape[0] // IN_SUB_TILE

    def normalize(k):
        x = x_ref[k * IN_SUB_TILE:(k + 1) * IN_SUB_TILE, :]
        ms = jnp.mean(x * x, axis=-1, keepdims=True)
        h_s[k % 2] = (x * lax.rsqrt(ms + EPS) * gain + shift).astype(jnp.bfloat16)

    outs = {name: o_ref for (name, _, _), o_ref in zip(IN_OUTPUTS, out_refs)}

    def project(k):
        r = slice(k * IN_SUB_TILE, (k + 1) * IN_SUB_TILE)

        def proj(names):
            lo = _SEGMENT_COLS[names[0]][0]
            width = sum(_SEGMENT_COLS[n][1] for n in names)
            assert width % MXU_COLS == 0
            val = _bdot(h_s[k % 2], w_ref[:, lo:lo + width])
            return {n: val[:, _SEGMENT_COLS[n][0] - lo:_SEGMENT_COLS[n][0] - lo + _SEGMENT_COLS[n][1]]
                    for n in names}

        z = proj(("z",))["z"]
        outs["sz"][r, :] = (z * jax.nn.sigmoid(z)).astype(outs["sz"].dtype)
        conv = proj(("cb", "cc", "cx"))
        outs["u"][r, :] = (conv["cc"] * conv["cx"]).astype(outs["u"].dtype)
        outs["cb"][r, :] = conv["cb"].astype(outs["cb"].dtype)
        for names in (("gq", "gk", "gv"), ("mq", "misc", "mkv")):
            for name, val in proj(names).items():
                outs[name][r, :] = val.astype(outs[name].dtype)

    normalize(0)
    for k in range(n_sub):
        project(k)
        if k + 1 < n_sub:
            normalize(k + 1)


def _layer_spec(layer, shape):
    return pl.BlockSpec((None,) + shape, lambda *_: (layer,) + (0,) * len(shape))


def _whole_spec(arr):
    return pl.BlockSpec(arr.shape, lambda *_: (0,) * arr.ndim)


def _inproj_call(layer, x2, mod, norm_g, w_in_p, w_out, seq_len):
    t = x2.shape[0]
    tm = IN_ROW_TILE
    n_steps = t // tm
    tiles_per_seq = seq_len // tm
    d_out = w_out.shape[-1]
    wout_rows = w_out.shape[1] // n_steps
    n_out = len(IN_OUTPUTS)
    in_specs = [
        pl.BlockSpec((tm, D_MODEL), lambda i: (i, 0)),
        _whole_spec(mod),
        _whole_spec(norm_g),
        pl.BlockSpec((D_MODEL, IN_DIM_P), lambda i: (0, 0), pipeline_mode=pl.Buffered(1)),
        pl.BlockSpec((None, wout_rows, d_out), lambda i: (layer, i, 0)),
    ]
    operands = [x2, mod, norm_g, w_in_p, w_out]
    out_shape = [jax.ShapeDtypeStruct((t, w), dt) for _, w, dt in IN_OUTPUTS]
    out_specs = [pl.BlockSpec((tm, w), lambda i: (i, 0)) for _, w, _ in IN_OUTPUTS]
    out_shape.append(jax.ShapeDtypeStruct(w_out.shape[1:], jnp.bfloat16))
    out_specs.append(pl.BlockSpec((wout_rows, d_out), lambda i: (i, 0)))
    outs = pl.pallas_call(
        functools.partial(_inproj_kernel, layer, tiles_per_seq),
        grid=(n_steps,),
        in_specs=in_specs,
        out_specs=out_specs,
        out_shape=out_shape,
        scratch_shapes=[pltpu.VMEM((2, IN_SUB_TILE, D_MODEL), jnp.bfloat16)],
        compiler_params=pltpu.CompilerParams(
            dimension_semantics=("arbitrary",), vmem_limit_bytes=VMEM_LIMIT),
        name="in_proj",
    )(*operands)
    return outs[:n_out], outs[n_out]


def _log_sigmoid(x):
    return jnp.minimum(x, 0.0) - jnp.log(1.0 + jnp.exp(-jnp.abs(x)))


def _gla_kernel(layer, n_w_slabs, q_ref, k_ref, v_ref, misc_ref, wgf_ref, bgf_ref, wgb_ref,
                bgb_ref, g_ref, *refs):
    if n_w_slabs:
        wt_ref, o_ref, w_prep_ref = refs[0:3]
        step = pl.program_id(0) * pl.num_programs(1) + pl.program_id(1)

        @pl.when(step < n_w_slabs)
        def _prep_next_layer_weight():
            _w_in_prep_kernel(wt_ref, w_prep_ref)

        refs = refs[3:]
    else:
        o_ref, refs = refs[0], refs[1:]
    bf_s, bb_s, qcat_s, kv_s, dec_s, st_s, intra_s = refs
    s = q_ref.shape[0]
    cdim = GLA_CHUNK
    n_chunks = s // cdim
    pair_dk = 2 * GLA_DK
    pair_dv = 2 * GLA_DV
    inv_t = 1.0 / GLA_GATE_TEMP
    q_scale = GLA_DK ** -0.5

    assert math.frexp(inv_t)[0] == 0.5
    t_row = lax.broadcasted_iota(jnp.int32, (SCAN_ROWS, SCAN_ROWS), 0)
    t_col = lax.broadcasted_iota(jnp.int32, (SCAN_ROWS, SCAN_ROWS), 1)
    same_chunk = (t_row // cdim) == (t_col // cdim)
    tri_prefix = jnp.where(same_chunk & (t_col <= t_row), inv_t, 0.0).astype(jnp.bfloat16)
    tri_suffix = jnp.where(same_chunk & (t_col >= t_row), inv_t, 0.0).astype(jnp.bfloat16)

    def split(terms):
        return jnp.concatenate(_split_bf16(terms), axis=1)

    def scan_body(i, carry):
        blocks = [pl.ds(pl.multiple_of((i * SCAN_GROUP + c) * SCAN_ROWS, SCAN_ROWS), SCAN_ROWS)
                  for c in range(SCAN_GROUP)]
        pre = []
        for r in blocks:
            m = misc_ref[r, :].astype(jnp.bfloat16)
            pre.append((_bdot(m, wgf_ref[...]), _bdot(m, wgb_ref[...])))
        terms = [(split(_log_sigmoid(pf + bgf_ref[...])),
                  split(_log_sigmoid(pb + bgb_ref[...]))) for pf, pb in pre]
        sums = [(_bdot(tri_prefix, tf), _bdot(tri_suffix, tb)) for tf, tb in terms]
        for r, (sf, sb) in zip(blocks, sums):
            bf_s[r, :] = sf[:, 0:pair_dk] + sf[:, pair_dk:]
            bb_s[r, :] = sb[:, 0:pair_dk] + sb[:, pair_dk:]
        return carry

    lax.fori_loop(0, s // (SCAN_ROWS * SCAN_GROUP), scan_body, 0)

    row = lax.broadcasted_iota(jnp.int32, (cdim, pair_dk), 0)
    key_pos = lax.broadcasted_iota(jnp.int32, (cdim, pair_dk), 1) & (cdim - 1)
    keep_f = key_pos <= row
    keep_b = key_pos > row
    cat_block = lax.broadcasted_iota(jnp.int32, (cdim, 2 * pair_dk), 1) // GLA_DK
    v_head = lax.broadcasted_iota(jnp.int32, (cdim, pair_dv), 1) // GLA_DV
    head0_lanes = (lax.broadcasted_iota(jnp.int32, (GLA_DV, 2 * pair_dk), 1) // GLA_DK) % 2 == 0

    def intra_body(i, carry):
        chunks = [i * GLA_GROUP + c for c in range(GLA_GROUP)]
        rows = [pl.ds(pl.multiple_of(n * cdim, cdim), cdim) for n in chunks]
        scores, k_ends = [], []
        for n, r in zip(chunks, rows):
            bf = bf_s[r, :]
            bb = bb_s[r, :]
            bf_last = bf[cdim - 1:cdim, :]
            bb_last = bb[0:1, :]
            q = q_ref[r, :].astype(jnp.float32) * q_scale
            k = k_ref[r, :].astype(jnp.float32)
            q_cat = jnp.concatenate([q * jnp.exp(bf), q * jnp.exp(bb)],
                                    axis=1).astype(jnp.bfloat16)
            k_inv = jnp.concatenate([k * jnp.exp(-bf), k * jnp.exp(-bb)], axis=1)
            k_ends.append(jnp.concatenate(
                [k * jnp.exp(bf_last - bf), k * jnp.exp(bb_last - bb)], axis=1).astype(jnp.bfloat16))
            dec_s[n] = jnp.concatenate([jnp.exp(bf_last), jnp.exp(bb_last)], axis=1)
            qcat_s[r, :] = q_cat
            k_bd = jnp.concatenate([jnp.where(cat_block == c, k_inv, 0.0) for c in range(4)],
                                   axis=0).astype(jnp.bfloat16)
            scores.append(lax.dot_general(q_cat, k_bd, _NT,
                                          preferred_element_type=jnp.float32))
        for n, r, k_end in zip(chunks, rows, k_ends):
            kv_t = lax.dot_general(v_ref[r, :], k_end, _TN,
                                   preferred_element_type=jnp.float32)
            kv_s[n] = jnp.where(head0_lanes, kv_t[0:GLA_DV, :], kv_t[GLA_DV:, :])
        for r, sc in zip(rows, scores):
            p = (jnp.where(keep_f, sc[:, 0:pair_dk], 0.0)
                 + jnp.where(keep_b, sc[:, pair_dk:], 0.0)).astype(jnp.bfloat16)
            vv = v_ref[r, :]
            v_bd = jnp.concatenate(
                [jnp.where(v_head == j, vv, jnp.zeros_like(vv)) for j in range(2)], axis=0)
            intra_s[r, :] = _bdot(p, v_bd)
        return carry

    lax.fori_loop(0, n_chunks // GLA_GROUP, intra_body, 0)

    def state_body(i, states):
        sf, sb = states
        nf = i
        nb = n_chunks - 1 - i
        st_s[nf, :, 0:pair_dk] = sf.astype(jnp.bfloat16)
        st_s[nb, :, pair_dk:] = sb.astype(jnp.bfloat16)
        sf = sf * dec_s[nf][:, 0:pair_dk] + kv_s[nf, :, 0:pair_dk]
        sb = sb * dec_s[nb][:, pair_dk:] + kv_s[nb, :, pair_dk:]
        return sf, sb

    zero = jnp.zeros((GLA_DV, pair_dk), jnp.float32)
    lax.fori_loop(0, n_chunks, state_body, (zero, zero))

    g = g_ref[layer:layer + 1, :]

    def out_body(i, carry):
        chunks = [i * GLA_GROUP + c for c in range(GLA_GROUP)]
        rows = [pl.ds(pl.multiple_of(n * cdim, cdim), cdim) for n in chunks]
        inter = []
        for n, r in zip(chunks, rows):
            packed = st_s[n]
            none = jnp.zeros_like(packed)
            st_bd = jnp.concatenate([jnp.where(head0_lanes, packed, none),
                                     jnp.where(head0_lanes, none, packed)], axis=0)
            inter.append(lax.dot_general(qcat_s[r, :], st_bd, _NT,
                                         preferred_element_type=jnp.float32))
        for r, o_inter in zip(rows, inter):
            o = intra_s[r, :] + o_inter
            for j in range(2):
                o_ref[r, j * GLA_DV:(j + 1) * GLA_DV] = _rms(
                    o[:, j * GLA_DV:(j + 1) * GLA_DV], g).astype(o_ref.dtype)
        return carry

    lax.fori_loop(0, n_chunks // GLA_GROUP, out_body, 0)


def _gla_call(layer, gq, gk, gv, misc, wg_cat, bg_cat, gla_norm_g, w_in_t=None):
    bsz, s, _ = gq.shape
    pairs = GLA_HEADS // 2
    n_chunks = s // GLA_CHUNK
    pair_dk, pair_dv = 2 * GLA_DK, 2 * GLA_DV
    qk_spec = pl.BlockSpec((None, s, pair_dk), lambda b, p: (b, 0, p))
    v_spec = pl.BlockSpec((None, s, pair_dv), lambda b, p: (b, 0, p))

    def gate_specs(direction):
        return (pl.BlockSpec((None, MISC_W, pair_dk), lambda b, p: (layer, 0, direction * pairs + p)),
                pl.BlockSpec((None, 1, pair_dk), lambda b, p: (layer, 0, direction * pairs + p)))

    in_specs = [qk_spec, qk_spec, v_spec,
                pl.BlockSpec((None, s, MISC_W), lambda b, p: (b, 0, 0)),
                *gate_specs(0), *gate_specs(1),
                _whole_spec(gla_norm_g)]
    operands = [gq, gk, gv, misc, wg_cat, bg_cat, wg_cat, bg_cat, gla_norm_g]
    out_specs = [v_spec]
    out_shape = [jax.ShapeDtypeStruct((bsz, s, GLA_WIDTH), jnp.bfloat16)]
    n_w_slabs = 0
    if w_in_t is not None:
        n_w_slabs = D_MODEL // W_PREP_ROWS
        assert n_w_slabs <= bsz * pairs

        def slab(b, p):
            return jnp.minimum(b * pairs + p, n_w_slabs - 1)

        in_specs.append(pl.BlockSpec((None, IN_DIM, W_PREP_ROWS),
                                     lambda b, p: (layer + 1, 0, slab(b, p))))
        operands.append(w_in_t)
        out_specs.append(pl.BlockSpec((W_PREP_ROWS, IN_DIM_P), lambda b, p: (slab(b, p), 0)))
        out_shape.append(jax.ShapeDtypeStruct((D_MODEL, IN_DIM_P), jnp.bfloat16))

    return pl.pallas_call(
        functools.partial(_gla_kernel, layer, n_w_slabs),
        grid=(bsz, pairs),
        in_specs=in_specs,
        out_specs=out_specs,
        out_shape=out_shape,
        scratch_shapes=[
            pltpu.VMEM((s, pair_dk), jnp.float32),
            pltpu.VMEM((s, pair_dk), jnp.float32),
            pltpu.VMEM((s, 2 * pair_dk), jnp.bfloat16),
            pltpu.VMEM((n_chunks, GLA_DV, 2 * pair_dk), jnp.float32),
            pltpu.VMEM((n_chunks, 1, 2 * pair_dk), jnp.float32),
            pltpu.VMEM((n_chunks, GLA_DV, 2 * pair_dk), jnp.bfloat16),
            pltpu.VMEM((s, pair_dv), jnp.float32),
        ],
        compiler_params=pltpu.CompilerParams(
            dimension_semantics=("arbitrary", "arbitrary"), vmem_limit_bytes=VMEM_LIMIT),
        name="gla",
    )(*operands)


def _mla_kernel(layer, mq_ref, mkv_ref, misc_ref, c_ref, sa_ref, sb_ref, gq_ref, gkv_ref,
                wuq_ref, wuk_ref, wuvt_ref, o_ref, ckv_s, q_s, k_s, vt_s, sc_s, top_s):
    s = mkv_ref.shape[0]
    n_tiles = s // Q_TILE
    assert n_tiles % 2 == 0 and MLA_HEADS % 2 == 0
    blocks = [slice(i * KV_ROWS, (i + 1) * KV_ROWS) for i in range(s // KV_ROWS)]
    q_mult = (MLA_NOPE + MLA_ROPE) ** -0.5 * math.log2(math.e)

    for r in blocks:
        tabs = (c_ref[r, :], sa_ref[r, :], sb_ref[r, :])
        ckv_s[r, :] = _rms(mkv_ref[r, :].astype(jnp.float32),
                           gkv_ref[layer:layer + 1, :]).astype(jnp.bfloat16)
        k_rope = _rope(misc_ref[r, :], *tabs).astype(jnp.bfloat16)
        for slot in range(2):
            k_s[slot, r, MLA_NOPE:] = k_rope
        cq = _rms(mq_ref[r, :].astype(jnp.float32),
                  gq_ref[layer:layer + 1, :]).astype(jnp.bfloat16)
        qf = _bdot(cq, wuq_ref[...])
        cos_t, sin_t = tabs[0], tabs[1] + tabs[2]
        for h in range(MLA_HEADS):
            qh = qf[:, h * QK_W:(h + 1) * QK_W]
            rot = qh[:, MLA_NOPE:]
            q_rope = rot * cos_t + pltpu.roll(rot, MLA_ROPE, axis=1) * sin_t
            q_s[h, r, :] = (jnp.concatenate([qh[:, 0:MLA_NOPE], q_rope], axis=1)
                            * q_mult).astype(jnp.bfloat16)
    for slot in range(2):
        vt_s[slot, MLA_DV:, :] = jnp.ones((vt_s.shape[1] - MLA_DV, s), jnp.bfloat16)

    def build_kv(h, slot):
        for r in blocks:
            ckv = ckv_s[r, :]
            k_s[slot, r, 0:MLA_NOPE] = _bdot(ckv, wuk_ref[h]).astype(jnp.bfloat16)
            vt_s[slot, 0:MLA_DV, r] = lax.dot_general(
                wuvt_ref[h], ckv, _NT, preferred_element_type=jnp.float32).astype(jnp.bfloat16)

    def scores_t(h, slot, j):
        q = q_s[h, j * Q_TILE:(j + 1) * Q_TILE, :]
        sc_t = lax.dot_general(k_s[slot], q, _NT, preferred_element_type=jnp.float32)
        sc_s[j % 2] = sc_t
        top_s[j % 2] = jnp.broadcast_to(jnp.max(sc_t, axis=0, keepdims=True), (F32_ROWS, Q_TILE))

    def finish(h, slot, j):
        p_t = jnp.exp2(sc_s[j % 2] - top_s[j % 2][0:1, :]).astype(jnp.bfloat16)
        o_t = _bdot(vt_s[slot], p_t)
        o_ref[h, j * Q_TILE:(j + 1) * Q_TILE, :] = (
            o_t[0:MLA_DV, :] / o_t[MLA_DV:MLA_DV + 1, :]).T.astype(o_ref.dtype)

    build_kv(0, 0)
    scores_t(0, 0, 0)

    def head_pair(i, carry):
        for slot in range(2):
            h = 2 * i + slot
            h_next = jnp.minimum(h + 1, MLA_HEADS - 1)
            for j in range(n_tiles):
                if j + 1 < n_tiles:
                    scores_t(h, slot, j + 1)
                else:
                    build_kv(h_next, 1 - slot)
                    scores_t(h_next, 1 - slot, 0)
                finish(h, slot, j)
        return carry

    lax.fori_loop(0, MLA_HEADS // 2, head_pair, 0)


def _mla_call(layer, mq, mkv, misc, tabs, gq, gkv, wuq_p, wuk_p, wuvt_p):
    bsz, s, _ = mq.shape

    def per_row(width):
        return pl.BlockSpec((None, s, width), lambda b: (b, 0, 0))

    return pl.pallas_call(
        functools.partial(_mla_kernel, layer),
        grid=(bsz,),
        in_specs=[
            per_row(MLA_Q_LORA), per_row(MLA_KV_LORA), per_row(MISC_W),
            per_row(LANES), per_row(LANES), per_row(LANES),
            _whole_spec(gq), _whole_spec(gkv),
            _layer_spec(layer, (MLA_Q_LORA, MLA_HEADS * QK_W)),
            _layer_spec(layer, (MLA_HEADS, MLA_KV_LORA, MLA_NOPE)),
            _layer_spec(layer, (MLA_HEADS, MLA_DV, MLA_KV_LORA)),
        ],
        out_specs=pl.BlockSpec((None, MLA_HEADS, s, MLA_DV), lambda b: (b, 0, 0, 0)),
        out_shape=jax.ShapeDtypeStruct((bsz, MLA_HEADS, s, MLA_DV), jnp.bfloat16),
        scratch_shapes=[
            pltpu.VMEM((s, MLA_KV_LORA), jnp.bfloat16),
            pltpu.VMEM((MLA_HEADS, s, QK_W), jnp.bfloat16),
            pltpu.VMEM((2, s, QK_W), jnp.bfloat16),
            pltpu.VMEM((2, MLA_DV + BF16_ROWS, s), jnp.bfloat16),
            pltpu.VMEM((2, s, Q_TILE), jnp.float32),
            pltpu.VMEM((2, F32_ROWS, Q_TILE), jnp.float32),
        ],
        compiler_params=pltpu.CompilerParams(
            dimension_semantics=("arbitrary",),
            vmem_limit_bytes=VMEM_LIMIT),
        name="mla",
    )(mq, mkv, misc, *tabs, gq, gkv, wuq_p, wuk_p, wuvt_p)


def _outproj_kernel(layer, tiles_per_seq, final, with_next_mod, x_ref, mod_ref, ogla_ref, omla_ref,
                    cb_ref, u_ref, up_ref, un_ref, sz_ref, mg_ref, cw_ref, cg_ref, fg_ref, w_ref,
                    *refs):
    if with_next_mod:
        c_ref, ada_w_ref, ada_b_ref, o_ref, next_mod_ref = refs
        _ada_kernel(c_ref, ada_w_ref, ada_b_ref, next_mod_ref)
    else:
        (o_ref,) = refs
    f32 = jnp.float32
    j = pl.program_id(0) % tiles_per_seq
    tm = x_ref.shape[0]

    u = u_ref[...].astype(f32)
    prev_ok = (j > 0).astype(f32)
    next_ok = (j < tiles_per_seq - 1).astype(f32)
    u_prev = up_ref[...].astype(f32)[HALO_ROWS - 1:, :] * prev_ok
    u_next = un_ref[...].astype(f32)[0:1, :] * next_ok
    rows = lax.broadcasted_iota(jnp.int32, u.shape, 0)
    up = jnp.where(rows == 0, u_prev, pltpu.roll(u, 1, axis=0))
    un = jnp.where(rows == tm - 1, u_next, pltpu.roll(u, tm - 1, axis=0))
    cw = cw_ref[...]
    conv = up * cw[0:1] + u * cw[1:2] + un * cw[2:3]
    gate = mod_ref[pl.ds(pl.program_id(0) // tiles_per_seq, 1), :][:, 2 * D_MODEL:]

    def gated_dot(r, lo, y_part):
        hi = lo + y_part.shape[1]
        y = (y_part * sz_ref[r, lo:hi].astype(f32)).astype(jnp.bfloat16)
        return _bdot(y, w_ref[lo:hi, :])

    def finish(r, acc):
        out = x_ref[r, :] + gate * acc
        if final:
            out = _rms(out, fg_ref[...])
        o_ref[r, :] = out

    pending = None
    for k in range(tm // OUT_SUB_TILE):
        r = slice(k * OUT_SUB_TILE, (k + 1) * OUT_SUB_TILE)
        acc = gated_dot(r, 0, ogla_ref[r, :].astype(f32))
        if pending is not None:
            finish(*pending)
        omla = jnp.concatenate([omla_ref[h, r, :] for h in range(MLA_HEADS)], axis=1)
        acc += gated_dot(r, GLA_WIDTH, _rms(omla.astype(f32), mg_ref[layer:layer + 1, :]))
        acc += gated_dot(r, GLA_WIDTH + MLA_WIDTH,
                         _rms(cb_ref[r, :].astype(f32) * conv[r, :], cg_ref[layer:layer + 1, :]))
        pending = (r, acc)
    finish(*pending)


def _outproj_call(layer, x2, mod, o_gla, o_mla, cb, u, sz, mla_out_g, conv_w, conv_out_g,
                  final_g, w_out_b, seq_len, final, next_ada=None):
    t = x2.shape[0]
    tm = OUT_ROW_TILE
    n_steps = t // tm
    tiles_per_seq = seq_len // tm
    halo_per_tile = tm // HALO_ROWS
    n_halo = t // HALO_ROWS

    def row_spec(w):
        return pl.BlockSpec((tm, w), lambda i: (i, 0))

    prev_spec = pl.BlockSpec((HALO_ROWS, CONV_CH),
                             lambda i: (jnp.maximum(i * halo_per_tile - 1, 0), 0))
    next_spec = pl.BlockSpec((HALO_ROWS, CONV_CH),
                             lambda i: (jnp.minimum((i + 1) * halo_per_tile, n_halo - 1), 0))

    in_specs = [
        row_spec(D_MODEL),
        _whole_spec(mod),
        row_spec(GLA_WIDTH),
        pl.BlockSpec((None, MLA_HEADS, tm, MLA_DV),
                     lambda i: (i // tiles_per_seq, 0, i % tiles_per_seq, 0)),
        row_spec(CONV_CH), row_spec(CONV_CH), prev_spec, next_spec,
        row_spec(D_MODEL),
        _whole_spec(mla_out_g), _layer_spec(layer, (3, CONV_CH)), _whole_spec(conv_out_g),
        pl.BlockSpec((1, D_MODEL), lambda i: (0, 0)),
        pl.BlockSpec((D_MODEL, D_MODEL), lambda i: (0, 0), pipeline_mode=pl.Buffered(1)),
    ]
    operands = [x2, mod, o_gla, o_mla, cb, u, u, u, sz,
                mla_out_g, conv_w, conv_out_g, final_g, w_out_b]
    out_specs = [row_spec(D_MODEL)]
    out_shape = [jax.ShapeDtypeStruct((t, D_MODEL), jnp.float32)]
    if next_ada is not None:
        n_mod = next_ada[1].shape[-1]
        ada_in, ada_out = _ada_specs(layer + 1, n_mod // n_steps)
        in_specs += ada_in
        operands += list(next_ada)
        out_specs.append(ada_out)
        out_shape.append(jax.ShapeDtypeStruct((F32_ROWS, n_mod), jnp.float32))
    return pl.pallas_call(
        functools.partial(_outproj_kernel, layer, tiles_per_seq, final, next_ada is not None),
        grid=(n_steps,),
        in_specs=in_specs,
        out_specs=out_specs,
        out_shape=out_shape,
        compiler_params=pltpu.CompilerParams(
            dimension_semantics=("arbitrary",), vmem_limit_bytes=VMEM_LIMIT),
        name="out_proj",
    )(*operands)


_O_GLR = 2 * GLA_HEADS * GLA_DK + GLA_WIDTH
_O_MQ = _O_GLR + 2 * GLA_GATE_RANK
_O_MKV = _O_MQ + MLA_Q_LORA
_O_MKR = _O_MKV + MLA_KV_LORA
_O_CB = _O_MKR + MLA_ROPE
IN_DIM = _O_CB + 3 * CONV_CH + D_MODEL
W_PREP_ROWS = 256


W_PREP_CHUNK = 512


def _w_in_prep_kernel(wt_ref, o_ref):
    rows = o_ref.shape[0]
    eye = (lax.broadcasted_iota(jnp.int32, (rows, rows), 0)
           == lax.broadcasted_iota(jnp.int32, (rows, rows), 1)).astype(jnp.bfloat16)

    def put(dst, block):
        o_ref[:, dst:dst + block.shape[0]] = lax.dot_general(
            eye, block.astype(jnp.bfloat16), _NT,
            preferred_element_type=jnp.float32).astype(jnp.bfloat16)

    def put_range(dst, lo, hi):
        for start in range(lo, hi, W_PREP_CHUNK):
            stop = min(start + W_PREP_CHUNK, hi)
            put(dst + start - lo, wt_ref[start:stop, :])

    pad_rows = MISC_W - MLA_ROPE - 2 * GLA_GATE_RANK
    misc = jnp.concatenate([wt_ref[_O_MKR:_O_CB, :], wt_ref[_O_GLR:_O_MQ, :],
                            jnp.zeros((pad_rows, rows), jnp.float32)], axis=0)
    off = 0
    for src in ((0, _O_GLR), (_O_MQ, _O_MKV), misc, (_O_MKV, _O_MKR), (_O_CB, IN_DIM)):
        if isinstance(src, tuple):
            put_range(off, *src)
            off += src[1] - src[0]
        else:
            put(off, src)
            off += src.shape[0]


def _first_prep_kernel(wt_ref, pos_ref, invf_ref, o_ref, c_ref, sa_ref, sb_ref):
    _w_in_prep_kernel(wt_ref, o_ref)
    _rope_tables(pos_ref, invf_ref, c_ref, sa_ref, sb_ref)


def _first_prep_call(w_in_t, positions):
    _, n, d = w_in_t.shape
    n_steps = d // W_PREP_ROWS
    t = positions.size
    tile = t // n_steps
    pairs, invf = _rope_table_inputs(positions, tile)
    tab_spec = pl.BlockSpec((tile, LANES), lambda i: (i, 0))
    tab_shape = jax.ShapeDtypeStruct((t, LANES), jnp.float32)
    w_in_p, *tabs = pl.pallas_call(
        _first_prep_kernel,
        grid=(n_steps,),
        in_specs=[pl.BlockSpec((None, n, W_PREP_ROWS), lambda i: (0, 0, i)),
                  pl.BlockSpec((None, tile // 2, 2), lambda i: (i, 0, 0)),
                  pl.BlockSpec((1, LANES), lambda i: (0, 0))],
        out_specs=[pl.BlockSpec((W_PREP_ROWS, IN_DIM_P), lambda i: (i, 0))] + [tab_spec] * 3,
        out_shape=[jax.ShapeDtypeStruct((d, IN_DIM_P), jnp.bfloat16)] + [tab_shape] * 3,
        compiler_params=pltpu.CompilerParams(
            dimension_semantics=("arbitrary",), vmem_limit_bytes=VMEM_LIMIT),
        name="w_in_prep",
    )(w_in_t, pairs, invf)
    return w_in_p, tabs


def _gate_params(wg_f, bg_f, wg_b, bg_b):
    depth, rank, hk = wg_f.shape
    zeros = jnp.zeros((depth, rank, hk), jnp.float32)
    rows = jnp.concatenate([jnp.concatenate([wg_f, zeros], axis=2),
                            jnp.concatenate([zeros, wg_b], axis=2)], axis=1)
    wg_cat = jnp.pad(rows, ((0, 0), (MLA_ROPE, MISC_W - MLA_ROPE - 2 * rank), (0, 0)))
    bg_cat = jnp.concatenate([bg_f, bg_b], axis=1)[:, None, :]
    return wg_cat.astype(jnp.bfloat16), bg_cat


def _permute_w_uq(w):
    depth = w.shape[0]
    half = MLA_ROPE // 2
    w4 = w.reshape(depth, MLA_Q_LORA, MLA_HEADS, MLA_NOPE + MLA_ROPE)
    nope, x1, x2 = w4[..., :MLA_NOPE], w4[..., MLA_NOPE:MLA_NOPE + half], w4[..., MLA_NOPE + half:]
    cols = jnp.concatenate([nope, x1, x2, x2, x1], axis=-1)
    return cols.reshape(depth, MLA_Q_LORA, MLA_HEADS * QK_W).astype(jnp.bfloat16)


def _split_w_ukv(w):
    depth = w.shape[0]
    w4 = w.reshape(depth, MLA_KV_LORA, MLA_HEADS, MLA_NOPE + MLA_DV)
    wk = w4[..., :MLA_NOPE].transpose(0, 2, 1, 3)
    wv_t = w4[..., MLA_NOPE:].transpose(0, 2, 3, 1)
    return wk.astype(jnp.bfloat16), wv_t.astype(jnp.bfloat16)


def kernel(x, c, positions, ada_w, ada_b, norm_g, w_in, gla_wg_f, gla_bg_f, gla_wg_b, gla_bg_b,
           gla_norm_g, mla_q_norm_g, mla_kv_norm_g, mla_w_uq, mla_w_ukv, mla_out_g, conv_w,
           conv_out_g, w_out, final_g):
    bsz, s, d = x.shape
    t = bsz * s

    ada = (jnp.pad(c, ((0, F32_ROWS - bsz), (0, 0))), ada_w, ada_b[:, None, :])
    mod = _ada_call(0, *ada)

    def b3(a):
        return a.reshape(bsz, s, a.shape[-1])

    w_in_t = jnp.swapaxes(w_in, 1, 2)
    w_in_p, tabs = _first_prep_call(w_in_t, positions)
    tabs = tuple(b3(tab) for tab in tabs)
    wg_cat, bg_cat = _gate_params(gla_wg_f, gla_bg_f, gla_wg_b, gla_bg_b)
    wuq_p = _permute_w_uq(mla_w_uq)
    wuk_p, wuvt_p = _split_w_ukv(mla_w_ukv)

    h = x.reshape(t, d)
    for l in range(DEPTH):
        segs, w_out_b = _inproj_call(l, h, mod, norm_g, w_in_p, w_out, s)
        gq, gk, gv, mq, misc, mkv, cb, u, sz = segs
        o_gla, *next_w = _gla_call(l, b3(gq), b3(gk), b3(gv), b3(misc), wg_cat, bg_cat,
                                   gla_norm_g, w_in_t if l + 1 < DEPTH else None)
        if next_w:
            w_in_p = next_w[0]
        o_mla = _mla_call(l, b3(mq), b3(mkv), b3(misc), tabs, mla_q_norm_g, mla_kv_norm_g,
                          wuq_p, wuk_p, wuvt_p)
        h, *next_mod = _outproj_call(l, h, mod, o_gla.reshape(t, -1), o_mla, cb, u, sz,
                                     mla_out_g, conv_w, conv_out_g, final_g.reshape(1, d),
                                     w_out_b, s, l == DEPTH - 1,
                                     ada if l + 1 < DEPTH else None)
        if next_mod:
            mod = next_mod[0]
    return h.reshape(bsz, s, d)
```

```python
import functools
import math

import jax
import jax.numpy as jnp
from jax import lax
from jax.experimental import pallas as pl
from jax.experimental.pallas import tpu as pltpu

D_MODEL = 2048
DEPTH = 2
GLA_HEADS = 6
GLA_DK = 64
GLA_DV = 128
GLA_GATE_RANK = 16
GLA_GATE_TEMP = 16.0
GLA_CHUNK = 64
GLA_WIDTH = GLA_HEADS * GLA_DV
MLA_HEADS = 6
MLA_Q_LORA = 384
MLA_KV_LORA = 256
MLA_NOPE = 128
MLA_ROPE = 64
MLA_DV = 128
MLA_WIDTH = MLA_HEADS * MLA_DV
ROPE_THETA = 10000.0
CONV_CH = D_MODEL - GLA_WIDTH - MLA_WIDTH
EPS = 1e-6

LANES = 128
MXU_COLS = 256
F32_ROWS = 8
BF16_ROWS = 16
HALO_ROWS = BF16_ROWS
MISC_W = LANES
QK_W = 2 * LANES

IN_SEGMENTS = (
    ("gq", GLA_HEADS * GLA_DK, jnp.bfloat16),
    ("gk", GLA_HEADS * GLA_DK, jnp.bfloat16),
    ("gv", GLA_WIDTH, jnp.bfloat16),
    ("mq", MLA_Q_LORA, jnp.bfloat16),
    ("misc", MISC_W, jnp.float32),
    ("mkv", MLA_KV_LORA, jnp.bfloat16),
    ("cb", CONV_CH, jnp.bfloat16),
    ("cc", CONV_CH, jnp.bfloat16),
    ("cx", CONV_CH, jnp.bfloat16),
    ("z", D_MODEL, jnp.bfloat16),
)
IN_DIM_P = sum(w for _, w, _ in IN_SEGMENTS)
_SEGMENT_COLS = {}
for _name, _width, _ in IN_SEGMENTS:
    _SEGMENT_COLS[_name] = (sum(w for _, w in _SEGMENT_COLS.values()), _width)
IN_OUTPUTS = tuple(seg for seg in IN_SEGMENTS if seg[0] not in ("cc", "cx", "z")) + (
    ("u", CONV_CH, jnp.bfloat16),
    ("sz", D_MODEL, jnp.bfloat16),
)

IN_ROW_TILE = 512
IN_SUB_TILE = 256
OUT_ROW_TILE = 512
OUT_SUB_TILE = 256
Q_TILE = 512
KV_ROWS = 512
SCAN_ROWS = 256
SCAN_GROUP = 8
GLA_GROUP = 16
VMEM_LIMIT = 56 * 1024 * 1024

_NT = (((1,), (1,)), ((), ()))
_TN = (((0,), (0,)), ((), ()))


def _rms(x, g):
    ms = jnp.mean(x * x, axis=-1, keepdims=True)
    return x * lax.rsqrt(ms + EPS) * g


def _bdot(a, b):
    return jnp.dot(a, b, preferred_element_type=jnp.float32)


ADA_COLS = 1024


def _split_bf16(a):
    high = a.astype(jnp.bfloat16)
    return high, (a - high.astype(jnp.float32)).astype(jnp.bfloat16)


def _ada_kernel(c_ref, w_ref, b_ref, o_ref):
    c = c_ref[...]
    act = jnp.concatenate(_split_bf16(c * jax.nn.sigmoid(c)), axis=0)
    w_high, w_rest = _split_bf16(w_ref[...])
    both = _bdot(act, w_high) + _bdot(act, w_rest)
    o_ref[...] = both[0:F32_ROWS] + both[F32_ROWS:] + b_ref[...]


def _ada_specs(layer, tn):
    in_specs = [pl.BlockSpec((F32_ROWS, D_MODEL), lambda j: (0, 0)),
                pl.BlockSpec((None, D_MODEL, tn), lambda j: (layer, 0, j)),
                pl.BlockSpec((None, 1, tn), lambda j: (layer, 0, j))]
    return in_specs, pl.BlockSpec((F32_ROWS, tn), lambda j: (0, j))


def _ada_call(layer, c_pad, ada_w, ada_b):
    n = ada_w.shape[-1]
    in_specs, out_spec = _ada_specs(layer, ADA_COLS)
    return pl.pallas_call(
        _ada_kernel,
        grid=(n // ADA_COLS,),
        in_specs=in_specs,
        out_specs=out_spec,
        out_shape=jax.ShapeDtypeStruct((F32_ROWS, n), jnp.float32),
        compiler_params=pltpu.CompilerParams(
            dimension_semantics=("arbitrary",), vmem_limit_bytes=VMEM_LIMIT),
        name="ada_mod",
    )(c_pad, ada_w, ada_b)


def _rope_tables(pos_ref, invf_ref, c_ref, sa_ref, sb_ref):
    rows = pos_ref.shape[0]
    lane = lax.broadcasted_iota(jnp.int32, (rows, LANES), 1)
    pos = jnp.where(lane < MLA_ROPE, pos_ref[:, 0:1], pos_ref[:, 1:2]).astype(jnp.float32)
    ang = pos * invf_ref[...]
    half = MLA_ROPE // 2
    in_first, in_second = lane < half, (lane >= half) & (lane < MLA_ROPE)
    cos2, sin2 = jnp.cos(ang), jnp.sin(ang)
    for r, shift in ((slice(0, rows), 0), (slice(rows, 2 * rows), MLA_ROPE)):
        cos = cos2 if shift == 0 else pltpu.roll(cos2, shift, axis=1)
        sin = sin2 if shift == 0 else pltpu.roll(sin2, shift, axis=1)
        c_ref[r, :] = jnp.where(lane < MLA_ROPE, cos, 0.0)
        sa_ref[r, :] = jnp.where(in_first, -sin, 0.0)
        sb_ref[r, :] = jnp.where(in_second, sin, 0.0)


def _rope_table_inputs(positions, tile_rows):
    inv_freq = ROPE_THETA ** (-jnp.arange(0, MLA_ROPE, 2, dtype=jnp.float32) / MLA_ROPE)
    invf = jnp.tile(inv_freq, LANES // inv_freq.shape[0]).reshape(1, LANES)
    pairs = positions.reshape(-1, 2, tile_rows // 2).transpose(0, 2, 1)
    return pairs, invf


def _rope(x, c, sa, sb):
    return (x * c + pltpu.roll(x, LANES - MLA_ROPE // 2, axis=1) * sa
            + pltpu.roll(x, MLA_ROPE // 2, axis=1) * sb)


def _inproj_kernel(layer, tiles_per_seq, x_ref, mod_ref, g_ref, w_ref, wout_ref, *refs):
    out_refs, wout_b_ref, h_s = refs[:-2], refs[-2], refs[-1]
    wout_b_ref[...] = wout_ref[...].astype(jnp.bfloat16)
    mod = mod_ref[pl.ds(pl.program_id(0) // tiles_per_seq, 1), :]
    shift = mod[:, 0:D_MODEL]
    gain = g_ref[layer:layer + 1, :] * (1.0 + mod[:, D_MODEL:2 * D_MODEL])
    n_sub = x_ref.shape[0] // IN_SUB_TILE

    def normalize(k):
        x = x_ref[k * IN_SUB_TILE:(k + 1) * IN_SUB_TILE, :]
        ms = jnp.mean(x * x, axis=-1, keepdims=True)
        h_s[k % 2] = (x * lax.rsqrt(ms + EPS) * gain + shift).astype(jnp.bfloat16)

    outs = {name: o_ref for (name, _, _), o_ref in zip(IN_OUTPUTS, out_refs)}

    def project(k):
        r = slice(k * IN_SUB_TILE, (k + 1) * IN_SUB_TILE)

        def proj(names):
            lo = _SEGMENT_COLS[names[0]][0]
            width = sum(_SEGMENT_COLS[n][1] for n in names)
            assert width % MXU_COLS == 0
            val = _bdot(h_s[k % 2], w_ref[:, lo:lo + width])
            return {n: val[:, _SEGMENT_COLS[n][0] - lo:_SEGMENT_COLS[n][0] - lo + _SEGMENT_COLS[n][1]]
                    for n in names}

        z = proj(("z",))["z"]
        outs["sz"][r, :] = (z * jax.nn.sigmoid(z)).astype(outs["sz"].dtype)
        conv = proj(("cb", "cc", "cx"))
        outs["u"][r, :] = (conv["cc"] * conv["cx"]).astype(outs["u"].dtype)
        outs["cb"][r, :] = conv["cb"].astype(outs["cb"].dtype)
        for names in (("gq", "gk", "gv"), ("mq", "misc", "mkv")):
            for name, val in proj(names).items():
                outs[name][r, :] = val.astype(outs[name].dtype)

    normalize(0)
    for k in range(n_sub):
        project(k)
        if k + 1 < n_sub:
            normalize(k + 1)


def _layer_spec(layer, shape):
    return pl.BlockSpec((None,) + shape, lambda *_: (layer,) + (0,) * len(shape))


def _whole_spec(arr):
    return pl.BlockSpec(arr.shape, lambda *_: (0,) * arr.ndim)


def _inproj_call(layer, x2, mod, norm_g, w_in_p, w_out, seq_len):
    t = x2.shape[0]
    tm = IN_ROW_TILE
    n_steps = t // tm
    tiles_per_seq = seq_len // tm
    d_out = w_out.shape[-1]
    wout_rows = w_out.shape[1] // n_steps
    n_out = len(IN_OUTPUTS)
    in_specs = [
        pl.BlockSpec((tm, D_MODEL), lambda i: (i, 0)),
        _whole_spec(mod),
        _whole_spec(norm_g),
        pl.BlockSpec((D_MODEL, IN_DIM_P), lambda i: (0, 0), pipeline_mode=pl.Buffered(1)),
        pl.BlockSpec((None, wout_rows, d_out), lambda i: (layer, i, 0)),
    ]
    operands = [x2, mod, norm_g, w_in_p, w_out]
    out_shape = [jax.ShapeDtypeStruct((t, w), dt) for _, w, dt in IN_OUTPUTS]
    out_specs = [pl.BlockSpec((tm, w), lambda i: (i, 0)) for _, w, _ in IN_OUTPUTS]
    out_shape.append(jax.ShapeDtypeStruct(w_out.shape[1:], jnp.bfloat16))
    out_specs.append(pl.BlockSpec((wout_rows, d_out), lambda i: (i, 0)))
    outs = pl.pallas_call(
        functools.partial(_inproj_kernel, layer, tiles_per_seq),
        grid=(n_steps,),
        in_specs=in_specs,
        out_specs=out_specs,
        out_shape=out_shape,
        scratch_shapes=[pltpu.VMEM((2, IN_SUB_TILE, D_MODEL), jnp.bfloat16)],
        compiler_params=pltpu.CompilerParams(
            dimension_semantics=("arbitrary",), vmem_limit_bytes=VMEM_LIMIT),
        name="in_proj",
    )(*operands)
    return outs[:n_out], outs[n_out]


def _log_sigmoid(x):
    return jnp.minimum(x, 0.0) - jnp.log(1.0 + jnp.exp(-jnp.abs(x)))


def _gla_kernel(layer, n_w_slabs, q_ref, k_ref, v_ref, misc_ref, wgf_ref, bgf_ref, wgb_ref,
                bgb_ref, g_ref, *refs):
    if n_w_slabs:
        wt_ref, o_ref, w_prep_ref = refs[0:3]
        step = pl.program_id(0) * pl.num_programs(1) + pl.program_id(1)

        @pl.when(step < n_w_slabs)
        def _prep_next_layer_weight():
            _w_in_prep_kernel(wt_ref, w_prep_ref)

        refs = refs[3:]
    else:
        o_ref, refs = refs[0], refs[1:]
    bf_s, bb_s, qcat_s, kv_s, dec_s, st_s, intra_s = refs
    s = q_ref.shape[0]
    cdim = GLA_CHUNK
    n_chunks = s // cdim
    pair_dk = 2 * GLA_DK
    pair_dv = 2 * GLA_DV
    inv_t = 1.0 / GLA_GATE_TEMP
    q_scale = GLA_DK ** -0.5

    assert math.frexp(inv_t)[0] == 0.5
    t_row = lax.broadcasted_iota(jnp.int32, (SCAN_ROWS, SCAN_ROWS), 0)
    t_col = lax.broadcasted_iota(jnp.int32, (SCAN_ROWS, SCAN_ROWS), 1)
    same_chunk = (t_row // cdim) == (t_col // cdim)
    tri_prefix = jnp.where(same_chunk & (t_col <= t_row), inv_t, 0.0).astype(jnp.bfloat16)
    tri_suffix = jnp.where(same_chunk & (t_col >= t_row), inv_t, 0.0).astype(jnp.bfloat16)

    def split(terms):
        return jnp.concatenate(_split_bf16(terms), axis=1)

    def scan_body(i, carry):
        blocks = [pl.ds(pl.multiple_of((i * SCAN_GROUP + c) * SCAN_ROWS, SCAN_ROWS), SCAN_ROWS)
                  for c in range(SCAN_GROUP)]
        pre = []
        for r in blocks:
            m = misc_ref[r, :].astype(jnp.bfloat16)
            pre.append((_bdot(m, wgf_ref[...]), _bdot(m, wgb_ref[...])))
        terms = [(split(_log_sigmoid(pf + bgf_ref[...])),
                  split(_log_sigmoid(pb + bgb_ref[...]))) for pf, pb in pre]
        sums = [(_bdot(tri_prefix, tf), _bdot(tri_suffix, tb)) for tf, tb in terms]
        for r, (sf, sb) in zip(blocks, sums):
            bf_s[r, :] = sf[:, 0:pair_dk] + sf[:, pair_dk:]
            bb_s[r, :] = sb[:, 0:pair_dk] + sb[:, pair_dk:]
        return carry

    lax.fori_loop(0, s // (SCAN_ROWS * SCAN_GROUP), scan_body, 0)

    row = lax.broadcasted_iota(jnp.int32, (cdim, pair_dk), 0)
    key_pos = lax.broadcasted_iota(jnp.int32, (cdim, pair_dk), 1) & (cdim - 1)
    keep_f = key_pos <= row
    keep_b = key_pos > row
    cat_block = lax.broadcasted_iota(jnp.int32, (cdim, 2 * pair_dk), 1) // GLA_DK
    v_head = lax.broadcasted_iota(jnp.int32, (cdim, pair_dv), 1) // GLA_DV
    head0_lanes = (lax.broadcasted_iota(jnp.int32, (GLA_DV, 2 * pair_dk), 1) // GLA_DK) % 2 == 0

    def intra_body(i, carry):
        chunks = [i * GLA_GROUP + c for c in range(GLA_GROUP)]
        rows = [pl.ds(pl.multiple_of(n * cdim, cdim), cdim) for n in chunks]
        scores, k_ends = [], []
        for n, r in zip(chunks, rows):
            bf = bf_s[r, :]
            bb = bb_s[r, :]
            bf_last = bf[cdim - 1:cdim, :]
            bb_last = bb[0:1, :]
            q = q_ref[r, :].astype(jnp.float32) * q_scale
            k = k_ref[r, :].astype(jnp.float32)
            q_cat = jnp.concatenate([q * jnp.exp(bf), q * jnp.exp(bb)],
                                    axis=1).astype(jnp.bfloat16)
            k_inv = jnp.concatenate([k * jnp.exp(-bf), k * jnp.exp(-bb)], axis=1)
            k_ends.append(jnp.concatenate(
                [k * jnp.exp(bf_last - bf), k * jnp.exp(bb_last - bb)], axis=1).astype(jnp.bfloat16))
            dec_s[n] = jnp.concatenate([jnp.exp(bf_last), jnp.exp(bb_last)], axis=1)
            qcat_s[r, :] = q_cat
            k_bd = jnp.concatenate([jnp.where(cat_block == c, k_inv, 0.0) for c in range(4)],
                                   axis=0).astype(jnp.bfloat16)
            scores.append(lax.dot_general(q_cat, k_bd, _NT,
                                          preferred_element_type=jnp.float32))
        for n, r, k_end in zip(chunks, rows, k_ends):
            kv_t = lax.dot_general(v_ref[r, :], k_end, _TN,
                                   preferred_element_type=jnp.float32)
            kv_s[n] = jnp.where(head0_lanes, kv_t[0:GLA_DV, :], kv_t[GLA_DV:, :])
        for r, sc in zip(rows, scores):
            p = (jnp.where(keep_f, sc[:, 0:pair_dk], 0.0)
                 + jnp.where(keep_b, sc[:, pair_dk:], 0.0)).astype(jnp.bfloat16)
            vv = v_ref[r, :]
            v_bd = jnp.concatenate(
                [jnp.where(v_head == j, vv, jnp.zeros_like(vv)) for j in range(2)], axis=0)
            intra_s[r, :] = _bdot(p, v_bd)
        return carry

    lax.fori_loop(0, n_chunks // GLA_GROUP, intra_body, 0)

    def state_body(i, states):
        sf, sb = states
        nf = i
        nb = n_chunks - 1 - i
        st_s[nf, :, 0:pair_dk] = sf.astype(jnp.bfloat16)
        st_s[nb, :, pair_dk:] = sb.astype(jnp.bfloat16)
        sf = sf * dec_s[nf][:, 0:pair_dk] + kv_s[nf, :, 0:pair_dk]
        sb = sb * dec_s[nb][:, pair_dk:] + kv_s[nb, :, pair_dk:]
        return sf, sb

    zero = jnp.zeros((GLA_DV, pair_dk), jnp.float32)
    lax.fori_loop(0, n_chunks, state_body, (zero, zero))

    g = g_ref[layer:layer + 1, :]

    def out_body(i, carry):
        chunks = [i * GLA_GROUP + c for c in range(GLA_GROUP)]
        rows = [pl.ds(pl.multiple_of(n * cdim, cdim), cdim) for n in chunks]
        inter = []
        for n, r in zip(chunks, rows):
            packed = st_s[n]
            none = jnp.zeros_like(packed)
            st_bd = jnp.concatenate([jnp.where(head0_lanes, packed, none),
                                     jnp.where(head0_lanes, none, packed)], axis=0)
            inter.append(lax.dot_general(qcat_s[r, :], st_bd, _NT,
                                         preferred_element_type=jnp.float32))
        for r, o_inter in zip(rows, inter):
            o = intra_s[r, :] + o_inter
            for j in range(2):
                o_ref[r, j * GLA_DV:(j + 1) * GLA_DV] = _rms(
                    o[:, j * GLA_DV:(j + 1) * GLA_DV], g).astype(o_ref.dtype)
        return carry

    lax.fori_loop(0, n_chunks // GLA_GROUP, out_body, 0)


def _gla_call(layer, gq, gk, gv, misc, wg_cat, bg_cat, gla_norm_g, w_in_t=None):
    bsz, s, _ = gq.shape
    pairs = GLA_HEADS // 2
    n_chunks = s // GLA_CHUNK
    pair_dk, pair_dv = 2 * GLA_DK, 2 * GLA_DV
    qk_spec = pl.BlockSpec((None, s, pair_dk), lambda b, p: (b, 0, p))
    v_spec = pl.BlockSpec((None, s, pair_dv), lambda b, p: (b, 0, p))

    def gate_specs(direction):
        return (pl.BlockSpec((None, MISC_W, pair_dk), lambda b, p: (layer, 0, direction * pairs + p)),
                pl.BlockSpec((None, 1, pair_dk), lambda b, p: (layer, 0, direction * pairs + p)))

    in_specs = [qk_spec, qk_spec, v_spec,
                pl.BlockSpec((None, s, MISC_W), lambda b, p: (b, 0, 0)),
                *gate_specs(0), *gate_specs(1),
                _whole_spec(gla_norm_g)]
    operands = [gq, gk, gv, misc, wg_cat, bg_cat, wg_cat, bg_cat, gla_norm_g]
    out_specs = [v_spec]
    out_shape = [jax.ShapeDtypeStruct((bsz, s, GLA_WIDTH), jnp.bfloat16)]
    n_w_slabs = 0
    if w_in_t is not None:
        n_w_slabs = D_MODEL // W_PREP_ROWS
        assert n_w_slabs <= bsz * pairs

        def slab(b, p):
            return jnp.minimum(b * pairs + p, n_w_slabs - 1)

        in_specs.append(pl.BlockSpec((None, IN_DIM, W_PREP_ROWS),
                                     lambda b, p: (layer + 1, 0, slab(b, p))))
        operands.append(w_in_t)
        out_specs.append(pl.BlockSpec((W_PREP_ROWS, IN_DIM_P), lambda b, p: (slab(b, p), 0)))
        out_shape.append(jax.ShapeDtypeStruct((D_MODEL, IN_DIM_P), jnp.bfloat16))

    return pl.pallas_call(
        functools.partial(_gla_kernel, layer, n_w_slabs),
        grid=(bsz, pairs),
        in_specs=in_specs,
        out_specs=out_specs,
        out_shape=out_shape,
        scratch_shapes=[
            pltpu.VMEM((s, pair_dk), jnp.float32),
            pltpu.VMEM((s, pair_dk), jnp.float32),
            pltpu.VMEM((s, 2 * pair_dk), jnp.bfloat16),
            pltpu.VMEM((n_chunks, GLA_DV, 2 * pair_dk), jnp.float32),
            pltpu.VMEM((n_chunks, 1, 2 * pair_dk), jnp.float32),
            pltpu.VMEM((n_chunks, GLA_DV, 2 * pair_dk), jnp.bfloat16),
            pltpu.VMEM((s, pair_dv), jnp.float32),
        ],
        compiler_params=pltpu.CompilerParams(
            dimension_semantics=("arbitrary", "arbitrary"), vmem_limit_bytes=VMEM_LIMIT),
        name="gla",
    )(*operands)


def _mla_kernel(layer, mq_ref, mkv_ref, misc_ref, c_ref, sa_ref, sb_ref, gq_ref, gkv_ref,
                wuq_ref, wuk_ref, wuvt_ref, o_ref, ckv_s, q_s, k_s, vt_s, sc_s, top_s):
    s = mkv_ref.shape[0]
    n_tiles = s // Q_TILE
    assert n_tiles % 2 == 0 and MLA_HEADS % 2 == 0
    blocks = [slice(i * KV_ROWS, (i + 1) * KV_ROWS) for i in range(s // KV_ROWS)]
    q_mult = (MLA_NOPE + MLA_ROPE) ** -0.5 * math.log2(math.e)

    for r in blocks:
        tabs = (c_ref[r, :], sa_ref[r, :], sb_ref[r, :])
        ckv_s[r, :] = _rms(mkv_ref[r, :].astype(jnp.float32),
                           gkv_ref[layer:layer + 1, :]).astype(jnp.bfloat16)
        k_rope = _rope(misc_ref[r, :], *tabs).astype(jnp.bfloat16)
        for slot in range(2):
            k_s[slot, r, MLA_NOPE:] = k_rope
        cq = _rms(mq_ref[r, :].astype(jnp.float32),
                  gq_ref[layer:layer + 1, :]).astype(jnp.bfloat16)
        qf = _bdot(cq, wuq_ref[...])
        cos_t, sin_t = tabs[0], tabs[1] + tabs[2]
        for h in range(MLA_HEADS):
            qh = qf[:, h * QK_W:(h + 1) * QK_W]
            rot = qh[:, MLA_NOPE:]
            q_rope = rot * cos_t + pltpu.roll(rot, MLA_ROPE, axis=1) * sin_t
            q_s[h, r, :] = (jnp.concatenate([qh[:, 0:MLA_NOPE], q_rope], axis=1)
                            * q_mult).astype(jnp.bfloat16)
    for slot in range(2):
        vt_s[slot, MLA_DV:, :] = jnp.ones((vt_s.shape[1] - MLA_DV, s), jnp.bfloat16)

    def build_kv(h, slot):
        for r in blocks:
            ckv = ckv_s[r, :]
            k_s[slot, r, 0:MLA_NOPE] = _bdot(ckv, wuk_ref[h]).astype(jnp.bfloat16)
            vt_s[slot, 0:MLA_DV, r] = lax.dot_general(
                wuvt_ref[h], ckv, _NT, preferred_element_type=jnp.float32).astype(jnp.bfloat16)

    def scores_t(h, slot, j):
        q = q_s[h, j * Q_TILE:(j + 1) * Q_TILE, :]
        sc_t = lax.dot_general(k_s[slot], q, _NT, preferred_element_type=jnp.float32)
        sc_s[j % 2] = sc_t
        top_s[j % 2] = jnp.broadcast_to(jnp.max(sc_t, axis=0, keepdims=True), (F32_ROWS, Q_TILE))

    def finish(h, slot, j):
        p_t = jnp.exp2(sc_s[j % 2] - top_s[j % 2][0:1, :]).astype(jnp.bfloat16)
        o_t = _bdot(vt_s[slot], p_t)
        o_ref[h, j * Q_TILE:(j + 1) * Q_TILE, :] = (
            o_t[0:MLA_DV, :] / o_t[MLA_DV:MLA_DV + 1, :]).T.astype(o_ref.dtype)

    build_kv(0, 0)
    scores_t(0, 0, 0)

    def head_pair(i, carry):
        for slot in range(2):
            h = 2 * i + slot
            h_next = jnp.minimum(h + 1, MLA_HEADS - 1)
            for j in range(n_tiles):
                if j + 1 < n_tiles:
                    scores_t(h, slot, j + 1)
                else:
                    build_kv(h_next, 1 - slot)
                    scores_t(h_next, 1 - slot, 0)
                finish(h, slot, j)
        return carry

    lax.fori_loop(0, MLA_HEADS // 2, head_pair, 0)


def _mla_call(layer, mq, mkv, misc, tabs, gq, gkv, wuq_p, wuk_p, wuvt_p):
    bsz, s, _ = mq.shape

    def per_row(width):
        return pl.BlockSpec((None, s, width), lambda b: (b, 0, 0))

    return pl.pallas_call(
        functools.partial(_mla_kernel, layer),
        grid=(bsz,),
        in_specs=[
            per_row(MLA_Q_LORA), per_row(MLA_KV_LORA), per_row(MISC_W),
            per_row(LANES), per_row(LANES), per_row(LANES),
            _whole_spec(gq), _whole_spec(gkv),
            _layer_spec(layer, (MLA_Q_LORA, MLA_HEADS * QK_W)),
            _layer_spec(layer, (MLA_HEADS, MLA_KV_LORA, MLA_NOPE)),
            _layer_spec(layer, (MLA_HEADS, MLA_DV, MLA_KV_LORA)),
        ],
        out_specs=pl.BlockSpec((None, MLA_HEADS, s, MLA_DV), lambda b: (b, 0, 0, 0)),
        out_shape=jax.ShapeDtypeStruct((bsz, MLA_HEADS, s, MLA_DV), jnp.bfloat16),
        scratch_shapes=[
            pltpu.VMEM((s, MLA_KV_LORA), jnp.bfloat16),
            pltpu.VMEM((MLA_HEADS, s, QK_W), jnp.bfloat16),
            pltpu.VMEM((2, s, QK_W), jnp.bfloat16),
            pltpu.VMEM((2, MLA_DV + BF16_ROWS, s), jnp.bfloat16),
            pltpu.VMEM((2, s, Q_TILE), jnp.float32),
            pltpu.VMEM((2, F32_ROWS, Q_TILE), jnp.float32),
        ],
        compiler_params=pltpu.CompilerParams(
            dimension_semantics=("arbitrary",),
            vmem_limit_bytes=VMEM_LIMIT),
        name="mla",
    )(mq, mkv, misc, *tabs, gq, gkv, wuq_p, wuk_p, wuvt_p)


def _outproj_kernel(layer, tiles_per_seq, final, with_next_mod, x_ref, mod_ref, ogla_ref, omla_ref,
                    cb_ref, u_ref, up_ref, un_ref, sz_ref, mg_ref, cw_ref, cg_ref, fg_ref, w_ref,
                    *refs):
    if with_next_mod:
        c_ref, ada_w_ref, ada_b_ref, o_ref, next_mod_ref = refs
        _ada_kernel(c_ref, ada_w_ref, ada_b_ref, next_mod_ref)
    else:
        (o_ref,) = refs
    f32 = jnp.float32
    j = pl.program_id(0) % tiles_per_seq
    tm = x_ref.shape[0]

    u = u_ref[...].astype(f32)
    prev_ok = (j > 0).astype(f32)
    next_ok = (j < tiles_per_seq - 1).astype(f32)
    u_prev = up_ref[...].astype(f32)[HALO_ROWS - 1:, :] * prev_ok
    u_next = un_ref[...].astype(f32)[0:1, :] * next_ok
    rows = lax.broadcasted_iota(jnp.int32, u.shape, 0)
    up = jnp.where(rows == 0, u_prev, pltpu.roll(u, 1, axis=0))
    un = jnp.where(rows == tm - 1, u_next, pltpu.roll(u, tm - 1, axis=0))
    cw = cw_ref[...]
    conv = up * cw[0:1] + u * cw[1:2] + un * cw[2:3]
    gate = mod_ref[pl.ds(pl.program_id(0) // tiles_per_seq, 1), :][:, 2 * D_MODEL:]

    def gated_dot(r, lo, y_part):
        hi = lo + y_part.shape[1]
        y = (y_part * sz_ref[r, lo:hi].astype(f32)).astype(jnp.bfloat16)
        return _bdot(y, w_ref[lo:hi, :])

    def finish(r, acc):
        out = x_ref[r, :] + gate * acc
        if final:
            out = _rms(out, fg_ref[...])
        o_ref[r, :] = out

    pending = None
    for k in range(tm // OUT_SUB_TILE):
        r = slice(k * OUT_SUB_TILE, (k + 1) * OUT_SUB_TILE)
        acc = gated_dot(r, 0, ogla_ref[r, :].astype(f32))
        if pending is not None:
            finish(*pending)
        omla = jnp.concatenate([omla_ref[h, r, :] for h in range(MLA_HEADS)], axis=1)
        acc += gated_dot(r, GLA_WIDTH, _rms(omla.astype(f32), mg_ref[layer:layer + 1, :]))
        acc += gated_dot(r, GLA_WIDTH + MLA_WIDTH,
                         _rms(cb_ref[r, :].astype(f32) * conv[r, :], cg_ref[layer:layer + 1, :]))
        pending = (r, acc)
    finish(*pending)


def _outproj_call(layer, x2, mod, o_gla, o_mla, cb, u, sz, mla_out_g, conv_w, conv_out_g,
                  final_g, w_out_b, seq_len, final, next_ada=None):
    t = x2.shape[0]
    tm = OUT_ROW_TILE
    n_steps = t // tm
    tiles_per_seq = seq_len // tm
    halo_per_tile = tm // HALO_ROWS
    n_halo = t // HALO_ROWS

    def row_spec(w):
        return pl.BlockSpec((tm, w), lambda i: (i, 0))

    prev_spec = pl.BlockSpec((HALO_ROWS, CONV_CH),
                             lambda i: (jnp.maximum(i * halo_per_tile - 1, 0), 0))
    next_spec = pl.BlockSpec((HALO_ROWS, CONV_CH),
                             lambda i: (jnp.minimum((i + 1) * halo_per_tile, n_halo - 1), 0))

    in_specs = [
        row_spec(D_MODEL),
        _whole_spec(mod),
        row_spec(GLA_WIDTH),
        pl.BlockSpec((None, MLA_HEADS, tm, MLA_DV),
                     lambda i: (i // tiles_per_seq, 0, i % tiles_per_seq, 0)),
        row_spec(CONV_CH), row_spec(CONV_CH), prev_spec, next_spec,
        row_spec(D_MODEL),
        _whole_spec(mla_out_g), _layer_spec(layer, (3, CONV_CH)), _whole_spec(conv_out_g),
        pl.BlockSpec((1, D_MODEL), lambda i: (0, 0)),
        pl.BlockSpec((D_MODEL, D_MODEL), lambda i: (0, 0), pipeline_mode=pl.Buffered(1)),
    ]
    operands = [x2, mod, o_gla, o_mla, cb, u, u, u, sz,
                mla_out_g, conv_w, conv_out_g, final_g, w_out_b]
    out_specs = [row_spec(D_MODEL)]
    out_shape = [jax.ShapeDtypeStruct((t, D_MODEL), jnp.float32)]
    if next_ada is not None:
        n_mod = next_ada[1].shape[-1]
        ada_in, ada_out = _ada_specs(layer + 1, n_mod // n_steps)
        in_specs += ada_in
        operands += list(next_ada)
        out_specs.append(ada_out)
        out_shape.append(jax.ShapeDtypeStruct((F32_ROWS, n_mod), jnp.float32))
    return pl.pallas_call(
        functools.partial(_outproj_kernel, layer, tiles_per_seq, final, next_ada is not None),
        grid=(n_steps,),
        in_specs=in_specs,
        out_specs=out_specs,
        out_shape=out_shape,
        compiler_params=pltpu.CompilerParams(
            dimension_semantics=("arbitrary",), vmem_limit_bytes=VMEM_LIMIT),
        name="out_proj",
    )(*operands)


_O_GLR = 2 * GLA_HEADS * GLA_DK + GLA_WIDTH
_O_MQ = _O_GLR + 2 * GLA_GATE_RANK
_O_MKV = _O_MQ + MLA_Q_LORA
_O_MKR = _O_MKV + MLA_KV_LORA
_O_CB = _O_MKR + MLA_ROPE
IN_DIM = _O_CB + 3 * CONV_CH + D_MODEL
W_PREP_ROWS = 256


W_PREP_CHUNK = 512


def _w_in_prep_kernel(wt_ref, o_ref):
    rows = o_ref.shape[0]
    eye = (lax.broadcasted_iota(jnp.int32, (rows, rows), 0)
           == lax.broadcasted_iota(jnp.int32, (rows, rows), 1)).astype(jnp.bfloat16)

    def put(dst, block):
        o_ref[:, dst:dst + block.shape[0]] = lax.dot_general(
            eye, block.astype(jnp.bfloat16), _NT,
            preferred_element_type=jnp.float32).astype(jnp.bfloat16)

    def put_range(dst, lo, hi):
        for start in range(lo, hi, W_PREP_CHUNK):
            stop = min(start + W_PREP_CHUNK, hi)
            put(dst + start - lo, wt_ref[start:stop, :])

    pad_rows = MISC_W - MLA_ROPE - 2 * GLA_GATE_RANK
    misc = jnp.concatenate([wt_ref[_O_MKR:_O_CB, :], wt_ref[_O_GLR:_O_MQ, :],
                            jnp.zeros((pad_rows, rows), jnp.float32)], axis=0)
    off = 0
    for src in ((0, _O_GLR), (_O_MQ, _O_MKV), misc, (_O_MKV, _O_MKR), (_O_CB, IN_DIM)):
        if isinstance(src, tuple):
            put_range(off, *src)
            off += src[1] - src[0]
        else:
            put(off, src)
            off += src.shape[0]


def _first_prep_kernel(wt_ref, pos_ref, invf_ref, o_ref, c_ref, sa_ref, sb_ref):
    _w_in_prep_kernel(wt_ref, o_ref)
    _rope_tables(pos_ref, invf_ref, c_ref, sa_ref, sb_ref)


def _first_prep_call(w_in_t, positions):
    _, n, d = w_in_t.shape
    n_steps = d // W_PREP_ROWS
    t = positions.size
    tile = t // n_steps
    pairs, invf = _rope_table_inputs(positions, tile)
    tab_spec = pl.BlockSpec((tile, LANES), lambda i: (i, 0))
    tab_shape = jax.ShapeDtypeStruct((t, LANES), jnp.float32)
    w_in_p, *tabs = pl.pallas_call(
        _first_prep_kernel,
        grid=(n_steps,),
        in_specs=[pl.BlockSpec((None, n, W_PREP_ROWS), lambda i: (0, 0, i)),
                  pl.BlockSpec((None, tile // 2, 2), lambda i: (i, 0, 0)),
                  pl.BlockSpec((1, LANES), lambda i: (0, 0))],
        out_specs=[pl.BlockSpec((W_PREP_ROWS, IN_DIM_P), lambda i: (i, 0))] + [tab_spec] * 3,
        out_shape=[jax.ShapeDtypeStruct((d, IN_DIM_P), jnp.bfloat16)] + [tab_shape] * 3,
        compiler_params=pltpu.CompilerParams(
            dimension_semantics=("arbitrary",), vmem_limit_bytes=VMEM_LIMIT),
        name="w_in_prep",
    )(w_in_t, pairs, invf)
    return w_in_p, tabs


def _gate_params(wg_f, bg_f, wg_b, bg_b):
    depth, rank, hk = wg_f.shape
    zeros = jnp.zeros((depth, rank, hk), jnp.float32)
    rows = jnp.concatenate([jnp.concatenate([wg_f, zeros], axis=2),
                            jnp.concatenate([zeros, wg_b], axis=2)], axis=1)
    wg_cat = jnp.pad(rows, ((0, 0), (MLA_ROPE, MISC_W - MLA_ROPE - 2 * rank), (0, 0)))
    bg_cat = jnp.concatenate([bg_f, bg_b], axis=1)[:, None, :]
    return wg_cat.astype(jnp.bfloat16), bg_cat


def _permute_w_uq(w):
    depth = w.shape[0]
    half = MLA_ROPE // 2
    w4 = w.reshape(depth, MLA_Q_LORA, MLA_HEADS, MLA_NOPE + MLA_ROPE)
    nope, x1, x2 = w4[..., :MLA_NOPE], w4[..., MLA_NOPE:MLA_NOPE + half], w4[..., MLA_NOPE + half:]
    cols = jnp.concatenate([nope, x1, x2, x2, x1], axis=-1)
    return cols.reshape(depth, MLA_Q_LORA, MLA_HEADS * QK_W).astype(jnp.bfloat16)


def _split_w_ukv(w):
    depth = w.shape[0]
    w4 = w.reshape(depth, MLA_KV_LORA, MLA_HEADS, MLA_NOPE + MLA_DV)
    wk = w4[..., :MLA_NOPE].transpose(0, 2, 1, 3)
    wv_t = w4[..., MLA_NOPE:].transpose(0, 2, 3, 1)
    return wk.astype(jnp.bfloat16), wv_t.astype(jnp.bfloat16)


def kernel(x, c, positions, ada_w, ada_b, norm_g, w_in, gla_wg_f, gla_bg_f, gla_wg_b, gla_bg_b,
           gla_norm_g, mla_q_norm_g, mla_kv_norm_g, mla_w_uq, mla_w_ukv, mla_out_g, conv_w,
           conv_out_g, w_out, final_g):
    bsz, s, d = x.shape
    t = bsz * s

    ada = (jnp.pad(c, ((0, F32_ROWS - bsz), (0, 0))), ada_w, ada_b[:, None, :])
    mod = _ada_call(0, *ada)

    def b3(a):
        return a.reshape(bsz, s, a.shape[-1])

    w_in_t = jnp.swapaxes(w_in, 1, 2)
    w_in_p, tabs = _first_prep_call(w_in_t, positions)
    tabs = tuple(b3(tab) for tab in tabs)
    wg_cat, bg_cat = _gate_params(gla_wg_f, gla_bg_f, gla_wg_b, gla_bg_b)
    wuq_p = _permute_w_uq(mla_w_uq)
    wuk_p, wuvt_p = _split_w_ukv(mla_w_ukv)

    h = x.reshape(t, d)
    for l in range(DEPTH):
        segs, w_out_b = _inproj_call(l, h, mod, norm_g, w_in_p, w_out, s)
        gq, gk, gv, mq, misc, mkv, cb, u, sz = segs
        o_gla, *next_w = _gla_call(l, b3(gq), b3(gk), b3(gv), b3(misc), wg_cat, bg_cat,
                                   gla_norm_g, w_in_t if l + 1 < DEPTH else None)
        if next_w:
            w_in_p = next_w[0]
        o_mla = _mla_call(l, b3(mq), b3(mkv), b3(misc), tabs, mla_q_norm_g, mla_kv_norm_g,
                          wuq_p, wuk_p, wuvt_p)
        h, *next_mod = _outproj_call(l, h, mod, o_gla.reshape(t, -1), o_mla, cb, u, sz,
                                     mla_out_g, conv_w, conv_out_g, final_g.reshape(1, d),
                                     w_out_b, s, l == DEPTH - 1,
                                     ada if l + 1 < DEPTH else None)
        if next_mod:
            mod = next_mod[0]
    return h.reshape(bsz, s, d)
```

```python
import functools
import math

import jax
import jax.numpy as jnp
from jax import lax
from jax.experimental import pallas as pl
from jax.experimental.pallas import tpu as pltpu

D_MODEL = 2048
DEPTH = 2
GLA_HEADS = 6
GLA_DK = 64
GLA_DV = 128
GLA_GATE_RANK = 16
GLA_GATE_TEMP = 16.0
GLA_CHUNK = 64
GLA_WIDTH = GLA_HEADS * GLA_DV
MLA_HEADS = 6
MLA_Q_LORA = 384
MLA_KV_LORA = 256
MLA_NOPE = 128
MLA_ROPE = 64
MLA_DV = 128
MLA_WIDTH = MLA_HEADS * MLA_DV
ROPE_THETA = 10000.0
CONV_CH = D_MODEL - GLA_WIDTH - MLA_WIDTH
EPS = 1e-6

LANES = 128
MXU_COLS = 256
F32_ROWS = 8
BF16_ROWS = 16
HALO_ROWS = BF16_ROWS
MISC_W = LANES
QK_W = 2 * LANES

IN_SEGMENTS = (
    ("gq", GLA_HEADS * GLA_DK, jnp.bfloat16),
    ("gk", GLA_HEADS * GLA_DK, jnp.bfloat16),
    ("gv", GLA_WIDTH, jnp.bfloat16),
    ("mq", MLA_Q_LORA, jnp.bfloat16),
    ("misc", MISC_W, jnp.float32),
    ("mkv", MLA_KV_LORA, jnp.bfloat16),
    ("cb", CONV_CH, jnp.bfloat16),
    ("cc", CONV_CH, jnp.bfloat16),
    ("cx", CONV_CH, jnp.bfloat16),
    ("z", D_MODEL, jnp.bfloat16),
)
IN_DIM_P = sum(w for _, w, _ in IN_SEGMENTS)
_SEGMENT_COLS = {}
for _name, _width, _ in IN_SEGMENTS:
    _SEGMENT_COLS[_name] = (sum(w for _, w in _SEGMENT_COLS.values()), _width)
IN_OUTPUTS = tuple(seg for seg in IN_SEGMENTS if seg[0] not in ("cc", "cx", "z")) + (
    ("u", CONV_CH, jnp.bfloat16),
    ("sz", D_MODEL, jnp.bfloat16),
)

IN_ROW_TILE = 512
IN_SUB_TILE = 256
OUT_ROW_TILE = 512
OUT_SUB_TILE = 256
Q_TILE = 512
KV_ROWS = 512
SCAN_ROWS = 256
SCAN_GROUP = 8
GLA_GROUP = 32
VMEM_LIMIT = 56 * 1024 * 1024

_NT = (((1,), (1,)), ((), ()))
_TN = (((0,), (0,)), ((), ()))


def _rms(x, g):
    ms = jnp.mean(x * x, axis=-1, keepdims=True)
    return x * lax.rsqrt(ms + EPS) * g


def _bdot(a, b):
    return jnp.dot(a, b, preferred_element_type=jnp.float32)


ADA_COLS = 1024


def _split_bf16(a):
    high = a.astype(jnp.bfloat16)
    return high, (a - high.astype(jnp.float32)).astype(jnp.bfloat16)


def _ada_kernel(c_ref, w_ref, b_ref, o_ref):
    c = c_ref[...]
    act = jnp.concatenate(_split_bf16(c * jax.nn.sigmoid(c)), axis=0)
    w_high, w_rest = _split_bf16(w_ref[...])
    both = _bdot(act, w_high) + _bdot(act, w_rest)
    o_ref[...] = both[0:F32_ROWS] + both[F32_ROWS:] + b_ref[...]


def _ada_specs(layer, tn):
    in_specs = [pl.BlockSpec((F32_ROWS, D_MODEL), lambda j: (0, 0)),
                pl.BlockSpec((None, D_MODEL, tn), lambda j: (layer, 0, j)),
                pl.BlockSpec((None, 1, tn), lambda j: (layer, 0, j))]
    return in_specs, pl.BlockSpec((F32_ROWS, tn), lambda j: (0, j))


def _ada_call(layer, c_pad, ada_w, ada_b):
    n = ada_w.shape[-1]
    in_specs, out_spec = _ada_specs(layer, ADA_COLS)
    return pl.pallas_call(
        _ada_kernel,
        grid=(n // ADA_COLS,),
        in_specs=in_specs,
        out_specs=out_spec,
        out_shape=jax.ShapeDtypeStruct((F32_ROWS, n), jnp.float32),
        compiler_params=pltpu.CompilerParams(
            dimension_semantics=("arbitrary",), vmem_limit_bytes=VMEM_LIMIT),
        name="ada_mod",
    )(c_pad, ada_w, ada_b)


def _rope_tables(pos_ref, invf_ref, c_ref, sa_ref, sb_ref):
    rows = pos_ref.shape[0]
    lane = lax.broadcasted_iota(jnp.int32, (rows, LANES), 1)
    pos = jnp.where(lane < MLA_ROPE, pos_ref[:, 0:1], pos_ref[:, 1:2]).astype(jnp.float32)
    ang = pos * invf_ref[...]
    half = MLA_ROPE // 2
    in_first, in_second = lane < half, (lane >= half) & (lane < MLA_ROPE)
    cos2, sin2 = jnp.cos(ang), jnp.sin(ang)
    for r, shift in ((slice(0, rows), 0), (slice(rows, 2 * rows), MLA_ROPE)):
        cos = cos2 if shift == 0 else pltpu.roll(cos2, shift, axis=1)
        sin = sin2 if shift == 0 else pltpu.roll(sin2, shift, axis=1)
        c_ref[r, :] = jnp.where(lane < MLA_ROPE, cos, 0.0)
        sa_ref[r, :] = jnp.where(in_first, -sin, 0.0)
        sb_ref[r, :] = jnp.where(in_second, sin, 0.0)


def _rope_table_inputs(positions, tile_rows):
    inv_freq = ROPE_THETA ** (-jnp.arange(0, MLA_ROPE, 2, dtype=jnp.float32) / MLA_ROPE)
    invf = jnp.tile(inv_freq, LANES // inv_freq.shape[0]).reshape(1, LANES)
    pairs = positions.reshape(-1, 2, tile_rows // 2).transpose(0, 2, 1)
    return pairs, invf


def _rope(x, c, sa, sb):
    return (x * c + pltpu.roll(x, LANES - MLA_ROPE // 2, axis=1) * sa
            + pltpu.roll(x, MLA_ROPE // 2, axis=1) * sb)


def _inproj_kernel(layer, tiles_per_seq, x_ref, mod_ref, g_ref, w_ref, wout_ref, *refs):
    out_refs, wout_b_ref, h_s = refs[:-2], refs[-2], refs[-1]
    wout_b_ref[...] = wout_ref[...].astype(jnp.bfloat16)
    mod = mod_ref[pl.ds(pl.program_id(0) // tiles_per_seq, 1), :]
    shift = mod[:, 0:D_MODEL]
    gain = g_ref[layer:layer + 1, :] * (1.0 + mod[:, D_MODEL:2 * D_MODEL])
    n_sub = x_ref.shape[0] // IN_SUB_TILE

    def normalize(k):
        x = x_ref[k * IN_SUB_TILE:(k + 1) * IN_SUB_TILE, :]
        ms = jnp.mean(x * x, axis=-1, keepdims=True)
        h_s[k % 2] = (x * lax.rsqrt(ms + EPS) * gain + shift).astype(jnp.bfloat16)

    outs = {name: o_ref for (name, _, _), o_ref in zip(IN_OUTPUTS, out_refs)}

    def project(k):
        r = slice(k * IN_SUB_TILE, (k + 1) * IN_SUB_TILE)

        def proj(names):
            lo = _SEGMENT_COLS[names[0]][0]
            width = sum(_SEGMENT_COLS[n][1] for n in names)
            assert width % MXU_COLS == 0
            val = _bdot(h_s[k % 2], w_ref[:, lo:lo + width])
            return {n: val[:, _SEGMENT_COLS[n][0] - lo:_SEGMENT_COLS[n][0] - lo + _SEGMENT_COLS[n][1]]
                    for n in names}

        z = proj(("z",))["z"]
        outs["sz"][r, :] = (z * jax.nn.sigmoid(z)).astype(outs["sz"].dtype)
        conv = proj(("cb", "cc", "cx"))
        outs["u"][r, :] = (conv["cc"] * conv["cx"]).astype(outs["u"].dtype)
        outs["cb"][r, :] = conv["cb"].astype(outs["cb"].dtype)
        for names in (("gq", "gk", "gv"), ("mq", "misc", "mkv")):
            for name, val in proj(names).items():
                outs[name][r, :] = val.astype(outs[name].dtype)

    normalize(0)
    for k in range(n_sub):
        project(k)
        if k + 1 < n_sub:
            normalize(k + 1)


def _layer_spec(layer, shape):
    return pl.BlockSpec((None,) + shape, lambda *_: (layer,) + (0,) * len(shape))


def _whole_spec(arr):
    return pl.BlockSpec(arr.shape, lambda *_: (0,) * arr.ndim)


def _inproj_call(layer, x2, mod, norm_g, w_in_p, w_out, seq_len):
    t = x2.shape[0]
    tm = IN_ROW_TILE
    n_steps = t // tm
    tiles_per_seq = seq_len // tm
    d_out = w_out.shape[-1]
    wout_rows = w_out.shape[1] // n_steps
    n_out = len(IN_OUTPUTS)
    in_specs = [
        pl.BlockSpec((tm, D_MODEL), lambda i: (i, 0)),
        _whole_spec(mod),
        _whole_spec(norm_g),
        pl.BlockSpec((D_MODEL, IN_DIM_P), lambda i: (0, 0), pipeline_mode=pl.Buffered(1)),
        pl.BlockSpec((None, wout_rows, d_out), lambda i: (layer, i, 0)),
    ]
    operands = [x2, mod, norm_g, w_in_p, w_out]
    out_shape = [jax.ShapeDtypeStruct((t, w), dt) for _, w, dt in IN_OUTPUTS]
    out_specs = [pl.BlockSpec((tm, w), lambda i: (i, 0)) for _, w, _ in IN_OUTPUTS]
    out_shape.append(jax.ShapeDtypeStruct(w_out.shape[1:], jnp.bfloat16))
    out_specs.append(pl.BlockSpec((wout_rows, d_out), lambda i: (i, 0)))
    outs = pl.pallas_call(
        functools.partial(_inproj_kernel, layer, tiles_per_seq),
        grid=(n_steps,),
        in_specs=in_specs,
        out_specs=out_specs,
        out_shape=out_shape,
        scratch_shapes=[pltpu.VMEM((2, IN_SUB_TILE, D_MODEL), jnp.bfloat16)],
        compiler_params=pltpu.CompilerParams(
            dimension_semantics=("arbitrary",), vmem_limit_bytes=VMEM_LIMIT),
        name="in_proj",
    )(*operands)
    return outs[:n_out], outs[n_out]


def _log_sigmoid(x):
    return jnp.minimum(x, 0.0) - jnp.log(1.0 + jnp.exp(-jnp.abs(x)))


def _gla_kernel(layer, n_w_slabs, q_ref, k_ref, v_ref, misc_ref, wgf_ref, bgf_ref, wgb_ref,
                bgb_ref, g_ref, *refs):
    if n_w_slabs:
        wt_ref, o_ref, w_prep_ref = refs[0:3]
        step = pl.program_id(0) * pl.num_programs(1) + pl.program_id(1)

        @pl.when(step < n_w_slabs)
        def _prep_next_layer_weight():
            _w_in_prep_kernel(wt_ref, w_prep_ref)

        refs = refs[3:]
    else:
        o_ref, refs = refs[0], refs[1:]
    bf_s, bb_s, qcat_s, kv_s, dec_s, st_s, intra_s = refs
    s = q_ref.shape[0]
    cdim = GLA_CHUNK
    n_chunks = s // cdim
    pair_dk = 2 * GLA_DK
    pair_dv = 2 * GLA_DV
    inv_t = 1.0 / GLA_GATE_TEMP
    q_scale = GLA_DK ** -0.5

    assert math.frexp(inv_t)[0] == 0.5
    t_row = lax.broadcasted_iota(jnp.int32, (SCAN_ROWS, SCAN_ROWS), 0)
    t_col = lax.broadcasted_iota(jnp.int32, (SCAN_ROWS, SCAN_ROWS), 1)
    same_chunk = (t_row // cdim) == (t_col // cdim)
    tri_prefix = jnp.where(same_chunk & (t_col <= t_row), inv_t, 0.0).astype(jnp.bfloat16)
    tri_suffix = jnp.where(same_chunk & (t_col >= t_row), inv_t, 0.0).astype(jnp.bfloat16)

    def split(terms):
        return jnp.concatenate(_split_bf16(terms), axis=1)

    def scan_body(i, carry):
        blocks = [pl.ds(pl.multiple_of((i * SCAN_GROUP + c) * SCAN_ROWS, SCAN_ROWS), SCAN_ROWS)
                  for c in range(SCAN_GROUP)]
        pre = []
        for r in blocks:
            m = misc_ref[r, :].astype(jnp.bfloat16)
            pre.append((_bdot(m, wgf_ref[...]), _bdot(m, wgb_ref[...])))
        terms = [(split(_log_sigmoid(pf + bgf_ref[...])),
                  split(_log_sigmoid(pb + bgb_ref[...]))) for pf, pb in pre]
        sums = [(_bdot(tri_prefix, tf), _bdot(tri_suffix, tb)) for tf, tb in terms]
        for r, (sf, sb) in zip(blocks, sums):
            bf_s[r, :] = sf[:, 0:pair_dk] + sf[:, pair_dk:]
            bb_s[r, :] = sb[:, 0:pair_dk] + sb[:, pair_dk:]
        return carry

    lax.fori_loop(0, s // (SCAN_ROWS * SCAN_GROUP), scan_body, 0)

    row = lax.broadcasted_iota(jnp.int32, (cdim, pair_dk), 0)
    key_pos = lax.broadcasted_iota(jnp.int32, (cdim, pair_dk), 1) & (cdim - 1)
    keep_f = key_pos <= row
    keep_b = key_pos > row
    cat_block = lax.broadcasted_iota(jnp.int32, (cdim, 2 * pair_dk), 1) // GLA_DK
    v_head = lax.broadcasted_iota(jnp.int32, (cdim, pair_dv), 1) // GLA_DV
    head0_lanes = (lax.broadcasted_iota(jnp.int32, (GLA_DV, 2 * pair_dk), 1) // GLA_DK) % 2 == 0

    def intra_body(i, carry):
        chunks = [i * GLA_GROUP + c for c in range(GLA_GROUP)]
        rows = [pl.ds(pl.multiple_of(n * cdim, cdim), cdim) for n in chunks]
        scores, k_ends = [], []
        for n, r in zip(chunks, rows):
            bf = bf_s[r, :]
            bb = bb_s[r, :]
            bf_last = bf[cdim - 1:cdim, :]
            bb_last = bb[0:1, :]
            q = q_ref[r, :].astype(jnp.float32) * q_scale
            k = k_ref[r, :].astype(jnp.float32)
            q_cat = jnp.concatenate([q * jnp.exp(bf), q * jnp.exp(bb)],
                                    axis=1).astype(jnp.bfloat16)
            k_inv = jnp.concatenate([k * jnp.exp(-bf), k * jnp.exp(-bb)], axis=1)
            k_ends.append(jnp.concatenate(
                [k * jnp.exp(bf_last - bf), k * jnp.exp(bb_last - bb)], axis=1).astype(jnp.bfloat16))
            dec_s[n] = jnp.concatenate([jnp.exp(bf_last), jnp.exp(bb_last)], axis=1)
            qcat_s[r, :] = q_cat
            k_bd = jnp.concatenate([jnp.where(cat_block == c, k_inv, 0.0) for c in range(4)],
                                   axis=0).astype(jnp.bfloat16)
            scores.append(lax.dot_general(q_cat, k_bd, _NT,
                                          preferred_element_type=jnp.float32))
        for n, r, k_end in zip(chunks, rows, k_ends):
            kv_t = lax.dot_general(v_ref[r, :], k_end, _TN,
                                   preferred_element_type=jnp.float32)
            kv_s[n] = jnp.where(head0_lanes, kv_t[0:GLA_DV, :], kv_t[GLA_DV:, :])
        for r, sc in zip(rows, scores):
            p = (jnp.where(keep_f, sc[:, 0:pair_dk], 0.0)
                 + jnp.where(keep_b, sc[:, pair_dk:], 0.0)).astype(jnp.bfloat16)
            vv = v_ref[r, :]
            v_bd = jnp.concatenate(
                [jnp.where(v_head == j, vv, jnp.zeros_like(vv)) for j in range(2)], axis=0)
            intra_s[r, :] = _bdot(p, v_bd)
        return carry

    lax.fori_loop(0, n_chunks // GLA_GROUP, intra_body, 0)

    def state_body(i, states):
        sf, sb = states
        nf = i
        nb = n_chunks - 1 - i
        st_s[nf, :, 0:pair_dk] = sf.astype(jnp.bfloat16)
        st_s[nb, :, pair_dk:] = sb.astype(jnp.bfloat16)
        sf = sf * dec_s[nf][:, 0:pair_dk] + kv_s[nf, :, 0:pair_dk]
        sb = sb * dec_s[nb][:, pair_dk:] + kv_s[nb, :, pair_dk:]
        return sf, sb

    zero = jnp.zeros((GLA_DV, pair_dk), jnp.float32)
    lax.fori_loop(0, n_chunks, state_body, (zero, zero))

    g = g_ref[layer:layer + 1, :]

    def out_body(i, carry):
        chunks = [i * GLA_GROUP + c for c in range(GLA_GROUP)]
        rows = [pl.ds(pl.multiple_of(n * cdim, cdim), cdim) for n in chunks]
        inter = []
        for n, r in zip(chunks, rows):
            packed = st_s[n]
            none = jnp.zeros_like(packed)
            st_bd = jnp.concatenate([jnp.where(head0_lanes, packed, none),
                                     jnp.where(head0_lanes, none, packed)], axis=0)
            inter.append(lax.dot_general(qcat_s[r, :], st_bd, _NT,
                                         preferred_element_type=jnp.float32))
        for r, o_inter in zip(rows, inter):
            o = intra_s[r, :] + o_inter
            for j in range(2):
                o_ref[r, j * GLA_DV:(j + 1) * GLA_DV] = _rms(
                    o[:, j * GLA_DV:(j + 1) * GLA_DV], g).astype(o_ref.dtype)
        return carry

    lax.fori_loop(0, n_chunks // GLA_GROUP, out_body, 0)


def _gla_call(layer, gq, gk, gv, misc, wg_cat, bg_cat, gla_norm_g, w_in_t=None):
    bsz, s, _ = gq.shape
    pairs = GLA_HEADS // 2
    n_chunks = s // GLA_CHUNK
    pair_dk, pair_dv = 2 * GLA_DK, 2 * GLA_DV
    qk_spec = pl.BlockSpec((None, s, pair_dk), lambda b, p: (b, 0, p))
    v_spec = pl.BlockSpec((None, s, pair_dv), lambda b, p: (b, 0, p))

    def gate_specs(direction):
        return (pl.BlockSpec((None, MISC_W, pair_dk), lambda b, p: (layer, 0, direction * pairs + p)),
                pl.BlockSpec((None, 1, pair_dk), lambda b, p: (layer, 0, direction * pairs + p)))

    in_specs = [qk_spec, qk_spec, v_spec,
                pl.BlockSpec((None, s, MISC_W), lambda b, p: (b, 0, 0)),
                *gate_specs(0), *gate_specs(1),
                _whole_spec(gla_norm_g)]
    operands = [gq, gk, gv, misc, wg_cat, bg_cat, wg_cat, bg_cat, gla_norm_g]
    out_specs = [v_spec]
    out_shape = [jax.ShapeDtypeStruct((bsz, s, GLA_WIDTH), jnp.bfloat16)]
    n_w_slabs = 0
    if w_in_t is not None:
        n_w_slabs = D_MODEL // W_PREP_ROWS
        assert n_w_slabs <= bsz * pairs

        def slab(b, p):
            return jnp.minimum(b * pairs + p, n_w_slabs - 1)

        in_specs.append(pl.BlockSpec((None, IN_DIM, W_PREP_ROWS),
                                     lambda b, p: (layer + 1, 0, slab(b, p))))
        operands.append(w_in_t)
        out_specs.append(pl.BlockSpec((W_PREP_ROWS, IN_DIM_P), lambda b, p: (slab(b, p), 0)))
        out_shape.append(jax.ShapeDtypeStruct((D_MODEL, IN_DIM_P), jnp.bfloat16))

    return pl.pallas_call(
        functools.partial(_gla_kernel, layer, n_w_slabs),
        grid=(bsz, pairs),
        in_specs=in_specs,
        out_specs=out_specs,
        out_shape=out_shape,
        scratch_shapes=[
            pltpu.VMEM((s, pair_dk), jnp.float32),
            pltpu.VMEM((s, pair_dk), jnp.float32),
            pltpu.VMEM((s, 2 * pair_dk), jnp.bfloat16),
            pltpu.VMEM((n_chunks, GLA_DV, 2 * pair_dk), jnp.float32),
            pltpu.VMEM((n_chunks, 1, 2 * pair_dk), jnp.float32),
            pltpu.VMEM((n_chunks, GLA_DV, 2 * pair_dk), jnp.bfloat16),
            pltpu.VMEM((s, pair_dv), jnp.float32),
        ],
        compiler_params=pltpu.CompilerParams(
            dimension_semantics=("arbitrary", "arbitrary"), vmem_limit_bytes=VMEM_LIMIT),
        name="gla",
    )(*operands)


def _mla_kernel(layer, mq_ref, mkv_ref, misc_ref, c_ref, sa_ref, sb_ref, gq_ref, gkv_ref,
                wuq_ref, wuk_ref, wuvt_ref, o_ref, ckv_s, q_s, k_s, vt_s, sc_s, top_s):
    s = mkv_ref.shape[0]
    n_tiles = s // Q_TILE
    assert n_tiles % 2 == 0 and MLA_HEADS % 2 == 0
    blocks = [slice(i * KV_ROWS, (i + 1) * KV_ROWS) for i in range(s // KV_ROWS)]
    q_mult = (MLA_NOPE + MLA_ROPE) ** -0.5 * math.log2(math.e)

    for r in blocks:
        tabs = (c_ref[r, :], sa_ref[r, :], sb_ref[r, :])
        ckv_s[r, :] = _rms(mkv_ref[r, :].astype(jnp.float32),
                           gkv_ref[layer:layer + 1, :]).astype(jnp.bfloat16)
        k_rope = _rope(misc_ref[r, :], *tabs).astype(jnp.bfloat16)
        for slot in range(2):
            k_s[slot, r, MLA_NOPE:] = k_rope
        cq = _rms(mq_ref[r, :].astype(jnp.float32),
                  gq_ref[layer:layer + 1, :]).astype(jnp.bfloat16)
        qf = _bdot(cq, wuq_ref[...])
        cos_t, sin_t = tabs[0], tabs[1] + tabs[2]
        for h in range(MLA_HEADS):
            qh = qf[:, h * QK_W:(h + 1) * QK_W]
            rot = qh[:, MLA_NOPE:]
            q_rope = rot * cos_t + pltpu.roll(rot, MLA_ROPE, axis=1) * sin_t
            q_s[h, r, :] = (jnp.concatenate([qh[:, 0:MLA_NOPE], q_rope], axis=1)
                            * q_mult).astype(jnp.bfloat16)
    for slot in range(2):
        vt_s[slot, MLA_DV:, :] = jnp.ones((vt_s.shape[1] - MLA_DV, s), jnp.bfloat16)

    def build_kv(h, slot):
        for r in blocks:
            ckv = ckv_s[r, :]
            k_s[slot, r, 0:MLA_NOPE] = _bdot(ckv, wuk_ref[h]).astype(jnp.bfloat16)
            vt_s[slot, 0:MLA_DV, r] = lax.dot_general(
                wuvt_ref[h], ckv, _NT, preferred_element_type=jnp.float32).astype(jnp.bfloat16)

    def scores_t(h, slot, j):
        q = q_s[h, j * Q_TILE:(j + 1) * Q_TILE, :]
        sc_t = lax.dot_general(k_s[slot], q, _NT, preferred_element_type=jnp.float32)
        sc_s[j % 2] = sc_t
        top_s[j % 2] = jnp.broadcast_to(jnp.max(sc_t, axis=0, keepdims=True), (F32_ROWS, Q_TILE))

    def finish(h, slot, j):
        p_t = jnp.exp2(sc_s[j % 2] - top_s[j % 2][0:1, :]).astype(jnp.bfloat16)
        o_t = _bdot(vt_s[slot], p_t)
        o_ref[h, j * Q_TILE:(j + 1) * Q_TILE, :] = (
            o_t[0:MLA_DV, :] / o_t[MLA_DV:MLA_DV + 1, :]).T.astype(o_ref.dtype)

    build_kv(0, 0)
    scores_t(0, 0, 0)

    def head_pair(i, carry):
        for slot in range(2):
            h = 2 * i + slot
            h_next = jnp.minimum(h + 1, MLA_HEADS - 1)
            for j in range(n_tiles):
                if j + 1 < n_tiles:
                    scores_t(h, slot, j + 1)
                else:
                    build_kv(h_next, 1 - slot)
                    scores_t(h_next, 1 - slot, 0)
                finish(h, slot, j)
        return carry

    lax.fori_loop(0, MLA_HEADS // 2, head_pair, 0)


def _mla_call(layer, mq, mkv, misc, tabs, gq, gkv, wuq_p, wuk_p, wuvt_p):
    bsz, s, _ = mq.shape

    def per_row(width):
        return pl.BlockSpec((None, s, width), lambda b: (b, 0, 0))

    return pl.pallas_call(
        functools.partial(_mla_kernel, layer),
        grid=(bsz,),
        in_specs=[
            per_row(MLA_Q_LORA), per_row(MLA_KV_LORA), per_row(MISC_W),
            per_row(LANES), per_row(LANES), per_row(LANES),
            _whole_spec(gq), _whole_spec(gkv),
            _layer_spec(layer, (MLA_Q_LORA, MLA_HEADS * QK_W)),
            _layer_spec(layer, (MLA_HEADS, MLA_KV_LORA, MLA_NOPE)),
            _layer_spec(layer, (MLA_HEADS, MLA_DV, MLA_KV_LORA)),
        ],
        out_specs=pl.BlockSpec((None, MLA_HEADS, s, MLA_DV), lambda b: (b, 0, 0, 0)),
        out_shape=jax.ShapeDtypeStruct((bsz, MLA_HEADS, s, MLA_DV), jnp.bfloat16),
        scratch_shapes=[
            pltpu.VMEM((s, MLA_KV_LORA), jnp.bfloat16),
            pltpu.VMEM((MLA_HEADS, s, QK_W), jnp.bfloat16),
            pltpu.VMEM((2, s, QK_W), jnp.bfloat16),
            pltpu.VMEM((2, MLA_DV + BF16_ROWS, s), jnp.bfloat16),
            pltpu.VMEM((2, s, Q_TILE), jnp.float32),
            pltpu.VMEM((2, F32_ROWS, Q_TILE), jnp.float32),
        ],
        compiler_params=pltpu.CompilerParams(
            dimension_semantics=("arbitrary",),
            vmem_limit_bytes=VMEM_LIMIT),
        name="mla",
    )(mq, mkv, misc, *tabs, gq, gkv, wuq_p, wuk_p, wuvt_p)


def _outproj_kernel(layer, tiles_per_seq, final, with_next_mod, x_ref, mod_ref, ogla_ref, omla_ref,
                    cb_ref, u_ref, up_ref, un_ref, sz_ref, mg_ref, cw_ref, cg_ref, fg_ref, w_ref,
                    *refs):
    if with_next_mod:
        c_ref, ada_w_ref, ada_b_ref, o_ref, next_mod_ref = refs
        _ada_kernel(c_ref, ada_w_ref, ada_b_ref, next_mod_ref)
    else:
        (o_ref,) = refs
    f32 = jnp.float32
    j = pl.program_id(0) % tiles_per_seq
    tm = x_ref.shape[0]

    u = u_ref[...].astype(f32)
    prev_ok = (j > 0).astype(f32)
    next_ok = (j < tiles_per_seq - 1).astype(f32)
    u_prev = up_ref[...].astype(f32)[HALO_ROWS - 1:, :] * prev_ok
    u_next = un_ref[...].astype(f32)[0:1, :] * next_ok
    rows = lax.broadcasted_iota(jnp.int32, u.shape, 0)
    up = jnp.where(rows == 0, u_prev, pltpu.roll(u, 1, axis=0))
    un = jnp.where(rows == tm - 1, u_next, pltpu.roll(u, tm - 1, axis=0))
    cw = cw_ref[...]
    conv = up * cw[0:1] + u * cw[1:2] + un * cw[2:3]
    gate = mod_ref[pl.ds(pl.program_id(0) // tiles_per_seq, 1), :][:, 2 * D_MODEL:]

    def gated_dot(r, lo, y_part):
        hi = lo + y_part.shape[1]
        y = (y_part * sz_ref[r, lo:hi].astype(f32)).astype(jnp.bfloat16)
        return _bdot(y, w_ref[lo:hi, :])

    def finish(r, acc):
        out = x_ref[r, :] + gate * acc
        if final:
            out = _rms(out, fg_ref[...])
        o_ref[r, :] = out

    pending = None
    for k in range(tm // OUT_SUB_TILE):
        r = slice(k * OUT_SUB_TILE, (k + 1) * OUT_SUB_TILE)
        acc = gated_dot(r, 0, ogla_ref[r, :].astype(f32))
        if pending is not None:
            finish(*pending)
        omla = jnp.concatenate([omla_ref[h, r, :] for h in range(MLA_HEADS)], axis=1)
        acc += gated_dot(r, GLA_WIDTH, _rms(omla.astype(f32), mg_ref[layer:layer + 1, :]))
        acc += gated_dot(r, GLA_WIDTH + MLA_WIDTH,
                         _rms(cb_ref[r, :].astype(f32) * conv[r, :], cg_ref[layer:layer + 1, :]))
        pending = (r, acc)
    finish(*pending)


def _outproj_call(layer, x2, mod, o_gla, o_mla, cb, u, sz, mla_out_g, conv_w, conv_out_g,
                  final_g, w_out_b, seq_len, final, next_ada=None):
    t = x2.shape[0]
    tm = OUT_ROW_TILE
    n_steps = t // tm
    tiles_per_seq = seq_len // tm
    halo_per_tile = tm // HALO_ROWS
    n_halo = t // HALO_ROWS

    def row_spec(w):
        return pl.BlockSpec((tm, w), lambda i: (i, 0))

    prev_spec = pl.BlockSpec((HALO_ROWS, CONV_CH),
                             lambda i: (jnp.maximum(i * halo_per_tile - 1, 0), 0))
    next_spec = pl.BlockSpec((HALO_ROWS, CONV_CH),
                             lambda i: (jnp.minimum((i + 1) * halo_per_tile, n_halo - 1), 0))

    in_specs = [
        row_spec(D_MODEL),
        _whole_spec(mod),
        row_spec(GLA_WIDTH),
        pl.BlockSpec((None, MLA_HEADS, tm, MLA_DV),
                     lambda i: (i // tiles_per_seq, 0, i % tiles_per_seq, 0)),
        row_spec(CONV_CH), row_spec(CONV_CH), prev_spec, next_spec,
        row_spec(D_MODEL),
        _whole_spec(mla_out_g), _layer_spec(layer, (3, CONV_CH)), _whole_spec(conv_out_g),
        pl.BlockSpec((1, D_MODEL), lambda i: (0, 0)),
        pl.BlockSpec((D_MODEL, D_MODEL), lambda i: (0, 0), pipeline_mode=pl.Buffered(1)),
    ]
    operands = [x2, mod, o_gla, o_mla, cb, u, u, u, sz,
                mla_out_g, conv_w, conv_out_g, final_g, w_out_b]
    out_specs = [row_spec(D_MODEL)]
    out_shape = [jax.ShapeDtypeStruct((t, D_MODEL), jnp.float32)]
    if next_ada is not None:
        n_mod = next_ada[1].shape[-1]
        ada_in, ada_out = _ada_specs(layer + 1, n_mod // n_steps)
        in_specs += ada_in
        operands += list(next_ada)
        out_specs.append(ada_out)
        out_shape.append(jax.ShapeDtypeStruct((F32_ROWS, n_mod), jnp.float32))
    return pl.pallas_call(
        functools.partial(_outproj_kernel, layer, tiles_per_seq, final, next_ada is not None),
        grid=(n_steps,),
        in_specs=in_specs,
        out_specs=out_specs,
        out_shape=out_shape,
        compiler_params=pltpu.CompilerParams(
            dimension_semantics=("arbitrary",), vmem_limit_bytes=VMEM_LIMIT),
        name="out_proj",
    )(*operands)


_O_GLR = 2 * GLA_HEADS * GLA_DK + GLA_WIDTH
_O_MQ = _O_GLR + 2 * GLA_GATE_RANK
_O_MKV = _O_MQ + MLA_Q_LORA
_O_MKR = _O_MKV + MLA_KV_LORA
_O_CB = _O_MKR + MLA_ROPE
IN_DIM = _O_CB + 3 * CONV_CH + D_MODEL
W_PREP_ROWS = 256


W_PREP_CHUNK = 512


def _w_in_prep_kernel(wt_ref, o_ref):
    rows = o_ref.shape[0]
    eye = (lax.broadcasted_iota(jnp.int32, (rows, rows), 0)
           == lax.broadcasted_iota(jnp.int32, (rows, rows), 1)).astype(jnp.bfloat16)

    def put(dst, block):
        o_ref[:, dst:dst + block.shape[0]] = lax.dot_general(
            eye, block.astype(jnp.bfloat16), _NT,
            preferred_element_type=jnp.float32).astype(jnp.bfloat16)

    def put_range(dst, lo, hi):
        for start in range(lo, hi, W_PREP_CHUNK):
            stop = min(start + W_PREP_CHUNK, hi)
            put(dst + start - lo, wt_ref[start:stop, :])

    pad_rows = MISC_W - MLA_ROPE - 2 * GLA_GATE_RANK
    misc = jnp.concatenate([wt_ref[_O_MKR:_O_CB, :], wt_ref[_O_GLR:_O_MQ, :],
                            jnp.zeros((pad_rows, rows), jnp.float32)], axis=0)
    off = 0
    for src in ((0, _O_GLR), (_O_MQ, _O_MKV), misc, (_O_MKV, _O_MKR), (_O_CB, IN_DIM)):
        if isinstance(src, tuple):
            put_range(off, *src)
            off += src[1] - src[0]
        else:
            put(off, src)
            off += src.shape[0]


def _first_prep_kernel(wt_ref, pos_ref, invf_ref, o_ref, c_ref, sa_ref, sb_ref):
    _w_in_prep_kernel(wt_ref, o_ref)
    _rope_tables(pos_ref, invf_ref, c_ref, sa_ref, sb_ref)


def _first_prep_call(w_in_t, positions):
    _, n, d = w_in_t.shape
    n_steps = d // W_PREP_ROWS
    t = positions.size
    tile = t // n_steps
    pairs, invf = _rope_table_inputs(positions, tile)
    tab_spec = pl.BlockSpec((tile, LANES), lambda i: (i, 0))
    tab_shape = jax.ShapeDtypeStruct((t, LANES), jnp.float32)
    w_in_p, *tabs = pl.pallas_call(
        _first_prep_kernel,
        grid=(n_steps,),
        in_specs=[pl.BlockSpec((None, n, W_PREP_ROWS), lambda i: (0, 0, i)),
                  pl.BlockSpec((None, tile // 2, 2), lambda i: (i, 0, 0)),
                  pl.BlockSpec((1, LANES), lambda i: (0, 0))],
        out_specs=[pl.BlockSpec((W_PREP_ROWS, IN_DIM_P), lambda i: (i, 0))] + [tab_spec] * 3,
        out_shape=[jax.ShapeDtypeStruct((d, IN_DIM_P), jnp.bfloat16)] + [tab_shape] * 3,
        compiler_params=pltpu.CompilerParams(
            dimension_semantics=("arbitrary",), vmem_limit_bytes=VMEM_LIMIT),
        name="w_in_prep",
    )(w_in_t, pairs, invf)
    return w_in_p, tabs


def _gate_params(wg_f, bg_f, wg_b, bg_b):
    depth, rank, hk = wg_f.shape
    zeros = jnp.zeros((depth, rank, hk), jnp.float32)
    rows = jnp.concatenate([jnp.concatenate([wg_f, zeros], axis=2),
                            jnp.concatenate([zeros, wg_b], axis=2)], axis=1)
    wg_cat = jnp.pad(rows, ((0, 0), (MLA_ROPE, MISC_W - MLA_ROPE - 2 * rank), (0, 0)))
    bg_cat = jnp.concatenate([bg_f, bg_b], axis=1)[:, None, :]
    return wg_cat.astype(jnp.bfloat16), bg_cat


def _permute_w_uq(w):
    depth = w.shape[0]
    half = MLA_ROPE // 2
    w4 = w.reshape(depth, MLA_Q_LORA, MLA_HEADS, MLA_NOPE + MLA_ROPE)
    nope, x1, x2 = w4[..., :MLA_NOPE], w4[..., MLA_NOPE:MLA_NOPE + half], w4[..., MLA_NOPE + half:]
    cols = jnp.concatenate([nope, x1, x2, x2, x1], axis=-1)
    return cols.reshape(depth, MLA_Q_LORA, MLA_HEADS * QK_W).astype(jnp.bfloat16)


def _split_w_ukv(w):
    depth = w.shape[0]
    w4 = w.reshape(depth, MLA_KV_LORA, MLA_HEADS, MLA_NOPE + MLA_DV)
    wk = w4[..., :MLA_NOPE].transpose(0, 2, 1, 3)
    wv_t = w4[..., MLA_NOPE:].transpose(0, 2, 3, 1)
    return wk.astype(jnp.bfloat16), wv_t.astype(jnp.bfloat16)


def kernel(x, c, positions, ada_w, ada_b, norm_g, w_in, gla_wg_f, gla_bg_f, gla_wg_b, gla_bg_b,
           gla_norm_g, mla_q_norm_g, mla_kv_norm_g, mla_w_uq, mla_w_ukv, mla_out_g, conv_w,
           conv_out_g, w_out, final_g):
    bsz, s, d = x.shape
    t = bsz * s

    ada = (jnp.pad(c, ((0, F32_ROWS - bsz), (0, 0))), ada_w, ada_b[:, None, :])
    mod = _ada_call(0, *ada)

    def b3(a):
        return a.reshape(bsz, s, a.shape[-1])

    w_in_t = jnp.swapaxes(w_in, 1, 2)
    w_in_p, tabs = _first_prep_call(w_in_t, positions)
    tabs = tuple(b3(tab) for tab in tabs)
    wg_cat, bg_cat = _gate_params(gla_wg_f, gla_bg_f, gla_wg_b, gla_bg_b)
    wuq_p = _permute_w_uq(mla_w_uq)
    wuk_p, wuvt_p = _split_w_ukv(mla_w_ukv)

    h = x.reshape(t, d)
    for l in range(DEPTH):
        segs, w_out_b = _inproj_call(l, h, mod, norm_g, w_in_p, w_out, s)
        gq, gk, gv, mq, misc, mkv, cb, u, sz = segs
        o_gla, *next_w = _gla_call(l, b3(gq), b3(gk), b3(gv), b3(misc), wg_cat, bg_cat,
                                   gla_norm_g, w_in_t if l + 1 < DEPTH else None)
        if next_w:
            w_in_p = next_w[0]
        o_mla = _mla_call(l, b3(mq), b3(mkv), b3(misc), tabs, mla_q_norm_g, mla_kv_norm_g,
                          wuq_p, wuk_p, wuvt_p)
        h, *next_mod = _outproj_call(l, h, mod, o_gla.reshape(t, -1), o_mla, cb, u, sz,
                                     mla_out_g, conv_w, conv_out_g, final_g.reshape(1, d),
                                     w_out_b, s, l == DEPTH - 1,
                                     ada if l + 1 < DEPTH else None)
        if next_mod:
            mod = next_mod[0]
    return h.reshape(bsz, s, d)
```
